```python
import math
import jax
import jax.numpy as jnp
from jax import lax
import numpy as np

D_MODEL = 1024
BATCH = 2
SEQ = 8192
DEPTH = 2

N_META = 16
CHUNK = 128
PAD = CHUNK - N_META
EPS = 1e-6

S5_WIDTH = D_MODEL // 4
S5_GROUP = 16
S5_NGROUPS = S5_WIDTH // S5_GROUP
S5_STATE = 64
S5_STEP_MIN = 1e-3
S5_STEP_MAX = 1e-1

HG_HEADS = 4
HG_DK = 64
HG_DV = 64
HG_KEY_WIDTH = HG_HEADS * HG_DK
HG_WIDTH = HG_HEADS * HG_DV
CONV_K = 4

RET_HEADS = 8
RET_DK = 32
RET_DV = 64
RET_KEY_WIDTH = RET_HEADS * RET_DK
RET_WIDTH = RET_HEADS * RET_DV
ROPE_BASE = 10000.0

D_MIX = S5_WIDTH + HG_WIDTH + RET_WIDTH
IN_SPLITS = (S5_WIDTH, HG_KEY_WIDTH, HG_KEY_WIDTH, HG_WIDTH, HG_WIDTH,
             RET_KEY_WIDTH, RET_KEY_WIDTH, RET_WIDTH, RET_WIDTH)
IN_COLS = S5_WIDTH + 2 * HG_KEY_WIDTH + 2 * HG_WIDTH + 2 * RET_KEY_WIDTH + 2 * RET_WIDTH

D_FF = 2816
N_EXPERTS = 8
TOP_K = 2
D_FF_EXPERT = 3584
N_DENSE = (DEPTH + 1) // 2
N_MOE = DEPTH // 2

kernel_name = 'hybrid_s5_hgrn2_retention_moe'


def rmsnorm(x, g):
    xf = x.astype(jnp.float32)
    y = xf * lax.rsqrt(jnp.mean(xf * xf, axis=-1, keepdims=True) + EPS)
    return (y * g.astype(jnp.float32)).astype(x.dtype)


def head_rmsnorm(o):
    return o * lax.rsqrt(jnp.mean(o * o, axis=-1, keepdims=True) + EPS)


def head_groupnorm(o):
    c = o - jnp.mean(o, axis=-1, keepdims=True)
    return c * lax.rsqrt(jnp.mean(c * c, axis=-1, keepdims=True) + EPS)


def causal_dwconv(x, w):
    xp = jnp.pad(x, ((0, 0), (CONV_K - 1, 0), (0, 0)))
    return lax.conv_general_dilated(
        xp, w[:, None, :].astype(x.dtype), window_strides=(1,), padding='VALID',
        dimension_numbers=('NWC', 'WIO', 'NWC'), feature_group_count=x.shape[-1])


def rotary(t, pos):
    half = t.shape[-1] // 2
    inv_freq = ROPE_BASE ** (-jnp.arange(half, dtype=jnp.float32) / half)
    ang = pos[:, None] * inv_freq[None, :]
    cos = jnp.cos(ang)[None, :, None, :]
    sin = jnp.sin(ang)[None, :, None, :]
    t1, t2 = t[..., :half], t[..., half:]
    return jnp.concatenate([t1 * cos - t2 * sin, t1 * sin + t2 * cos], axis=-1)


def to_chunks(t):
    b, l, h, d = t.shape
    return t.reshape(b, l // CHUNK, CHUNK, h, d).transpose(1, 0, 3, 2, 4)


def from_chunks(t):
    n, b, h, c, d = t.shape
    return t.transpose(1, 0, 3, 2, 4).reshape(b, n * c, h, d)


def _ssm_combine(a, b):
    a_decay, a_state = a
    b_decay, b_state = b
    return b_decay * a_decay, b_decay * a_state + b_state


def s5_mixer(u, lam_re, lam_im, b_re, b_im, c_re, c_im, d_skip, log_step, w_glu):
    f32 = jnp.float32
    bsz, seq_len, _ = u.shape
    uf = u.astype(f32).reshape(bsz, seq_len, S5_NGROUPS, S5_GROUP)
    lam = lax.complex(lam_re.astype(f32), lam_im.astype(f32))
    step = jnp.exp(log_step.astype(f32))[:, None]
    lam_bar = jnp.exp(lam * step)
    b_mat = lax.complex(b_re.astype(f32), b_im.astype(f32))
    c_mat = lax.complex(c_re.astype(f32), c_im.astype(f32))
    b_bar = ((lam_bar - 1.0) / lam)[..., None] * b_mat
    bu = jnp.einsum('gph,blgh->blgp', b_bar, uf.astype(jnp.complex64))
    decay = jnp.broadcast_to(lam_bar, bu.shape)
    _, states = lax.associative_scan(_ssm_combine, (decay, bu), axis=1)
    y = jnp.einsum('ghp,blgp->blgh', c_mat, states).real + d_skip.astype(f32) * uf
    y = jax.nn.gelu(y.reshape(bsz, seq_len, S5_WIDTH))
    return y * jax.nn.sigmoid(y @ w_glu.astype(f32))


def hgrn2_chunked(q, k, log_f, v):
    bsz, _, h, dk = q.shape
    dv = v.shape[-1]
    causal = jnp.tril(jnp.ones((CHUNK, CHUNK), dtype=bool))

    def step(state, blk):
        qb, kb, gb, vb = blk
        g_cum = jnp.cumsum(gb, axis=2)
        o_inter = jnp.einsum('bhik,bhkv->bhiv', qb * jnp.exp(g_cum), state)
        rel = jnp.where(causal[None, None, :, :, None],
                        g_cum[:, :, :, None, :] - g_cum[:, :, None, :, :], -jnp.inf)
        scores = jnp.einsum('bhik,bhjk,bhijk->bhij', qb, kb, jnp.exp(rel))
        o_intra = jnp.einsum('bhij,bhjv->bhiv', scores, vb)
        g_last = g_cum[:, :, -1:, :]
        new_state = (jnp.exp(g_last[:, :, 0, :])[..., None] * state
                     + jnp.einsum('bhjk,bhjv->bhkv', kb * jnp.exp(g_last - g_cum), vb))
        return new_state, o_inter + o_intra

    s0 = jnp.zeros((bsz, h, dk, dv), jnp.float32)
    _, out = lax.scan(step, s0, (to_chunks(q), to_chunks(k), to_chunks(log_f), to_chunks(v)))
    return from_chunks(out)


def hgrn2_mixer(hq, hf, hi, gate, lb, out_g):
    f32 = jnp.float32
    bsz, seq_len, _ = hq.shape
    q = jax.nn.silu(hq.astype(f32)).reshape(bsz, seq_len, HG_HEADS, HG_DK)
    z = hf.astype(f32).reshape(bsz, seq_len, HG_HEADS, HG_DK)
    lbh = lb.reshape(HG_HEADS, HG_DK)
    log_f = jnp.logaddexp(jnp.log(lbh), jnp.log1p(-lbh) + jax.nn.log_sigmoid(z))
    k = -jnp.expm1(log_f)
    v = hi.astype(f32).reshape(bsz, seq_len, HG_HEADS, HG_DV)
    o = head_rmsnorm(hgrn2_chunked(q, k, log_f, v)).reshape(bsz, seq_len, HG_WIDTH)
    return o * out_g.astype(f32) * jax.nn.silu(gate.astype(f32))


def retention_chunked(q, k, v, log_gamma):
    bsz, _, h, dk = q.shape
    dv = v.shape[-1]
    n = jnp.arange(CHUNK, dtype=jnp.float32)
    lg = log_gamma[:, None]
    causal = jnp.tril(jnp.ones((CHUNK, CHUNK), dtype=bool))
    intra = jnp.exp(jnp.where(causal[None], (n[:, None] - n[None, :])[None] * lg[:, :, None], -jnp.inf))
    inter = jnp.exp((n[None, :] + 1.0) * lg)[None, :, :, None]
    to_state = jnp.exp((CHUNK - 1.0 - n[None, :]) * lg)[None, :, :, None]
    carry_decay = jnp.exp(CHUNK * lg)[None, :, :, None]

    def step(state, blk):
        qb, kb, vb = blk
        scores = jnp.einsum('bhik,bhjk->bhij', qb, kb) * intra[None]
        o = (jnp.einsum('bhij,bhjv->bhiv', scores, vb)
             + jnp.einsum('bhik,bhkv->bhiv', qb, state) * inter)
        new_state = carry_decay * state + jnp.einsum('bhjk,bhjv->bhkv', kb * to_state, vb)
        return new_state, o

    s0 = jnp.zeros((bsz, h, dk, dv), jnp.float32)
    _, out = lax.scan(step, s0, (to_chunks(q), to_chunks(k), to_chunks(v)))
    return from_chunks(out)


def retention_mixer(rq, rk, rv, gate, pos, log_gamma, out_g):
    f32 = jnp.float32
    bsz, seq_len, _ = rq.shape
    q = rotary(rq.astype(f32).reshape(bsz, seq_len, RET_HEADS, RET_DK), pos) * (RET_DK ** -0.5)
    k = rotary(rk.astype(f32).reshape(bsz, seq_len, RET_HEADS, RET_DK), pos)
    v = rv.astype(f32).reshape(bsz, seq_len, RET_HEADS, RET_DV)
    o = head_groupnorm(retention_chunked(q, k, v, log_gamma)).reshape(bsz, seq_len, RET_WIDTH)
    return o * out_g.astype(f32) * jax.nn.silu(gate.astype(f32))


def swiglu(h, w1, w3, w2):
    a = jnp.einsum('bld,df->blf', h, w1)
    b = jnp.einsum('bld,df->blf', h, w3)
    return jnp.einsum('blf,fd->bld', jax.nn.silu(a) * b, w2)


def moe_swiglu(h, router, w1, w3, w2):
    logits = jnp.einsum('bld,de->ble', h, router).astype(jnp.float32)
    top_vals, top_idx = lax.top_k(logits, TOP_K)
    gates = jax.nn.softmax(top_vals, axis=-1)
    dense_gate = jnp.sum(jax.nn.one_hot(top_idx, N_EXPERTS, dtype=jnp.float32) * gates[..., None], axis=-2)
    out = jnp.zeros_like(h)
    for e in range(N_EXPERTS):
        out = out + dense_gate[..., e:e + 1].astype(h.dtype) * swiglu(h, w1[e], w3[e], w2[e])
    return out


def setup_inputs(seed: int = 0) -> dict:
    key = jax.random.key(seed)
    keys = iter(jax.random.split(key, 40))

    def nrm(shape, scale):
        return scale * jax.random.normal(next(keys), shape, jnp.float32)

    def gain(shape):
        return 1.0 + nrm(shape, 0.02)

    n_idx = jnp.arange(S5_STATE, dtype=jnp.float32)
    return {
        'x': nrm((BATCH, SEQ, D_MODEL), 1.0),
        'meta_tokens': nrm((N_META, D_MODEL), 1.0),
        'norm_mix_g': gain((DEPTH, D_MODEL)),
        'w_in': nrm((DEPTH, D_MODEL, IN_COLS), D_MODEL ** -0.5),
        's5_lam_re': -0.5 + nrm((DEPTH, S5_NGROUPS, S5_STATE), 0.01),
        's5_lam_im': math.pi * n_idx + nrm((DEPTH, S5_NGROUPS, S5_STATE), 0.01),
        's5_b_re': nrm((DEPTH, S5_NGROUPS, S5_STATE, S5_GROUP), (2 * S5_GROUP) ** -0.5),
        's5_b_im': nrm((DEPTH, S5_NGROUPS, S5_STATE, S5_GROUP), (2 * S5_GROUP) ** -0.5),
        's5_c_re': nrm((DEPTH, S5_NGROUPS, S5_GROUP, S5_STATE), (2 * S5_STATE) ** -0.5),
        's5_c_im': nrm((DEPTH, S5_NGROUPS, S5_GROUP, S5_STATE), (2 * S5_STATE) ** -0.5),
        's5_d': nrm((DEPTH, S5_NGROUPS, S5_GROUP), 1.0),
        's5_log_step': jax.random.uniform(next(keys), (DEPTH, S5_NGROUPS), jnp.float32,
                                          math.log(S5_STEP_MIN), math.log(S5_STEP_MAX)),
        's5_w_glu': nrm((DEPTH, S5_WIDTH, S5_WIDTH), S5_WIDTH ** -0.5),
        's5_out_g': gain((DEPTH, S5_WIDTH)),
        'hg_conv_w': nrm((DEPTH, CONV_K, 2 * HG_KEY_WIDTH + HG_WIDTH), CONV_K ** -0.5),
        'hg_lb_param': nrm((DEPTH, HG_KEY_WIDTH), 1.0),
        'hg_out_g': gain((DEPTH, HG_WIDTH)),
        'ret_out_g': gain((DEPTH, RET_WIDTH)),
        'w_out': nrm((DEPTH, D_MIX, D_MODEL), D_MIX ** -0.5),
        'norm_ffn_g': gain((DEPTH, D_MODEL)),
        'ffn_w1': nrm((N_DENSE, D_MODEL, D_FF), D_MODEL ** -0.5),
        'ffn_w3': nrm((N_DENSE, D_MODEL, D_FF), D_MODEL ** -0.5),
        'ffn_w2': nrm((N_DENSE, D_FF, D_MODEL), D_FF ** -0.5),
        'moe_router': nrm((N_MOE, D_MODEL, N_EXPERTS), D_MODEL ** -0.5),
        'moe_w1': nrm((N_MOE, N_EXPERTS, D_MODEL, D_FF_EXPERT), D_MODEL ** -0.5),
        'moe_w3': nrm((N_MOE, N_EXPERTS, D_MODEL, D_FF_EXPERT), D_MODEL ** -0.5),
        'moe_w2': nrm((N_MOE, N_EXPERTS, D_FF_EXPERT, D_MODEL), D_FF_EXPERT ** -0.5),
        'final_norm_g': gain((D_MODEL,)),
    }


def reference(x, meta_tokens, norm_mix_g, w_in, s5_lam_re, s5_lam_im, s5_b_re, s5_b_im,
              s5_c_re, s5_c_im, s5_d, s5_log_step, s5_w_glu, s5_out_g, hg_conv_w,
              hg_lb_param, hg_out_g, ret_out_g, w_out, norm_ffn_g, ffn_w1, ffn_w3, ffn_w2,
              moe_router, moe_w1, moe_w3, moe_w2, final_norm_g):
    f32 = jnp.float32
    dt = x.dtype
    bsz, seq_len, d = x.shape
    total = seq_len + CHUNK
    meta = jnp.broadcast_to(meta_tokens.astype(dt)[None], (bsz, N_META, d))
    h = jnp.concatenate([jnp.zeros((bsz, PAD, d), dt), meta, x], axis=1)
    mask = (jnp.arange(total) >= PAD).astype(dt)[None, :, None]
    pos = (jnp.arange(total) - PAD).astype(f32)

    lb_all = jnp.cumsum(jax.nn.softmax(hg_lb_param.astype(f32), axis=0), axis=0)
    lb_all = lb_all - lb_all[0]
    log_gamma = jnp.log1p(-jnp.power(2.0, -5.0 - jnp.arange(RET_HEADS, dtype=f32)))

    split_idx = [int(i) for i in np.cumsum(IN_SPLITS)[:-1]]
    hg_idx = [HG_KEY_WIDTH, 2 * HG_KEY_WIDTH]

    for l in range(DEPTH):
        hn = rmsnorm(h, norm_mix_g[l])
        proj = jnp.einsum('bld,dc->blc', hn, w_in[l]) * mask
        u, hq, hf, hi, hgate, rq, rk, rv, rgate = jnp.split(proj, split_idx, axis=-1)

        y_a = rmsnorm(s5_mixer(u, s5_lam_re[l], s5_lam_im[l], s5_b_re[l], s5_b_im[l],
                               s5_c_re[l], s5_c_im[l], s5_d[l], s5_log_step[l], s5_w_glu[l]),
                      s5_out_g[l])
        hqfi = causal_dwconv(jnp.concatenate([hq, hf, hi], axis=-1), hg_conv_w[l])
        cq, cf, ci = jnp.split(hqfi, hg_idx, axis=-1)
        y_b = hgrn2_mixer(cq, cf, ci, hgate, lb_all[l], hg_out_g[l])
        y_c = retention_mixer(rq, rk, rv, rgate, pos, log_gamma, ret_out_g[l])

        mixed = jnp.concatenate([y_a, y_b, y_c], axis=-1).astype(dt)
        h = h + jnp.einsum('blc,cd->bld', mixed, w_out[l])

        hn = rmsnorm(h, norm_ffn_g[l])
        if l % 2 == 0:
            ffn = swiglu(hn, ffn_w1[l // 2], ffn_w3[l // 2], ffn_w2[l // 2])
        else:
            ffn = moe_swiglu(hn, moe_router[l // 2], moe_w1[l // 2], moe_w3[l // 2], moe_w2[l // 2])
        h = h + ffn

    h = rmsnorm(h, final_norm_g)
    return h[:, CHUNK:, :]
```

```python
import functools
import math

import jax
import jax.numpy as jnp
import numpy as np
from jax import lax
from jax.experimental import pallas as pl
from jax.experimental.pallas import tpu as pltpu

F32 = jnp.float32
BF16 = jnp.bfloat16

CHUNK = 128
N_META = 16
PAD = CHUNK - N_META
EPS = 1e-6

S5_WIDTH = 256
S5_GROUP = 16
S5_NGROUPS = 16
S5_STATE = 64
S5_NSTATE = S5_NGROUPS * S5_STATE

HG_HEADS = 4
HG_DK = 64
HG_WIDTH = 256
CONV_K = 4
HG_CONV_W = 3 * HG_WIDTH
HG_LEVELS = 7

RET_HEADS = 8
RET_DK = 32
RET_DV = 64
RET_KEY_WIDTH = 256
RET_WIDTH = 512
ROPE_BASE = 10000.0

D_MIX = 1024
IN_COLS = 2816
C_U, C_HQ, C_HGATE, C_RQ, C_RK, C_RV, C_RGATE = 0, 256, 1024, 1280, 1536, 1792, 2304

N_EXPERTS = 8
TOP_K = 2
ROUTE_LANES = 128

VMEM_LIMIT = 56 * 1024 * 1024


def _sigmoid(x):
    return 1.0 / (1.0 + jnp.exp(-x))


def _ind(cond):
    return jnp.where(cond, 1.0, 0.0).astype(BF16)


def _split_bf16(x):
    hi = x.astype(BF16)
    lo = (x - hi.astype(F32)).astype(BF16)
    return hi, lo


def _pick_tile(n, candidates):
    for t in candidates:
        if n % t == 0:
            return t
    raise ValueError(f"no tile in {candidates} divides {n}")


def _dot(a, b):
    return jnp.dot(a, b, preferred_element_type=F32)


def _dot_nt(a, b):
    return lax.dot_general(a, b, (((1,), (1,)), ((), ())), preferred_element_type=F32)


def _dot_tn(a, b):
    return lax.dot_general(a, b, (((0,), (0,)), ((), ())), preferred_element_type=F32)


def _seg_mean(x, avg_bf16):
    hi, lo = _split_bf16(x)
    return _dot(hi, avg_bf16) + _dot(lo, avg_bf16)


def _in_proj_kernel(h_ref, g_ref, w_ref, o_ref, *, tm, tiles_per_batch):
    x = h_ref[...]
    y = x * lax.rsqrt(jnp.mean(x * x, axis=-1, keepdims=True) + EPS) * g_ref[...]
    proj = _dot(y.astype(BF16), w_ref[...])
    row0 = (pl.program_id(0) % tiles_per_batch) * tm
    rows = row0 + lax.broadcasted_iota(jnp.int32, (tm, 1), 0)
    o_ref[...] = jnp.where(rows >= PAD, proj, 0.0)


def _in_proj(h, g, w_bf16, total):
    n = h.shape[0]
    tm = _pick_tile(total, (320, 128))
    return pl.pallas_call(
        functools.partial(_in_proj_kernel, tm=tm, tiles_per_batch=total // tm),
        grid=(n // tm,),
        in_specs=[
            pl.BlockSpec((tm, h.shape[1]), lambda i: (i, 0)),
            pl.BlockSpec((1, h.shape[1]), lambda i: (0, 0)),
            pl.BlockSpec(w_bf16.shape, lambda i: (0, 0)),
        ],
        out_specs=pl.BlockSpec((tm, IN_COLS), lambda i: (i, 0)),
        out_shape=jax.ShapeDtypeStruct((n, IN_COLS), F32),
        compiler_params=pltpu.CompilerParams(
            dimension_semantics=("arbitrary",), vmem_limit_bytes=VMEM_LIMIT),
        name="in_proj",
    )(h, g, w_bf16)


def _s5_chunk(u, wb_ref, wc_ref, pn_re_ref, pn_im_ref, pp_re_ref, pp_im_ref, lam_ref,
              d_ref, wglu_ref, g_ref, ltri_ref, st_ref):
    ns = S5_NSTATE
    bu = _dot(u.astype(BF16), wb_ref[...])
    bu_re, bu_im = bu[:, :ns], bu[:, ns:]
    pn_re, pn_im = pn_re_ref[...], pn_im_ref[...]
    w_re = (pn_re * bu_re - pn_im * bu_im).astype(BF16)
    w_im = (pn_re * bu_im + pn_im * bu_re).astype(BF16)
    ltri = ltri_ref[...]
    lam_re, lam_im = lam_ref[0:1, :], lam_ref[1:2, :]
    s_re, s_im = st_ref[0:1, :], st_ref[1:2, :]
    z_re = _dot(ltri, w_re) + (lam_re * s_re - lam_im * s_im)
    z_im = _dot(ltri, w_im) + (lam_re * s_im + lam_im * s_re)
    pp_re, pp_im = pp_re_ref[...], pp_im_ref[...]
    st_re = pp_re * z_re - pp_im * z_im
    st_im = pp_re * z_im + pp_im * z_re
    st_ref[0:1, :] = st_re[CHUNK - 1:CHUNK, :]
    st_ref[1:2, :] = st_im[CHUNK - 1:CHUNK, :]
    y = (_dot(st_re.astype(BF16), wc_ref[0:ns, :]) + _dot(st_im.astype(BF16), wc_ref[ns:2 * ns, :])
         + d_ref[...] * u)
    y = 0.5 * y * (1.0 + jnp.tanh(math.sqrt(2.0 / math.pi) * (y + 0.044715 * (y * y * y))))
    y = y * _sigmoid(_dot(y.astype(BF16), wglu_ref[...]))
    return y * lax.rsqrt(jnp.mean(y * y, axis=-1, keepdims=True) + EPS) * g_ref[...]


def _hgrn2_chunk(xc_ref, gate, convw_ref, lbv_ref, mall_ref, lvl_ref, bones_ref, avg_ref,
                 g_ref, st_ref):
    w = HG_WIDTH
    conv = None
    for i in range(CONV_K):
        term = xc_ref[pl.ds(8 - (CONV_K - 1) + i, CHUNK), :] * convw_ref[i:i + 1, :]
        conv = term if conv is None else conv + term
    cq, cf, v = conv[:, :w], conv[:, w:2 * w], conv[:, 2 * w:]
    q = cq * _sigmoid(cq)
    log_lb, log_1m_lb, one_m_lb = lbv_ref[0:1, :], lbv_ref[1:2, :], lbv_ref[2:3, :]
    log_sig = jnp.minimum(cf, 0.0) - jnp.log(1.0 + jnp.exp(-jnp.abs(cf)))
    b = log_1m_lb + log_sig
    logf = jnp.maximum(log_lb, b) + jnp.log(1.0 + jnp.exp(-jnp.abs(log_lb - b)))
    kk = one_m_lb * _sigmoid(-cf)
    hi, lo = _split_bf16(logf)
    hl = jnp.concatenate([hi, lo], axis=1)

    def interval_sums(blk):
        d = _dot(mall_ref[blk * CHUNK:(blk + 1) * CHUNK, :], hl)
        return d[:, :w] + d[:, w:]

    lane = lax.broadcasted_iota(jnp.int32, (1, 2 * HG_DK), 1)
    head_sel = [_ind(lane < HG_DK), _ind(lane >= HG_DK)]
    scores = [None] * HG_HEADS
    for lvl in range(HG_LEVELS):
        e = jnp.exp(interval_sums(lvl))
        qe = (q * e).astype(BF16)
        ke = (kk * e).astype(BF16)
        m = lvl_ref[lvl]
        for h in range(HG_HEADS):
            p = h // 2
            qh = qe[:, p * 128:(p + 1) * 128] * head_sel[h % 2]
            s = _dot_nt(qh, ke[:, p * 128:(p + 1) * 128]) * m
            scores[h] = s if scores[h] is None else scores[h] + s
    v_bf = v.astype(BF16)
    o_parts = []
    for p in range(HG_HEADS // 2):
        sc = jnp.concatenate([scores[2 * p].astype(BF16), scores[2 * p + 1].astype(BF16)], axis=1)
        vp = v_bf[:, p * 128:(p + 1) * 128]
        vv = jnp.concatenate([vp * head_sel[0], vp * head_sel[1]], axis=0)
        o_parts.append(_dot(sc, vv))
    o = jnp.concatenate(o_parts, axis=1)
    bones = bones_ref[...]
    o = o + _dot((q * kk).astype(BF16), bones) * v
    g_cum = interval_sums(HG_LEVELS)
    g_suffix = interval_sums(HG_LEVELS + 1)
    st = st_ref[...]
    o = o + _dot_nt((q * jnp.exp(g_cum)).astype(BF16), st.astype(BF16))
    kd = (kk * jnp.exp(g_suffix)).astype(BF16)
    upd = _dot_tn(v_bf, kd) * bones.astype(F32)
    st_ref[...] = st * jnp.exp(g_cum[CHUNK - 1:CHUNK, :]) + upd
    ms = _seg_mean(o * o, avg_ref[...])
    return o * lax.rsqrt(ms + EPS) * g_ref[...] * (gate * _sigmoid(gate))


def _ret_chunk(rq, rk, v, gate, cos, sin, intra_ref, inter_ref, tostate_ref, carry_ref,
               bmask_ref, avg_ref, g_ref, st_ref):
    hw = RET_KEY_WIDTH // 2

    def rot(t):
        t1, t2 = t[:, :hw], t[:, hw:]
        return jnp.concatenate([t1 * cos - t2 * sin, t1 * sin + t2 * cos], axis=1)

    qr, kr = rot(rq), rot(rk)
    qr_bf, kr_bf, v_bf = qr.astype(BF16), kr.astype(BF16), v.astype(BF16)
    lane_k = lax.broadcasted_iota(jnp.int32, (1, RET_KEY_WIDTH), 1) % hw
    lane_v = lax.broadcasted_iota(jnp.int32, (1, 2 * RET_DV), 1)
    v_sel = [_ind(lane_v < RET_DV), _ind(lane_v >= RET_DV)]
    half = RET_DK // 2
    o_parts = []
    for p in range(RET_HEADS // 2):
        sc = []
        for h in (2 * p, 2 * p + 1):
            sel = _ind((lane_k >= h * half) & (lane_k < (h + 1) * half))
            sc.append((_dot_nt(qr_bf * sel, kr_bf) * intra_ref[h]).astype(BF16))
        vp = v_bf[:, p * 128:(p + 1) * 128]
        vv = jnp.concatenate([vp * v_sel[0], vp * v_sel[1]], axis=0)
        o_parts.append(_dot(jnp.concatenate(sc, axis=1), vv))
    o = jnp.concatenate(o_parts, axis=1)
    st = st_ref[...]
    o = o + _dot(qr_bf, st.astype(BF16)) * inter_ref[...]
    kd = (kr * tostate_ref[...]).astype(BF16)
    st_ref[...] = st * carry_ref[...] + _dot_tn(kd, v_bf) * bmask_ref[...]
    avg = avg_ref[...]
    outs = []
    for s in range(RET_WIDTH // 256):
        os_ = o[:, s * 256:(s + 1) * 256]
        c = os_ - _seg_mean(os_, avg)
        outs.append(c * lax.rsqrt(_seg_mean(c * c, avg) + EPS))
    return jnp.concatenate(outs, axis=1) * g_ref[...] * (gate * _sigmoid(gate))


def _mixer_kernel(proj_ref, cos_ref, sin_ref,
                  wb_ref, wc_ref, pn_re_ref, pn_im_ref, pp_re_ref, pp_im_ref, lam_ref, d_ref,
                  wglu_ref, s5g_ref, ltri_ref,
                  convw_ref, lbv_ref, mall_ref, lvl_ref, bones_ref, avg_ref, hgg_ref,
                  intra_ref, inter_ref, tostate_ref, carry_ref, bmask_ref, retg_ref,
                  o_ref,
                  s5_st, hg_xc, hg_st, ret_st):
    @pl.when(pl.program_id(1) == 0)
    def _():
        s5_st[...] = jnp.zeros_like(s5_st)
        hg_xc[...] = jnp.zeros_like(hg_xc)
        hg_st[...] = jnp.zeros_like(hg_st)
        ret_st[...] = jnp.zeros_like(ret_st)

    u = proj_ref[:, C_U:C_U + S5_WIDTH]
    y_a = _s5_chunk(u, wb_ref, wc_ref, pn_re_ref, pn_im_ref, pp_re_ref, pp_im_ref, lam_ref,
                    d_ref, wglu_ref, s5g_ref, ltri_ref, s5_st)
    o_ref[:, 0:S5_WIDTH] = y_a.astype(o_ref.dtype)

    hg_xc[8:8 + CHUNK, :] = proj_ref[:, C_HQ:C_HQ + HG_CONV_W]
    y_b = _hgrn2_chunk(hg_xc, proj_ref[:, C_HGATE:C_HGATE + HG_WIDTH], convw_ref, lbv_ref,
                       mall_ref, lvl_ref, bones_ref, avg_ref, hgg_ref, hg_st)
    hg_xc[0:8, :] = hg_xc[CHUNK:CHUNK + 8, :]
    o_ref[:, S5_WIDTH:S5_WIDTH + HG_WIDTH] = y_b.astype(o_ref.dtype)

    y_c = _ret_chunk(proj_ref[:, C_RQ:C_RQ + RET_KEY_WIDTH], proj_ref[:, C_RK:C_RK + RET_KEY_WIDTH],
                     proj_ref[:, C_RV:C_RV + RET_WIDTH], proj_ref[:, C_RGATE:C_RGATE + RET_WIDTH],
                     cos_ref[...], sin_ref[...], intra_ref, inter_ref, tostate_ref, carry_ref,
                     bmask_ref, avg_ref, retg_ref, ret_st)
    o_ref[:, S5_WIDTH + HG_WIDTH:] = y_c.astype(o_ref.dtype)


def _const_spec(a):
    nd = a.ndim
    return pl.BlockSpec(a.shape, lambda b, c, _nd=nd: (0,) * _nd)


def _mixers(proj, cos, sin, consts, bsz, n_chunks):
    n = proj.shape[0]
    in_specs = [
        pl.BlockSpec((CHUNK, IN_COLS), lambda b, c: (b * n_chunks + c, 0)),
        pl.BlockSpec((CHUNK, 128), lambda b, c: (c, 0)),
        pl.BlockSpec((CHUNK, 128), lambda b, c: (c, 0)),
    ] + [_const_spec(a) for a in consts]
    return pl.pallas_call(
        _mixer_kernel,
        grid=(bsz, n_chunks),
        in_specs=in_specs,
        out_specs=pl.BlockSpec((CHUNK, D_MIX), lambda b, c: (b * n_chunks + c, 0)),
        out_shape=jax.ShapeDtypeStruct((n, D_MIX), BF16),
        scratch_shapes=[
            pltpu.VMEM((2, S5_NSTATE), F32),
            pltpu.VMEM((CHUNK + 8, HG_CONV_W), F32),
            pltpu.VMEM((HG_WIDTH, HG_WIDTH), F32),
            pltpu.VMEM((RET_KEY_WIDTH, RET_WIDTH), F32),
        ],
        compiler_params=pltpu.CompilerParams(
            dimension_semantics=("arbitrary", "arbitrary"), vmem_limit_bytes=VMEM_LIMIT),
        name="mixers",
    )(proj, cos, sin, *consts)


def _route(hn, router_ref):
    r = router_ref[...]
    r_hi, r_lo = _split_bf16(r)
    h_hi, h_lo = _split_bf16(hn)
    logits = _dot(h_hi, r_hi) + (_dot(h_lo, r_hi) + _dot(h_hi, r_lo))
    lane_i = lax.broadcasted_iota(jnp.int32, logits.shape, 1)
    lane = lane_i.astype(F32)
    neg = jnp.float32(-jnp.inf)
    logits = jnp.where(lane_i < N_EXPERTS, logits, neg)
    m1 = jnp.max(logits, axis=-1, keepdims=True)
    i1 = jnp.min(jnp.where(logits == m1, lane, float(ROUTE_LANES)), axis=-1, keepdims=True)
    rest = jnp.where(lane == i1, neg, logits)
    m2 = jnp.max(rest, axis=-1, keepdims=True)
    i2 = jnp.min(jnp.where(rest == m2, lane, float(ROUTE_LANES)), axis=-1, keepdims=True)
    e2 = jnp.exp(m2 - m1)
    g1 = 1.0 / (1.0 + e2)
    g2 = e2 / (1.0 + e2)
    out = jnp.where(lane == 0, i1, 0.0)
    out = jnp.where(lane == 1, i2, out)
    out = jnp.where(lane == 2, g1, out)
    return jnp.where(lane == 3, g2, out)


def _out_proj_kernel(mixed_ref, h_ref, w_ref, g_ref, *rest, routed):
    if routed:
        router_ref, h1_ref, hn_ref, route_ref = rest
    else:
        h1_ref, hn_ref = rest
    h1 = h_ref[...] + _dot(mixed_ref[...], w_ref[...])
    h1_ref[...] = h1
    hn = h1 * lax.rsqrt(jnp.mean(h1 * h1, axis=-1, keepdims=True) + EPS) * g_ref[...]
    hn_ref[...] = hn.astype(BF16)
    if routed:
        route_ref[...] = _route(hn, router_ref)


def _out_proj(mixed, h, w_bf16, g, router_pad=None):
    n, d = h.shape
    tm = _pick_tile(n, (640, 128))
    routed = router_pad is not None
    row = lambda i: (i, 0)
    fixed = lambda i: (0, 0)
    in_specs = [pl.BlockSpec((tm, D_MIX), row), pl.BlockSpec((tm, d), row),
                pl.BlockSpec(w_bf16.shape, fixed), pl.BlockSpec((1, d), fixed)]
    out_specs = [pl.BlockSpec((tm, d), row), pl.BlockSpec((tm, d), row)]
    out_shape = [jax.ShapeDtypeStruct((n, d), F32), jax.ShapeDtypeStruct((n, d), BF16)]
    args = [mixed, h, w_bf16, g]
    if routed:
        in_specs.append(pl.BlockSpec(router_pad.shape, fixed))
        out_specs.append(pl.BlockSpec((tm, ROUTE_LANES), row))
        out_shape.append(jax.ShapeDtypeStruct((n, ROUTE_LANES), F32))
        args.append(router_pad)
    return pl.pallas_call(
        functools.partial(_out_proj_kernel, routed=routed),
        grid=(n // tm,),
        in_specs=in_specs,
        out_specs=out_specs,
        out_shape=out_shape,
        compiler_params=pltpu.CompilerParams(
            dimension_semantics=("arbitrary",), vmem_limit_bytes=VMEM_LIMIT),
        name="out_proj_routed" if routed else "out_proj",
    )(*args)


def _ffn_kernel(hn_ref, h1_ref, w1_ref, w3_ref, w2_ref, o_ref, acc_ref):
    f = pl.program_id(1)
    x = hn_ref[...]
    a = _dot(x, w1_ref[...])
    b = _dot(x, w3_ref[...])
    part = _dot((a * _sigmoid(a) * b).astype(BF16), w2_ref[...])

    @pl.when(f == 0)
    def _():
        acc_ref[...] = part

    @pl.when(f > 0)
    def _():
        acc_ref[...] += part

    @pl.when(f == pl.num_programs(1) - 1)
    def _():
        o_ref[...] = h1_ref[...] + acc_ref[...]


def _ffn(hn, h1, w1, w3, w2):
    n, d = h1.shape
    dff = w1.shape[1]
    tm, tf = _pick_tile(n, (640, 128)), _pick_tile(dff, (1408, 128))
    return pl.pallas_call(
        _ffn_kernel,
        grid=(n // tm, dff // tf),
        in_specs=[
            pl.BlockSpec((tm, d), lambda i, f: (i, 0)),
            pl.BlockSpec((tm, d), lambda i, f: (i, 0)),
            pl.BlockSpec((d, tf), lambda i, f: (0, f)),
            pl.BlockSpec((d, tf), lambda i, f: (0, f)),
            pl.BlockSpec((tf, d), lambda i, f: (f, 0)),
        ],
        out_specs=pl.BlockSpec((tm, d), lambda i, f: (i, 0)),
        out_shape=jax.ShapeDtypeStruct((n, d), F32),
        scratch_shapes=[pltpu.VMEM((tm, d), F32)],
        compiler_params=pltpu.CompilerParams(
            dimension_semantics=("arbitrary", "arbitrary"), vmem_limit_bytes=VMEM_LIMIT),
        name="ffn_dense",
    )(hn, h1, w1, w3, w2)


def _moe_kernel(te_ref, nv_ref, xs_ref, gate_ref, w1_ref, w3_ref, w2_ref, o_ref, acc_ref):
    i, f = pl.program_id(0), pl.program_id(1)

    @pl.when(i < nv_ref[0])
    def _():
        x = xs_ref[...]
        a = _dot(x, w1_ref[0])
        b = _dot(x, w3_ref[0])
        part = _dot((a * _sigmoid(a) * b).astype(BF16), w2_ref[0])

        @pl.when(f == 0)
        def _():
            acc_ref[...] = part

        @pl.when(f > 0)
        def _():
            acc_ref[...] += part

    @pl.when(f == pl.num_programs(1) - 1)
    def _():
        valid = i < nv_ref[0]
        o_ref[...] = jnp.where(valid, acc_ref[...] * gate_ref[...], 0.0).astype(o_ref.dtype)


def _live_f(i, f, nv):
    return jnp.where(i < nv[0], f, 0)


def _moe_grouped(tile_expert, n_valid, xs, gate_col, w1, w3, w2, tm):
    p, d = xs.shape
    dff = w1.shape[2]
    tf = dff // 2
    grid_spec = pltpu.PrefetchScalarGridSpec(
        num_scalar_prefetch=2,
        grid=(p // tm, dff // tf),
        in_specs=[
            pl.BlockSpec((tm, d), lambda i, f, te, nv: (i, 0)),
            pl.BlockSpec((tm, 1), lambda i, f, te, nv: (i, 0)),
            pl.BlockSpec((1, d, tf), lambda i, f, te, nv: (te[i], 0, _live_f(i, f, nv))),
            pl.BlockSpec((1, d, tf), lambda i, f, te, nv: (te[i], 0, _live_f(i, f, nv))),
            pl.BlockSpec((1, tf, d), lambda i, f, te, nv: (te[i], _live_f(i, f, nv), 0)),
        ],
        out_specs=pl.BlockSpec((tm, d), lambda i, f, te, nv: (i, 0)),
        scratch_shapes=[pltpu.VMEM((tm, d), F32)],
    )
    return pl.pallas_call(
        _moe_kernel,
        grid_spec=grid_spec,
        out_shape=jax.ShapeDtypeStruct((p, d), BF16),
        compiler_params=pltpu.CompilerParams(
            dimension_semantics=("arbitrary", "arbitrary"), vmem_limit_bytes=VMEM_LIMIT),
        name="moe_grouped",
    )(tile_expert, n_valid, xs, gate_col, w1, w3, w2)


def _moe_dispatch(route, tm):
    n = route.shape[0]
    eid = route[:, 0:TOP_K].astype(jnp.int32).reshape(-1)
    gate = route[:, TOP_K:2 * TOP_K].reshape(-1)
    n_slots = n * TOP_K
    p_rows = n_slots + N_EXPERTS * tm
    p_rows = ((p_rows + tm - 1) // tm) * tm
    order = jnp.argsort(eid, stable=True).astype(jnp.int32)
    counts = jnp.sum((eid[:, None] == jnp.arange(N_EXPERTS, dtype=jnp.int32)[None, :]).astype(jnp.int32),
                     axis=0)
    padded = ((counts + tm - 1) // tm) * tm
    pend = jnp.cumsum(padded)
    pstart = pend - padded
    gstart = jnp.cumsum(counts) - counts
    pos = jnp.arange(p_rows, dtype=jnp.int32)
    g_of_pos = jnp.minimum(jnp.searchsorted(pend, pos, side='right'), N_EXPERTS - 1).astype(jnp.int32)
    rank = pos - pstart[g_of_pos]
    valid = rank < counts[g_of_pos]
    sorted_idx = jnp.clip(gstart[g_of_pos] + rank, 0, n_slots - 1)
    slot_of_pos = order[sorted_idx]
    tok_of_pos = jnp.where(valid, slot_of_pos // TOP_K, 0)
    gate_of_pos = jnp.where(valid, gate[slot_of_pos], 0.0)
    inv = jnp.argsort(order).astype(jnp.int32)
    sorted_eid = eid
    pos_of_slot = pstart[sorted_eid] + (inv - gstart[sorted_eid])
    tile_expert = g_of_pos[::tm]
    n_valid = (pend[-1] // tm).astype(jnp.int32).reshape(1)
    return tok_of_pos, gate_of_pos, pos_of_slot.reshape(n, TOP_K), tile_expert, n_valid, p_rows


def _final_kernel(h1_ref, y0_ref, y1_ref, g_ref, o_ref):
    h = h1_ref[...] + (y0_ref[...].astype(F32) + y1_ref[...].astype(F32))
    o_ref[...] = h * lax.rsqrt(jnp.mean(h * h, axis=-1, keepdims=True) + EPS) * g_ref[...]


def _final(h1, y0, y1, g):
    n, d = h1.shape
    tm = _pick_tile(n, (640, 128))
    row = lambda i: (i, 0)
    return pl.pallas_call(
        _final_kernel,
        grid=(n // tm,),
        in_specs=[pl.BlockSpec((tm, d), row), pl.BlockSpec((tm, d), row),
                  pl.BlockSpec((tm, d), row), pl.BlockSpec((1, d), lambda i: (0, 0))],
        out_specs=pl.BlockSpec((tm, d), row),
        out_shape=jax.ShapeDtypeStruct((n, d), F32),
        compiler_params=pltpu.CompilerParams(
            dimension_semantics=("arbitrary",), vmem_limit_bytes=VMEM_LIMIT),
        name="final_norm",
    )(h1, y0, y1, g)


def _rope_perm():
    half = RET_DK // 2
    idx = []
    for part in range(2):
        for h in range(RET_HEADS):
            for dd in range(half):
                idx.append(h * RET_DK + part * half + dd)
    return np.asarray(idx, np.int32)


def _in_col_perm():
    cols = np.arange(IN_COLS, dtype=np.int32)
    rp = _rope_perm()
    cols[C_RQ:C_RQ + RET_KEY_WIDTH] = C_RQ + rp
    cols[C_RK:C_RK + RET_KEY_WIDTH] = C_RK + rp
    return cols


def _hg_tables():
    c = CHUNK
    i = np.arange(c)[:, None]
    t = np.arange(c)[None, :]
    blocks, masks = [], []
    for lvl in range(HG_LEVELS):
        s = 1 << lvl
        r = (i // (2 * s)) * (2 * s) + s - 1
        right = i > r
        m = np.where(right, (t > r) & (t <= i), (t > i) & (t <= r))
        blocks.append(m)
        j = t
        same = (i // (2 * s)) == (j // (2 * s))
        masks.append(same & right & (j <= r))
    blocks.append(t <= i)
    blocks.append(t > i)
    mall = np.concatenate(blocks, axis=0).astype(np.float32)
    lvl = np.stack(masks, axis=0).astype(np.float32)
    ch = np.arange(HG_WIDTH)
    bones = (ch[:, None] // HG_DK == ch[None, :] // HG_DK).astype(np.float32)
    return mall, lvl, bones


def _ret_tables():
    f32 = jnp.float32
    log_gamma = jnp.log1p(-jnp.power(2.0, -5.0 - jnp.arange(RET_HEADS, dtype=f32)))
    n = jnp.arange(CHUNK, dtype=f32)
    lg = log_gamma[:, None]
    causal = jnp.tril(jnp.ones((CHUNK, CHUNK), dtype=bool))
    intra = jnp.exp(jnp.where(causal[None], (n[:, None] - n[None, :])[None] * lg[:, :, None], -jnp.inf))
    scale = RET_DK ** -0.5
    inter = jnp.exp((n[None, :] + 1.0) * lg)
    to_state = jnp.exp((CHUNK - 1.0 - n[None, :]) * lg)
    carry = jnp.exp(CHUNK * lg)[:, 0]
    head_of_v = np.arange(RET_WIDTH) // RET_DV
    head_of_k = (np.arange(RET_KEY_WIDTH) % (RET_KEY_WIDTH // 2)) // (RET_DK // 2)
    inter_t = (inter * scale).T[:, head_of_v]
    to_state_t = to_state.T[:, head_of_k]
    carry_row = carry[head_of_v][None, :]
    bmask = jnp.asarray((head_of_k[:, None] == head_of_v[None, :]).astype(np.float32))
    return intra * scale, inter_t, to_state_t, carry_row, bmask


def _s5_tables(lam_re, lam_im, b_re, b_im, c_re, c_im, d_skip, log_step):
    f32 = jnp.float32
    lam = lax.complex(lam_re.astype(f32), lam_im.astype(f32))
    step = jnp.exp(log_step.astype(f32))[:, None]
    lam_dt = lam * step
    lam_bar = jnp.exp(lam_dt)
    b_bar = ((lam_bar - 1.0) / lam)[..., None] * lax.complex(b_re.astype(f32), b_im.astype(f32))
    eye = jnp.eye(S5_NGROUPS, dtype=f32)
    wb_re = jnp.einsum('gph,gk->ghkp', jnp.real(b_bar), eye).reshape(S5_WIDTH, S5_NSTATE)
    wb_im = jnp.einsum('gph,gk->ghkp', jnp.imag(b_bar), eye).reshape(S5_WIDTH, S5_NSTATE)
    wb = jnp.concatenate([wb_re, wb_im], axis=1)
    wc_re = jnp.einsum('ghp,gk->gpkh', c_re.astype(f32), eye).reshape(S5_NSTATE, S5_WIDTH)
    wc_im = jnp.einsum('ghp,gk->gpkh', c_im.astype(f32), eye).reshape(S5_NSTATE, S5_WIDTH)
    wc = jnp.concatenate([wc_re, -wc_im], axis=0)
    t = jnp.arange(CHUNK, dtype=f32)[:, None, None]
    pp = jnp.exp(lam_dt[None] * t).reshape(CHUNK, S5_NSTATE)
    pn = jnp.exp(-lam_dt[None] * t).reshape(CHUNK, S5_NSTATE)
    lam_rows = jnp.stack([jnp.real(lam_bar).reshape(-1), jnp.imag(lam_bar).reshape(-1)], axis=0)
    return (wb.astype(BF16), wc.astype(BF16), jnp.real(pn), jnp.imag(pn), jnp.real(pp), jnp.imag(pp),
            lam_rows, d_skip.astype(f32).reshape(1, S5_WIDTH))


def kernel(x, meta_tokens, norm_mix_g, w_in, s5_lam_re, s5_lam_im, s5_b_re, s5_b_im, s5_c_re, s5_c_im, s5_d, s5_log_step, s5_w_glu, s5_out_g, hg_conv_w, hg_lb_param, hg_out_g, ret_out_g, w_out, norm_ffn_g, ffn_w1, ffn_w3, ffn_w2, moe_router, moe_w1, moe_w3, moe_w2, final_norm_g):
    bsz, seq_len, d = x.shape
    depth = w_in.shape[0]
    total = seq_len + CHUNK
    n_chunks = total // CHUNK
    n = bsz * total

    meta = jnp.broadcast_to(meta_tokens.astype(F32)[None], (bsz, N_META, d))
    h = jnp.concatenate([jnp.zeros((bsz, PAD, d), F32), meta, x.astype(F32)], axis=1).reshape(n, d)

    pos = (jnp.arange(total) - PAD).astype(F32)
    half = RET_DK // 2
    inv_freq = ROPE_BASE ** (-jnp.arange(half, dtype=F32) / half)
    ang = pos[:, None] * inv_freq[None, :]
    cos_t = jnp.tile(jnp.cos(ang), (1, RET_HEADS))
    sin_t = jnp.tile(jnp.sin(ang), (1, RET_HEADS))

    lb_all = jnp.cumsum(jax.nn.softmax(hg_lb_param.astype(F32), axis=0), axis=0)
    lb_all = lb_all - lb_all[0]

    mall_np, lvl_np, bones_np = _hg_tables()
    mall = jnp.asarray(mall_np, BF16)
    lvl = jnp.asarray(lvl_np, F32)
    bones = jnp.asarray(bones_np, BF16)
    avg = jnp.asarray(bones_np / HG_DK, BF16)
    ltri = jnp.asarray(np.tril(np.ones((CHUNK, CHUNK), np.float32)), BF16)
    intra, inter_t, to_state_t, carry_row, bmask = _ret_tables()
    col_perm = jnp.asarray(_in_col_perm())

    out = None
    for l in range(depth):
        w_in_l = jnp.take(w_in[l], col_perm, axis=1).astype(BF16)
        proj = _in_proj(h, norm_mix_g[l].astype(F32).reshape(1, d), w_in_l, total)

        s5c = _s5_tables(s5_lam_re[l], s5_lam_im[l], s5_b_re[l], s5_b_im[l], s5_c_re[l], s5_c_im[l],
                         s5_d[l], s5_log_step[l])
        lb = lb_all[l][None, :]
        lbv = jnp.concatenate([jnp.log(lb), jnp.log1p(-lb), 1.0 - lb], axis=0)
        consts = list(s5c) + [
            s5_w_glu[l].astype(BF16), s5_out_g[l].astype(F32).reshape(1, -1), ltri,
            hg_conv_w[l].astype(F32), lbv, mall, lvl, bones, avg, hg_out_g[l].astype(F32).reshape(1, -1),
            intra, inter_t, to_state_t, carry_row, bmask, ret_out_g[l].astype(F32).reshape(1, -1),
        ]
        mixed = _mixers(proj, cos_t, sin_t, consts, bsz, n_chunks)

        g_ffn = norm_ffn_g[l].astype(F32).reshape(1, d)
        w_out_l = w_out[l].astype(BF16)
        if l % 2 == 0:
            h1, hn = _out_proj(mixed, h, w_out_l, g_ffn)
            j = l // 2
            h = _ffn(hn, h1, ffn_w1[j].astype(BF16), ffn_w3[j].astype(BF16), ffn_w2[j].astype(BF16))
            y0 = y1 = None
        else:
            j = l // 2
            router_pad = jnp.zeros((d, ROUTE_LANES), F32).at[:, :N_EXPERTS].set(moe_router[j].astype(F32))
            h1, hn, route = _out_proj(mixed, h, w_out_l, g_ffn, router_pad)
            tm = 512
            tok_of_pos, gate_of_pos, pos_of_slot, tile_expert, n_valid, _ = _moe_dispatch(route, tm)
            xs = jnp.take(hn, tok_of_pos, axis=0)
            ys = _moe_grouped(tile_expert, n_valid, xs, gate_of_pos[:, None],
                              moe_w1[j].astype(BF16), moe_w3[j].astype(BF16), moe_w2[j].astype(BF16), tm)
            y0 = jnp.take(ys, pos_of_slot[:, 0], axis=0)
            y1 = jnp.take(ys, pos_of_slot[:, 1], axis=0)
            h = h1 if l == depth - 1 else h1 + (y0.astype(F32) + y1.astype(F32))

        if l == depth - 1:
            if y0 is None:
                y0 = y1 = jnp.zeros((n, d), BF16)
                h1 = h
            out = _final(h1, y0, y1, final_norm_g.astype(F32).reshape(1, d))

    return out.reshape(bsz, total, d)[:, CHUNK:, :].astype(x.dtype)
```

```python
import functools
import math

import jax
import jax.numpy as jnp
import numpy as np
from jax import lax
from jax.experimental import pallas as pl
from jax.experimental.pallas import tpu as pltpu

F32 = jnp.float32
BF16 = jnp.bfloat16

CHUNK = 128
N_META = 16
PAD = CHUNK - N_META
EPS = 1e-6

S5_WIDTH = 256
S5_GROUP = 16
S5_NGROUPS = 16
S5_STATE = 64
S5_NSTATE = S5_NGROUPS * S5_STATE

HG_HEADS = 4
HG_DK = 64
HG_WIDTH = 256
CONV_K = 4
HG_CONV_W = 3 * HG_WIDTH
HG_LEVELS = 7

RET_HEADS = 8
RET_DK = 32
RET_DV = 64
RET_KEY_WIDTH = 256
RET_WIDTH = 512
ROPE_BASE = 10000.0

D_MIX = 1024
IN_COLS = 2816
C_U, C_HQ, C_HGATE, C_RQ, C_RK, C_RV, C_RGATE = 0, 256, 1024, 1280, 1536, 1792, 2304

N_EXPERTS = 8
TOP_K = 2
ROUTE_LANES = 128

VMEM_LIMIT = 56 * 1024 * 1024


def _sigmoid(x):
    return 1.0 / (1.0 + jnp.exp(-x))


def _split_bf16(x):
    hi = x.astype(BF16)
    lo = (x - hi.astype(F32)).astype(BF16)
    return hi, lo


def _pick_tile(n, candidates):
    for t in candidates:
        if n % t == 0:
            return t
    raise ValueError(f"no tile in {candidates} divides {n}")


def _dot(a, b):
    return jnp.dot(a, b, preferred_element_type=F32)


def _dot_nt(a, b):
    return lax.dot_general(a, b, (((1,), (1,)), ((), ())), preferred_element_type=F32)


def _dot_tn(a, b):
    return lax.dot_general(a, b, (((0,), (0,)), ((), ())), preferred_element_type=F32)


def _in_proj_kernel(h_ref, g_ref, w_ref, o_ref, *, tm, tiles_per_batch):
    x = h_ref[...]
    y = x * lax.rsqrt(jnp.mean(x * x, axis=-1, keepdims=True) + EPS) * g_ref[...]
    proj = _dot(y.astype(BF16), w_ref[...])
    row0 = (pl.program_id(0) % tiles_per_batch) * tm
    rows = row0 + lax.broadcasted_iota(jnp.int32, (tm, 1), 0)
    o_ref[...] = jnp.where(rows >= PAD, proj, 0.0)


def _in_proj(h, g, w_bf16, total):
    n = h.shape[0]
    tm = _pick_tile(total, (320, 128))
    return pl.pallas_call(
        functools.partial(_in_proj_kernel, tm=tm, tiles_per_batch=total // tm),
        grid=(n // tm,),
        in_specs=[
            pl.BlockSpec((tm, h.shape[1]), lambda i: (i, 0)),
            pl.BlockSpec((1, h.shape[1]), lambda i: (0, 0)),
            pl.BlockSpec(w_bf16.shape, lambda i: (0, 0)),
        ],
        out_specs=pl.BlockSpec((tm, IN_COLS), lambda i: (i, 0)),
        out_shape=jax.ShapeDtypeStruct((n, IN_COLS), F32),
        compiler_params=pltpu.CompilerParams(
            dimension_semantics=("arbitrary",), vmem_limit_bytes=VMEM_LIMIT),
        name="in_proj",
    )(h, g, w_bf16)


def _s5_chunk(u, wb_ref, wc_ref, pn_re_ref, pn_im_ref, pp_re_ref, pp_im_ref, lam_ref,
              d_ref, wglu_ref, g_ref, ltri_ref, st_ref):
    ns = S5_NSTATE
    bu = _dot(u.astype(BF16), wb_ref[...])
    bu_re, bu_im = bu[:, :ns], bu[:, ns:]
    pn_re, pn_im = pn_re_ref[...], pn_im_ref[...]
    w_re = (pn_re * bu_re - pn_im * bu_im).astype(BF16)
    w_im = (pn_re * bu_im + pn_im * bu_re).astype(BF16)
    ltri = ltri_ref[...]
    lam_re, lam_im = lam_ref[0:1, :], lam_ref[1:2, :]
    s_re, s_im = st_ref[0:1, :], st_ref[1:2, :]
    z_re = _dot(ltri, w_re) + (lam_re * s_re - lam_im * s_im)
    z_im = _dot(ltri, w_im) + (lam_re * s_im + lam_im * s_re)
    pp_re, pp_im = pp_re_ref[...], pp_im_ref[...]
    st_re = pp_re * z_re - pp_im * z_im
    st_im = pp_re * z_im + pp_im * z_re
    st_ref[0:1, :] = st_re[CHUNK - 1:CHUNK, :]
    st_ref[1:2, :] = st_im[CHUNK - 1:CHUNK, :]
    y = (_dot(st_re.astype(BF16), wc_ref[0:ns, :]) + _dot(st_im.astype(BF16), wc_ref[ns:2 * ns, :])
         + d_ref[...] * u)
    y = 0.5 * y * (1.0 + jnp.tanh(math.sqrt(2.0 / math.pi) * (y + 0.044715 * (y * y * y))))
    y = y * _sigmoid(_dot(y.astype(BF16), wglu_ref[...]))
    return y * lax.rsqrt(jnp.mean(y * y, axis=-1, keepdims=True) + EPS) * g_ref[...]


def _hgrn2_chunk(xc_ref, gate, convw_ref, lbv_ref, mall_ref, lvl_ref, rsel_ref, bones_ref, avg_ref,
                 g_ref, st_ref):
    w = HG_WIDTH
    conv = None
    for i in range(CONV_K):
        term = xc_ref[pl.ds(8 - (CONV_K - 1) + i, CHUNK), :] * convw_ref[i:i + 1, :]
        conv = term if conv is None else conv + term
    cq, cf, v = conv[:, :w], conv[:, w:2 * w], conv[:, 2 * w:]
    q = cq * _sigmoid(cq)
    log_lb, log_1m_lb, one_m_lb = lbv_ref[0:1, :], lbv_ref[1:2, :], lbv_ref[2:3, :]
    log_sig = jnp.minimum(cf, 0.0) - jnp.log(1.0 + jnp.exp(-jnp.abs(cf)))
    b = log_1m_lb + log_sig
    logf = jnp.maximum(log_lb, b) + jnp.log(1.0 + jnp.exp(-jnp.abs(log_lb - b)))
    kk = one_m_lb * _sigmoid(-cf)
    hi, lo = _split_bf16(logf)
    hl = jnp.concatenate([hi, lo], axis=1)

    def interval_sums(blk):
        d = _dot(mall_ref[blk * CHUNK:(blk + 1) * CHUNK, :], hl)
        return d[:, :w] + d[:, w:]

    lane = lax.broadcasted_iota(jnp.int32, (1, 2 * HG_DK), 1)
    head_sel = [jnp.where(lane < HG_DK, 1.0, 0.0), jnp.where(lane >= HG_DK, 1.0, 0.0)]
    n_pairs = HG_HEADS // 2
    scores = [None] * n_pairs
    for lvl in range(HG_LEVELS):
        s = 1 << lvl
        e = jnp.exp(interval_sums(lvl))
        if s >= 8:
            qk = jnp.concatenate([(q if (r // s) % 2 else kk)[r:r + s] for r in range(0, CHUNK, s)], axis=0)
        else:
            qk = jnp.where(rsel_ref[lvl] > 0.5, q, kk)
        x = qk * e
        m = lvl_ref[lvl]
        for p in range(n_pairs):
            xp = x[:, p * 128:(p + 1) * 128]
            lhs = jnp.concatenate([xp * head_sel[0], xp * head_sel[1]], axis=0).astype(BF16)
            sc = _dot_nt(lhs, xp.astype(BF16)) * m
            scores[p] = sc if scores[p] is None else scores[p] + sc
    v_bf = v.astype(BF16)
    o_parts = []
    for p in range(n_pairs):
        sc = scores[p].astype(BF16)
        sc = jnp.concatenate([sc[:CHUNK], sc[CHUNK:]], axis=1)
        vp = v[:, p * 128:(p + 1) * 128]
        vv = jnp.concatenate([vp * head_sel[0], vp * head_sel[1]], axis=0).astype(BF16)
        o_parts.append(_dot(sc, vv))
    o = jnp.concatenate(o_parts, axis=1)
    bones = bones_ref[...]
    o = o + _dot((q * kk).astype(BF16), bones) * v
    g_cum = interval_sums(HG_LEVELS)
    g_suffix = interval_sums(HG_LEVELS + 1)
    st = st_ref[...]
    o = o + _dot_nt((q * jnp.exp(g_cum)).astype(BF16), st.astype(BF16))
    kd = (kk * jnp.exp(g_suffix)).astype(BF16)
    upd = _dot_tn(v_bf, kd) * bones.astype(F32)
    st_ref[...] = st * jnp.exp(g_cum[CHUNK - 1:CHUNK, :]) + upd
    ms = _dot((o * o).astype(BF16), avg_ref[...])
    return o * lax.rsqrt(ms + EPS) * g_ref[...] * (gate * _sigmoid(gate))


def _ret_chunk(rq, rk, v, gate, cos, sin, intra_ref, inter_ref, tostate_ref, carry_ref,
               bmask_ref, avg_ref, g_ref, st_ref):
    hw = RET_KEY_WIDTH // 2

    def rot(t):
        t1, t2 = t[:, :hw], t[:, hw:]
        return jnp.concatenate([t1 * cos - t2 * sin, t1 * sin + t2 * cos], axis=1)

    qr, kr = rot(rq), rot(rk)
    qr_bf, kr_bf, v_bf = qr.astype(BF16), kr.astype(BF16), v.astype(BF16)
    lane_k = lax.broadcasted_iota(jnp.int32, (1, RET_KEY_WIDTH), 1) % hw
    lane_v = lax.broadcasted_iota(jnp.int32, (1, 2 * RET_DV), 1)
    v_sel = [jnp.where(lane_v < RET_DV, 1.0, 0.0), jnp.where(lane_v >= RET_DV, 1.0, 0.0)]
    half = RET_DK // 2
    o_parts = []
    for p in range(RET_HEADS // 2):
        sel = [jnp.where((lane_k >= h * half) & (lane_k < (h + 1) * half), 1.0, 0.0)
               for h in (2 * p, 2 * p + 1)]
        lhs = jnp.concatenate([qr * sel[0], qr * sel[1]], axis=0).astype(BF16)
        sc = (_dot_nt(lhs, kr_bf) * intra_ref[p]).astype(BF16)
        sc = jnp.concatenate([sc[:CHUNK], sc[CHUNK:]], axis=1)
        vp = v[:, p * 128:(p + 1) * 128]
        vv = jnp.concatenate([vp * v_sel[0], vp * v_sel[1]], axis=0).astype(BF16)
        o_parts.append(_dot(sc, vv))
    o = jnp.concatenate(o_parts, axis=1)
    st = st_ref[...]
    o = o + _dot(qr_bf, st.astype(BF16)) * inter_ref[...]
    kd = (kr * tostate_ref[...]).astype(BF16)
    st_ref[...] = st * carry_ref[...] + _dot_tn(kd, v_bf) * bmask_ref[...]
    avg = avg_ref[...]
    outs = []
    for s in range(RET_WIDTH // 256):
        os_ = o[:, s * 256:(s + 1) * 256]
        c = os_ - _dot(os_.astype(BF16), avg)
        outs.append(c * lax.rsqrt(_dot((c * c).astype(BF16), avg) + EPS))
    return jnp.concatenate(outs, axis=1) * g_ref[...] * (gate * _sigmoid(gate))


def _mixer_kernel(proj_ref, cos_ref, sin_ref,
                  wb_ref, wc_ref, pn_re_ref, pn_im_ref, pp_re_ref, pp_im_ref, lam_ref, d_ref,
                  wglu_ref, s5g_ref, ltri_ref,
                  convw_ref, lbv_ref, mall_ref, lvl_ref, rsel_ref, bones_ref, avg_ref, hgg_ref,
                  intra_ref, inter_ref, tostate_ref, carry_ref, bmask_ref, retg_ref,
                  o_ref,
                  s5_st, hg_xc, hg_st, ret_st):
    @pl.when(pl.program_id(0) == 0)
    def _():
        s5_st[...] = jnp.zeros_like(s5_st)
        hg_xc[...] = jnp.zeros_like(hg_xc)
        hg_st[...] = jnp.zeros_like(hg_st)
        ret_st[...] = jnp.zeros_like(ret_st)

    for b in range(proj_ref.shape[0]):
        pr, out, xc = proj_ref.at[b], o_ref.at[b], hg_xc.at[b]
        u = pr[:, C_U:C_U + S5_WIDTH]
        y_a = _s5_chunk(u, wb_ref, wc_ref, pn_re_ref, pn_im_ref, pp_re_ref, pp_im_ref, lam_ref,
                        d_ref, wglu_ref, s5g_ref, ltri_ref, s5_st.at[b])
        out[:, 0:S5_WIDTH] = y_a.astype(out.dtype)

        xc[8:8 + CHUNK, :] = pr[:, C_HQ:C_HQ + HG_CONV_W]
        y_b = _hgrn2_chunk(xc, pr[:, C_HGATE:C_HGATE + HG_WIDTH], convw_ref, lbv_ref,
                           mall_ref, lvl_ref, rsel_ref, bones_ref, avg_ref, hgg_ref, hg_st.at[b])
        xc[0:8, :] = xc[CHUNK:CHUNK + 8, :]
        out[:, S5_WIDTH:S5_WIDTH + HG_WIDTH] = y_b.astype(out.dtype)

        y_c = _ret_chunk(pr[:, C_RQ:C_RQ + RET_KEY_WIDTH], pr[:, C_RK:C_RK + RET_KEY_WIDTH],
                         pr[:, C_RV:C_RV + RET_WIDTH], pr[:, C_RGATE:C_RGATE + RET_WIDTH],
                         cos_ref[...], sin_ref[...], intra_ref, inter_ref, tostate_ref, carry_ref,
                         bmask_ref, avg_ref, retg_ref, ret_st.at[b])
        out[:, S5_WIDTH + HG_WIDTH:] = y_c.astype(out.dtype)


def _const_spec(a):
    nd = a.ndim
    return pl.BlockSpec(a.shape, lambda c, _nd=nd: (0,) * _nd)


def _mixers(proj, cos, sin, consts, bsz, n_chunks):
    total = proj.shape[1]
    in_specs = [
        pl.BlockSpec((bsz, CHUNK, IN_COLS), lambda c: (0, c, 0)),
        pl.BlockSpec((CHUNK, 128), lambda c: (c, 0)),
        pl.BlockSpec((CHUNK, 128), lambda c: (c, 0)),
    ] + [_const_spec(a) for a in consts]
    return pl.pallas_call(
        _mixer_kernel,
        grid=(n_chunks,),
        in_specs=in_specs,
        out_specs=pl.BlockSpec((bsz, CHUNK, D_MIX), lambda c: (0, c, 0)),
        out_shape=jax.ShapeDtypeStruct((bsz, total, D_MIX), BF16),
        scratch_shapes=[
            pltpu.VMEM((bsz, 2, S5_NSTATE), F32),
            pltpu.VMEM((bsz, CHUNK + 8, HG_CONV_W), F32),
            pltpu.VMEM((bsz, HG_WIDTH, HG_WIDTH), F32),
            pltpu.VMEM((bsz, RET_KEY_WIDTH, RET_WIDTH), F32),
        ],
        compiler_params=pltpu.CompilerParams(
            dimension_semantics=("arbitrary",), vmem_limit_bytes=VMEM_LIMIT),
        name="mixers",
    )(proj, cos, sin, *consts)


def _route(hn, router_ref):
    r = router_ref[...]
    r_hi, r_lo = _split_bf16(r)
    h_hi, h_lo = _split_bf16(hn)
    logits = _dot(h_hi, r_hi) + (_dot(h_lo, r_hi) + _dot(h_hi, r_lo))
    lane_i = lax.broadcasted_iota(jnp.int32, logits.shape, 1)
    lane = lane_i.astype(F32)
    neg = jnp.float32(-jnp.inf)
    logits = jnp.where(lane_i < N_EXPERTS, logits, neg)
    m1 = jnp.max(logits, axis=-1, keepdims=True)
    i1 = jnp.min(jnp.where(logits == m1, lane, float(ROUTE_LANES)), axis=-1, keepdims=True)
    rest = jnp.where(lane == i1, neg, logits)
    m2 = jnp.max(rest, axis=-1, keepdims=True)
    i2 = jnp.min(jnp.where(rest == m2, lane, float(ROUTE_LANES)), axis=-1, keepdims=True)
    e2 = jnp.exp(m2 - m1)
    g1 = 1.0 / (1.0 + e2)
    g2 = e2 / (1.0 + e2)
    out = jnp.where(lane == 0, i1, 0.0)
    out = jnp.where(lane == 1, i2, out)
    out = jnp.where(lane == 2, g1, out)
    return jnp.where(lane == 3, g2, out)


def _out_proj_kernel(mixed_ref, h_ref, w_ref, g_ref, *rest, routed):
    if routed:
        router_ref, h1_ref, hn_ref, route_ref = rest
    else:
        h1_ref, hn_ref = rest
    h1 = h_ref[...] + _dot(mixed_ref[...], w_ref[...])
    h1_ref[...] = h1
    hn = h1 * lax.rsqrt(jnp.mean(h1 * h1, axis=-1, keepdims=True) + EPS) * g_ref[...]
    hn_ref[...] = hn.astype(BF16)
    if routed:
        route_ref[...] = _route(hn, router_ref)


def _out_proj(mixed, h, w_bf16, g, router_pad=None):
    n, d = h.shape
    tm = _pick_tile(n, (640, 128))
    routed = router_pad is not None
    row = lambda i: (i, 0)
    fixed = lambda i: (0, 0)
    in_specs = [pl.BlockSpec((tm, D_MIX), row), pl.BlockSpec((tm, d), row),
                pl.BlockSpec(w_bf16.shape, fixed), pl.BlockSpec((1, d), fixed)]
    out_specs = [pl.BlockSpec((tm, d), row), pl.BlockSpec((tm, d), row)]
    out_shape = [jax.ShapeDtypeStruct((n, d), F32), jax.ShapeDtypeStruct((n, d), BF16)]
    args = [mixed, h, w_bf16, g]
    if routed:
        in_specs.append(pl.BlockSpec(router_pad.shape, fixed))
        out_specs.append(pl.BlockSpec((tm, ROUTE_LANES), row))
        out_shape.append(jax.ShapeDtypeStruct((n, ROUTE_LANES), F32))
        args.append(router_pad)
    return pl.pallas_call(
        functools.partial(_out_proj_kernel, routed=routed),
        grid=(n // tm,),
        in_specs=in_specs,
        out_specs=out_specs,
        out_shape=out_shape,
        compiler_params=pltpu.CompilerParams(
            dimension_semantics=("arbitrary",), vmem_limit_bytes=VMEM_LIMIT),
        name="out_proj_routed" if routed else "out_proj",
    )(*args)


def _ffn_kernel(hn_ref, h1_ref, w1_ref, w3_ref, w2_ref, o_ref, acc_ref):
    f = pl.program_id(1)
    x = hn_ref[...]
    a = _dot(x, w1_ref[...])
    b = _dot(x, w3_ref[...])
    part = _dot((a * _sigmoid(a) * b).astype(BF16), w2_ref[...])

    @pl.when(f == 0)
    def _():
        acc_ref[...] = part

    @pl.when(f > 0)
    def _():
        acc_ref[...] += part

    @pl.when(f == pl.num_programs(1) - 1)
    def _():
        o_ref[...] = h1_ref[...] + acc_ref[...]


def _ffn(hn, h1, w1, w3, w2):
    n, d = h1.shape
    dff = w1.shape[1]
    tm, tf = _pick_tile(n, (640, 128)), _pick_tile(dff, (1408, 128))
    return pl.pallas_call(
        _ffn_kernel,
        grid=(n // tm, dff // tf),
        in_specs=[
            pl.BlockSpec((tm, d), lambda i, f: (i, 0)),
            pl.BlockSpec((tm, d), lambda i, f: (i, 0)),
            pl.BlockSpec((d, tf), lambda i, f: (0, f)),
            pl.BlockSpec((d, tf), lambda i, f: (0, f)),
            pl.BlockSpec((tf, d), lambda i, f: (f, 0)),
        ],
        out_specs=pl.BlockSpec((tm, d), lambda i, f: (i, 0)),
        out_shape=jax.ShapeDtypeStruct((n, d), F32),
        scratch_shapes=[pltpu.VMEM((tm, d), F32)],
        compiler_params=pltpu.CompilerParams(
            dimension_semantics=("arbitrary", "arbitrary"), vmem_limit_bytes=VMEM_LIMIT),
        name="ffn_dense",
    )(hn, h1, w1, w3, w2)


def _moe_kernel(te_ref, nv_ref, xs_ref, gate_ref, w1_ref, w3_ref, w2_ref, o_ref, acc_ref):
    i, f = pl.program_id(0), pl.program_id(1)

    @pl.when(i < nv_ref[0])
    def _():
        x = xs_ref[...]
        a = _dot(x, w1_ref[0])
        b = _dot(x, w3_ref[0])
        part = _dot((a * _sigmoid(a) * b).astype(BF16), w2_ref[0])

        @pl.when(f == 0)
        def _():
            acc_ref[...] = part

        @pl.when(f > 0)
        def _():
            acc_ref[...] += part

    @pl.when(f == pl.num_programs(1) - 1)
    def _():
        valid = i < nv_ref[0]
        o_ref[...] = jnp.where(valid, acc_ref[...] * gate_ref[...], 0.0).astype(o_ref.dtype)


def _live_f(i, f, nv):
    return jnp.where(i < nv[0], f, 0)


def _moe_grouped(tile_expert, n_valid, xs, gate_col, w1, w3, w2, tm):
    p, d = xs.shape
    dff = w1.shape[2]
    tf = dff // 2
    grid_spec = pltpu.PrefetchScalarGridSpec(
        num_scalar_prefetch=2,
        grid=(p // tm, dff // tf),
        in_specs=[
            pl.BlockSpec((tm, d), lambda i, f, te, nv: (i, 0)),
            pl.BlockSpec((tm, 1), lambda i, f, te, nv: (i, 0)),
            pl.BlockSpec((1, d, tf), lambda i, f, te, nv: (te[i], 0, _live_f(i, f, nv))),
            pl.BlockSpec((1, d, tf), lambda i, f, te, nv: (te[i], 0, _live_f(i, f, nv))),
            pl.BlockSpec((1, tf, d), lambda i, f, te, nv: (te[i], _live_f(i, f, nv), 0)),
        ],
        out_specs=pl.BlockSpec((tm, d), lambda i, f, te, nv: (i, 0)),
        scratch_shapes=[pltpu.VMEM((tm, d), F32)],
    )
    return pl.pallas_call(
        _moe_kernel,
        grid_spec=grid_spec,
        out_shape=jax.ShapeDtypeStruct((p, d), BF16),
        compiler_params=pltpu.CompilerParams(
            dimension_semantics=("arbitrary", "arbitrary"), vmem_limit_bytes=VMEM_LIMIT),
        name="moe_grouped",
    )(tile_expert, n_valid, xs, gate_col, w1, w3, w2)


def _moe_dispatch(route, tm):
    n = route.shape[0]
    eid = route[:, 0:TOP_K].astype(jnp.int32).reshape(-1)
    gate = route[:, TOP_K:2 * TOP_K].reshape(-1)
    n_slots = n * TOP_K
    p_rows = n_slots + N_EXPERTS * tm
    p_rows = ((p_rows + tm - 1) // tm) * tm
    order = jnp.argsort(eid, stable=True).astype(jnp.int32)
    counts = jnp.sum((eid[:, None] == jnp.arange(N_EXPERTS, dtype=jnp.int32)[None, :]).astype(jnp.int32),
                     axis=0)
    padded = ((counts + tm - 1) // tm) * tm
    pend = jnp.cumsum(padded)
    pstart = pend - padded
    gstart = jnp.cumsum(counts) - counts
    pos = jnp.arange(p_rows, dtype=jnp.int32)
    g_of_pos = jnp.minimum(jnp.searchsorted(pend, pos, side='right'), N_EXPERTS - 1).astype(jnp.int32)
    rank = pos - pstart[g_of_pos]
    valid = rank < counts[g_of_pos]
    sorted_idx = jnp.clip(gstart[g_of_pos] + rank, 0, n_slots - 1)
    slot_of_pos = order[sorted_idx]
    tok_of_pos = jnp.where(valid, slot_of_pos // TOP_K, 0)
    gate_of_pos = jnp.where(valid, gate[slot_of_pos], 0.0)
    inv = jnp.argsort(order).astype(jnp.int32)
    sorted_eid = eid
    pos_of_slot = pstart[sorted_eid] + (inv - gstart[sorted_eid])
    tile_expert = g_of_pos[::tm]
    n_valid = (pend[-1] // tm).astype(jnp.int32).reshape(1)
    return tok_of_pos, gate_of_pos, pos_of_slot.reshape(n, TOP_K), tile_expert, n_valid, p_rows


def _final_kernel(h1_ref, y0_ref, y1_ref, g_ref, o_ref):
    h = h1_ref[...] + (y0_ref[...].astype(F32) + y1_ref[...].astype(F32))
    o_ref[...] = h * lax.rsqrt(jnp.mean(h * h, axis=-1, keepdims=True) + EPS) * g_ref[...]


def _final(h1, y0, y1, g):
    n, d = h1.shape
    tm = _pick_tile(n, (640, 128))
    row = lambda i: (i, 0)
    return pl.pallas_call(
        _final_kernel,
        grid=(n // tm,),
        in_specs=[pl.BlockSpec((tm, d), row), pl.BlockSpec((tm, d), row),
                  pl.BlockSpec((tm, d), row), pl.BlockSpec((1, d), lambda i: (0, 0))],
        out_specs=pl.BlockSpec((tm, d), row),
        out_shape=jax.ShapeDtypeStruct((n, d), F32),
        compiler_params=pltpu.CompilerParams(
            dimension_semantics=("arbitrary",), vmem_limit_bytes=VMEM_LIMIT),
        name="final_norm",
    )(h1, y0, y1, g)


def _rope_perm():
    half = RET_DK // 2
    idx = []
    for part in range(2):
        for h in range(RET_HEADS):
            for dd in range(half):
                idx.append(h * RET_DK + part * half + dd)
    return np.asarray(idx, np.int32)


def _in_col_perm():
    cols = np.arange(IN_COLS, dtype=np.int32)
    rp = _rope_perm()
    cols[C_RQ:C_RQ + RET_KEY_WIDTH] = C_RQ + rp
    cols[C_RK:C_RK + RET_KEY_WIDTH] = C_RK + rp
    return cols


def _hg_tables():
    c = CHUNK
    i = np.arange(c)[:, None]
    t = np.arange(c)[None, :]
    blocks, masks, rsel = [], [], []
    for lvl in range(HG_LEVELS):
        s = 1 << lvl
        r = (i // (2 * s)) * (2 * s) + s - 1
        right = i > r
        m = np.where(right, (t > r) & (t <= i), (t > i) & (t <= r))
        blocks.append(m)
        j = t
        same = (i // (2 * s)) == (j // (2 * s))
        mk = same & right & (j <= r)
        masks.append(np.concatenate([mk, mk], axis=0))
        if s < 8:
            rsel.append(np.broadcast_to(right, (c, HG_WIDTH)))
    blocks.append(t <= i)
    blocks.append(t > i)
    mall = np.concatenate(blocks, axis=0).astype(np.float32)
    lvl = np.stack(masks, axis=0).astype(np.float32)
    ch = np.arange(HG_WIDTH)
    bones = (ch[:, None] // HG_DK == ch[None, :] // HG_DK).astype(np.float32)
    return mall, lvl, np.stack(rsel, axis=0).astype(np.float32), bones


def _ret_tables():
    f32 = jnp.float32
    log_gamma = jnp.log1p(-jnp.power(2.0, -5.0 - jnp.arange(RET_HEADS, dtype=f32)))
    n = jnp.arange(CHUNK, dtype=f32)
    lg = log_gamma[:, None]
    causal = jnp.tril(jnp.ones((CHUNK, CHUNK), dtype=bool))
    intra = jnp.exp(jnp.where(causal[None], (n[:, None] - n[None, :])[None] * lg[:, :, None], -jnp.inf))
    scale = RET_DK ** -0.5
    inter = jnp.exp((n[None, :] + 1.0) * lg)
    to_state = jnp.exp((CHUNK - 1.0 - n[None, :]) * lg)
    carry = jnp.exp(CHUNK * lg)[:, 0]
    head_of_v = np.arange(RET_WIDTH) // RET_DV
    head_of_k = (np.arange(RET_KEY_WIDTH) % (RET_KEY_WIDTH // 2)) // (RET_DK // 2)
    inter_t = (inter * scale).T[:, head_of_v]
    to_state_t = to_state.T[:, head_of_k]
    carry_row = carry[head_of_v][None, :]
    bmask = jnp.asarray((head_of_k[:, None] == head_of_v[None, :]).astype(np.float32))
    intra_pairs = (intra * scale).reshape(RET_HEADS // 2, 2 * CHUNK, CHUNK)
    return intra_pairs, inter_t, to_state_t, carry_row, bmask


def _s5_tables(lam_re, lam_im, b_re, b_im, c_re, c_im, d_skip, log_step):
    f32 = jnp.float32
    lam = lax.complex(lam_re.astype(f32), lam_im.astype(f32))
    step = jnp.exp(log_step.astype(f32))[:, None]
    lam_dt = lam * step
    lam_bar = jnp.exp(lam_dt)
    b_bar = ((lam_bar - 1.0) / lam)[..., None] * lax.complex(b_re.astype(f32), b_im.astype(f32))
    eye = jnp.eye(S5_NGROUPS, dtype=f32)
    wb_re = jnp.einsum('gph,gk->ghkp', jnp.real(b_bar), eye).reshape(S5_WIDTH, S5_NSTATE)
    wb_im = jnp.einsum('gph,gk->ghkp', jnp.imag(b_bar), eye).reshape(S5_WIDTH, S5_NSTATE)
    wb = jnp.concatenate([wb_re, wb_im], axis=1)
    wc_re = jnp.einsum('ghp,gk->gpkh', c_re.astype(f32), eye).reshape(S5_NSTATE, S5_WIDTH)
    wc_im = jnp.einsum('ghp,gk->gpkh', c_im.astype(f32), eye).reshape(S5_NSTATE, S5_WIDTH)
    wc = jnp.concatenate([wc_re, -wc_im], axis=0)
    t = jnp.arange(CHUNK, dtype=f32)[:, None, None]
    pp = jnp.exp(lam_dt[None] * t).reshape(CHUNK, S5_NSTATE)
    pn = jnp.exp(-lam_dt[None] * t).reshape(CHUNK, S5_NSTATE)
    lam_rows = jnp.stack([jnp.real(lam_bar).reshape(-1), jnp.imag(lam_bar).reshape(-1)], axis=0)
    return (wb.astype(BF16), wc.astype(BF16), jnp.real(pn), jnp.imag(pn), jnp.real(pp), jnp.imag(pp),
            lam_rows, d_skip.astype(f32).reshape(1, S5_WIDTH))


def kernel(x, meta_tokens, norm_mix_g, w_in, s5_lam_re, s5_lam_im, s5_b_re, s5_b_im, s5_c_re, s5_c_im, s5_d, s5_log_step, s5_w_glu, s5_out_g, hg_conv_w, hg_lb_param, hg_out_g, ret_out_g, w_out, norm_ffn_g, ffn_w1, ffn_w3, ffn_w2, moe_router, moe_w1, moe_w3, moe_w2, final_norm_g):
    bsz, seq_len, d = x.shape
    depth = w_in.shape[0]
    total = seq_len + CHUNK
    n_chunks = total // CHUNK
    n = bsz * total

    meta = jnp.broadcast_to(meta_tokens.astype(F32)[None], (bsz, N_META, d))
    h = jnp.concatenate([jnp.zeros((bsz, PAD, d), F32), meta, x.astype(F32)], axis=1).reshape(n, d)

    pos = (jnp.arange(total) - PAD).astype(F32)
    half = RET_DK // 2
    inv_freq = ROPE_BASE ** (-jnp.arange(half, dtype=F32) / half)
    ang = pos[:, None] * inv_freq[None, :]
    cos_t = jnp.tile(jnp.cos(ang), (1, RET_HEADS))
    sin_t = jnp.tile(jnp.sin(ang), (1, RET_HEADS))

    lb_all = jnp.cumsum(jax.nn.softmax(hg_lb_param.astype(F32), axis=0), axis=0)
    lb_all = lb_all - lb_all[0]

    mall_np, lvl_np, rsel_np, bones_np = _hg_tables()
    mall = jnp.asarray(mall_np, BF16)
    lvl = jnp.asarray(lvl_np, F32)
    rsel = jnp.asarray(rsel_np, F32)
    bones = jnp.asarray(bones_np, BF16)
    avg = jnp.asarray(bones_np / HG_DK, BF16)
    ltri = jnp.asarray(np.tril(np.ones((CHUNK, CHUNK), np.float32)), BF16)
    intra, inter_t, to_state_t, carry_row, bmask = _ret_tables()
    col_perm = jnp.asarray(_in_col_perm())

    out = None
    for l in range(depth):
        w_in_l = jnp.take(w_in[l], col_perm, axis=1).astype(BF16)
        proj = _in_proj(h, norm_mix_g[l].astype(F32).reshape(1, d), w_in_l, total)

        s5c = _s5_tables(s5_lam_re[l], s5_lam_im[l], s5_b_re[l], s5_b_im[l], s5_c_re[l], s5_c_im[l],
                         s5_d[l], s5_log_step[l])
        lb = lb_all[l][None, :]
        lbv = jnp.concatenate([jnp.log(lb), jnp.log1p(-lb), 1.0 - lb], axis=0)
        consts = list(s5c) + [
            s5_w_glu[l].astype(BF16), s5_out_g[l].astype(F32).reshape(1, -1), ltri,
            hg_conv_w[l].astype(F32), lbv, mall, lvl, rsel, bones, avg, hg_out_g[l].astype(F32).reshape(1, -1),
            intra, inter_t, to_state_t, carry_row, bmask, ret_out_g[l].astype(F32).reshape(1, -1),
        ]
        mixed = _mixers(proj.reshape(bsz, total, IN_COLS), cos_t, sin_t, consts, bsz, n_chunks)
        mixed = mixed.reshape(n, D_MIX)

        g_ffn = norm_ffn_g[l].astype(F32).reshape(1, d)
        w_out_l = w_out[l].astype(BF16)
        if l % 2 == 0:
            h1, hn = _out_proj(mixed, h, w_out_l, g_ffn)
            j = l // 2
            h = _ffn(hn, h1, ffn_w1[j].astype(BF16), ffn_w3[j].astype(BF16), ffn_w2[j].astype(BF16))
            y0 = y1 = None
        else:
            j = l // 2
            router_pad = jnp.zeros((d, ROUTE_LANES), F32).at[:, :N_EXPERTS].set(moe_router[j].astype(F32))
            h1, hn, route = _out_proj(mixed, h, w_out_l, g_ffn, router_pad)
            tm = 512
            tok_of_pos, gate_of_pos, pos_of_slot, tile_expert, n_valid, _ = _moe_dispatch(route, tm)
            xs = jnp.take(hn, tok_of_pos, axis=0)
            ys = _moe_grouped(tile_expert, n_valid, xs, gate_of_pos[:, None],
                              moe_w1[j].astype(BF16), moe_w3[j].astype(BF16), moe_w2[j].astype(BF16), tm)
            y0 = jnp.take(ys, pos_of_slot[:, 0], axis=0)
            y1 = jnp.take(ys, pos_of_slot[:, 1], axis=0)
            h = h1 if l == depth - 1 else h1 + (y0.astype(F32) + y1.astype(F32))

        if l == depth - 1:
            if y0 is None:
                y0 = y1 = jnp.zeros((n, d), BF16)
                h1 = h
            out = _final(h1, y0, y1, final_norm_g.astype(F32).reshape(1, d))

    return out.reshape(bsz, total, d)[:, CHUNK:, :].astype(x.dtype)
```

```python
import functools
import math

import jax
import jax.numpy as jnp
import numpy as np
from jax import lax
from jax.experimental import pallas as pl
from jax.experimental.pallas import tpu as pltpu

F32 = jnp.float32
BF16 = jnp.bfloat16

CHUNK = 128
N_META = 16
PAD = CHUNK - N_META
EPS = 1e-6

S5_WIDTH = 256
S5_GROUP = 16
S5_NGROUPS = 16
S5_STATE = 64
S5_NSTATE = S5_NGROUPS * S5_STATE

HG_HEADS = 4
HG_DK = 64
HG_WIDTH = 256
CONV_K = 4
HG_CONV_W = 3 * HG_WIDTH
HG_LEVELS = 7

RET_HEADS = 8
RET_DK = 32
RET_DV = 64
RET_KEY_WIDTH = 256
RET_WIDTH = 512
ROPE_BASE = 10000.0

D_MIX = 1024
IN_COLS = 2816
C_U, C_HQ, C_HGATE, C_RQ, C_RK, C_RV, C_RGATE = 0, 256, 1024, 1280, 1536, 1792, 2304

N_EXPERTS = 8
TOP_K = 2
ROUTE_LANES = 128
MOE_F_STEPS = 2

VMEM_LIMIT = 56 * 1024 * 1024


def _sigmoid(x):
    return 1.0 / (1.0 + jnp.exp(-x))


def _split_bf16(x):
    hi = x.astype(BF16)
    lo = (x - hi.astype(F32)).astype(BF16)
    return hi, lo


def _pick_tile(n, candidates):
    for t in candidates:
        if n % t == 0:
            return t
    raise ValueError(f"no tile in {candidates} divides {n}")


def _dot(a, b):
    return jnp.dot(a, b, preferred_element_type=F32)


def _dot_nt(a, b):
    return lax.dot_general(a, b, (((1,), (1,)), ((), ())), preferred_element_type=F32)


def _dot_tn(a, b):
    return lax.dot_general(a, b, (((0,), (0,)), ((), ())), preferred_element_type=F32)


def _in_proj_kernel(h_ref, g_ref, w_ref, o_ref, *, tm, tiles_per_batch):
    x = h_ref[...]
    y = x * lax.rsqrt(jnp.mean(x * x, axis=-1, keepdims=True) + EPS) * g_ref[...]
    proj = _dot(y.astype(BF16), w_ref[...])
    row0 = (pl.program_id(0) % tiles_per_batch) * tm
    rows = row0 + lax.broadcasted_iota(jnp.int32, (tm, 1), 0)
    o_ref[...] = jnp.where(rows >= PAD, proj, 0.0)


def _in_proj(h, g, w_bf16, total):
    n = h.shape[0]
    tm = _pick_tile(total, (320, 128))
    return pl.pallas_call(
        functools.partial(_in_proj_kernel, tm=tm, tiles_per_batch=total // tm),
        grid=(n // tm,),
        in_specs=[
            pl.BlockSpec((tm, h.shape[1]), lambda i: (i, 0)),
            pl.BlockSpec((1, h.shape[1]), lambda i: (0, 0)),
            pl.BlockSpec(w_bf16.shape, lambda i: (0, 0)),
        ],
        out_specs=pl.BlockSpec((tm, IN_COLS), lambda i: (i, 0)),
        out_shape=jax.ShapeDtypeStruct((n, IN_COLS), F32),
        compiler_params=pltpu.CompilerParams(
            dimension_semantics=("arbitrary",), vmem_limit_bytes=VMEM_LIMIT),
        name="in_proj",
    )(h, g, w_bf16)


def _s5_chunk(u, wb_ref, wc_ref, pn_re_ref, pn_im_ref, pp_re_ref, pp_im_ref, lam_ref,
              d_ref, wglu_ref, g_ref, ltri_ref, st_ref):
    ns = S5_NSTATE
    bu = _dot(u.astype(BF16), wb_ref[...])
    bu_re, bu_im = bu[:, :ns], bu[:, ns:]
    pn_re, pn_im = pn_re_ref[...], pn_im_ref[...]
    w_re = (pn_re * bu_re - pn_im * bu_im).astype(BF16)
    w_im = (pn_re * bu_im + pn_im * bu_re).astype(BF16)
    ltri = ltri_ref[...]
    lam_re, lam_im = lam_ref[0:1, :], lam_ref[1:2, :]
    s_re, s_im = st_ref[0:1, :], st_ref[1:2, :]
    z_re = _dot(ltri, w_re) + (lam_re * s_re - lam_im * s_im)
    z_im = _dot(ltri, w_im) + (lam_re * s_im + lam_im * s_re)
    pp_re, pp_im = pp_re_ref[...], pp_im_ref[...]
    st_re = pp_re * z_re - pp_im * z_im
    st_im = pp_re * z_im + pp_im * z_re
    st_ref[0:1, :] = st_re[CHUNK - 1:CHUNK, :]
    st_ref[1:2, :] = st_im[CHUNK - 1:CHUNK, :]
    y = (_dot(st_re.astype(BF16), wc_ref[0:ns, :]) + _dot(st_im.astype(BF16), wc_ref[ns:2 * ns, :])
         + d_ref[...] * u)
    y = 0.5 * y * (1.0 + jnp.tanh(math.sqrt(2.0 / math.pi) * (y + 0.044715 * (y * y * y))))
    y = y * _sigmoid(_dot(y.astype(BF16), wglu_ref[...]))
    return y * lax.rsqrt(jnp.mean(y * y, axis=-1, keepdims=True) + EPS) * g_ref[...]


def _hgrn2_chunk(xc_ref, gate, convw_ref, lbv_ref, mall_ref, lvl_ref, rsel_ref, bones_ref, avg_ref,
                 g_ref, st_ref):
    w = HG_WIDTH
    conv = None
    for i in range(CONV_K):
        term = xc_ref[pl.ds(8 - (CONV_K - 1) + i, CHUNK), :] * convw_ref[i:i + 1, :]
        conv = term if conv is None else conv + term
    cq, cf, v = conv[:, :w], conv[:, w:2 * w], conv[:, 2 * w:]
    q = cq * _sigmoid(cq)
    log_lb, log_1m_lb, one_m_lb = lbv_ref[0:1, :], lbv_ref[1:2, :], lbv_ref[2:3, :]
    log_sig = jnp.minimum(cf, 0.0) - jnp.log(1.0 + jnp.exp(-jnp.abs(cf)))
    b = log_1m_lb + log_sig
    logf = jnp.maximum(log_lb, b) + jnp.log(1.0 + jnp.exp(-jnp.abs(log_lb - b)))
    kk = one_m_lb * _sigmoid(-cf)
    hi, lo = _split_bf16(logf)
    hl = jnp.concatenate([hi, lo], axis=1)

    def interval_sums(blk):
        d = _dot(mall_ref[blk * CHUNK:(blk + 1) * CHUNK, :], hl)
        return d[:, :w] + d[:, w:]

    lane = lax.broadcasted_iota(jnp.int32, (1, 2 * HG_DK), 1)
    head_sel = [jnp.where(lane < HG_DK, 1.0, 0.0), jnp.where(lane >= HG_DK, 1.0, 0.0)]
    n_pairs = HG_HEADS // 2
    scores = [None] * n_pairs
    for lvl in range(HG_LEVELS):
        s = 1 << lvl
        e = jnp.exp(interval_sums(lvl))
        if s >= 8:
            qk = jnp.concatenate([(q if (r // s) % 2 else kk)[r:r + s] for r in range(0, CHUNK, s)], axis=0)
        else:
            qk = jnp.where(rsel_ref[lvl] > 0.5, q, kk)
        x = qk * e
        m = lvl_ref[lvl]
        for p in range(n_pairs):
            xp = x[:, p * 128:(p + 1) * 128]
            lhs = jnp.concatenate([xp * head_sel[0], xp * head_sel[1]], axis=0).astype(BF16)
            sc = _dot_nt(lhs, xp.astype(BF16)) * m
            scores[p] = sc if scores[p] is None else scores[p] + sc
    v_bf = v.astype(BF16)
    o_parts = []
    for p in range(n_pairs):
        sc = scores[p].astype(BF16)
        sc = jnp.concatenate([sc[:CHUNK], sc[CHUNK:]], axis=1)
        vp = v[:, p * 128:(p + 1) * 128]
        vv = jnp.concatenate([vp * head_sel[0], vp * head_sel[1]], axis=0).astype(BF16)
        o_parts.append(_dot(sc, vv))
    o = jnp.concatenate(o_parts, axis=1)
    bones = bones_ref[...]
    o = o + _dot((q * kk).astype(BF16), bones) * v
    g_cum = interval_sums(HG_LEVELS)
    g_suffix = interval_sums(HG_LEVELS + 1)
    st = st_ref[...]
    o = o + _dot_nt((q * jnp.exp(g_cum)).astype(BF16), st.astype(BF16))
    kd = (kk * jnp.exp(g_suffix)).astype(BF16)
    upd = _dot_tn(v_bf, kd) * bones.astype(F32)
    st_ref[...] = st * jnp.exp(g_cum[CHUNK - 1:CHUNK, :]) + upd
    ms = _dot((o * o).astype(BF16), avg_ref[...])
    return o * lax.rsqrt(ms + EPS) * g_ref[...] * (gate * _sigmoid(gate))


def _ret_chunk(rq, rk, v, gate, cos, sin, intra_ref, inter_ref, tostate_ref, carry_ref,
               bmask_ref, avg_ref, g_ref, st_ref):
    hw = RET_KEY_WIDTH // 2

    def rot(t):
        t1, t2 = t[:, :hw], t[:, hw:]
        return jnp.concatenate([t1 * cos - t2 * sin, t1 * sin + t2 * cos], axis=1)

    qr, kr = rot(rq), rot(rk)
    qr_bf, kr_bf, v_bf = qr.astype(BF16), kr.astype(BF16), v.astype(BF16)
    lane_k = lax.broadcasted_iota(jnp.int32, (1, RET_KEY_WIDTH), 1) % hw
    lane_v = lax.broadcasted_iota(jnp.int32, (1, 2 * RET_DV), 1)
    v_sel = [jnp.where(lane_v < RET_DV, 1.0, 0.0), jnp.where(lane_v >= RET_DV, 1.0, 0.0)]
    half = RET_DK // 2
    o_parts = []
    for p in range(RET_HEADS // 2):
        sel = [jnp.where((lane_k >= h * half) & (lane_k < (h + 1) * half), 1.0, 0.0)
               for h in (2 * p, 2 * p + 1)]
        lhs = jnp.concatenate([qr * sel[0], qr * sel[1]], axis=0).astype(BF16)
        sc = (_dot_nt(lhs, kr_bf) * intra_ref[p]).astype(BF16)
        sc = jnp.concatenate([sc[:CHUNK], sc[CHUNK:]], axis=1)
        vp = v[:, p * 128:(p + 1) * 128]
        vv = jnp.concatenate([vp * v_sel[0], vp * v_sel[1]], axis=0).astype(BF16)
        o_parts.append(_dot(sc, vv))
    o = jnp.concatenate(o_parts, axis=1)
    st = st_ref[...]
    o = o + _dot(qr_bf, st.astype(BF16)) * inter_ref[...]
    kd = (kr * tostate_ref[...]).astype(BF16)
    st_ref[...] = st * carry_ref[...] + _dot_tn(kd, v_bf) * bmask_ref[...]
    avg = avg_ref[...]
    outs = []
    for s in range(RET_WIDTH // 256):
        os_ = o[:, s * 256:(s + 1) * 256]
        c = os_ - _dot(os_.astype(BF16), avg)
        outs.append(c * lax.rsqrt(_dot((c * c).astype(BF16), avg) + EPS))
    return jnp.concatenate(outs, axis=1) * g_ref[...] * (gate * _sigmoid(gate))


def _mixer_kernel(proj_ref, cos_ref, sin_ref,
                  wb_ref, wc_ref, pn_re_ref, pn_im_ref, pp_re_ref, pp_im_ref, lam_ref, d_ref,
                  wglu_ref, s5g_ref, ltri_ref,
                  convw_ref, lbv_ref, mall_ref, lvl_ref, rsel_ref, bones_ref, avg_ref, hgg_ref,
                  intra_ref, inter_ref, tostate_ref, carry_ref, bmask_ref, retg_ref,
                  o_ref,
                  s5_st, hg_xc, hg_st, ret_st):
    @pl.when(pl.program_id(0) == 0)
    def _():
        s5_st[...] = jnp.zeros_like(s5_st)
        hg_xc[...] = jnp.zeros_like(hg_xc)
        hg_st[...] = jnp.zeros_like(hg_st)
        ret_st[...] = jnp.zeros_like(ret_st)

    for b in range(proj_ref.shape[0]):
        pr, out, xc = proj_ref.at[b], o_ref.at[b], hg_xc.at[b]
        u = pr[:, C_U:C_U + S5_WIDTH]
        y_a = _s5_chunk(u, wb_ref, wc_ref, pn_re_ref, pn_im_ref, pp_re_ref, pp_im_ref, lam_ref,
                        d_ref, wglu_ref, s5g_ref, ltri_ref, s5_st.at[b])
        out[:, 0:S5_WIDTH] = y_a.astype(out.dtype)

        xc[8:8 + CHUNK, :] = pr[:, C_HQ:C_HQ + HG_CONV_W]
        y_b = _hgrn2_chunk(xc, pr[:, C_HGATE:C_HGATE + HG_WIDTH], convw_ref, lbv_ref,
                           mall_ref, lvl_ref, rsel_ref, bones_ref, avg_ref, hgg_ref, hg_st.at[b])
        xc[0:8, :] = xc[CHUNK:CHUNK + 8, :]
        out[:, S5_WIDTH:S5_WIDTH + HG_WIDTH] = y_b.astype(out.dtype)

        y_c = _ret_chunk(pr[:, C_RQ:C_RQ + RET_KEY_WIDTH], pr[:, C_RK:C_RK + RET_KEY_WIDTH],
                         pr[:, C_RV:C_RV + RET_WIDTH], pr[:, C_RGATE:C_RGATE + RET_WIDTH],
                         cos_ref[...], sin_ref[...], intra_ref, inter_ref, tostate_ref, carry_ref,
                         bmask_ref, avg_ref, retg_ref, ret_st.at[b])
        out[:, S5_WIDTH + HG_WIDTH:] = y_c.astype(out.dtype)


def _const_spec(a):
    nd = a.ndim
    return pl.BlockSpec(a.shape, lambda c, _nd=nd: (0,) * _nd)


def _mixers(proj, cos, sin, consts, bsz, n_chunks):
    total = proj.shape[1]
    in_specs = [
        pl.BlockSpec((bsz, CHUNK, IN_COLS), lambda c: (0, c, 0)),
        pl.BlockSpec((CHUNK, 128), lambda c: (c, 0)),
        pl.BlockSpec((CHUNK, 128), lambda c: (c, 0)),
    ] + [_const_spec(a) for a in consts]
    return pl.pallas_call(
        _mixer_kernel,
        grid=(n_chunks,),
        in_specs=in_specs,
        out_specs=pl.BlockSpec((bsz, CHUNK, D_MIX), lambda c: (0, c, 0)),
        out_shape=jax.ShapeDtypeStruct((bsz, total, D_MIX), BF16),
        scratch_shapes=[
            pltpu.VMEM((bsz, 2, S5_NSTATE), F32),
            pltpu.VMEM((bsz, CHUNK + 8, HG_CONV_W), F32),
            pltpu.VMEM((bsz, HG_WIDTH, HG_WIDTH), F32),
            pltpu.VMEM((bsz, RET_KEY_WIDTH, RET_WIDTH), F32),
        ],
        compiler_params=pltpu.CompilerParams(
            dimension_semantics=("arbitrary",), vmem_limit_bytes=VMEM_LIMIT),
        name="mixers",
    )(proj, cos, sin, *consts)


def _route(hn, router_ref, ltri_ref, count_ref):
    r = router_ref[...]
    r_hi, r_lo = _split_bf16(r)
    h_hi, h_lo = _split_bf16(hn)
    logits = _dot(h_hi, r_hi) + (_dot(h_lo, r_hi) + _dot(h_hi, r_lo))
    lane_i = lax.broadcasted_iota(jnp.int32, logits.shape, 1)
    lane = lane_i.astype(F32)
    neg = jnp.float32(-jnp.inf)
    logits = jnp.where(lane_i < N_EXPERTS, logits, neg)
    m1 = jnp.max(logits, axis=-1, keepdims=True)
    i1 = jnp.min(jnp.where(logits == m1, lane, float(ROUTE_LANES)), axis=-1, keepdims=True)
    rest = jnp.where(lane == i1, neg, logits)
    m2 = jnp.max(rest, axis=-1, keepdims=True)
    i2 = jnp.min(jnp.where(rest == m2, lane, float(ROUTE_LANES)), axis=-1, keepdims=True)
    e2 = jnp.exp(m2 - m1)
    g1 = 1.0 / (1.0 + e2)
    g2 = e2 / (1.0 + e2)
    onehot = jnp.where((lane == i1) | (lane == i2), 1.0, 0.0)
    before = _dot(ltri_ref[...], onehot.astype(BF16)) + count_ref[...]
    r1 = jnp.sum(jnp.where(lane == i1, before, 0.0), axis=-1, keepdims=True)
    r2 = jnp.sum(jnp.where(lane == i2, before, 0.0), axis=-1, keepdims=True)
    count_ref[...] += jnp.sum(onehot, axis=0, keepdims=True)
    out = jnp.where(lane == 0, i1, 0.0)
    out = jnp.where(lane == 1, i2, out)
    out = jnp.where(lane == 2, g1, out)
    out = jnp.where(lane == 3, g2, out)
    out = jnp.where(lane == 4, r1, out)
    return jnp.where(lane == 5, r2, out)


def _out_proj_kernel(mixed_ref, h_ref, w_ref, g_ref, *rest, routed):
    if routed:
        router_ref, ltri_ref, h1_ref, hn_ref, route_ref, count_ref = rest

        @pl.when(pl.program_id(0) == 0)
        def _():
            count_ref[...] = jnp.zeros_like(count_ref)
    else:
        h1_ref, hn_ref = rest
    h1 = h_ref[...] + _dot(mixed_ref[...], w_ref[...])
    h1_ref[...] = h1
    hn = h1 * lax.rsqrt(jnp.mean(h1 * h1, axis=-1, keepdims=True) + EPS) * g_ref[...]
    hn_ref[...] = hn.astype(hn_ref.dtype)
    if routed:
        route_ref[...] = _route(hn, router_ref, ltri_ref, count_ref)


def _out_proj(mixed, h, w_bf16, g, router_pad=None):
    n, d = h.shape
    tm = _pick_tile(n, (640, 128))
    routed = router_pad is not None
    row = lambda i: (i, 0)
    fixed = lambda i: (0, 0)
    in_specs = [pl.BlockSpec((tm, D_MIX), row), pl.BlockSpec((tm, d), row),
                pl.BlockSpec(w_bf16.shape, fixed), pl.BlockSpec((1, d), fixed)]
    out_specs = [pl.BlockSpec((tm, d), row), pl.BlockSpec((tm, d), row)]
    out_shape = [jax.ShapeDtypeStruct((n, d), F32), jax.ShapeDtypeStruct((n, d), F32 if routed else BF16)]
    args = [mixed, h, w_bf16, g]
    if routed:
        ltri = jnp.asarray(np.tril(np.ones((tm, tm), np.float32), -1), BF16)
        in_specs += [pl.BlockSpec(router_pad.shape, fixed), pl.BlockSpec((tm, tm), fixed)]
        out_specs += [pl.BlockSpec((tm, ROUTE_LANES), row), pl.BlockSpec((1, ROUTE_LANES), fixed)]
        out_shape += [jax.ShapeDtypeStruct((n, ROUTE_LANES), F32),
                      jax.ShapeDtypeStruct((1, ROUTE_LANES), F32)]
        args += [router_pad, ltri]
    return pl.pallas_call(
        functools.partial(_out_proj_kernel, routed=routed),
        grid=(n // tm,),
        in_specs=in_specs,
        out_specs=out_specs,
        out_shape=out_shape,
        compiler_params=pltpu.CompilerParams(
            dimension_semantics=("arbitrary",), vmem_limit_bytes=VMEM_LIMIT),
        name="out_proj_routed" if routed else "out_proj",
    )(*args)


def _ffn_kernel(hn_ref, h1_ref, w1_ref, w3_ref, w2_ref, o_ref, acc_ref):
    f = pl.program_id(1)
    x = hn_ref[...]
    a = _dot(x, w1_ref[...])
    b = _dot(x, w3_ref[...])
    part = _dot((a * _sigmoid(a) * b).astype(BF16), w2_ref[...])

    @pl.when(f == 0)
    def _():
        acc_ref[...] = part

    @pl.when(f > 0)
    def _():
        acc_ref[...] += part

    @pl.when(f == pl.num_programs(1) - 1)
    def _():
        o_ref[...] = h1_ref[...] + acc_ref[...]


def _ffn(hn, h1, w1, w3, w2):
    n, d = h1.shape
    dff = w1.shape[1]
    tm, tf = _pick_tile(n, (640, 128)), _pick_tile(dff, (1408, 128))
    return pl.pallas_call(
        _ffn_kernel,
        grid=(n // tm, dff // tf),
        in_specs=[
            pl.BlockSpec((tm, d), lambda i, f: (i, 0)),
            pl.BlockSpec((tm, d), lambda i, f: (i, 0)),
            pl.BlockSpec((d, tf), lambda i, f: (0, f)),
            pl.BlockSpec((d, tf), lambda i, f: (0, f)),
            pl.BlockSpec((tf, d), lambda i, f: (f, 0)),
        ],
        out_specs=pl.BlockSpec((tm, d), lambda i, f: (i, 0)),
        out_shape=jax.ShapeDtypeStruct((n, d), F32),
        scratch_shapes=[pltpu.VMEM((tm, d), F32)],
        compiler_params=pltpu.CompilerParams(
            dimension_semantics=("arbitrary", "arbitrary"), vmem_limit_bytes=VMEM_LIMIT),
        name="ffn_dense",
    )(hn, h1, w1, w3, w2)


def _moe_kernel(te_ref, nv_ref, tok_ref, hn_hbm, w1_ref, w3_ref, w2_ref, o_ref,
                xs_ref, xb_ref, acc_ref, sem, *, tm):
    i, f = pl.program_id(0), pl.program_id(1)
    nf = pl.num_programs(1)
    n_valid = nv_ref[0]
    share = tm // MOE_F_STEPS

    def row_copy(tile, r):
        return pltpu.make_async_copy(hn_hbm.at[pl.ds(tok_ref[tile * tm + r], 1)],
                                     xs_ref.at[pl.ds(r, 1)], sem)

    @pl.when((i == 0) & (f == 0))
    def _():
        for r in range(tm):
            row_copy(0, r).start()

    @pl.when((f == 0) & (i <= n_valid))
    def _():
        pltpu.make_async_copy(xs_ref, xs_ref, sem).wait()
        xb_ref[...] = xs_ref[...].astype(BF16)

    @pl.when(i < n_valid)
    def _():
        for r in range(share):
            row_copy(i + 1, f * share + r).start()
        x = xb_ref[...]
        a = _dot(x, w1_ref[0])
        b = _dot(x, w3_ref[0])
        part = _dot((a * _sigmoid(a) * b).astype(BF16), w2_ref[0])

        @pl.when(f == 0)
        def _():
            acc_ref[...] = part

        @pl.when(f > 0)
        def _():
            acc_ref[...] += part

    @pl.when(f == nf - 1)
    def _():
        o_ref[...] = jnp.where(i < n_valid, acc_ref[...], 0.0).astype(o_ref.dtype)


def _live_f(i, f, nv):
    return jnp.where(i < nv[0], f, 0)


def _moe_grouped(tile_expert, n_valid, tok_of_pos, hn, w1, w3, w2, tm):
    d = hn.shape[1]
    p = tok_of_pos.shape[0]
    dff = w1.shape[2]
    tf = dff // MOE_F_STEPS
    grid_spec = pltpu.PrefetchScalarGridSpec(
        num_scalar_prefetch=3,
        grid=(p // tm, MOE_F_STEPS),
        in_specs=[
            pl.BlockSpec(memory_space=pl.ANY),
            pl.BlockSpec((1, d, tf), lambda i, f, te, nv, tok: (te[i], 0, _live_f(i, f, nv))),
            pl.BlockSpec((1, d, tf), lambda i, f, te, nv, tok: (te[i], 0, _live_f(i, f, nv))),
            pl.BlockSpec((1, tf, d), lambda i, f, te, nv, tok: (te[i], _live_f(i, f, nv), 0)),
        ],
        out_specs=pl.BlockSpec((tm, d), lambda i, f, te, nv, tok: (i, 0)),
        scratch_shapes=[pltpu.VMEM((tm, d), F32), pltpu.VMEM((tm, d), BF16), pltpu.VMEM((tm, d), F32),
                        pltpu.SemaphoreType.DMA(())],
    )
    return pl.pallas_call(
        functools.partial(_moe_kernel, tm=tm),
        grid_spec=grid_spec,
        out_shape=jax.ShapeDtypeStruct((p, d), BF16),
        compiler_params=pltpu.CompilerParams(
            dimension_semantics=("arbitrary", "arbitrary"), vmem_limit_bytes=VMEM_LIMIT),
        name="moe_grouped",
    )(tile_expert, n_valid, tok_of_pos, hn, w1, w3, w2)


def _moe_dispatch(route, counts_row, tm):
    n = route.shape[0]
    n_slots = n * TOP_K
    p_rows = ((n_slots + N_EXPERTS * tm + tm - 1) // tm) * tm
    counts = counts_row[0, :N_EXPERTS].astype(jnp.int32)
    padded = ((counts + tm - 1) // tm) * tm
    pend = jnp.cumsum(padded)
    pstart = pend - padded
    eid = route[:, 0:TOP_K].astype(jnp.int32)
    rank = route[:, 2 * TOP_K:3 * TOP_K].astype(jnp.int32)
    pos = pstart[eid] + rank
    tok = jnp.broadcast_to(jnp.arange(n, dtype=jnp.int32)[:, None], (n, TOP_K))
    tok_of_pos = jnp.zeros((p_rows,), jnp.int32).at[pos.reshape(-1)].set(
        tok.reshape(-1), unique_indices=True, mode='promise_in_bounds')
    tile_start = jnp.arange(p_rows // tm, dtype=jnp.int32) * tm
    tile_expert = jnp.minimum(jnp.searchsorted(pend, tile_start, side='right'),
                              N_EXPERTS - 1).astype(jnp.int32)
    n_valid = (pend[-1] // tm).astype(jnp.int32).reshape(1)
    return tok_of_pos, pos, tile_expert, n_valid


def _final_kernel(h1_ref, y0_ref, y1_ref, route_ref, g_ref, o_ref):
    g0 = route_ref[:, TOP_K:TOP_K + 1]
    g1 = route_ref[:, TOP_K + 1:TOP_K + 2]
    h = h1_ref[...] + (g0 * y0_ref[...].astype(F32) + g1 * y1_ref[...].astype(F32))
    o_ref[...] = h * lax.rsqrt(jnp.mean(h * h, axis=-1, keepdims=True) + EPS) * g_ref[...]


def _final(h1, y0, y1, route, g):
    n, d = h1.shape
    tm = _pick_tile(n, (640, 128))
    row = lambda i: (i, 0)
    return pl.pallas_call(
        _final_kernel,
        grid=(n // tm,),
        in_specs=[pl.BlockSpec((tm, d), row), pl.BlockSpec((tm, d), row), pl.BlockSpec((tm, d), row),
                  pl.BlockSpec((tm, ROUTE_LANES), row), pl.BlockSpec((1, d), lambda i: (0, 0))],
        out_specs=pl.BlockSpec((tm, d), row),
        out_shape=jax.ShapeDtypeStruct((n, d), F32),
        compiler_params=pltpu.CompilerParams(
            dimension_semantics=("arbitrary",), vmem_limit_bytes=VMEM_LIMIT),
        name="final_norm",
    )(h1, y0, y1, route, g)


def _rope_perm():
    half = RET_DK // 2
    idx = []
    for part in range(2):
        for h in range(RET_HEADS):
            for dd in range(half):
                idx.append(h * RET_DK + part * half + dd)
    return np.asarray(idx, np.int32)


def _in_col_perm():
    cols = np.arange(IN_COLS, dtype=np.int32)
    rp = _rope_perm()
    cols[C_RQ:C_RQ + RET_KEY_WIDTH] = C_RQ + rp
    cols[C_RK:C_RK + RET_KEY_WIDTH] = C_RK + rp
    return cols


def _hg_tables():
    c = CHUNK
    i = np.arange(c)[:, None]
    t = np.arange(c)[None, :]
    blocks, masks, rsel = [], [], []
    for lvl in range(HG_LEVELS):
        s = 1 << lvl
        r = (i // (2 * s)) * (2 * s) + s - 1
        right = i > r
        m = np.where(right, (t > r) & (t <= i), (t > i) & (t <= r))
        blocks.append(m)
        j = t
        same = (i // (2 * s)) == (j // (2 * s))
        mk = same & right & (j <= r)
        masks.append(np.concatenate([mk, mk], axis=0))
        if s < 8:
            rsel.append(np.broadcast_to(right, (c, HG_WIDTH)))
    blocks.append(t <= i)
    blocks.append(t > i)
    mall = np.concatenate(blocks, axis=0).astype(np.float32)
    lvl = np.stack(masks, axis=0).astype(np.float32)
    ch = np.arange(HG_WIDTH)
    bones = (ch[:, None] // HG_DK == ch[None, :] // HG_DK).astype(np.float32)
    return mall, lvl, np.stack(rsel, axis=0).astype(np.float32), bones


def _ret_tables():
    f32 = jnp.float32
    log_gamma = jnp.log1p(-jnp.power(2.0, -5.0 - jnp.arange(RET_HEADS, dtype=f32)))
    n = jnp.arange(CHUNK, dtype=f32)
    lg = log_gamma[:, None]
    causal = jnp.tril(jnp.ones((CHUNK, CHUNK), dtype=bool))
    intra = jnp.exp(jnp.where(causal[None], (n[:, None] - n[None, :])[None] * lg[:, :, None], -jnp.inf))
    scale = RET_DK ** -0.5
    inter = jnp.exp((n[None, :] + 1.0) * lg)
    to_state = jnp.exp((CHUNK - 1.0 - n[None, :]) * lg)
    carry = jnp.exp(CHUNK * lg)[:, 0]
    head_of_v = np.arange(RET_WIDTH) // RET_DV
    head_of_k = (np.arange(RET_KEY_WIDTH) % (RET_KEY_WIDTH // 2)) // (RET_DK // 2)
    inter_t = (inter * scale).T[:, head_of_v]
    to_state_t = to_state.T[:, head_of_k]
    carry_row = carry[head_of_v][None, :]
    bmask = jnp.asarray((head_of_k[:, None] == head_of_v[None, :]).astype(np.float32))
    intra_pairs = (intra * scale).reshape(RET_HEADS // 2, 2 * CHUNK, CHUNK)
    return intra_pairs, inter_t, to_state_t, carry_row, bmask


def _s5_tables(lam_re, lam_im, b_re, b_im, c_re, c_im, d_skip, log_step):
    f32 = jnp.float32
    lam = lax.complex(lam_re.astype(f32), lam_im.astype(f32))
    step = jnp.exp(log_step.astype(f32))[:, None]
    lam_dt = lam * step
    lam_bar = jnp.exp(lam_dt)
    b_bar = ((lam_bar - 1.0) / lam)[..., None] * lax.complex(b_re.astype(f32), b_im.astype(f32))
    eye = jnp.eye(S5_NGROUPS, dtype=f32)
    wb_re = jnp.einsum('gph,gk->ghkp', jnp.real(b_bar), eye).reshape(S5_WIDTH, S5_NSTATE)
    wb_im = jnp.einsum('gph,gk->ghkp', jnp.imag(b_bar), eye).reshape(S5_WIDTH, S5_NSTATE)
    wb = jnp.concatenate([wb_re, wb_im], axis=1)
    wc_re = jnp.einsum('ghp,gk->gpkh', c_re.astype(f32), eye).reshape(S5_NSTATE, S5_WIDTH)
    wc_im = jnp.einsum('ghp,gk->gpkh', c_im.astype(f32), eye).reshape(S5_NSTATE, S5_WIDTH)
    wc = jnp.concatenate([wc_re, -wc_im], axis=0)
    t = jnp.arange(CHUNK, dtype=f32)[:, None, None]
    pp = jnp.exp(lam_dt[None] * t).reshape(CHUNK, S5_NSTATE)
    pn = jnp.exp(-lam_dt[None] * t).reshape(CHUNK, S5_NSTATE)
    lam_rows = jnp.stack([jnp.real(lam_bar).reshape(-1), jnp.imag(lam_bar).reshape(-1)], axis=0)
    return (wb.astype(BF16), wc.astype(BF16), jnp.real(pn), jnp.imag(pn), jnp.real(pp), jnp.imag(pp),
            lam_rows, d_skip.astype(f32).reshape(1, S5_WIDTH))


def kernel(x, meta_tokens, norm_mix_g, w_in, s5_lam_re, s5_lam_im, s5_b_re, s5_b_im, s5_c_re, s5_c_im, s5_d, s5_log_step, s5_w_glu, s5_out_g, hg_conv_w, hg_lb_param, hg_out_g, ret_out_g, w_out, norm_ffn_g, ffn_w1, ffn_w3, ffn_w2, moe_router, moe_w1, moe_w3, moe_w2, final_norm_g):
    bsz, seq_len, d = x.shape
    depth = w_in.shape[0]
    total = seq_len + CHUNK
    n_chunks = total // CHUNK
    n = bsz * total

    meta = jnp.broadcast_to(meta_tokens.astype(F32)[None], (bsz, N_META, d))
    h = jnp.concatenate([jnp.zeros((bsz, PAD, d), F32), meta, x.astype(F32)], axis=1).reshape(n, d)

    pos = (jnp.arange(total) - PAD).astype(F32)
    half = RET_DK // 2
    inv_freq = ROPE_BASE ** (-jnp.arange(half, dtype=F32) / half)
    ang = pos[:, None] * inv_freq[None, :]
    cos_t = jnp.tile(jnp.cos(ang), (1, RET_HEADS))
    sin_t = jnp.tile(jnp.sin(ang), (1, RET_HEADS))

    lb_all = jnp.cumsum(jax.nn.softmax(hg_lb_param.astype(F32), axis=0), axis=0)
    lb_all = lb_all - lb_all[0]

    mall_np, lvl_np, rsel_np, bones_np = _hg_tables()
    mall = jnp.asarray(mall_np, BF16)
    lvl = jnp.asarray(lvl_np, F32)
    rsel = jnp.asarray(rsel_np, F32)
    bones = jnp.asarray(bones_np, BF16)
    avg = jnp.asarray(bones_np / HG_DK, BF16)
    ltri = jnp.asarray(np.tril(np.ones((CHUNK, CHUNK), np.float32)), BF16)
    intra, inter_t, to_state_t, carry_row, bmask = _ret_tables()
    col_perm = jnp.asarray(_in_col_perm())

    out = None
    for l in range(depth):
        w_in_l = jnp.take(w_in[l], col_perm, axis=1).astype(BF16)
        proj = _in_proj(h, norm_mix_g[l].astype(F32).reshape(1, d), w_in_l, total)

        s5c = _s5_tables(s5_lam_re[l], s5_lam_im[l], s5_b_re[l], s5_b_im[l], s5_c_re[l], s5_c_im[l],
                         s5_d[l], s5_log_step[l])
        lb = lb_all[l][None, :]
        lbv = jnp.concatenate([jnp.log(lb), jnp.log1p(-lb), 1.0 - lb], axis=0)
        consts = list(s5c) + [
            s5_w_glu[l].astype(BF16), s5_out_g[l].astype(F32).reshape(1, -1), ltri,
            hg_conv_w[l].astype(F32), lbv, mall, lvl, rsel, bones, avg, hg_out_g[l].astype(F32).reshape(1, -1),
            intra, inter_t, to_state_t, carry_row, bmask, ret_out_g[l].astype(F32).reshape(1, -1),
        ]
        mixed = _mixers(proj.reshape(bsz, total, IN_COLS), cos_t, sin_t, consts, bsz, n_chunks)
        mixed = mixed.reshape(n, D_MIX)

        g_ffn = norm_ffn_g[l].astype(F32).reshape(1, d)
        w_out_l = w_out[l].astype(BF16)
        if l % 2 == 0:
            h1, hn = _out_proj(mixed, h, w_out_l, g_ffn)
            j = l // 2
            h = _ffn(hn, h1, ffn_w1[j].astype(BF16), ffn_w3[j].astype(BF16), ffn_w2[j].astype(BF16))
            y0 = y1 = None
        else:
            j = l // 2
            router_pad = jnp.zeros((d, ROUTE_LANES), F32).at[:, :N_EXPERTS].set(moe_router[j].astype(F32))
            h1, hn, route, counts_row = _out_proj(mixed, h, w_out_l, g_ffn, router_pad)
            tm = 512
            tok_of_pos, pos_of_slot, tile_expert, n_valid = _moe_dispatch(route, counts_row, tm)
            ys = _moe_grouped(tile_expert, n_valid, tok_of_pos, hn,
                              moe_w1[j].astype(BF16), moe_w3[j].astype(BF16), moe_w2[j].astype(BF16), tm)
            y0 = jnp.take(ys, pos_of_slot[:, 0], axis=0)
            y1 = jnp.take(ys, pos_of_slot[:, 1], axis=0)
            if l < depth - 1:
                h = h1 + (route[:, TOP_K:TOP_K + 1] * y0.astype(F32)
                          + route[:, TOP_K + 1:TOP_K + 2] * y1.astype(F32))

        if l == depth - 1:
            if y0 is None:
                y0 = y1 = jnp.zeros((n, d), BF16)
                h1 = h
                route = jnp.zeros((n, ROUTE_LANES), F32)
            out = _final(h1, y0, y1, route, final_norm_g.astype(F32).reshape(1, d))

    return out.reshape(bsz, total, d)[:, CHUNK:, :].astype(x.dtype)
```

```python
import functools
import math

import jax
import jax.numpy as jnp
import numpy as np
from jax import lax
from jax.experimental import pallas as pl
from jax.experimental.pallas import tpu as pltpu

F32 = jnp.float32
BF16 = jnp.bfloat16

CHUNK = 128
N_META = 16
PAD = CHUNK - N_META
EPS = 1e-6

S5_WIDTH = 256
S5_GROUP = 16
S5_NGROUPS = 16
S5_STATE = 64
S5_NSTATE = S5_NGROUPS * S5_STATE

HG_HEADS = 4
HG_DK = 64
HG_WIDTH = 256
CONV_K = 4
HG_CONV_W = 3 * HG_WIDTH
HG_LEVELS = 7

RET_HEADS = 8
RET_DK = 32
RET_DV = 64
RET_KEY_WIDTH = 256
RET_WIDTH = 512
ROPE_BASE = 10000.0

D_MIX = 1024
IN_COLS = 2816
C_U, C_HQ, C_HGATE, C_RQ, C_RK, C_RV, C_RGATE = 0, 256, 1024, 1280, 1536, 1792, 2304

N_EXPERTS = 8
TOP_K = 2
ROUTE_LANES = 128
MOE_F_STEPS = 4
MOE_TILE = 1024

VMEM_LIMIT = 56 * 1024 * 1024


def _sigmoid(x):
    return 1.0 / (1.0 + jnp.exp(-x))


def _split_bf16(x):
    hi = x.astype(BF16)
    lo = (x - hi.astype(F32)).astype(BF16)
    return hi, lo


def _pick_tile(n, candidates):
    for t in candidates:
        if n % t == 0:
            return t
    raise ValueError(f"no tile in {candidates} divides {n}")


def _dot(a, b):
    return jnp.dot(a, b, preferred_element_type=F32)


def _dot_nt(a, b):
    return lax.dot_general(a, b, (((1,), (1,)), ((), ())), preferred_element_type=F32)


def _dot_tn(a, b):
    return lax.dot_general(a, b, (((0,), (0,)), ((), ())), preferred_element_type=F32)


def _in_proj_kernel(h_ref, g_ref, w_ref, o_ref, *, tm, tiles_per_batch):
    x = h_ref[...]
    y = x * lax.rsqrt(jnp.mean(x * x, axis=-1, keepdims=True) + EPS) * g_ref[...]
    proj = _dot(y.astype(BF16), w_ref[...])
    row0 = (pl.program_id(0) % tiles_per_batch) * tm
    rows = row0 + lax.broadcasted_iota(jnp.int32, (tm, 1), 0)
    o_ref[...] = jnp.where(rows >= PAD, proj, 0.0)


def _in_proj(h, g, w_bf16, total):
    n = h.shape[0]
    tm = _pick_tile(total, (320, 128))
    return pl.pallas_call(
        functools.partial(_in_proj_kernel, tm=tm, tiles_per_batch=total // tm),
        grid=(n // tm,),
        in_specs=[
            pl.BlockSpec((tm, h.shape[1]), lambda i: (i, 0)),
            pl.BlockSpec((1, h.shape[1]), lambda i: (0, 0)),
            pl.BlockSpec(w_bf16.shape, lambda i: (0, 0)),
        ],
        out_specs=pl.BlockSpec((tm, IN_COLS), lambda i: (i, 0)),
        out_shape=jax.ShapeDtypeStruct((n, IN_COLS), F32),
        compiler_params=pltpu.CompilerParams(
            dimension_semantics=("arbitrary",), vmem_limit_bytes=VMEM_LIMIT),
        name="in_proj",
    )(h, g, w_bf16)


def _s5_chunk(u, wb_ref, wc_ref, pn_re_ref, pn_im_ref, pp_re_ref, pp_im_ref, lam_ref,
              d_ref, wglu_ref, g_ref, ltri_ref, st_ref):
    ns = S5_NSTATE
    bu = _dot(u.astype(BF16), wb_ref[...])
    bu_re, bu_im = bu[:, :ns], bu[:, ns:]
    pn_re, pn_im = pn_re_ref[...], pn_im_ref[...]
    w_re = (pn_re * bu_re - pn_im * bu_im).astype(BF16)
    w_im = (pn_re * bu_im + pn_im * bu_re).astype(BF16)
    ltri = ltri_ref[...]
    lam_re, lam_im = lam_ref[0:1, :], lam_ref[1:2, :]
    s_re, s_im = st_ref[0:1, :], st_ref[1:2, :]
    z_re = _dot(ltri, w_re) + (lam_re * s_re - lam_im * s_im)
    z_im = _dot(ltri, w_im) + (lam_re * s_im + lam_im * s_re)
    pp_re, pp_im = pp_re_ref[...], pp_im_ref[...]
    st_re = pp_re * z_re - pp_im * z_im
    st_im = pp_re * z_im + pp_im * z_re
    st_ref[0:1, :] = st_re[CHUNK - 1:CHUNK, :]
    st_ref[1:2, :] = st_im[CHUNK - 1:CHUNK, :]
    y = (_dot(st_re.astype(BF16), wc_ref[0:ns, :]) + _dot(st_im.astype(BF16), wc_ref[ns:2 * ns, :])
         + d_ref[...] * u)
    y = 0.5 * y * (1.0 + jnp.tanh(math.sqrt(2.0 / math.pi) * (y + 0.044715 * (y * y * y))))
    y = y * _sigmoid(_dot(y.astype(BF16), wglu_ref[...]))
    return y * lax.rsqrt(jnp.mean(y * y, axis=-1, keepdims=True) + EPS) * g_ref[...]


def _hgrn2_chunk(xc_ref, gate, convw_ref, lbv_ref, mall_ref, lvl_ref, rsel_ref, bones_ref, avg_ref,
                 g_ref, st_ref):
    w = HG_WIDTH
    conv = None
    for i in range(CONV_K):
        term = xc_ref[pl.ds(8 - (CONV_K - 1) + i, CHUNK), :] * convw_ref[i:i + 1, :]
        conv = term if conv is None else conv + term
    cq, cf, v = conv[:, :w], conv[:, w:2 * w], conv[:, 2 * w:]
    q = cq * _sigmoid(cq)
    log_lb, log_1m_lb, one_m_lb = lbv_ref[0:1, :], lbv_ref[1:2, :], lbv_ref[2:3, :]
    log_sig = jnp.minimum(cf, 0.0) - jnp.log(1.0 + jnp.exp(-jnp.abs(cf)))
    b = log_1m_lb + log_sig
    logf = jnp.maximum(log_lb, b) + jnp.log(1.0 + jnp.exp(-jnp.abs(log_lb - b)))
    kk = one_m_lb * _sigmoid(-cf)
    hi, lo = _split_bf16(logf)
    hl = jnp.concatenate([hi, lo], axis=1)

    def interval_sums(blk):
        d = _dot(mall_ref[blk * CHUNK:(blk + 1) * CHUNK, :], hl)
        return d[:, :w] + d[:, w:]

    lane = lax.broadcasted_iota(jnp.int32, (1, 2 * HG_DK), 1)
    head_sel = [jnp.where(lane < HG_DK, 1.0, 0.0), jnp.where(lane >= HG_DK, 1.0, 0.0)]
    n_pairs = HG_HEADS // 2
    scores = [None] * n_pairs
    for lvl in range(HG_LEVELS):
        s = 1 << lvl
        e = jnp.exp(interval_sums(lvl))
        if s >= 8:
            qk = jnp.concatenate([(q if (r // s) % 2 else kk)[r:r + s] for r in range(0, CHUNK, s)], axis=0)
        else:
            qk = jnp.where(rsel_ref[lvl] > 0.5, q, kk)
        x = qk * e
        m = lvl_ref[lvl]
        for p in range(n_pairs):
            xp = x[:, p * 128:(p + 1) * 128]
            lhs = jnp.concatenate([xp * head_sel[0], xp * head_sel[1]], axis=0).astype(BF16)
            sc = _dot_nt(lhs, xp.astype(BF16)) * m
            scores[p] = sc if scores[p] is None else scores[p] + sc
    v_bf = v.astype(BF16)
    o_parts = []
    for p in range(n_pairs):
        sc = scores[p].astype(BF16)
        sc = jnp.concatenate([sc[:CHUNK], sc[CHUNK:]], axis=1)
        vp = v[:, p * 128:(p + 1) * 128]
        vv = jnp.concatenate([vp * head_sel[0], vp * head_sel[1]], axis=0).astype(BF16)
        o_parts.append(_dot(sc, vv))
    o = jnp.concatenate(o_parts, axis=1)
    bones = bones_ref[...]
    o = o + _dot((q * kk).astype(BF16), bones) * v
    g_cum = interval_sums(HG_LEVELS)
    g_suffix = interval_sums(HG_LEVELS + 1)
    st = st_ref[...]
    o = o + _dot_nt((q * jnp.exp(g_cum)).astype(BF16), st.astype(BF16))
    kd = (kk * jnp.exp(g_suffix)).astype(BF16)
    upd = _dot_tn(v_bf, kd) * bones.astype(F32)
    st_ref[...] = st * jnp.exp(g_cum[CHUNK - 1:CHUNK, :]) + upd
    ms = _dot((o * o).astype(BF16), avg_ref[...])
    return o * lax.rsqrt(ms + EPS) * g_ref[...] * (gate * _sigmoid(gate))


def _ret_chunk(rq, rk, v, gate, cos, sin, intra_ref, inter_ref, tostate_ref, carry_ref,
               bmask_ref, avg_ref, g_ref, st_ref):
    hw = RET_KEY_WIDTH // 2

    def rot(t):
        t1, t2 = t[:, :hw], t[:, hw:]
        return jnp.concatenate([t1 * cos - t2 * sin, t1 * sin + t2 * cos], axis=1)

    qr, kr = rot(rq), rot(rk)
    qr_bf, kr_bf, v_bf = qr.astype(BF16), kr.astype(BF16), v.astype(BF16)
    lane_k = lax.broadcasted_iota(jnp.int32, (1, RET_KEY_WIDTH), 1) % hw
    lane_v = lax.broadcasted_iota(jnp.int32, (1, 2 * RET_DV), 1)
    v_sel = [jnp.where(lane_v < RET_DV, 1.0, 0.0), jnp.where(lane_v >= RET_DV, 1.0, 0.0)]
    half = RET_DK // 2
    o_parts = []
    for p in range(RET_HEADS // 2):
        sel = [jnp.where((lane_k >= h * half) & (lane_k < (h + 1) * half), 1.0, 0.0)
               for h in (2 * p, 2 * p + 1)]
        lhs = jnp.concatenate([qr * sel[0], qr * sel[1]], axis=0).astype(BF16)
        sc = (_dot_nt(lhs, kr_bf) * intra_ref[p]).astype(BF16)
        sc = jnp.concatenate([sc[:CHUNK], sc[CHUNK:]], axis=1)
        vp = v[:, p * 128:(p + 1) * 128]
        vv = jnp.concatenate([vp * v_sel[0], vp * v_sel[1]], axis=0).astype(BF16)
        o_parts.append(_dot(sc, vv))
    o = jnp.concatenate(o_parts, axis=1)
    st = st_ref[...]
    o = o + _dot(qr_bf, st.astype(BF16)) * inter_ref[...]
    kd = (kr * tostate_ref[...]).astype(BF16)
    st_ref[...] = st * carry_ref[...] + _dot_tn(kd, v_bf) * bmask_ref[...]
    avg = avg_ref[...]
    outs = []
    for s in range(RET_WIDTH // 256):
        os_ = o[:, s * 256:(s + 1) * 256]
        c = os_ - _dot(os_.astype(BF16), avg)
        outs.append(c * lax.rsqrt(_dot((c * c).astype(BF16), avg) + EPS))
    return jnp.concatenate(outs, axis=1) * g_ref[...] * (gate * _sigmoid(gate))


def _mixer_kernel(proj_ref, cos_ref, sin_ref,
                  wb_ref, wc_ref, pn_re_ref, pn_im_ref, pp_re_ref, pp_im_ref, lam_ref, d_ref,
                  wglu_ref, s5g_ref, ltri_ref,
                  convw_ref, lbv_ref, mall_ref, lvl_ref, rsel_ref, bones_ref, avg_ref, hgg_ref,
                  intra_ref, inter_ref, tostate_ref, carry_ref, bmask_ref, retg_ref,
                  o_ref,
                  s5_st, hg_xc, hg_st, ret_st):
    @pl.when(pl.program_id(0) == 0)
    def _():
        s5_st[...] = jnp.zeros_like(s5_st)
        hg_xc[...] = jnp.zeros_like(hg_xc)
        hg_st[...] = jnp.zeros_like(hg_st)
        ret_st[...] = jnp.zeros_like(ret_st)

    for b in range(proj_ref.shape[0]):
        pr, out, xc = proj_ref.at[b], o_ref.at[b], hg_xc.at[b]
        u = pr[:, C_U:C_U + S5_WIDTH]
        y_a = _s5_chunk(u, wb_ref, wc_ref, pn_re_ref, pn_im_ref, pp_re_ref, pp_im_ref, lam_ref,
                        d_ref, wglu_ref, s5g_ref, ltri_ref, s5_st.at[b])
        out[:, 0:S5_WIDTH] = y_a.astype(out.dtype)

        xc[8:8 + CHUNK, :] = pr[:, C_HQ:C_HQ + HG_CONV_W]
        y_b = _hgrn2_chunk(xc, pr[:, C_HGATE:C_HGATE + HG_WIDTH], convw_ref, lbv_ref,
                           mall_ref, lvl_ref, rsel_ref, bones_ref, avg_ref, hgg_ref, hg_st.at[b])
        xc[0:8, :] = xc[CHUNK:CHUNK + 8, :]
        out[:, S5_WIDTH:S5_WIDTH + HG_WIDTH] = y_b.astype(out.dtype)

        y_c = _ret_chunk(pr[:, C_RQ:C_RQ + RET_KEY_WIDTH], pr[:, C_RK:C_RK + RET_KEY_WIDTH],
                         pr[:, C_RV:C_RV + RET_WIDTH], pr[:, C_RGATE:C_RGATE + RET_WIDTH],
                         cos_ref[...], sin_ref[...], intra_ref, inter_ref, tostate_ref, carry_ref,
                         bmask_ref, avg_ref, retg_ref, ret_st.at[b])
        out[:, S5_WIDTH + HG_WIDTH:] = y_c.astype(out.dtype)


def _const_spec(a):
    nd = a.ndim
    return pl.BlockSpec(a.shape, lambda c, _nd=nd: (0,) * _nd)


def _mixers(proj, cos, sin, consts, bsz, n_chunks):
    total = proj.shape[1]
    in_specs = [
        pl.BlockSpec((bsz, CHUNK, IN_COLS), lambda c: (0, c, 0)),
        pl.BlockSpec((CHUNK, 128), lambda c: (c, 0)),
        pl.BlockSpec((CHUNK, 128), lambda c: (c, 0)),
    ] + [_const_spec(a) for a in consts]
    return pl.pallas_call(
        _mixer_kernel,
        grid=(n_chunks,),
        in_specs=in_specs,
        out_specs=pl.BlockSpec((bsz, CHUNK, D_MIX), lambda c: (0, c, 0)),
        out_shape=jax.ShapeDtypeStruct((bsz, total, D_MIX), BF16),
        scratch_shapes=[
            pltpu.VMEM((bsz, 2, S5_NSTATE), F32),
            pltpu.VMEM((bsz, CHUNK + 8, HG_CONV_W), F32),
            pltpu.VMEM((bsz, HG_WIDTH, HG_WIDTH), F32),
            pltpu.VMEM((bsz, RET_KEY_WIDTH, RET_WIDTH), F32),
        ],
        compiler_params=pltpu.CompilerParams(
            dimension_semantics=("arbitrary",), vmem_limit_bytes=VMEM_LIMIT),
        name="mixers",
    )(proj, cos, sin, *consts)


def _route(hn, router_ref, ltri_ref, count_ref):
    r = router_ref[...]
    r_hi, r_lo = _split_bf16(r)
    h_hi, h_lo = _split_bf16(hn)
    logits = _dot(h_hi, r_hi) + (_dot(h_lo, r_hi) + _dot(h_hi, r_lo))
    lane_i = lax.broadcasted_iota(jnp.int32, logits.shape, 1)
    lane = lane_i.astype(F32)
    neg = jnp.float32(-jnp.inf)
    logits = jnp.where(lane_i < N_EXPERTS, logits, neg)
    m1 = jnp.max(logits, axis=-1, keepdims=True)
    i1 = jnp.min(jnp.where(logits == m1, lane, float(ROUTE_LANES)), axis=-1, keepdims=True)
    rest = jnp.where(lane == i1, neg, logits)
    m2 = jnp.max(rest, axis=-1, keepdims=True)
    i2 = jnp.min(jnp.where(rest == m2, lane, float(ROUTE_LANES)), axis=-1, keepdims=True)
    e2 = jnp.exp(m2 - m1)
    g1 = 1.0 / (1.0 + e2)
    g2 = e2 / (1.0 + e2)
    onehot = jnp.where((lane == i1) | (lane == i2), 1.0, 0.0)
    before = _dot(ltri_ref[...], onehot.astype(BF16)) + count_ref[...]
    r1 = jnp.sum(jnp.where(lane == i1, before, 0.0), axis=-1, keepdims=True)
    r2 = jnp.sum(jnp.where(lane == i2, before, 0.0), axis=-1, keepdims=True)
    count_ref[...] += jnp.sum(onehot, axis=0, keepdims=True)
    out = jnp.where(lane == 0, i1, 0.0)
    out = jnp.where(lane == 1, i2, out)
    out = jnp.where(lane == 2, g1, out)
    out = jnp.where(lane == 3, g2, out)
    out = jnp.where(lane == 4, r1, out)
    return jnp.where(lane == 5, r2, out)


def _out_proj_kernel(mixed_ref, h_ref, w_ref, g_ref, *rest, routed):
    if routed:
        router_ref, ltri_ref, h1_ref, hn_ref, route_ref, count_ref = rest

        @pl.when(pl.program_id(0) == 0)
        def _():
            count_ref[...] = jnp.zeros_like(count_ref)
    else:
        h1_ref, hn_ref = rest
    h1 = h_ref[...] + _dot(mixed_ref[...], w_ref[...])
    h1_ref[...] = h1
    hn = h1 * lax.rsqrt(jnp.mean(h1 * h1, axis=-1, keepdims=True) + EPS) * g_ref[...]
    hn_ref[...] = hn.astype(hn_ref.dtype)
    if routed:
        route_ref[...] = _route(hn, router_ref, ltri_ref, count_ref)


def _out_proj(mixed, h, w_bf16, g, router_pad=None):
    n, d = h.shape
    tm = _pick_tile(n, (640, 128))
    routed = router_pad is not None
    row = lambda i: (i, 0)
    fixed = lambda i: (0, 0)
    in_specs = [pl.BlockSpec((tm, D_MIX), row), pl.BlockSpec((tm, d), row),
                pl.BlockSpec(w_bf16.shape, fixed), pl.BlockSpec((1, d), fixed)]
    out_specs = [pl.BlockSpec((tm, d), row), pl.BlockSpec((tm, d), row)]
    out_shape = [jax.ShapeDtypeStruct((n, d), F32), jax.ShapeDtypeStruct((n, d), F32 if routed else BF16)]
    args = [mixed, h, w_bf16, g]
    if routed:
        ltri = jnp.asarray(np.tril(np.ones((tm, tm), np.float32), -1), BF16)
        in_specs += [pl.BlockSpec(router_pad.shape, fixed), pl.BlockSpec((tm, tm), fixed)]
        out_specs += [pl.BlockSpec((tm, ROUTE_LANES), row), pl.BlockSpec((1, ROUTE_LANES), fixed)]
        out_shape += [jax.ShapeDtypeStruct((n, ROUTE_LANES), F32),
                      jax.ShapeDtypeStruct((1, ROUTE_LANES), F32)]
        args += [router_pad, ltri]
    return pl.pallas_call(
        functools.partial(_out_proj_kernel, routed=routed),
        grid=(n // tm,),
        in_specs=in_specs,
        out_specs=out_specs,
        out_shape=out_shape,
        compiler_params=pltpu.CompilerParams(
            dimension_semantics=("arbitrary",), vmem_limit_bytes=VMEM_LIMIT),
        name="out_proj_routed" if routed else "out_proj",
    )(*args)


def _ffn_kernel(hn_ref, h1_ref, w1_ref, w3_ref, w2_ref, o_ref, acc_ref):
    f = pl.program_id(1)
    x = hn_ref[...]
    a = _dot(x, w1_ref[...])
    b = _dot(x, w3_ref[...])
    part = _dot((a * _sigmoid(a) * b).astype(BF16), w2_ref[...])

    @pl.when(f == 0)
    def _():
        acc_ref[...] = part

    @pl.when(f > 0)
    def _():
        acc_ref[...] += part

    @pl.when(f == pl.num_programs(1) - 1)
    def _():
        o_ref[...] = h1_ref[...] + acc_ref[...]


def _ffn(hn, h1, w1, w3, w2):
    n, d = h1.shape
    dff = w1.shape[1]
    tm, tf = _pick_tile(n, (640, 128)), _pick_tile(dff, (1408, 128))
    return pl.pallas_call(
        _ffn_kernel,
        grid=(n // tm, dff // tf),
        in_specs=[
            pl.BlockSpec((tm, d), lambda i, f: (i, 0)),
            pl.BlockSpec((tm, d), lambda i, f: (i, 0)),
            pl.BlockSpec((d, tf), lambda i, f: (0, f)),
            pl.BlockSpec((d, tf), lambda i, f: (0, f)),
            pl.BlockSpec((tf, d), lambda i, f: (f, 0)),
        ],
        out_specs=pl.BlockSpec((tm, d), lambda i, f: (i, 0)),
        out_shape=jax.ShapeDtypeStruct((n, d), F32),
        scratch_shapes=[pltpu.VMEM((tm, d), F32)],
        compiler_params=pltpu.CompilerParams(
            dimension_semantics=("arbitrary", "arbitrary"), vmem_limit_bytes=VMEM_LIMIT),
        name="ffn_dense",
    )(hn, h1, w1, w3, w2)


def _moe_kernel(te_ref, nv_ref, src_ref, tok_ref, hn_hbm, w1_ref, w3_ref, w2_ref, o_ref,
                xs_ref, xb_ref, acc_ref, sem, *, tm):
    i, f = pl.program_id(0), pl.program_id(1)
    nf = pl.num_programs(1)
    n_valid = nv_ref[0]
    share = tm // MOE_F_STEPS

    def row_copy(tile, r):
        return pltpu.make_async_copy(hn_hbm.at[pl.ds(tok_ref[src_ref[tile] + r], 1)],
                                     xs_ref.at[pl.ds(r, 1)], sem)

    @pl.when((i == 0) & (f == 0))
    def _():
        def body(r, carry):
            row_copy(0, r).start()
            return carry
        lax.fori_loop(0, tm, body, 0)

    @pl.when((f == 0) & (i <= n_valid))
    def _():
        pltpu.make_async_copy(xs_ref, xs_ref, sem).wait()
        xb_ref[...] = xs_ref[...].astype(BF16)

    @pl.when(i < n_valid)
    def _():
        for r in range(share):
            row_copy(i + 1, f * share + r).start()
        x = xb_ref[...]
        a = _dot(x, w1_ref[0])
        b = _dot(x, w3_ref[0])
        part = _dot((a * _sigmoid(a) * b).astype(BF16), w2_ref[0])

        @pl.when(f == 0)
        def _():
            acc_ref[...] = part

        @pl.when(f > 0)
        def _():
            acc_ref[...] += part

    @pl.when(f == nf - 1)
    def _():
        o_ref[...] = jnp.where(i < n_valid, acc_ref[...], 0.0).astype(o_ref.dtype)


def _live_f(i, f, nv):
    return jnp.where(i < nv[0], f, 0)


def _moe_grouped(tile_expert, n_valid, tile_src, sorted_tok, hn, w1, w3, w2, tm):
    d = hn.shape[1]
    n_tiles = tile_expert.shape[0]
    dff = w1.shape[2]
    tf = dff // MOE_F_STEPS
    grid_spec = pltpu.PrefetchScalarGridSpec(
        num_scalar_prefetch=4,
        grid=(n_tiles, MOE_F_STEPS),
        in_specs=[
            pl.BlockSpec(memory_space=pl.ANY),
            pl.BlockSpec((1, d, tf), lambda i, f, te, nv, *_: (te[i], 0, _live_f(i, f, nv))),
            pl.BlockSpec((1, d, tf), lambda i, f, te, nv, *_: (te[i], 0, _live_f(i, f, nv))),
            pl.BlockSpec((1, tf, d), lambda i, f, te, nv, *_: (te[i], _live_f(i, f, nv), 0)),
        ],
        out_specs=pl.BlockSpec((tm, d), lambda i, f, *_: (i, 0)),
        scratch_shapes=[pltpu.VMEM((tm, d), F32), pltpu.VMEM((tm, d), BF16), pltpu.VMEM((tm, d), F32),
                        pltpu.SemaphoreType.DMA(())],
    )
    return pl.pallas_call(
        functools.partial(_moe_kernel, tm=tm),
        grid_spec=grid_spec,
        out_shape=jax.ShapeDtypeStruct((n_tiles * tm, d), BF16),
        compiler_params=pltpu.CompilerParams(
            dimension_semantics=("arbitrary", "arbitrary"), vmem_limit_bytes=VMEM_LIMIT),
        name="moe_grouped",
    )(tile_expert, n_valid, tile_src, sorted_tok, hn, w1, w3, w2)


def _moe_dispatch(route, counts_row, tm):
    n = route.shape[0]
    n_slots = n * TOP_K
    n_tiles = (n_slots + N_EXPERTS * tm + tm - 1) // tm
    counts = counts_row[0, :N_EXPERTS].astype(jnp.int32)
    padded = ((counts + tm - 1) // tm) * tm
    pend = jnp.cumsum(padded)
    pstart = pend - padded
    gstart = jnp.cumsum(counts) - counts
    eid = route[:, 0:TOP_K].astype(jnp.int32)
    rank = route[:, 2 * TOP_K:3 * TOP_K].astype(jnp.int32)
    pos = pstart[eid] + rank
    tok = jnp.broadcast_to(jnp.arange(n, dtype=jnp.int32)[:, None], (n, TOP_K))
    _, sorted_tok = lax.sort(((gstart[eid] + rank).reshape(-1), tok.reshape(-1)), num_keys=1)
    sorted_tok = jnp.concatenate([sorted_tok, jnp.zeros((tm,), jnp.int32)])
    tile_start = jnp.arange(n_tiles, dtype=jnp.int32) * tm
    tile_expert = jnp.minimum(jnp.sum((tile_start[:, None] >= pend[None, :]).astype(jnp.int32), axis=1),
                              N_EXPERTS - 1)
    tile_src = jnp.clip(tile_start - (pstart - gstart)[tile_expert], 0, n_slots)
    n_valid = (pend[-1] // tm).astype(jnp.int32).reshape(1)
    return sorted_tok, tile_src, pos, tile_expert, n_valid


def _final_kernel(h1_ref, y0_ref, y1_ref, route_ref, g_ref, o_ref):
    g0 = route_ref[:, TOP_K:TOP_K + 1]
    g1 = route_ref[:, TOP_K + 1:TOP_K + 2]
    h = h1_ref[...] + (g0 * y0_ref[...].astype(F32) + g1 * y1_ref[...].astype(F32))
    o_ref[...] = h * lax.rsqrt(jnp.mean(h * h, axis=-1, keepdims=True) + EPS) * g_ref[...]


def _final(h1, y0, y1, route, g):
    n, d = h1.shape
    tm = _pick_tile(n, (640, 128))
    row = lambda i: (i, 0)
    return pl.pallas_call(
        _final_kernel,
        grid=(n // tm,),
        in_specs=[pl.BlockSpec((tm, d), row), pl.BlockSpec((tm, d), row), pl.BlockSpec((tm, d), row),
                  pl.BlockSpec((tm, ROUTE_LANES), row), pl.BlockSpec((1, d), lambda i: (0, 0))],
        out_specs=pl.BlockSpec((tm, d), row),
        out_shape=jax.ShapeDtypeStruct((n, d), F32),
        compiler_params=pltpu.CompilerParams(
            dimension_semantics=("arbitrary",), vmem_limit_bytes=VMEM_LIMIT),
        name="final_norm",
    )(h1, y0, y1, route, g)


def _rope_perm():
    half = RET_DK // 2
    idx = []
    for part in range(2):
        for h in range(RET_HEADS):
            for dd in range(half):
                idx.append(h * RET_DK + part * half + dd)
    return np.asarray(idx, np.int32)


def _in_col_perm():
    cols = np.arange(IN_COLS, dtype=np.int32)
    rp = _rope_perm()
    cols[C_RQ:C_RQ + RET_KEY_WIDTH] = C_RQ + rp
    cols[C_RK:C_RK + RET_KEY_WIDTH] = C_RK + rp
    return cols


def _hg_tables():
    c = CHUNK
    i = np.arange(c)[:, None]
    t = np.arange(c)[None, :]
    blocks, masks, rsel = [], [], []
    for lvl in range(HG_LEVELS):
        s = 1 << lvl
        r = (i // (2 * s)) * (2 * s) + s - 1
        right = i > r
        m = np.where(right, (t > r) & (t <= i), (t > i) & (t <= r))
        blocks.append(m)
        j = t
        same = (i // (2 * s)) == (j // (2 * s))
        mk = same & right & (j <= r)
        masks.append(np.concatenate([mk, mk], axis=0))
        if s < 8:
            rsel.append(np.broadcast_to(right, (c, HG_WIDTH)))
    blocks.append(t <= i)
    blocks.append(t > i)
    mall = np.concatenate(blocks, axis=0).astype(np.float32)
    lvl = np.stack(masks, axis=0).astype(np.float32)
    ch = np.arange(HG_WIDTH)
    bones = (ch[:, None] // HG_DK == ch[None, :] // HG_DK).astype(np.float32)
    return mall, lvl, np.stack(rsel, axis=0).astype(np.float32), bones


def _ret_tables():
    f32 = jnp.float32
    log_gamma = jnp.log1p(-jnp.power(2.0, -5.0 - jnp.arange(RET_HEADS, dtype=f32)))
    n = jnp.arange(CHUNK, dtype=f32)
    lg = log_gamma[:, None]
    causal = jnp.tril(jnp.ones((CHUNK, CHUNK), dtype=bool))
    intra = jnp.exp(jnp.where(causal[None], (n[:, None] - n[None, :])[None] * lg[:, :, None], -jnp.inf))
    scale = RET_DK ** -0.5
    inter = jnp.exp((n[None, :] + 1.0) * lg)
    to_state = jnp.exp((CHUNK - 1.0 - n[None, :]) * lg)
    carry = jnp.exp(CHUNK * lg)[:, 0]
    head_of_v = np.arange(RET_WIDTH) // RET_DV
    head_of_k = (np.arange(RET_KEY_WIDTH) % (RET_KEY_WIDTH // 2)) // (RET_DK // 2)
    inter_t = (inter * scale).T[:, head_of_v]
    to_state_t = to_state.T[:, head_of_k]
    carry_row = carry[head_of_v][None, :]
    bmask = jnp.asarray((head_of_k[:, None] == head_of_v[None, :]).astype(np.float32))
    intra_pairs = (intra * scale).reshape(RET_HEADS // 2, 2 * CHUNK, CHUNK)
    return intra_pairs, inter_t, to_state_t, carry_row, bmask


def _s5_tables(lam_re, lam_im, b_re, b_im, c_re, c_im, d_skip, log_step):
    f32 = jnp.float32
    lam = lax.complex(lam_re.astype(f32), lam_im.astype(f32))
    step = jnp.exp(log_step.astype(f32))[:, None]
    lam_dt = lam * step
    lam_bar = jnp.exp(lam_dt)
    b_bar = ((lam_bar - 1.0) / lam)[..., None] * lax.complex(b_re.astype(f32), b_im.astype(f32))
    eye = jnp.eye(S5_NGROUPS, dtype=f32)
    wb_re = jnp.einsum('gph,gk->ghkp', jnp.real(b_bar), eye).reshape(S5_WIDTH, S5_NSTATE)
    wb_im = jnp.einsum('gph,gk->ghkp', jnp.imag(b_bar), eye).reshape(S5_WIDTH, S5_NSTATE)
    wb = jnp.concatenate([wb_re, wb_im], axis=1)
    wc_re = jnp.einsum('ghp,gk->gpkh', c_re.astype(f32), eye).reshape(S5_NSTATE, S5_WIDTH)
    wc_im = jnp.einsum('ghp,gk->gpkh', c_im.astype(f32), eye).reshape(S5_NSTATE, S5_WIDTH)
    wc = jnp.concatenate([wc_re, -wc_im], axis=0)
    t = jnp.arange(CHUNK, dtype=f32)[:, None, None]
    pp = jnp.exp(lam_dt[None] * t).reshape(CHUNK, S5_NSTATE)
    pn = jnp.exp(-lam_dt[None] * t).reshape(CHUNK, S5_NSTATE)
    lam_rows = jnp.stack([jnp.real(lam_bar).reshape(-1), jnp.imag(lam_bar).reshape(-1)], axis=0)
    return (wb.astype(BF16), wc.astype(BF16), jnp.real(pn), jnp.imag(pn), jnp.real(pp), jnp.imag(pp),
            lam_rows, d_skip.astype(f32).reshape(1, S5_WIDTH))


def kernel(x, meta_tokens, norm_mix_g, w_in, s5_lam_re, s5_lam_im, s5_b_re, s5_b_im, s5_c_re, s5_c_im, s5_d, s5_log_step, s5_w_glu, s5_out_g, hg_conv_w, hg_lb_param, hg_out_g, ret_out_g, w_out, norm_ffn_g, ffn_w1, ffn_w3, ffn_w2, moe_router, moe_w1, moe_w3, moe_w2, final_norm_g):
    bsz, seq_len, d = x.shape
    depth = w_in.shape[0]
    total = seq_len + CHUNK
    n_chunks = total // CHUNK
    n = bsz * total

    meta = jnp.broadcast_to(meta_tokens.astype(F32)[None], (bsz, N_META, d))
    h = jnp.concatenate([jnp.zeros((bsz, PAD, d), F32), meta, x.astype(F32)], axis=1).reshape(n, d)

    pos = (jnp.arange(total) - PAD).astype(F32)
    half = RET_DK // 2
    inv_freq = ROPE_BASE ** (-jnp.arange(half, dtype=F32) / half)
    ang = pos[:, None] * inv_freq[None, :]
    cos_t = jnp.tile(jnp.cos(ang), (1, RET_HEADS))
    sin_t = jnp.tile(jnp.sin(ang), (1, RET_HEADS))

    lb_all = jnp.cumsum(jax.nn.softmax(hg_lb_param.astype(F32), axis=0), axis=0)
    lb_all = lb_all - lb_all[0]

    mall_np, lvl_np, rsel_np, bones_np = _hg_tables()
    mall = jnp.asarray(mall_np, BF16)
    lvl = jnp.asarray(lvl_np, F32)
    rsel = jnp.asarray(rsel_np, F32)
    bones = jnp.asarray(bones_np, BF16)
    avg = jnp.asarray(bones_np / HG_DK, BF16)
    ltri = jnp.asarray(np.tril(np.ones((CHUNK, CHUNK), np.float32)), BF16)
    intra, inter_t, to_state_t, carry_row, bmask = _ret_tables()
    col_perm = jnp.asarray(_in_col_perm())

    out = None
    for l in range(depth):
        w_in_l = jnp.take(w_in[l], col_perm, axis=1).astype(BF16)
        proj = _in_proj(h, norm_mix_g[l].astype(F32).reshape(1, d), w_in_l, total)

        s5c = _s5_tables(s5_lam_re[l], s5_lam_im[l], s5_b_re[l], s5_b_im[l], s5_c_re[l], s5_c_im[l],
                         s5_d[l], s5_log_step[l])
        lb = lb_all[l][None, :]
        lbv = jnp.concatenate([jnp.log(lb), jnp.log1p(-lb), 1.0 - lb], axis=0)
        consts = list(s5c) + [
            s5_w_glu[l].astype(BF16), s5_out_g[l].astype(F32).reshape(1, -1), ltri,
            hg_conv_w[l].astype(F32), lbv, mall, lvl, rsel, bones, avg, hg_out_g[l].astype(F32).reshape(1, -1),
            intra, inter_t, to_state_t, carry_row, bmask, ret_out_g[l].astype(F32).reshape(1, -1),
        ]
        mixed = _mixers(proj.reshape(bsz, total, IN_COLS), cos_t, sin_t, consts, bsz, n_chunks)
        mixed = mixed.reshape(n, D_MIX)

        g_ffn = norm_ffn_g[l].astype(F32).reshape(1, d)
        w_out_l = w_out[l].astype(BF16)
        if l % 2 == 0:
            h1, hn = _out_proj(mixed, h, w_out_l, g_ffn)
            j = l // 2
            h = _ffn(hn, h1, ffn_w1[j].astype(BF16), ffn_w3[j].astype(BF16), ffn_w2[j].astype(BF16))
            y0 = y1 = None
        else:
            j = l // 2
            router_pad = jnp.zeros((d, ROUTE_LANES), F32).at[:, :N_EXPERTS].set(moe_router[j].astype(F32))
            h1, hn, route, counts_row = _out_proj(mixed, h, w_out_l, g_ffn, router_pad)
            tm = MOE_TILE
            sorted_tok, tile_src, pos_of_slot, tile_expert, n_valid = _moe_dispatch(route, counts_row, tm)
            ys = _moe_grouped(tile_expert, n_valid, tile_src, sorted_tok, hn,
                              moe_w1[j].astype(BF16), moe_w3[j].astype(BF16), moe_w2[j].astype(BF16), tm)
            y0 = jnp.take(ys, pos_of_slot[:, 0], axis=0)
            y1 = jnp.take(ys, pos_of_slot[:, 1], axis=0)
            if l < depth - 1:
                h = h1 + (route[:, TOP_K:TOP_K + 1] * y0.astype(F32)
                          + route[:, TOP_K + 1:TOP_K + 2] * y1.astype(F32))

        if l == depth - 1:
            if y0 is None:
                y0 = y1 = jnp.zeros((n, d), BF16)
                h1 = h
                route = jnp.zeros((n, ROUTE_LANES), F32)
            out = _final(h1, y0, y1, route, final_norm_g.astype(F32).reshape(1, d))

    return out.reshape(bsz, total, d)[:, CHUNK:, :].astype(x.dtype)
```

```python
import functools
import math

import jax
import jax.numpy as jnp
import numpy as np
from jax import lax
from jax.experimental import pallas as pl
from jax.experimental.pallas import tpu as pltpu

F32 = jnp.float32
BF16 = jnp.bfloat16

CHUNK = 128
N_META = 16
PAD = CHUNK - N_META
EPS = 1e-6

S5_WIDTH = 256
S5_GROUP = 16
S5_NGROUPS = 16
S5_STATE = 64
S5_NSTATE = S5_NGROUPS * S5_STATE

HG_HEADS = 4
HG_DK = 64
HG_WIDTH = 256
CONV_K = 4
HG_CONV_W = 3 * HG_WIDTH
HG_LEVELS = 7

RET_HEADS = 8
RET_DK = 32
RET_DV = 64
RET_KEY_WIDTH = 256
RET_WIDTH = 512
ROPE_BASE = 10000.0

D_MIX = 1024
IN_COLS = 2816
C_U, C_HQ, C_HGATE, C_RQ, C_RK, C_RV, C_RGATE = 0, 256, 1024, 1280, 1536, 1792, 2304

N_EXPERTS = 8
TOP_K = 2
ROUTE_LANES = 128
MOE_F_STEPS = 2
MOE_TILE = 512
MOE_DMA_QUEUES = 2

VMEM_LIMIT = 56 * 1024 * 1024


def _sigmoid(x):
    return 1.0 / (1.0 + jnp.exp(-x))


def _split_bf16(x):
    hi = x.astype(BF16)
    lo = (x - hi.astype(F32)).astype(BF16)
    return hi, lo


def _pick_tile(n, candidates):
    for t in candidates:
        if n % t == 0:
            return t
    raise ValueError(f"no tile in {candidates} divides {n}")


def _dot(a, b):
    return jnp.dot(a, b, preferred_element_type=F32)


def _dot_nt(a, b):
    return lax.dot_general(a, b, (((1,), (1,)), ((), ())), preferred_element_type=F32)


def _dot_tn(a, b):
    return lax.dot_general(a, b, (((0,), (0,)), ((), ())), preferred_element_type=F32)


def _in_proj_kernel(h_ref, g_ref, w_ref, o_ref, *, tm, tiles_per_batch):
    x = h_ref[...]
    y = x * lax.rsqrt(jnp.mean(x * x, axis=-1, keepdims=True) + EPS) * g_ref[...]
    proj = _dot(y.astype(BF16), w_ref[...])
    row0 = (pl.program_id(0) % tiles_per_batch) * tm
    rows = row0 + lax.broadcasted_iota(jnp.int32, (tm, 1), 0)
    o_ref[...] = jnp.where(rows >= PAD, proj, 0.0)


def _in_proj(h, g, w_bf16, total):
    n = h.shape[0]
    tm = _pick_tile(total, (320, 128))
    return pl.pallas_call(
        functools.partial(_in_proj_kernel, tm=tm, tiles_per_batch=total // tm),
        grid=(n // tm,),
        in_specs=[
            pl.BlockSpec((tm, h.shape[1]), lambda i: (i, 0)),
            pl.BlockSpec((1, h.shape[1]), lambda i: (0, 0)),
            pl.BlockSpec(w_bf16.shape, lambda i: (0, 0)),
        ],
        out_specs=pl.BlockSpec((tm, IN_COLS), lambda i: (i, 0)),
        out_shape=jax.ShapeDtypeStruct((n, IN_COLS), F32),
        compiler_params=pltpu.CompilerParams(
            dimension_semantics=("arbitrary",), vmem_limit_bytes=VMEM_LIMIT),
        name="in_proj",
    )(h, g, w_bf16)


def _s5_chunk(u, wb_ref, wc_ref, pn_re_ref, pn_im_ref, pp_re_ref, pp_im_ref, lam_ref,
              d_ref, wglu_ref, g_ref, ltri_ref, st_ref):
    ns = S5_NSTATE
    bu = _dot(u.astype(BF16), wb_ref[...])
    bu_re, bu_im = bu[:, :ns], bu[:, ns:]
    pn_re, pn_im = pn_re_ref[...], pn_im_ref[...]
    w_re = (pn_re * bu_re - pn_im * bu_im).astype(BF16)
    w_im = (pn_re * bu_im + pn_im * bu_re).astype(BF16)
    ltri = ltri_ref[...]
    lam_re, lam_im = lam_ref[0:1, :], lam_ref[1:2, :]
    s_re, s_im = st_ref[0:1, :], st_ref[1:2, :]
    z_re = _dot(ltri, w_re) + (lam_re * s_re - lam_im * s_im)
    z_im = _dot(ltri, w_im) + (lam_re * s_im + lam_im * s_re)
    pp_re, pp_im = pp_re_ref[...], pp_im_ref[...]
    st_re = pp_re * z_re - pp_im * z_im
    st_im = pp_re * z_im + pp_im * z_re
    st_ref[0:1, :] = st_re[CHUNK - 1:CHUNK, :]
    st_ref[1:2, :] = st_im[CHUNK - 1:CHUNK, :]
    y = (_dot(st_re.astype(BF16), wc_ref[0:ns, :]) + _dot(st_im.astype(BF16), wc_ref[ns:2 * ns, :])
         + d_ref[...] * u)
    y = 0.5 * y * (1.0 + jnp.tanh(math.sqrt(2.0 / math.pi) * (y + 0.044715 * (y * y * y))))
    y = y * _sigmoid(_dot(y.astype(BF16), wglu_ref[...]))
    return y * lax.rsqrt(jnp.mean(y * y, axis=-1, keepdims=True) + EPS) * g_ref[...]


def _hgrn2_chunk(xc_ref, gate, convw_ref, lbv_ref, mall_ref, lvl_ref, rsel_ref, bones_ref, avg_ref,
                 g_ref, st_ref):
    w = HG_WIDTH
    conv = None
    for i in range(CONV_K):
        term = xc_ref[pl.ds(8 - (CONV_K - 1) + i, CHUNK), :] * convw_ref[i:i + 1, :]
        conv = term if conv is None else conv + term
    cq, cf, v = conv[:, :w], conv[:, w:2 * w], conv[:, 2 * w:]
    q = cq * _sigmoid(cq)
    log_lb, log_1m_lb, one_m_lb = lbv_ref[0:1, :], lbv_ref[1:2, :], lbv_ref[2:3, :]
    log_sig = jnp.minimum(cf, 0.0) - jnp.log(1.0 + jnp.exp(-jnp.abs(cf)))
    b = log_1m_lb + log_sig
    logf = jnp.maximum(log_lb, b) + jnp.log(1.0 + jnp.exp(-jnp.abs(log_lb - b)))
    kk = one_m_lb * _sigmoid(-cf)
    hi, lo = _split_bf16(logf)
    hl = jnp.concatenate([hi, lo], axis=1)

    def interval_sums(blk):
        d = _dot(mall_ref[blk * CHUNK:(blk + 1) * CHUNK, :], hl)
        return d[:, :w] + d[:, w:]

    lane = lax.broadcasted_iota(jnp.int32, (1, 2 * HG_DK), 1)
    head_sel = [jnp.where(lane < HG_DK, 1.0, 0.0), jnp.where(lane >= HG_DK, 1.0, 0.0)]
    n_pairs = HG_HEADS // 2
    scores = [None] * n_pairs
    for lvl in range(HG_LEVELS):
        s = 1 << lvl
        e = jnp.exp(interval_sums(lvl))
        if s >= 8:
            qk = jnp.concatenate([(q if (r // s) % 2 else kk)[r:r + s] for r in range(0, CHUNK, s)], axis=0)
        else:
            qk = jnp.where(rsel_ref[lvl] > 0.5, q, kk)
        x = qk * e
        m = lvl_ref[lvl]
        for p in range(n_pairs):
            xp = x[:, p * 128:(p + 1) * 128]
            lhs = jnp.concatenate([xp * head_sel[0], xp * head_sel[1]], axis=0).astype(BF16)
            sc = _dot_nt(lhs, xp.astype(BF16)) * m
            scores[p] = sc if scores[p] is None else scores[p] + sc
    v_bf = v.astype(BF16)
    o_parts = []
    for p in range(n_pairs):
        sc = scores[p].astype(BF16)
        sc = jnp.concatenate([sc[:CHUNK], sc[CHUNK:]], axis=1)
        vp = v[:, p * 128:(p + 1) * 128]
        vv = jnp.concatenate([vp * head_sel[0], vp * head_sel[1]], axis=0).astype(BF16)
        o_parts.append(_dot(sc, vv))
    o = jnp.concatenate(o_parts, axis=1)
    bones = bones_ref[...]
    o = o + _dot((q * kk).astype(BF16), bones) * v
    g_cum = interval_sums(HG_LEVELS)
    g_suffix = interval_sums(HG_LEVELS + 1)
    st = st_ref[...]
    o = o + _dot_nt((q * jnp.exp(g_cum)).astype(BF16), st.astype(BF16))
    kd = (kk * jnp.exp(g_suffix)).astype(BF16)
    upd = _dot_tn(v_bf, kd) * bones.astype(F32)
    st_ref[...] = st * jnp.exp(g_cum[CHUNK - 1:CHUNK, :]) + upd
    ms = _dot((o * o).astype(BF16), avg_ref[...])
    return o * lax.rsqrt(ms + EPS) * g_ref[...] * (gate * _sigmoid(gate))


def _ret_chunk(rq, rk, v, gate, cos, sin, intra_ref, inter_ref, tostate_ref, carry_ref,
               bmask_ref, avg_ref, g_ref, st_ref):
    hw = RET_KEY_WIDTH // 2

    def rot(t):
        t1, t2 = t[:, :hw], t[:, hw:]
        return jnp.concatenate([t1 * cos - t2 * sin, t1 * sin + t2 * cos], axis=1)

    qr, kr = rot(rq), rot(rk)
    qr_bf, kr_bf, v_bf = qr.astype(BF16), kr.astype(BF16), v.astype(BF16)
    lane_k = lax.broadcasted_iota(jnp.int32, (1, RET_KEY_WIDTH), 1) % hw
    lane_v = lax.broadcasted_iota(jnp.int32, (1, 2 * RET_DV), 1)
    v_sel = [jnp.where(lane_v < RET_DV, 1.0, 0.0), jnp.where(lane_v >= RET_DV, 1.0, 0.0)]
    half = RET_DK // 2
    o_parts = []
    for p in range(RET_HEADS // 2):
        sel = [jnp.where((lane_k >= h * half) & (lane_k < (h + 1) * half), 1.0, 0.0)
               for h in (2 * p, 2 * p + 1)]
        lhs = jnp.concatenate([qr * sel[0], qr * sel[1]], axis=0).astype(BF16)
        sc = (_dot_nt(lhs, kr_bf) * intra_ref[p]).astype(BF16)
        sc = jnp.concatenate([sc[:CHUNK], sc[CHUNK:]], axis=1)
        vp = v[:, p * 128:(p + 1) * 128]
        vv = jnp.concatenate([vp * v_sel[0], vp * v_sel[1]], axis=0).astype(BF16)
        o_parts.append(_dot(sc, vv))
    o = jnp.concatenate(o_parts, axis=1)
    st = st_ref[...]
    o = o + _dot(qr_bf, st.astype(BF16)) * inter_ref[...]
    kd = (kr * tostate_ref[...]).astype(BF16)
    st_ref[...] = st * carry_ref[...] + _dot_tn(kd, v_bf) * bmask_ref[...]
    avg = avg_ref[...]
    outs = []
    for s in range(RET_WIDTH // 256):
        os_ = o[:, s * 256:(s + 1) * 256]
        c = os_ - _dot(os_.astype(BF16), avg)
        outs.append(c * lax.rsqrt(_dot((c * c).astype(BF16), avg) + EPS))
    return jnp.concatenate(outs, axis=1) * g_ref[...] * (gate * _sigmoid(gate))


def _mixer_kernel(proj_ref, cos_ref, sin_ref,
                  wb_ref, wc_ref, pn_re_ref, pn_im_ref, pp_re_ref, pp_im_ref, lam_ref, d_ref,
                  wglu_ref, s5g_ref, ltri_ref,
                  convw_ref, lbv_ref, mall_ref, lvl_ref, rsel_ref, bones_ref, avg_ref, hgg_ref,
                  intra_ref, inter_ref, tostate_ref, carry_ref, bmask_ref, retg_ref,
                  o_ref,
                  s5_st, hg_xc, hg_st, ret_st):
    @pl.when(pl.program_id(0) == 0)
    def _():
        s5_st[...] = jnp.zeros_like(s5_st)
        hg_xc[...] = jnp.zeros_like(hg_xc)
        hg_st[...] = jnp.zeros_like(hg_st)
        ret_st[...] = jnp.zeros_like(ret_st)

    for b in range(proj_ref.shape[0]):
        pr, out, xc = proj_ref.at[b], o_ref.at[b], hg_xc.at[b]
        u = pr[:, C_U:C_U + S5_WIDTH]
        y_a = _s5_chunk(u, wb_ref, wc_ref, pn_re_ref, pn_im_ref, pp_re_ref, pp_im_ref, lam_ref,
                        d_ref, wglu_ref, s5g_ref, ltri_ref, s5_st.at[b])
        out[:, 0:S5_WIDTH] = y_a.astype(out.dtype)

        xc[8:8 + CHUNK, :] = pr[:, C_HQ:C_HQ + HG_CONV_W]
        y_b = _hgrn2_chunk(xc, pr[:, C_HGATE:C_HGATE + HG_WIDTH], convw_ref, lbv_ref,
                           mall_ref, lvl_ref, rsel_ref, bones_ref, avg_ref, hgg_ref, hg_st.at[b])
        xc[0:8, :] = xc[CHUNK:CHUNK + 8, :]
        out[:, S5_WIDTH:S5_WIDTH + HG_WIDTH] = y_b.astype(out.dtype)

        y_c = _ret_chunk(pr[:, C_RQ:C_RQ + RET_KEY_WIDTH], pr[:, C_RK:C_RK + RET_KEY_WIDTH],
                         pr[:, C_RV:C_RV + RET_WIDTH], pr[:, C_RGATE:C_RGATE + RET_WIDTH],
                         cos_ref[...], sin_ref[...], intra_ref, inter_ref, tostate_ref, carry_ref,
                         bmask_ref, avg_ref, retg_ref, ret_st.at[b])
        out[:, S5_WIDTH + HG_WIDTH:] = y_c.astype(out.dtype)


def _const_spec(a):
    nd = a.ndim
    return pl.BlockSpec(a.shape, lambda c, _nd=nd: (0,) * _nd)


def _mixers(proj, cos, sin, consts, bsz, n_chunks):
    total = proj.shape[1]
    in_specs = [
        pl.BlockSpec((bsz, CHUNK, IN_COLS), lambda c: (0, c, 0)),
        pl.BlockSpec((CHUNK, 128), lambda c: (c, 0)),
        pl.BlockSpec((CHUNK, 128), lambda c: (c, 0)),
    ] + [_const_spec(a) for a in consts]
    return pl.pallas_call(
        _mixer_kernel,
        grid=(n_chunks,),
        in_specs=in_specs,
        out_specs=pl.BlockSpec((bsz, CHUNK, D_MIX), lambda c: (0, c, 0)),
        out_shape=jax.ShapeDtypeStruct((bsz, total, D_MIX), BF16),
        scratch_shapes=[
            pltpu.VMEM((bsz, 2, S5_NSTATE), F32),
            pltpu.VMEM((bsz, CHUNK + 8, HG_CONV_W), F32),
            pltpu.VMEM((bsz, HG_WIDTH, HG_WIDTH), F32),
            pltpu.VMEM((bsz, RET_KEY_WIDTH, RET_WIDTH), F32),
        ],
        compiler_params=pltpu.CompilerParams(
            dimension_semantics=("arbitrary",), vmem_limit_bytes=VMEM_LIMIT),
        name="mixers",
    )(proj, cos, sin, *consts)


def _route(hn, router_ref, ltri_ref, count_ref):
    r = router_ref[...]
    r_hi, r_lo = _split_bf16(r)
    h_hi, h_lo = _split_bf16(hn)
    logits = _dot(h_hi, r_hi) + (_dot(h_lo, r_hi) + _dot(h_hi, r_lo))
    lane_i = lax.broadcasted_iota(jnp.int32, logits.shape, 1)
    lane = lane_i.astype(F32)
    neg = jnp.float32(-jnp.inf)
    logits = jnp.where(lane_i < N_EXPERTS, logits, neg)
    m1 = jnp.max(logits, axis=-1, keepdims=True)
    i1 = jnp.min(jnp.where(logits == m1, lane, float(ROUTE_LANES)), axis=-1, keepdims=True)
    rest = jnp.where(lane == i1, neg, logits)
    m2 = jnp.max(rest, axis=-1, keepdims=True)
    i2 = jnp.min(jnp.where(rest == m2, lane, float(ROUTE_LANES)), axis=-1, keepdims=True)
    e2 = jnp.exp(m2 - m1)
    g1 = 1.0 / (1.0 + e2)
    g2 = e2 / (1.0 + e2)
    onehot = jnp.where((lane == i1) | (lane == i2), 1.0, 0.0)
    before = _dot(ltri_ref[...], onehot.astype(BF16)) + count_ref[...]
    r1 = jnp.sum(jnp.where(lane == i1, before, 0.0), axis=-1, keepdims=True)
    r2 = jnp.sum(jnp.where(lane == i2, before, 0.0), axis=-1, keepdims=True)
    count_ref[...] += jnp.sum(onehot, axis=0, keepdims=True)
    out = jnp.where(lane == 0, i1, 0.0)
    out = jnp.where(lane == 1, i2, out)
    out = jnp.where(lane == 2, g1, out)
    out = jnp.where(lane == 3, g2, out)
    out = jnp.where(lane == 4, r1, out)
    return jnp.where(lane == 5, r2, out)


def _out_proj_kernel(mixed_ref, h_ref, w_ref, g_ref, *rest, routed):
    if routed:
        router_ref, ltri_ref, h1_ref, hn_ref, route_ref, count_ref = rest

        @pl.when(pl.program_id(0) == 0)
        def _():
            count_ref[...] = jnp.zeros_like(count_ref)
    else:
        h1_ref, hn_ref = rest
    h1 = h_ref[...] + _dot(mixed_ref[...], w_ref[...])
    h1_ref[...] = h1
    hn = h1 * lax.rsqrt(jnp.mean(h1 * h1, axis=-1, keepdims=True) + EPS) * g_ref[...]
    hn_ref[...] = hn.astype(hn_ref.dtype)
    if routed:
        route_ref[...] = _route(hn, router_ref, ltri_ref, count_ref)


def _out_proj(mixed, h, w_bf16, g, router_pad=None):
    n, d = h.shape
    tm = _pick_tile(n, (640, 128))
    routed = router_pad is not None
    row = lambda i: (i, 0)
    fixed = lambda i: (0, 0)
    in_specs = [pl.BlockSpec((tm, D_MIX), row), pl.BlockSpec((tm, d), row),
                pl.BlockSpec(w_bf16.shape, fixed), pl.BlockSpec((1, d), fixed)]
    out_specs = [pl.BlockSpec((tm, d), row), pl.BlockSpec((tm, d), row)]
    out_shape = [jax.ShapeDtypeStruct((n, d), F32), jax.ShapeDtypeStruct((n, d), F32 if routed else BF16)]
    args = [mixed, h, w_bf16, g]
    if routed:
        ltri = jnp.asarray(np.tril(np.ones((tm, tm), np.float32), -1), BF16)
        in_specs += [pl.BlockSpec(router_pad.shape, fixed), pl.BlockSpec((tm, tm), fixed)]
        out_specs += [pl.BlockSpec((tm, ROUTE_LANES), row), pl.BlockSpec((1, ROUTE_LANES), fixed)]
        out_shape += [jax.ShapeDtypeStruct((n, ROUTE_LANES), F32),
                      jax.ShapeDtypeStruct((1, ROUTE_LANES), F32)]
        args += [router_pad, ltri]
    return pl.pallas_call(
        functools.partial(_out_proj_kernel, routed=routed),
        grid=(n // tm,),
        in_specs=in_specs,
        out_specs=out_specs,
        out_shape=out_shape,
        compiler_params=pltpu.CompilerParams(
            dimension_semantics=("arbitrary",), vmem_limit_bytes=VMEM_LIMIT),
        name="out_proj_routed" if routed else "out_proj",
    )(*args)


def _ffn_kernel(hn_ref, h1_ref, w1_ref, w3_ref, w2_ref, o_ref, acc_ref):
    f = pl.program_id(1)
    x = hn_ref[...]
    a = _dot(x, w1_ref[...])
    b = _dot(x, w3_ref[...])
    part = _dot((a * _sigmoid(a) * b).astype(BF16), w2_ref[...])

    @pl.when(f == 0)
    def _():
        acc_ref[...] = part

    @pl.when(f > 0)
    def _():
        acc_ref[...] += part

    @pl.when(f == pl.num_programs(1) - 1)
    def _():
        o_ref[...] = h1_ref[...] + acc_ref[...]


def _ffn(hn, h1, w1, w3, w2):
    n, d = h1.shape
    dff = w1.shape[1]
    tm, tf = _pick_tile(n, (640, 128)), _pick_tile(dff, (1408, 128))
    return pl.pallas_call(
        _ffn_kernel,
        grid=(n // tm, dff // tf),
        in_specs=[
            pl.BlockSpec((tm, d), lambda i, f: (i, 0)),
            pl.BlockSpec((tm, d), lambda i, f: (i, 0)),
            pl.BlockSpec((d, tf), lambda i, f: (0, f)),
            pl.BlockSpec((d, tf), lambda i, f: (0, f)),
            pl.BlockSpec((tf, d), lambda i, f: (f, 0)),
        ],
        out_specs=pl.BlockSpec((tm, d), lambda i, f: (i, 0)),
        out_shape=jax.ShapeDtypeStruct((n, d), F32),
        scratch_shapes=[pltpu.VMEM((tm, d), F32)],
        compiler_params=pltpu.CompilerParams(
            dimension_semantics=("arbitrary", "arbitrary"), vmem_limit_bytes=VMEM_LIMIT),
        name="ffn_dense",
    )(hn, h1, w1, w3, w2)


def _moe_kernel(te_ref, nv_ref, src_ref, tok_ref, hn_hbm, w1_ref, w3_ref, w2_ref, o_ref,
                xs_ref, xb_ref, acc_ref, sem, *, tm):
    i, f = pl.program_id(0), pl.program_id(1)
    nf = pl.num_programs(1)
    n_valid = nv_ref[0]
    share = tm // MOE_F_STEPS

    def row_copy(tile, r):
        return pltpu.make_async_copy(hn_hbm.at[pl.ds(tok_ref[src_ref[tile] + r], 1)],
                                     xs_ref.at[pl.ds(r, 1)], sem)

    @pl.when((i == 0) & (f == 0))
    def _():
        def body(r, carry):
            row_copy(0, r).start()
            return carry
        lax.fori_loop(0, tm, body, 0)

    @pl.when((f == 0) & (i <= n_valid))
    def _():
        pltpu.make_async_copy(xs_ref, xs_ref, sem).wait()
        xb_ref[...] = xs_ref[...].astype(BF16)

    @pl.when(i < n_valid)
    def _():
        for r in range(share):
            row_copy(i + 1, f * share + r).start(priority=r % MOE_DMA_QUEUES)
        x = xb_ref[...]
        a = _dot(x, w1_ref[0])
        b = _dot(x, w3_ref[0])
        part = _dot((a * _sigmoid(a) * b).astype(BF16), w2_ref[0])

        @pl.when(f == 0)
        def _():
            acc_ref[...] = part

        @pl.when(f > 0)
        def _():
            acc_ref[...] += part

    @pl.when(f == nf - 1)
    def _():
        o_ref[...] = jnp.where(i < n_valid, acc_ref[...], 0.0).astype(o_ref.dtype)


def _live_f(i, f, nv):
    return jnp.where(i < nv[0], f, 0)


def _moe_grouped(tile_expert, n_valid, tile_src, sorted_tok, hn, w1, w3, w2, tm):
    d = hn.shape[1]
    n_tiles = tile_expert.shape[0]
    dff = w1.shape[2]
    tf = dff // MOE_F_STEPS
    grid_spec = pltpu.PrefetchScalarGridSpec(
        num_scalar_prefetch=4,
        grid=(n_tiles, MOE_F_STEPS),
        in_specs=[
            pl.BlockSpec(memory_space=pl.ANY),
            pl.BlockSpec((1, d, tf), lambda i, f, te, nv, *_: (te[i], 0, _live_f(i, f, nv))),
            pl.BlockSpec((1, d, tf), lambda i, f, te, nv, *_: (te[i], 0, _live_f(i, f, nv))),
            pl.BlockSpec((1, tf, d), lambda i, f, te, nv, *_: (te[i], _live_f(i, f, nv), 0)),
        ],
        out_specs=pl.BlockSpec((tm, d), lambda i, f, *_: (i, 0)),
        scratch_shapes=[pltpu.VMEM((tm, d), F32), pltpu.VMEM((tm, d), BF16), pltpu.VMEM((tm, d), F32),
                        pltpu.SemaphoreType.DMA(())],
    )
    return pl.pallas_call(
        functools.partial(_moe_kernel, tm=tm),
        grid_spec=grid_spec,
        out_shape=jax.ShapeDtypeStruct((n_tiles * tm, d), BF16),
        compiler_params=pltpu.CompilerParams(
            dimension_semantics=("arbitrary", "arbitrary"), vmem_limit_bytes=VMEM_LIMIT),
        name="moe_grouped",
    )(tile_expert, n_valid, tile_src, sorted_tok, hn, w1, w3, w2)


def _moe_dispatch(route, counts_row, tm):
    n = route.shape[0]
    n_slots = n * TOP_K
    n_tiles = (n_slots + N_EXPERTS * tm + tm - 1) // tm
    counts = counts_row[0, :N_EXPERTS].astype(jnp.int32)
    padded = ((counts + tm - 1) // tm) * tm
    pend = jnp.cumsum(padded)
    pstart = pend - padded
    gstart = jnp.cumsum(counts) - counts
    eid = route[:, 0:TOP_K].astype(jnp.int32)
    rank = route[:, 2 * TOP_K:3 * TOP_K].astype(jnp.int32)
    pos = pstart[eid] + rank
    tok = jnp.broadcast_to(jnp.arange(n, dtype=jnp.int32)[:, None], (n, TOP_K))
    _, sorted_tok = lax.sort(((gstart[eid] + rank).reshape(-1), tok.reshape(-1)), num_keys=1)
    sorted_tok = jnp.concatenate([sorted_tok, jnp.zeros((tm,), jnp.int32)])
    tile_start = jnp.arange(n_tiles, dtype=jnp.int32) * tm
    tile_expert = jnp.minimum(jnp.sum((tile_start[:, None] >= pend[None, :]).astype(jnp.int32), axis=1),
                              N_EXPERTS - 1)
    tile_src = jnp.clip(tile_start - (pstart - gstart)[tile_expert], 0, n_slots)
    n_valid = (pend[-1] // tm).astype(jnp.int32).reshape(1)
    return sorted_tok, tile_src, pos, tile_expert, n_valid


def _final_kernel(h1_ref, y0_ref, y1_ref, route_ref, g_ref, o_ref):
    g0 = route_ref[:, TOP_K:TOP_K + 1]
    g1 = route_ref[:, TOP_K + 1:TOP_K + 2]
    h = h1_ref[...] + (g0 * y0_ref[...].astype(F32) + g1 * y1_ref[...].astype(F32))
    o_ref[...] = h * lax.rsqrt(jnp.mean(h * h, axis=-1, keepdims=True) + EPS) * g_ref[...]


def _final(h1, y0, y1, route, g):
    n, d = h1.shape
    tm = _pick_tile(n, (640, 128))
    row = lambda i: (i, 0)
    return pl.pallas_call(
        _final_kernel,
        grid=(n // tm,),
        in_specs=[pl.BlockSpec((tm, d), row), pl.BlockSpec((tm, d), row), pl.BlockSpec((tm, d), row),
                  pl.BlockSpec((tm, ROUTE_LANES), row), pl.BlockSpec((1, d), lambda i: (0, 0))],
        out_specs=pl.BlockSpec((tm, d), row),
        out_shape=jax.ShapeDtypeStruct((n, d), F32),
        compiler_params=pltpu.CompilerParams(
            dimension_semantics=("arbitrary",), vmem_limit_bytes=VMEM_LIMIT),
        name="final_norm",
    )(h1, y0, y1, route, g)


def _rope_perm():
    half = RET_DK // 2
    idx = []
    for part in range(2):
        for h in range(RET_HEADS):
            for dd in range(half):
                idx.append(h * RET_DK + part * half + dd)
    return np.asarray(idx, np.int32)


def _in_col_perm():
    cols = np.arange(IN_COLS, dtype=np.int32)
    rp = _rope_perm()
    cols[C_RQ:C_RQ + RET_KEY_WIDTH] = C_RQ + rp
    cols[C_RK:C_RK + RET_KEY_WIDTH] = C_RK + rp
    return cols


def _hg_tables():
    c = CHUNK
    i = np.arange(c)[:, None]
    t = np.arange(c)[None, :]
    blocks, masks, rsel = [], [], []
    for lvl in range(HG_LEVELS):
        s = 1 << lvl
        r = (i // (2 * s)) * (2 * s) + s - 1
        right = i > r
        m = np.where(right, (t > r) & (t <= i), (t > i) & (t <= r))
        blocks.append(m)
        j = t
        same = (i // (2 * s)) == (j // (2 * s))
        mk = same & right & (j <= r)
        masks.append(np.concatenate([mk, mk], axis=0))
        if s < 8:
            rsel.append(np.broadcast_to(right, (c, HG_WIDTH)))
    blocks.append(t <= i)
    blocks.append(t > i)
    mall = np.concatenate(blocks, axis=0).astype(np.float32)
    lvl = np.stack(masks, axis=0).astype(np.float32)
    ch = np.arange(HG_WIDTH)
    bones = (ch[:, None] // HG_DK == ch[None, :] // HG_DK).astype(np.float32)
    return mall, lvl, np.stack(rsel, axis=0).astype(np.float32), bones


def _ret_tables():
    f32 = jnp.float32
    log_gamma = jnp.log1p(-jnp.power(2.0, -5.0 - jnp.arange(RET_HEADS, dtype=f32)))
    n = jnp.arange(CHUNK, dtype=f32)
    lg = log_gamma[:, None]
    causal = jnp.tril(jnp.ones((CHUNK, CHUNK), dtype=bool))
    intra = jnp.exp(jnp.where(causal[None], (n[:, None] - n[None, :])[None] * lg[:, :, None], -jnp.inf))
    scale = RET_DK ** -0.5
    inter = jnp.exp((n[None, :] + 1.0) * lg)
    to_state = jnp.exp((CHUNK - 1.0 - n[None, :]) * lg)
    carry = jnp.exp(CHUNK * lg)[:, 0]
    head_of_v = np.arange(RET_WIDTH) // RET_DV
    head_of_k = (np.arange(RET_KEY_WIDTH) % (RET_KEY_WIDTH // 2)) // (RET_DK // 2)
    inter_t = (inter * scale).T[:, head_of_v]
    to_state_t = to_state.T[:, head_of_k]
    carry_row = carry[head_of_v][None, :]
    bmask = jnp.asarray((head_of_k[:, None] == head_of_v[None, :]).astype(np.float32))
    intra_pairs = (intra * scale).reshape(RET_HEADS // 2, 2 * CHUNK, CHUNK)
    return intra_pairs, inter_t, to_state_t, carry_row, bmask


def _s5_tables(lam_re, lam_im, b_re, b_im, c_re, c_im, d_skip, log_step):
    f32 = jnp.float32
    lam = lax.complex(lam_re.astype(f32), lam_im.astype(f32))
    step = jnp.exp(log_step.astype(f32))[:, None]
    lam_dt = lam * step
    lam_bar = jnp.exp(lam_dt)
    b_bar = ((lam_bar - 1.0) / lam)[..., None] * lax.complex(b_re.astype(f32), b_im.astype(f32))
    eye = jnp.eye(S5_NGROUPS, dtype=f32)
    wb_re = jnp.einsum('gph,gk->ghkp', jnp.real(b_bar), eye).reshape(S5_WIDTH, S5_NSTATE)
    wb_im = jnp.einsum('gph,gk->ghkp', jnp.imag(b_bar), eye).reshape(S5_WIDTH, S5_NSTATE)
    wb = jnp.concatenate([wb_re, wb_im], axis=1)
    wc_re = jnp.einsum('ghp,gk->gpkh', c_re.astype(f32), eye).reshape(S5_NSTATE, S5_WIDTH)
    wc_im = jnp.einsum('ghp,gk->gpkh', c_im.astype(f32), eye).reshape(S5_NSTATE, S5_WIDTH)
    wc = jnp.concatenate([wc_re, -wc_im], axis=0)
    t = jnp.arange(CHUNK, dtype=f32)[:, None, None]
    pp = jnp.exp(lam_dt[None] * t).reshape(CHUNK, S5_NSTATE)
    pn = jnp.exp(-lam_dt[None] * t).reshape(CHUNK, S5_NSTATE)
    lam_rows = jnp.stack([jnp.real(lam_bar).reshape(-1), jnp.imag(lam_bar).reshape(-1)], axis=0)
    return (wb.astype(BF16), wc.astype(BF16), jnp.real(pn), jnp.imag(pn), jnp.real(pp), jnp.imag(pp),
            lam_rows, d_skip.astype(f32).reshape(1, S5_WIDTH))


def kernel(x, meta_tokens, norm_mix_g, w_in, s5_lam_re, s5_lam_im, s5_b_re, s5_b_im, s5_c_re, s5_c_im, s5_d, s5_log_step, s5_w_glu, s5_out_g, hg_conv_w, hg_lb_param, hg_out_g, ret_out_g, w_out, norm_ffn_g, ffn_w1, ffn_w3, ffn_w2, moe_router, moe_w1, moe_w3, moe_w2, final_norm_g):
    bsz, seq_len, d = x.shape
    depth = w_in.shape[0]
    total = seq_len + CHUNK
    n_chunks = total // CHUNK
    n = bsz * total

    meta = jnp.broadcast_to(meta_tokens.astype(F32)[None], (bsz, N_META, d))
    h = jnp.concatenate([jnp.zeros((bsz, PAD, d), F32), meta, x.astype(F32)], axis=1).reshape(n, d)

    pos = (jnp.arange(total) - PAD).astype(F32)
    half = RET_DK // 2
    inv_freq = ROPE_BASE ** (-jnp.arange(half, dtype=F32) / half)
    ang = pos[:, None] * inv_freq[None, :]
    cos_t = jnp.tile(jnp.cos(ang), (1, RET_HEADS))
    sin_t = jnp.tile(jnp.sin(ang), (1, RET_HEADS))

    lb_all = jnp.cumsum(jax.nn.softmax(hg_lb_param.astype(F32), axis=0), axis=0)
    lb_all = lb_all - lb_all[0]

    mall_np, lvl_np, rsel_np, bones_np = _hg_tables()
    mall = jnp.asarray(mall_np, BF16)
    lvl = jnp.asarray(lvl_np, F32)
    rsel = jnp.asarray(rsel_np, F32)
    bones = jnp.asarray(bones_np, BF16)
    avg = jnp.asarray(bones_np / HG_DK, BF16)
    ltri = jnp.asarray(np.tril(np.ones((CHUNK, CHUNK), np.float32)), BF16)
    intra, inter_t, to_state_t, carry_row, bmask = _ret_tables()
    col_perm = jnp.asarray(_in_col_perm())

    out = None
    for l in range(depth):
        w_in_l = jnp.take(w_in[l], col_perm, axis=1).astype(BF16)
        proj = _in_proj(h, norm_mix_g[l].astype(F32).reshape(1, d), w_in_l, total)

        s5c = _s5_tables(s5_lam_re[l], s5_lam_im[l], s5_b_re[l], s5_b_im[l], s5_c_re[l], s5_c_im[l],
                         s5_d[l], s5_log_step[l])
        lb = lb_all[l][None, :]
        lbv = jnp.concatenate([jnp.log(lb), jnp.log1p(-lb), 1.0 - lb], axis=0)
        consts = list(s5c) + [
            s5_w_glu[l].astype(BF16), s5_out_g[l].astype(F32).reshape(1, -1), ltri,
            hg_conv_w[l].astype(F32), lbv, mall, lvl, rsel, bones, avg, hg_out_g[l].astype(F32).reshape(1, -1),
            intra, inter_t, to_state_t, carry_row, bmask, ret_out_g[l].astype(F32).reshape(1, -1),
        ]
        mixed = _mixers(proj.reshape(bsz, total, IN_COLS), cos_t, sin_t, consts, bsz, n_chunks)
        mixed = mixed.reshape(n, D_MIX)

        g_ffn = norm_ffn_g[l].astype(F32).reshape(1, d)
        w_out_l = w_out[l].astype(BF16)
        if l % 2 == 0:
            h1, hn = _out_proj(mixed, h, w_out_l, g_ffn)
            j = l // 2
            h = _ffn(hn, h1, ffn_w1[j].astype(BF16), ffn_w3[j].astype(BF16), ffn_w2[j].astype(BF16))
            y0 = y1 = None
        else:
            j = l // 2
            router_pad = jnp.zeros((d, ROUTE_LANES), F32).at[:, :N_EXPERTS].set(moe_router[j].astype(F32))
            h1, hn, route, counts_row = _out_proj(mixed, h, w_out_l, g_ffn, router_pad)
            tm = MOE_TILE
            sorted_tok, tile_src, pos_of_slot, tile_expert, n_valid = _moe_dispatch(route, counts_row, tm)
            ys = _moe_grouped(tile_expert, n_valid, tile_src, sorted_tok, hn,
                              moe_w1[j].astype(BF16), moe_w3[j].astype(BF16), moe_w2[j].astype(BF16), tm)
            y0 = jnp.take(ys, pos_of_slot[:, 0], axis=0)
            y1 = jnp.take(ys, pos_of_slot[:, 1], axis=0)
            if l < depth - 1:
                h = h1 + (route[:, TOP_K:TOP_K + 1] * y0.astype(F32)
                          + route[:, TOP_K + 1:TOP_K + 2] * y1.astype(F32))

        if l == depth - 1:
            if y0 is None:
                y0 = y1 = jnp.zeros((n, d), BF16)
                h1 = h
                route = jnp.zeros((n, ROUTE_LANES), F32)
            out = _final(h1, y0, y1, route, final_norm_g.astype(F32).reshape(1, d))

    return out.reshape(bsz, total, d)[:, CHUNK:, :].astype(x.dtype)
```

```python
import functools
import math

import jax
import jax.numpy as jnp
import numpy as np
from jax import lax
from jax.experimental import pallas as pl
from jax.experimental.pallas import tpu as pltpu

F32 = jnp.float32
BF16 = jnp.bfloat16

CHUNK = 128
N_META = 16
PAD = CHUNK - N_META
EPS = 1e-6

S5_WIDTH = 256
S5_GROUP = 16
S5_NGROUPS = 16
S5_STATE = 64
S5_NSTATE = S5_NGROUPS * S5_STATE

HG_HEADS = 4
HG_DK = 64
HG_WIDTH = 256
CONV_K = 4
HG_CONV_W = 3 * HG_WIDTH
HG_LEVELS = 7

RET_HEADS = 8
RET_DK = 32
RET_DV = 64
RET_KEY_WIDTH = 256
RET_WIDTH = 512
ROPE_BASE = 10000.0

D_MIX = 1024
IN_COLS = 2816
C_U, C_HQ, C_HGATE, C_RQ, C_RK, C_RV, C_RGATE = 0, 256, 1024, 1280, 1536, 1792, 2304

N_EXPERTS = 8
TOP_K = 2
ROUTE_LANES = 128
MOE_F_STEPS = 2
MOE_TILE = 512
MOE_DMA_QUEUES = 2

VMEM_LIMIT = 56 * 1024 * 1024


def _sigmoid(x):
    return 1.0 / (1.0 + jnp.exp(-x))


def _split_bf16(x):
    hi = x.astype(BF16)
    lo = (x - hi.astype(F32)).astype(BF16)
    return hi, lo


def _pick_tile(n, candidates):
    for t in candidates:
        if n % t == 0:
            return t
    raise ValueError(f"no tile in {candidates} divides {n}")


def _dot(a, b):
    return jnp.dot(a, b, preferred_element_type=F32)


def _dot_nt(a, b):
    return lax.dot_general(a, b, (((1,), (1,)), ((), ())), preferred_element_type=F32)


def _dot_tn(a, b):
    return lax.dot_general(a, b, (((0,), (0,)), ((), ())), preferred_element_type=F32)


def _in_proj_kernel(h_ref, g_ref, w_ref, o_ref, *, tm, tiles_per_batch):
    x = h_ref[...]
    y = x * lax.rsqrt(jnp.mean(x * x, axis=-1, keepdims=True) + EPS) * g_ref[...]
    proj = _dot(y.astype(BF16), w_ref[...])
    row0 = (pl.program_id(0) % tiles_per_batch) * tm
    rows = row0 + lax.broadcasted_iota(jnp.int32, (tm, 1), 0)
    o_ref[...] = jnp.where(rows >= PAD, proj, 0.0)


def _in_proj(h, g, w_bf16, total):
    n = h.shape[0]
    tm = _pick_tile(total, (320, 128))
    return pl.pallas_call(
        functools.partial(_in_proj_kernel, tm=tm, tiles_per_batch=total // tm),
        grid=(n // tm,),
        in_specs=[
            pl.BlockSpec((tm, h.shape[1]), lambda i: (i, 0)),
            pl.BlockSpec((1, h.shape[1]), lambda i: (0, 0)),
            pl.BlockSpec(w_bf16.shape, lambda i: (0, 0)),
        ],
        out_specs=pl.BlockSpec((tm, IN_COLS), lambda i: (i, 0)),
        out_shape=jax.ShapeDtypeStruct((n, IN_COLS), F32),
        compiler_params=pltpu.CompilerParams(
            dimension_semantics=("arbitrary",), vmem_limit_bytes=VMEM_LIMIT),
        name="in_proj",
    )(h, g, w_bf16)


def _rows(parts):
    return parts[0] if len(parts) == 1 else jnp.concatenate(parts, axis=0)


def _s5_chunks(u, wb_ref, wc_ref, pn_re_ref, pn_im_ref, pp_re_ref, pp_im_ref, lam_ref,
               d_ref, wglu_ref, g_ref, ltri_ref, st_ref):
    ns = S5_NSTATE
    nb = st_ref.shape[0]
    bu = _dot(u.astype(BF16), wb_ref[...])
    pn_re, pn_im = pn_re_ref[...], pn_im_ref[...]
    w_re, w_im = [], []
    for b in range(nb):
        bu_re = bu[b * CHUNK:(b + 1) * CHUNK, :ns]
        bu_im = bu[b * CHUNK:(b + 1) * CHUNK, ns:]
        w_re.append(pn_re * bu_re - pn_im * bu_im)
        w_im.append(pn_re * bu_im + pn_im * bu_re)
    ltri = ltri_ref[...]
    c_re = _dot(ltri, _rows(w_re).astype(BF16))
    c_im = _dot(ltri, _rows(w_im).astype(BF16))
    lam_re, lam_im = lam_ref[0:1, :], lam_ref[1:2, :]
    pp_re, pp_im = pp_re_ref[...], pp_im_ref[...]
    st_re, st_im = [], []
    for b in range(nb):
        s_re, s_im = st_ref[b, 0:1, :], st_ref[b, 1:2, :]
        z_re = c_re[b * CHUNK:(b + 1) * CHUNK] + (lam_re * s_re - lam_im * s_im)
        z_im = c_im[b * CHUNK:(b + 1) * CHUNK] + (lam_re * s_im + lam_im * s_re)
        t_re = pp_re * z_re - pp_im * z_im
        t_im = pp_re * z_im + pp_im * z_re
        st_ref[b, 0:1, :] = t_re[CHUNK - 1:CHUNK, :]
        st_ref[b, 1:2, :] = t_im[CHUNK - 1:CHUNK, :]
        st_re.append(t_re)
        st_im.append(t_im)
    y = (_dot(_rows(st_re).astype(BF16), wc_ref[0:ns, :]) + _dot(_rows(st_im).astype(BF16), wc_ref[ns:2 * ns, :])
         + d_ref[...] * u)
    y = 0.5 * y * (1.0 + jnp.tanh(math.sqrt(2.0 / math.pi) * (y + 0.044715 * (y * y * y))))
    y = y * _sigmoid(_dot(y.astype(BF16), wglu_ref[...]))
    return y * lax.rsqrt(jnp.mean(y * y, axis=-1, keepdims=True) + EPS) * g_ref[...]


def _hgrn2_chunks(xc_ref, gate, convw_ref, lbv_ref, mall_ref, lvl_ref, rsel_ref, bones_ref, avg_ref,
                  g_ref, st_ref):
    w = HG_WIDTH
    nb = st_ref.shape[0]
    rows = nb * CHUNK
    convs = []
    for b in range(nb):
        conv = None
        for i in range(CONV_K):
            term = xc_ref[b, pl.ds(8 - (CONV_K - 1) + i, CHUNK), :] * convw_ref[i:i + 1, :]
            conv = term if conv is None else conv + term
        convs.append(conv)
    conv = _rows(convs)
    cq, cf, v = conv[:, :w], conv[:, w:2 * w], conv[:, 2 * w:]
    q = cq * _sigmoid(cq)
    log_lb, log_1m_lb, one_m_lb = lbv_ref[0:1, :], lbv_ref[1:2, :], lbv_ref[2:3, :]
    log_sig = jnp.minimum(cf, 0.0) - jnp.log(1.0 + jnp.exp(-jnp.abs(cf)))
    b_ = log_1m_lb + log_sig
    logf = jnp.maximum(log_lb, b_) + jnp.log(1.0 + jnp.exp(-jnp.abs(log_lb - b_)))
    kk = one_m_lb * _sigmoid(-cf)
    hi, lo = _split_bf16(logf)

    lvl_sums = _dot(mall_ref[0:HG_LEVELS * rows, :], hi)
    tail = mall_ref[HG_LEVELS * rows:(HG_LEVELS + 2) * rows, :]
    cum_suf = _dot(tail, hi) + _dot(tail, lo)
    g_cum, g_suffix = cum_suf[:rows], cum_suf[rows:]

    lane = lax.broadcasted_iota(jnp.int32, (1, 2 * HG_DK), 1)
    head_sel = [jnp.where(lane < HG_DK, 1.0, 0.0), jnp.where(lane >= HG_DK, 1.0, 0.0)]
    n_pairs = HG_HEADS // 2
    scores = [[None] * n_pairs for _ in range(nb)]
    for lvl in range(HG_LEVELS):
        s = 1 << lvl
        e = jnp.exp(lvl_sums[lvl * rows:(lvl + 1) * rows])
        if s >= 8:
            qk = jnp.concatenate([(q if (r // s) % 2 else kk)[r:r + s] for r in range(0, rows, s)], axis=0)
        else:
            qk = jnp.where(rsel_ref[lvl] > 0.5, q, kk)
        x = qk * e
        m = lvl_ref[lvl]
        for b in range(nb):
            for p in range(n_pairs):
                xp = x[b * CHUNK:(b + 1) * CHUNK, p * 128:(p + 1) * 128]
                rhs = jnp.concatenate([xp * head_sel[0], xp * head_sel[1]], axis=0).astype(BF16)
                sc = _dot_nt(xp.astype(BF16), rhs) * m
                scores[b][p] = sc if scores[b][p] is None else scores[b][p] + sc
    v_bf = v.astype(BF16)
    bones = bones_ref[...]
    qg = (q * jnp.exp(g_cum)).astype(BF16)
    kd = (kk * jnp.exp(g_suffix)).astype(BF16)
    o_rows = []
    for b in range(nb):
        sl = slice(b * CHUNK, (b + 1) * CHUNK)
        o_parts = []
        for p in range(n_pairs):
            vp = v[sl, p * 128:(p + 1) * 128]
            vv = jnp.concatenate([vp * head_sel[0], vp * head_sel[1]], axis=0).astype(BF16)
            o_parts.append(_dot(scores[b][p].astype(BF16), vv))
        st = st_ref[b]
        o_rows.append(jnp.concatenate(o_parts, axis=1) + _dot_nt(qg[sl], st.astype(BF16)))
        upd = _dot_tn(v_bf[sl], kd[sl]) * bones.astype(F32)
        st_ref[b] = st * jnp.exp(g_cum[(b + 1) * CHUNK - 1:(b + 1) * CHUNK, :]) + upd
    o = _rows(o_rows) + _dot((q * kk).astype(BF16), bones) * v
    ms = _dot((o * o).astype(BF16), avg_ref[...])
    return o * lax.rsqrt(ms + EPS) * g_ref[...] * (gate * _sigmoid(gate))


def _ret_chunk(rq, rk, v, cos, sin, intra_ref, inter_ref, tostate_ref, carry_ref, bmask_ref, st_ref):
    hw = RET_KEY_WIDTH // 2

    def rot(t):
        t1, t2 = t[:, :hw], t[:, hw:]
        return jnp.concatenate([t1 * cos - t2 * sin, t1 * sin + t2 * cos], axis=1)

    qr, kr = rot(rq), rot(rk)
    qr_bf, v_bf = qr.astype(BF16), v.astype(BF16)
    lane_k = lax.broadcasted_iota(jnp.int32, (1, RET_KEY_WIDTH), 1) % hw
    lane_v = lax.broadcasted_iota(jnp.int32, (1, 2 * RET_DV), 1)
    v_sel = [jnp.where(lane_v < RET_DV, 1.0, 0.0), jnp.where(lane_v >= RET_DV, 1.0, 0.0)]
    half = RET_DK // 2
    o_parts = []
    for p in range(RET_HEADS // 2):
        sel = [jnp.where((lane_k >= h * half) & (lane_k < (h + 1) * half), 1.0, 0.0)
               for h in (2 * p, 2 * p + 1)]
        rhs = jnp.concatenate([kr * sel[0], kr * sel[1]], axis=0).astype(BF16)
        sc = (_dot_nt(qr_bf, rhs) * intra_ref[p]).astype(BF16)
        vp = v[:, p * 128:(p + 1) * 128]
        vv = jnp.concatenate([vp * v_sel[0], vp * v_sel[1]], axis=0).astype(BF16)
        o_parts.append(_dot(sc, vv))
    o = jnp.concatenate(o_parts, axis=1)
    st = st_ref[...]
    o = o + _dot(qr_bf, st.astype(BF16)) * inter_ref[...]
    kd = (kr * tostate_ref[...]).astype(BF16)
    st_ref[...] = st * carry_ref[...] + _dot_tn(kd, v_bf) * bmask_ref[...]
    return o


def _ret_norm_gate(o, gate, avg_ref, g_ref):
    avg = avg_ref[...]
    outs = []
    for s in range(RET_WIDTH // 256):
        os_ = o[:, s * 256:(s + 1) * 256]
        c = os_ - _dot(os_.astype(BF16), avg)
        outs.append(c * lax.rsqrt(_dot((c * c).astype(BF16), avg) + EPS))
    return jnp.concatenate(outs, axis=1) * g_ref[...] * (gate * _sigmoid(gate))


def _mixer_kernel(proj_ref, cos_ref, sin_ref,
                  wb_ref, wc_ref, pn_re_ref, pn_im_ref, pp_re_ref, pp_im_ref, lam_ref, d_ref,
                  wglu_ref, s5g_ref, ltri_ref,
                  convw_ref, lbv_ref, mall_ref, lvl_ref, rsel_ref, bones_ref, avg_ref, hgg_ref,
                  intra_ref, inter_ref, tostate_ref, carry_ref, bmask_ref, retg_ref,
                  o_ref,
                  s5_st, hg_xc, hg_st, ret_st):
    @pl.when(pl.program_id(0) == 0)
    def _():
        s5_st[...] = jnp.zeros_like(s5_st)
        hg_xc[...] = jnp.zeros_like(hg_xc)
        hg_st[...] = jnp.zeros_like(hg_st)
        ret_st[...] = jnp.zeros_like(ret_st)

    nb = proj_ref.shape[0]

    def cols(c0, width):
        return _rows([proj_ref[b, :, c0:c0 + width] for b in range(nb)])

    def emit(c0, y):
        for b in range(nb):
            o_ref[b, :, c0:c0 + y.shape[1]] = y[b * CHUNK:(b + 1) * CHUNK].astype(o_ref.dtype)

    y_a = _s5_chunks(cols(C_U, S5_WIDTH), wb_ref, wc_ref, pn_re_ref, pn_im_ref, pp_re_ref, pp_im_ref,
                     lam_ref, d_ref, wglu_ref, s5g_ref, ltri_ref, s5_st)
    emit(0, y_a)

    for b in range(nb):
        hg_xc[b, 8:8 + CHUNK, :] = proj_ref[b, :, C_HQ:C_HQ + HG_CONV_W]
    y_b = _hgrn2_chunks(hg_xc, cols(C_HGATE, HG_WIDTH), convw_ref, lbv_ref,
                        mall_ref, lvl_ref, rsel_ref, bones_ref, avg_ref, hgg_ref, hg_st)
    for b in range(nb):
        hg_xc[b, 0:8, :] = hg_xc[b, CHUNK:CHUNK + 8, :]
    emit(S5_WIDTH, y_b)

    o_c = [_ret_chunk(proj_ref[b, :, C_RQ:C_RQ + RET_KEY_WIDTH], proj_ref[b, :, C_RK:C_RK + RET_KEY_WIDTH],
                      proj_ref[b, :, C_RV:C_RV + RET_WIDTH], cos_ref[...], sin_ref[...],
                      intra_ref, inter_ref, tostate_ref, carry_ref, bmask_ref, ret_st.at[b])
           for b in range(nb)]
    y_c = _ret_norm_gate(_rows(o_c), cols(C_RGATE, RET_WIDTH), avg_ref, retg_ref)
    emit(S5_WIDTH + HG_WIDTH, y_c)


def _const_spec(a):
    nd = a.ndim
    return pl.BlockSpec(a.shape, lambda c, _nd=nd: (0,) * _nd)


def _mixers(proj, cos, sin, consts, bsz, n_chunks):
    total = proj.shape[1]
    in_specs = [
        pl.BlockSpec((bsz, CHUNK, IN_COLS), lambda c: (0, c, 0)),
        pl.BlockSpec((CHUNK, 128), lambda c: (c, 0)),
        pl.BlockSpec((CHUNK, 128), lambda c: (c, 0)),
    ] + [_const_spec(a) for a in consts]
    return pl.pallas_call(
        _mixer_kernel,
        grid=(n_chunks,),
        in_specs=in_specs,
        out_specs=pl.BlockSpec((bsz, CHUNK, D_MIX), lambda c: (0, c, 0)),
        out_shape=jax.ShapeDtypeStruct((bsz, total, D_MIX), BF16),
        scratch_shapes=[
            pltpu.VMEM((bsz, 2, S5_NSTATE), F32),
            pltpu.VMEM((bsz, CHUNK + 8, HG_CONV_W), F32),
            pltpu.VMEM((bsz, HG_WIDTH, HG_WIDTH), F32),
            pltpu.VMEM((bsz, RET_KEY_WIDTH, RET_WIDTH), F32),
        ],
        compiler_params=pltpu.CompilerParams(
            dimension_semantics=("arbitrary",), vmem_limit_bytes=VMEM_LIMIT),
        name="mixers",
    )(proj, cos, sin, *consts)


def _route(hn, router_ref, ltri_ref, count_ref):
    r = router_ref[...]
    r_hi, r_lo = _split_bf16(r)
    h_hi, h_lo = _split_bf16(hn)
    logits = _dot(h_hi, r_hi) + (_dot(h_lo, r_hi) + _dot(h_hi, r_lo))
    lane_i = lax.broadcasted_iota(jnp.int32, logits.shape, 1)
    lane = lane_i.astype(F32)
    neg = jnp.float32(-jnp.inf)
    logits = jnp.where(lane_i < N_EXPERTS, logits, neg)
    m1 = jnp.max(logits, axis=-1, keepdims=True)
    i1 = jnp.min(jnp.where(logits == m1, lane, float(ROUTE_LANES)), axis=-1, keepdims=True)
    rest = jnp.where(lane == i1, neg, logits)
    m2 = jnp.max(rest, axis=-1, keepdims=True)
    i2 = jnp.min(jnp.where(rest == m2, lane, float(ROUTE_LANES)), axis=-1, keepdims=True)
    e2 = jnp.exp(m2 - m1)
    g1 = 1.0 / (1.0 + e2)
    g2 = e2 / (1.0 + e2)
    onehot = jnp.where((lane == i1) | (lane == i2), 1.0, 0.0)
    before = _dot(ltri_ref[...], onehot.astype(BF16)) + count_ref[...]
    r1 = jnp.sum(jnp.where(lane == i1, before, 0.0), axis=-1, keepdims=True)
    r2 = jnp.sum(jnp.where(lane == i2, before, 0.0), axis=-1, keepdims=True)
    count_ref[...] += jnp.sum(onehot, axis=0, keepdims=True)
    out = jnp.where(lane == 0, i1, 0.0)
    out = jnp.where(lane == 1, i2, out)
    out = jnp.where(lane == 2, g1, out)
    out = jnp.where(lane == 3, g2, out)
    out = jnp.where(lane == 4, r1, out)
    return jnp.where(lane == 5, r2, out)


def _out_proj_kernel(mixed_ref, h_ref, w_ref, g_ref, *rest, routed):
    if routed:
        router_ref, ltri_ref, h1_ref, hn_ref, route_ref, count_ref = rest

        @pl.when(pl.program_id(0) == 0)
        def _():
            count_ref[...] = jnp.zeros_like(count_ref)
    else:
        h1_ref, hn_ref = rest
    h1 = h_ref[...] + _dot(mixed_ref[...], w_ref[...])
    h1_ref[...] = h1
    hn = h1 * lax.rsqrt(jnp.mean(h1 * h1, axis=-1, keepdims=True) + EPS) * g_ref[...]
    hn_ref[...] = hn.astype(hn_ref.dtype)
    if routed:
        route_ref[...] = _route(hn, router_ref, ltri_ref, count_ref)


def _out_proj(mixed, h, w_bf16, g, router_pad=None):
    n, d = h.shape
    tm = _pick_tile(n, (640, 128))
    routed = router_pad is not None
    row = lambda i: (i, 0)
    fixed = lambda i: (0, 0)
    in_specs = [pl.BlockSpec((tm, D_MIX), row), pl.BlockSpec((tm, d), row),
                pl.BlockSpec(w_bf16.shape, fixed), pl.BlockSpec((1, d), fixed)]
    out_specs = [pl.BlockSpec((tm, d), row), pl.BlockSpec((tm, d), row)]
    out_shape = [jax.ShapeDtypeStruct((n, d), F32), jax.ShapeDtypeStruct((n, d), F32 if routed else BF16)]
    args = [mixed, h, w_bf16, g]
    if routed:
        ltri = jnp.asarray(np.tril(np.ones((tm, tm), np.float32), -1), BF16)
        in_specs += [pl.BlockSpec(router_pad.shape, fixed), pl.BlockSpec((tm, tm), fixed)]
        out_specs += [pl.BlockSpec((tm, ROUTE_LANES), row), pl.BlockSpec((1, ROUTE_LANES), fixed)]
        out_shape += [jax.ShapeDtypeStruct((n, ROUTE_LANES), F32),
                      jax.ShapeDtypeStruct((1, ROUTE_LANES), F32)]
        args += [router_pad, ltri]
    return pl.pallas_call(
        functools.partial(_out_proj_kernel, routed=routed),
        grid=(n // tm,),
        in_specs=in_specs,
        out_specs=out_specs,
        out_shape=out_shape,
        compiler_params=pltpu.CompilerParams(
            dimension_semantics=("arbitrary",), vmem_limit_bytes=VMEM_LIMIT),
        name="out_proj_routed" if routed else "out_proj",
    )(*args)


def _ffn_kernel(hn_ref, h1_ref, w1_ref, w3_ref, w2_ref, o_ref, acc_ref):
    f = pl.program_id(1)
    x = hn_ref[...]
    a = _dot(x, w1_ref[...])
    b = _dot(x, w3_ref[...])
    part = _dot((a * _sigmoid(a) * b).astype(BF16), w2_ref[...])

    @pl.when(f == 0)
    def _():
        acc_ref[...] = part

    @pl.when(f > 0)
    def _():
        acc_ref[...] += part

    @pl.when(f == pl.num_programs(1) - 1)
    def _():
        o_ref[...] = h1_ref[...] + acc_ref[...]


def _ffn(hn, h1, w1, w3, w2):
    n, d = h1.shape
    dff = w1.shape[1]
    tm, tf = _pick_tile(n, (640, 128)), _pick_tile(dff, (1408, 128))
    return pl.pallas_call(
        _ffn_kernel,
        grid=(n // tm, dff // tf),
        in_specs=[
            pl.BlockSpec((tm, d), lambda i, f: (i, 0)),
            pl.BlockSpec((tm, d), lambda i, f: (i, 0)),
            pl.BlockSpec((d, tf), lambda i, f: (0, f)),
            pl.BlockSpec((d, tf), lambda i, f: (0, f)),
            pl.BlockSpec((tf, d), lambda i, f: (f, 0)),
        ],
        out_specs=pl.BlockSpec((tm, d), lambda i, f: (i, 0)),
        out_shape=jax.ShapeDtypeStruct((n, d), F32),
        scratch_shapes=[pltpu.VMEM((tm, d), F32)],
        compiler_params=pltpu.CompilerParams(
            dimension_semantics=("arbitrary", "arbitrary"), vmem_limit_bytes=VMEM_LIMIT),
        name="ffn_dense",
    )(hn, h1, w1, w3, w2)


def _moe_kernel(te_ref, nv_ref, src_ref, tok_ref, hn_hbm, w1_ref, w3_ref, w2_ref, o_ref,
                xs_ref, xb_ref, acc_ref, sem, *, tm):
    i, f = pl.program_id(0), pl.program_id(1)
    nf = pl.num_programs(1)
    n_valid = nv_ref[0]
    share = tm // MOE_F_STEPS

    def row_copy(tile, r):
        return pltpu.make_async_copy(hn_hbm.at[pl.ds(tok_ref[src_ref[tile] + r], 1)],
                                     xs_ref.at[pl.ds(r, 1)], sem)

    @pl.when((i == 0) & (f == 0))
    def _():
        def body(r, carry):
            row_copy(0, r).start()
            return carry
        lax.fori_loop(0, tm, body, 0)

    @pl.when((f == 0) & (i <= n_valid))
    def _():
        pltpu.make_async_copy(xs_ref, xs_ref, sem).wait()
        xb_ref[...] = xs_ref[...].astype(BF16)

    @pl.when(i < n_valid)
    def _():
        for r in range(share):
            row_copy(i + 1, f * share + r).start(priority=r % MOE_DMA_QUEUES)
        x = xb_ref[...]
        a = _dot(x, w1_ref[0])
        b = _dot(x, w3_ref[0])
        part = _dot((a * _sigmoid(a) * b).astype(BF16), w2_ref[0])

        @pl.when(f == 0)
        def _():
            acc_ref[...] = part

        @pl.when(f > 0)
        def _():
            acc_ref[...] += part

    @pl.when(f == nf - 1)
    def _():
        o_ref[...] = jnp.where(i < n_valid, acc_ref[...], 0.0).astype(o_ref.dtype)


def _live_f(i, f, nv):
    return jnp.where(i < nv[0], f, 0)


def _moe_grouped(tile_expert, n_valid, tile_src, sorted_tok, hn, w1, w3, w2, tm):
    d = hn.shape[1]
    n_tiles = tile_expert.shape[0]
    dff = w1.shape[2]
    tf = dff // MOE_F_STEPS
    grid_spec = pltpu.PrefetchScalarGridSpec(
        num_scalar_prefetch=4,
        grid=(n_tiles, MOE_F_STEPS),
        in_specs=[
            pl.BlockSpec(memory_space=pl.ANY),
            pl.BlockSpec((1, d, tf), lambda i, f, te, nv, *_: (te[i], 0, _live_f(i, f, nv))),
            pl.BlockSpec((1, d, tf), lambda i, f, te, nv, *_: (te[i], 0, _live_f(i, f, nv))),
            pl.BlockSpec((1, tf, d), lambda i, f, te, nv, *_: (te[i], _live_f(i, f, nv), 0)),
        ],
        out_specs=pl.BlockSpec((tm, d), lambda i, f, *_: (i, 0)),
        scratch_shapes=[pltpu.VMEM((tm, d), F32), pltpu.VMEM((tm, d), BF16), pltpu.VMEM((tm, d), F32),
                        pltpu.SemaphoreType.DMA(())],
    )
    return pl.pallas_call(
        functools.partial(_moe_kernel, tm=tm),
        grid_spec=grid_spec,
        out_shape=jax.ShapeDtypeStruct((n_tiles * tm, d), BF16),
        compiler_params=pltpu.CompilerParams(
            dimension_semantics=("arbitrary", "arbitrary"), vmem_limit_bytes=VMEM_LIMIT),
        name="moe_grouped",
    )(tile_expert, n_valid, tile_src, sorted_tok, hn, w1, w3, w2)


def _moe_dispatch(route, counts_row, tm):
    n = route.shape[0]
    n_slots = n * TOP_K
    n_tiles = (n_slots + N_EXPERTS * tm + tm - 1) // tm
    counts = counts_row[0, :N_EXPERTS].astype(jnp.int32)
    padded = ((counts + tm - 1) // tm) * tm
    pend = jnp.cumsum(padded)
    pstart = pend - padded
    gstart = jnp.cumsum(counts) - counts
    eid = route[:, 0:TOP_K].astype(jnp.int32)
    rank = route[:, 2 * TOP_K:3 * TOP_K].astype(jnp.int32)
    pos = pstart[eid] + rank
    tok = jnp.broadcast_to(jnp.arange(n, dtype=jnp.int32)[:, None], (n, TOP_K))
    _, sorted_tok = lax.sort(((gstart[eid] + rank).reshape(-1), tok.reshape(-1)), num_keys=1)
    sorted_tok = jnp.concatenate([sorted_tok, jnp.zeros((tm,), jnp.int32)])
    tile_start = jnp.arange(n_tiles, dtype=jnp.int32) * tm
    tile_expert = jnp.minimum(jnp.sum((tile_start[:, None] >= pend[None, :]).astype(jnp.int32), axis=1),
                              N_EXPERTS - 1)
    tile_src = jnp.clip(tile_start - (pstart - gstart)[tile_expert], 0, n_slots)
    n_valid = (pend[-1] // tm).astype(jnp.int32).reshape(1)
    return sorted_tok, tile_src, pos, tile_expert, n_valid


def _final_kernel(h1_ref, y0_ref, y1_ref, route_ref, g_ref, o_ref):
    g0 = route_ref[:, TOP_K:TOP_K + 1]
    g1 = route_ref[:, TOP_K + 1:TOP_K + 2]
    h = h1_ref[...] + (g0 * y0_ref[...].astype(F32) + g1 * y1_ref[...].astype(F32))
    o_ref[...] = h * lax.rsqrt(jnp.mean(h * h, axis=-1, keepdims=True) + EPS) * g_ref[...]


def _final(h1, y0, y1, route, g):
    n, d = h1.shape
    tm = _pick_tile(n, (640, 128))
    row = lambda i: (i, 0)
    return pl.pallas_call(
        _final_kernel,
        grid=(n // tm,),
        in_specs=[pl.BlockSpec((tm, d), row), pl.BlockSpec((tm, d), row), pl.BlockSpec((tm, d), row),
                  pl.BlockSpec((tm, ROUTE_LANES), row), pl.BlockSpec((1, d), lambda i: (0, 0))],
        out_specs=pl.BlockSpec((tm, d), row),
        out_shape=jax.ShapeDtypeStruct((n, d), F32),
        compiler_params=pltpu.CompilerParams(
            dimension_semantics=("arbitrary",), vmem_limit_bytes=VMEM_LIMIT),
        name="final_norm",
    )(h1, y0, y1, route, g)


def _rope_perm():
    half = RET_DK // 2
    idx = []
    for part in range(2):
        for h in range(RET_HEADS):
            for dd in range(half):
                idx.append(h * RET_DK + part * half + dd)
    return np.asarray(idx, np.int32)


def _in_col_perm():
    cols = np.arange(IN_COLS, dtype=np.int32)
    rp = _rope_perm()
    cols[C_RQ:C_RQ + RET_KEY_WIDTH] = C_RQ + rp
    cols[C_RK:C_RK + RET_KEY_WIDTH] = C_RK + rp
    return cols


def _hg_tables(nb):
    c = CHUNK
    i = np.arange(c)[:, None]
    t = np.arange(c)[None, :]
    blocks, masks, rsel = [], [], []
    for lvl in range(HG_LEVELS):
        s = 1 << lvl
        r = (i // (2 * s)) * (2 * s) + s - 1
        right = i > r
        m = np.where(right, (t > r) & (t <= i), (t > i) & (t <= r))
        blocks.append(m)
        j = t
        same = (i // (2 * s)) == (j // (2 * s))
        mk = same & right & (j <= r)
        masks.append(np.concatenate([mk, mk], axis=1))
        if s < 8:
            rsel.append(np.broadcast_to(right, (c, HG_WIDTH)))
    blocks.append(t <= i)
    blocks.append(t > i)
    eye = np.eye(nb, dtype=np.float32)
    mall = np.concatenate([np.kron(eye, blk.astype(np.float32)) for blk in blocks], axis=0)
    lvl = np.stack(masks, axis=0).astype(np.float32)
    ch = np.arange(HG_WIDTH)
    bones = (ch[:, None] // HG_DK == ch[None, :] // HG_DK).astype(np.float32)
    rsel = np.stack([np.tile(r, (nb, 1)) for r in rsel], axis=0).astype(np.float32)
    return mall, lvl, rsel, bones


def _ret_tables():
    f32 = jnp.float32
    log_gamma = jnp.log1p(-jnp.power(2.0, -5.0 - jnp.arange(RET_HEADS, dtype=f32)))
    n = jnp.arange(CHUNK, dtype=f32)
    lg = log_gamma[:, None]
    causal = jnp.tril(jnp.ones((CHUNK, CHUNK), dtype=bool))
    intra = jnp.exp(jnp.where(causal[None], (n[:, None] - n[None, :])[None] * lg[:, :, None], -jnp.inf))
    scale = RET_DK ** -0.5
    inter = jnp.exp((n[None, :] + 1.0) * lg)
    to_state = jnp.exp((CHUNK - 1.0 - n[None, :]) * lg)
    carry = jnp.exp(CHUNK * lg)[:, 0]
    head_of_v = np.arange(RET_WIDTH) // RET_DV
    head_of_k = (np.arange(RET_KEY_WIDTH) % (RET_KEY_WIDTH // 2)) // (RET_DK // 2)
    inter_t = (inter * scale).T[:, head_of_v]
    to_state_t = to_state.T[:, head_of_k]
    carry_row = carry[head_of_v][None, :]
    bmask = jnp.asarray((head_of_k[:, None] == head_of_v[None, :]).astype(np.float32))
    intra_pairs = (intra * scale).reshape(RET_HEADS // 2, 2, CHUNK, CHUNK)
    intra_pairs = jnp.concatenate([intra_pairs[:, 0], intra_pairs[:, 1]], axis=2)
    return intra_pairs, inter_t, to_state_t, carry_row, bmask


def _s5_tables(lam_re, lam_im, b_re, b_im, c_re, c_im, d_skip, log_step):
    f32 = jnp.float32
    lam = lax.complex(lam_re.astype(f32), lam_im.astype(f32))
    step = jnp.exp(log_step.astype(f32))[:, None]
    lam_dt = lam * step
    lam_bar = jnp.exp(lam_dt)
    b_bar = ((lam_bar - 1.0) / lam)[..., None] * lax.complex(b_re.astype(f32), b_im.astype(f32))
    eye = jnp.eye(S5_NGROUPS, dtype=f32)
    wb_re = jnp.einsum('gph,gk->ghkp', jnp.real(b_bar), eye).reshape(S5_WIDTH, S5_NSTATE)
    wb_im = jnp.einsum('gph,gk->ghkp', jnp.imag(b_bar), eye).reshape(S5_WIDTH, S5_NSTATE)
    wb = jnp.concatenate([wb_re, wb_im], axis=1)
    wc_re = jnp.einsum('ghp,gk->gpkh', c_re.astype(f32), eye).reshape(S5_NSTATE, S5_WIDTH)
    wc_im = jnp.einsum('ghp,gk->gpkh', c_im.astype(f32), eye).reshape(S5_NSTATE, S5_WIDTH)
    wc = jnp.concatenate([wc_re, -wc_im], axis=0)
    t = jnp.arange(CHUNK, dtype=f32)[:, None, None]
    pp = jnp.exp(lam_dt[None] * t).reshape(CHUNK, S5_NSTATE)
    pn = jnp.exp(-lam_dt[None] * t).reshape(CHUNK, S5_NSTATE)
    lam_rows = jnp.stack([jnp.real(lam_bar).reshape(-1), jnp.imag(lam_bar).reshape(-1)], axis=0)
    return (wb.astype(BF16), wc.astype(BF16), jnp.real(pn), jnp.imag(pn), jnp.real(pp), jnp.imag(pp),
            lam_rows, d_skip.astype(f32).reshape(1, S5_WIDTH))


def kernel(x, meta_tokens, norm_mix_g, w_in, s5_lam_re, s5_lam_im, s5_b_re, s5_b_im, s5_c_re, s5_c_im, s5_d, s5_log_step, s5_w_glu, s5_out_g, hg_conv_w, hg_lb_param, hg_out_g, ret_out_g, w_out, norm_ffn_g, ffn_w1, ffn_w3, ffn_w2, moe_router, moe_w1, moe_w3, moe_w2, final_norm_g):
    bsz, seq_len, d = x.shape
    depth = w_in.shape[0]
    total = seq_len + CHUNK
    n_chunks = total // CHUNK
    n = bsz * total

    meta = jnp.broadcast_to(meta_tokens.astype(F32)[None], (bsz, N_META, d))
    h = jnp.concatenate([jnp.zeros((bsz, PAD, d), F32), meta, x.astype(F32)], axis=1).reshape(n, d)

    pos = (jnp.arange(total) - PAD).astype(F32)
    half = RET_DK // 2
    inv_freq = ROPE_BASE ** (-jnp.arange(half, dtype=F32) / half)
    ang = pos[:, None] * inv_freq[None, :]
    cos_t = jnp.tile(jnp.cos(ang), (1, RET_HEADS))
    sin_t = jnp.tile(jnp.sin(ang), (1, RET_HEADS))

    lb_all = jnp.cumsum(jax.nn.softmax(hg_lb_param.astype(F32), axis=0), axis=0)
    lb_all = lb_all - lb_all[0]

    mall_np, lvl_np, rsel_np, bones_np = _hg_tables(bsz)
    mall = jnp.asarray(mall_np, BF16)
    lvl = jnp.asarray(lvl_np, F32)
    rsel = jnp.asarray(rsel_np, F32)
    bones = jnp.asarray(bones_np, BF16)
    avg = jnp.asarray(bones_np / HG_DK, BF16)
    ltri = jnp.asarray(np.kron(np.eye(bsz, dtype=np.float32),
                               np.tril(np.ones((CHUNK, CHUNK), np.float32))), BF16)
    intra, inter_t, to_state_t, carry_row, bmask = _ret_tables()
    col_perm = jnp.asarray(_in_col_perm())

    out = None
    for l in range(depth):
        w_in_l = jnp.take(w_in[l], col_perm, axis=1).astype(BF16)
        proj = _in_proj(h, norm_mix_g[l].astype(F32).reshape(1, d), w_in_l, total)

        s5c = _s5_tables(s5_lam_re[l], s5_lam_im[l], s5_b_re[l], s5_b_im[l], s5_c_re[l], s5_c_im[l],
                         s5_d[l], s5_log_step[l])
        lb = lb_all[l][None, :]
        lbv = jnp.concatenate([jnp.log(lb), jnp.log1p(-lb), 1.0 - lb], axis=0)
        consts = list(s5c) + [
            s5_w_glu[l].astype(BF16), s5_out_g[l].astype(F32).reshape(1, -1), ltri,
            hg_conv_w[l].astype(F32), lbv, mall, lvl, rsel, bones, avg, hg_out_g[l].astype(F32).reshape(1, -1),
            intra, inter_t, to_state_t, carry_row, bmask, ret_out_g[l].astype(F32).reshape(1, -1),
        ]
        mixed = _mixers(proj.reshape(bsz, total, IN_COLS), cos_t, sin_t, consts, bsz, n_chunks)
        mixed = mixed.reshape(n, D_MIX)

        g_ffn = norm_ffn_g[l].astype(F32).reshape(1, d)
        w_out_l = w_out[l].astype(BF16)
        if l % 2 == 0:
            h1, hn = _out_proj(mixed, h, w_out_l, g_ffn)
            j = l // 2
            h = _ffn(hn, h1, ffn_w1[j].astype(BF16), ffn_w3[j].astype(BF16), ffn_w2[j].astype(BF16))
            y0 = y1 = None
        else:
            j = l // 2
            router_pad = jnp.zeros((d, ROUTE_LANES), F32).at[:, :N_EXPERTS].set(moe_router[j].astype(F32))
            h1, hn, route, counts_row = _out_proj(mixed, h, w_out_l, g_ffn, router_pad)
            tm = MOE_TILE
            sorted_tok, tile_src, pos_of_slot, tile_expert, n_valid = _moe_dispatch(route, counts_row, tm)
            ys = _moe_grouped(tile_expert, n_valid, tile_src, sorted_tok, hn,
                              moe_w1[j].astype(BF16), moe_w3[j].astype(BF16), moe_w2[j].astype(BF16), tm)
            y0 = jnp.take(ys, pos_of_slot[:, 0], axis=0)
            y1 = jnp.take(ys, pos_of_slot[:, 1], axis=0)
            if l < depth - 1:
                h = h1 + (route[:, TOP_K:TOP_K + 1] * y0.astype(F32)
                          + route[:, TOP_K + 1:TOP_K + 2] * y1.astype(F32))

        if l == depth - 1:
            if y0 is None:
                y0 = y1 = jnp.zeros((n, d), BF16)
                h1 = h
                route = jnp.zeros((n, ROUTE_LANES), F32)
            out = _final(h1, y0, y1, route, final_norm_g.astype(F32).reshape(1, d))

    return out.reshape(bsz, total, d)[:, CHUNK:, :].astype(x.dtype)
```

```python
import functools
import math

import jax
import jax.numpy as jnp
import numpy as np
from jax import lax
from jax.experimental import pallas as pl
from jax.experimental.pallas import tpu as pltpu

F32 = jnp.float32
BF16 = jnp.bfloat16

CHUNK = 128
N_META = 16
PAD = CHUNK - N_META
EPS = 1e-6

S5_WIDTH = 256
S5_GROUP = 16
S5_NGROUPS = 16
S5_STATE = 64
S5_NSTATE = S5_NGROUPS * S5_STATE

HG_HEADS = 4
HG_DK = 64
HG_WIDTH = 256
CONV_K = 4
HG_CONV_W = 3 * HG_WIDTH
HG_LEVELS = 7

RET_HEADS = 8
RET_DK = 32
RET_DV = 64
RET_KEY_WIDTH = 256
RET_WIDTH = 512
ROPE_BASE = 10000.0

D_MIX = 1024
IN_COLS = 2816
C_U, C_HQ, C_HGATE, C_RQ, C_RK, C_RV, C_RGATE = 0, 256, 1024, 1280, 1536, 1792, 2304

N_EXPERTS = 8
TOP_K = 2
ROUTE_LANES = 128
MOE_F_STEPS = 2
MOE_TILE = 512
MOE_DMA_QUEUES = 2

VMEM_LIMIT = 56 * 1024 * 1024


def _sigmoid(x):
    return 1.0 / (1.0 + jnp.exp(-x))


def _split_bf16(x):
    hi = x.astype(BF16)
    lo = (x - hi.astype(F32)).astype(BF16)
    return hi, lo


def _pick_tile(n, candidates):
    for t in candidates:
        if n % t == 0:
            return t
    raise ValueError(f"no tile in {candidates} divides {n}")


def _dot(a, b):
    return jnp.dot(a, b, preferred_element_type=F32)


def _dot_nt(a, b):
    return lax.dot_general(a, b, (((1,), (1,)), ((), ())), preferred_element_type=F32)


def _dot_tn(a, b):
    return lax.dot_general(a, b, (((0,), (0,)), ((), ())), preferred_element_type=F32)


def _in_proj_kernel(h_ref, g_ref, w_ref, o_ref, *, tm, tiles_per_batch):
    x = h_ref[...]
    y = x * lax.rsqrt(jnp.mean(x * x, axis=-1, keepdims=True) + EPS) * g_ref[...]
    proj = _dot(y.astype(BF16), w_ref[...])
    row0 = (pl.program_id(0) % tiles_per_batch) * tm
    rows = row0 + lax.broadcasted_iota(jnp.int32, (tm, 1), 0)
    o_ref[...] = jnp.where(rows >= PAD, proj, 0.0)


def _in_proj(h, g, w_bf16, total):
    n = h.shape[0]
    tm = _pick_tile(total, (320, 128))
    return pl.pallas_call(
        functools.partial(_in_proj_kernel, tm=tm, tiles_per_batch=total // tm),
        grid=(n // tm,),
        in_specs=[
            pl.BlockSpec((tm, h.shape[1]), lambda i: (i, 0)),
            pl.BlockSpec((1, h.shape[1]), lambda i: (0, 0)),
            pl.BlockSpec(w_bf16.shape, lambda i: (0, 0)),
        ],
        out_specs=pl.BlockSpec((tm, IN_COLS), lambda i: (i, 0)),
        out_shape=jax.ShapeDtypeStruct((n, IN_COLS), F32),
        compiler_params=pltpu.CompilerParams(
            dimension_semantics=("arbitrary",), vmem_limit_bytes=VMEM_LIMIT),
        name="in_proj",
    )(h, g, w_bf16)


def _rows(parts):
    return parts[0] if len(parts) == 1 else jnp.concatenate(parts, axis=0)


def _s5_chunks(u, wb_ref, wc_ref, pn_re_ref, pn_im_ref, pp_re_ref, pp_im_ref, lam_ref,
               d_ref, wglu_ref, g_ref, ltri_ref, st_ref):
    ns = S5_NSTATE
    nb = st_ref.shape[0]
    bu = _dot(u.astype(BF16), wb_ref[...])
    pn_re, pn_im = pn_re_ref[...], pn_im_ref[...]
    w_re, w_im = [], []
    for b in range(nb):
        bu_re = bu[b * CHUNK:(b + 1) * CHUNK, :ns]
        bu_im = bu[b * CHUNK:(b + 1) * CHUNK, ns:]
        w_re.append(pn_re * bu_re - pn_im * bu_im)
        w_im.append(pn_re * bu_im + pn_im * bu_re)
    ltri = ltri_ref[...]
    c_re = _dot(ltri, _rows(w_re).astype(BF16))
    c_im = _dot(ltri, _rows(w_im).astype(BF16))
    lam_re, lam_im = lam_ref[0:1, :], lam_ref[1:2, :]
    pp_re, pp_im = pp_re_ref[...], pp_im_ref[...]
    st_re, st_im = [], []
    for b in range(nb):
        s_re, s_im = st_ref[b, 0:1, :], st_ref[b, 1:2, :]
        z_re = c_re[b * CHUNK:(b + 1) * CHUNK] + (lam_re * s_re - lam_im * s_im)
        z_im = c_im[b * CHUNK:(b + 1) * CHUNK] + (lam_re * s_im + lam_im * s_re)
        t_re = pp_re * z_re - pp_im * z_im
        t_im = pp_re * z_im + pp_im * z_re
        st_ref[b, 0:1, :] = t_re[CHUNK - 1:CHUNK, :]
        st_ref[b, 1:2, :] = t_im[CHUNK - 1:CHUNK, :]
        st_re.append(t_re)
        st_im.append(t_im)
    y = (_dot(_rows(st_re).astype(BF16), wc_ref[0:ns, :]) + _dot(_rows(st_im).astype(BF16), wc_ref[ns:2 * ns, :])
         + d_ref[...] * u)
    y = 0.5 * y * (1.0 + jnp.tanh(math.sqrt(2.0 / math.pi) * (y + 0.044715 * (y * y * y))))
    y = y * _sigmoid(_dot(y.astype(BF16), wglu_ref[...]))
    return y * lax.rsqrt(jnp.mean(y * y, axis=-1, keepdims=True) + EPS) * g_ref[...]


def _hgrn2_chunks(xc_ref, gate, convw_ref, lbv_ref, mall_ref, lvl_ref, rsel_ref, bones_ref, avg_ref,
                  g_ref, st_ref):
    w = HG_WIDTH
    nb = st_ref.shape[0]
    rows = nb * CHUNK
    convs = []
    for b in range(nb):
        conv = None
        for i in range(CONV_K):
            term = xc_ref[b, pl.ds(8 - (CONV_K - 1) + i, CHUNK), :] * convw_ref[i:i + 1, :]
            conv = term if conv is None else conv + term
        convs.append(conv)
    conv = _rows(convs)
    cq, cf, v = conv[:, :w], conv[:, w:2 * w], conv[:, 2 * w:]
    q = cq * _sigmoid(cq)
    log_lb, log_1m_lb, one_m_lb = lbv_ref[0:1, :], lbv_ref[1:2, :], lbv_ref[2:3, :]
    log_sig = jnp.minimum(cf, 0.0) - jnp.log(1.0 + jnp.exp(-jnp.abs(cf)))
    b_ = log_1m_lb + log_sig
    logf = jnp.maximum(log_lb, b_) + jnp.log(1.0 + jnp.exp(-jnp.abs(log_lb - b_)))
    kk = one_m_lb * _sigmoid(-cf)
    hi, lo = _split_bf16(logf)

    lvl_sums = _dot(mall_ref[0:HG_LEVELS * rows, :], hi)
    tail = mall_ref[HG_LEVELS * rows:(HG_LEVELS + 2) * rows, :]
    cum_suf = _dot(tail, hi) + _dot(tail, lo)
    g_cum, g_suffix = cum_suf[:rows], cum_suf[rows:]

    lane = lax.broadcasted_iota(jnp.int32, (1, 2 * HG_DK), 1)
    head_sel = [jnp.where(lane < HG_DK, 1.0, 0.0), jnp.where(lane >= HG_DK, 1.0, 0.0)]
    n_pairs = HG_HEADS // 2
    scores = [[None] * n_pairs for _ in range(nb)]
    for lvl in range(HG_LEVELS):
        s = 1 << lvl
        e = jnp.exp(lvl_sums[lvl * rows:(lvl + 1) * rows])
        if s >= 8:
            qk = jnp.concatenate([(q if (r // s) % 2 else kk)[r:r + s] for r in range(0, rows, s)], axis=0)
        else:
            qk = jnp.where(rsel_ref[lvl] > 0.5, q, kk)
        x = qk * e
        m = lvl_ref[lvl]
        for b in range(nb):
            for p in range(n_pairs):
                xp = x[b * CHUNK:(b + 1) * CHUNK, p * 128:(p + 1) * 128]
                rhs = jnp.concatenate([xp * head_sel[0], xp * head_sel[1]], axis=0).astype(BF16)
                sc = _dot_nt(xp.astype(BF16), rhs) * m
                scores[b][p] = sc if scores[b][p] is None else scores[b][p] + sc
    v_bf = v.astype(BF16)
    bones = bones_ref[...]
    qg = (q * jnp.exp(g_cum)).astype(BF16)
    kd = (kk * jnp.exp(g_suffix)).astype(BF16)
    o_rows = []
    for b in range(nb):
        sl = slice(b * CHUNK, (b + 1) * CHUNK)
        o_parts = []
        for p in range(n_pairs):
            vp = v[sl, p * 128:(p + 1) * 128]
            vv = jnp.concatenate([vp * head_sel[0], vp * head_sel[1]], axis=0).astype(BF16)
            o_parts.append(_dot(scores[b][p].astype(BF16), vv))
        st = st_ref[b]
        o_rows.append(jnp.concatenate(o_parts, axis=1) + _dot_nt(qg[sl], st.astype(BF16)))
        upd = _dot_tn(v_bf[sl], kd[sl]) * bones.astype(F32)
        st_ref[b] = st * jnp.exp(g_cum[(b + 1) * CHUNK - 1:(b + 1) * CHUNK, :]) + upd
    o = _rows(o_rows) + _dot((q * kk).astype(BF16), bones) * v
    ms = _dot((o * o).astype(BF16), avg_ref[...])
    return o * lax.rsqrt(ms + EPS) * g_ref[...] * (gate * _sigmoid(gate))


def _ret_chunk(rq, rk, v, cos, sin, intra_ref, inter_ref, tostate_ref, carry_ref, bmask_ref, st_ref):
    hw = RET_KEY_WIDTH // 2

    def rot(t):
        t1, t2 = t[:, :hw], t[:, hw:]
        return jnp.concatenate([t1 * cos - t2 * sin, t1 * sin + t2 * cos], axis=1)

    qr, kr = rot(rq), rot(rk)
    qr_bf, v_bf = qr.astype(BF16), v.astype(BF16)
    lane_k = lax.broadcasted_iota(jnp.int32, (1, RET_KEY_WIDTH), 1) % hw
    lane_v = lax.broadcasted_iota(jnp.int32, (1, 2 * RET_DV), 1)
    v_sel = [jnp.where(lane_v < RET_DV, 1.0, 0.0), jnp.where(lane_v >= RET_DV, 1.0, 0.0)]
    half = RET_DK // 2
    o_parts = []
    for p in range(RET_HEADS // 2):
        sel = [jnp.where((lane_k >= h * half) & (lane_k < (h + 1) * half), 1.0, 0.0)
               for h in (2 * p, 2 * p + 1)]
        rhs = jnp.concatenate([kr * sel[0], kr * sel[1]], axis=0).astype(BF16)
        sc = (_dot_nt(qr_bf, rhs) * intra_ref[p]).astype(BF16)
        vp = v[:, p * 128:(p + 1) * 128]
        vv = jnp.concatenate([vp * v_sel[0], vp * v_sel[1]], axis=0).astype(BF16)
        o_parts.append(_dot(sc, vv))
    o = jnp.concatenate(o_parts, axis=1)
    st = st_ref[...]
    o = o + _dot(qr_bf, st.astype(BF16)) * inter_ref[...]
    kd = (kr * tostate_ref[...]).astype(BF16)
    st_ref[...] = st * carry_ref[...] + _dot_tn(kd, v_bf) * bmask_ref[...]
    return o


def _ret_norm_gate(o, gate, avg_ref, g_ref):
    avg = avg_ref[...]
    outs = []
    for s in range(RET_WIDTH // 256):
        os_ = o[:, s * 256:(s + 1) * 256]
        c = os_ - _dot(os_.astype(BF16), avg)
        outs.append(c * lax.rsqrt(_dot((c * c).astype(BF16), avg) + EPS))
    return jnp.concatenate(outs, axis=1) * g_ref[...] * (gate * _sigmoid(gate))


def _mixer_kernel(proj_ref, cos_ref, sin_ref,
                  wb_ref, wc_ref, pn_re_ref, pn_im_ref, pp_re_ref, pp_im_ref, lam_ref, d_ref,
                  wglu_ref, s5g_ref, ltri_ref,
                  convw_ref, lbv_ref, mall_ref, lvl_ref, rsel_ref, bones_ref, avg_ref, hgg_ref,
                  intra_ref, inter_ref, tostate_ref, carry_ref, bmask_ref, retg_ref,
                  *rest, n_cast, cast_steps):
    cast_in, o_ref, cast_out = rest[:n_cast], rest[n_cast], rest[n_cast + 1:2 * n_cast + 1]
    s5_st, hg_xc, hg_st, ret_st = rest[2 * n_cast + 1:]

    @pl.when(pl.program_id(0) == 0)
    def _():
        s5_st[...] = jnp.zeros_like(s5_st)
        hg_xc[...] = jnp.zeros_like(hg_xc)
        hg_st[...] = jnp.zeros_like(hg_st)
        ret_st[...] = jnp.zeros_like(ret_st)

    if n_cast:
        @pl.when(pl.program_id(0) < cast_steps)
        def _():
            for src, dst in zip(cast_in, cast_out):
                dst[...] = src[...].astype(dst.dtype)

    nb = proj_ref.shape[0]

    def cols(c0, width):
        return _rows([proj_ref[b, :, c0:c0 + width] for b in range(nb)])

    def emit(c0, y):
        for b in range(nb):
            o_ref[b, :, c0:c0 + y.shape[1]] = y[b * CHUNK:(b + 1) * CHUNK].astype(o_ref.dtype)

    y_a = _s5_chunks(cols(C_U, S5_WIDTH), wb_ref, wc_ref, pn_re_ref, pn_im_ref, pp_re_ref, pp_im_ref,
                     lam_ref, d_ref, wglu_ref, s5g_ref, ltri_ref, s5_st)
    emit(0, y_a)

    for b in range(nb):
        hg_xc[b, 8:8 + CHUNK, :] = proj_ref[b, :, C_HQ:C_HQ + HG_CONV_W]
    y_b = _hgrn2_chunks(hg_xc, cols(C_HGATE, HG_WIDTH), convw_ref, lbv_ref,
                        mall_ref, lvl_ref, rsel_ref, bones_ref, avg_ref, hgg_ref, hg_st)
    for b in range(nb):
        hg_xc[b, 0:8, :] = hg_xc[b, CHUNK:CHUNK + 8, :]
    emit(S5_WIDTH, y_b)

    o_c = [_ret_chunk(proj_ref[b, :, C_RQ:C_RQ + RET_KEY_WIDTH], proj_ref[b, :, C_RK:C_RK + RET_KEY_WIDTH],
                      proj_ref[b, :, C_RV:C_RV + RET_WIDTH], cos_ref[...], sin_ref[...],
                      intra_ref, inter_ref, tostate_ref, carry_ref, bmask_ref, ret_st.at[b])
           for b in range(nb)]
    y_c = _ret_norm_gate(_rows(o_c), cols(C_RGATE, RET_WIDTH), avg_ref, retg_ref)
    emit(S5_WIDTH + HG_WIDTH, y_c)


def _const_spec(a):
    nd = a.ndim
    return pl.BlockSpec(a.shape, lambda c, _nd=nd: (0,) * _nd)


def _cast_steps(n_chunks, arrays):
    for steps in range(n_chunks, 0, -1):
        if all(a.shape[0] % (steps * 16) == 0 for a in arrays):
            return steps
    raise ValueError("no slab split")


def _mixers(proj, cos, sin, consts, bsz, n_chunks, to_bf16=()):
    total = proj.shape[1]
    n_cast = len(to_bf16)
    cast_steps = _cast_steps(n_chunks, to_bf16) if n_cast else 0
    slab = lambda c: (jnp.minimum(c, cast_steps - 1), 0)
    cast_specs = [pl.BlockSpec((a.shape[0] // cast_steps, a.shape[1]), slab) for a in to_bf16]
    in_specs = [
        pl.BlockSpec((bsz, CHUNK, IN_COLS), lambda c: (0, c, 0)),
        pl.BlockSpec((CHUNK, 128), lambda c: (c, 0)),
        pl.BlockSpec((CHUNK, 128), lambda c: (c, 0)),
    ] + [_const_spec(a) for a in consts] + cast_specs
    outs = pl.pallas_call(
        functools.partial(_mixer_kernel, n_cast=n_cast, cast_steps=cast_steps),
        grid=(n_chunks,),
        in_specs=in_specs,
        out_specs=[pl.BlockSpec((bsz, CHUNK, D_MIX), lambda c: (0, c, 0))] + cast_specs,
        out_shape=[jax.ShapeDtypeStruct((bsz, total, D_MIX), BF16)]
        + [jax.ShapeDtypeStruct(a.shape, BF16) for a in to_bf16],
        scratch_shapes=[
            pltpu.VMEM((bsz, 2, S5_NSTATE), F32),
            pltpu.VMEM((bsz, CHUNK + 8, HG_CONV_W), F32),
            pltpu.VMEM((bsz, HG_WIDTH, HG_WIDTH), F32),
            pltpu.VMEM((bsz, RET_KEY_WIDTH, RET_WIDTH), F32),
        ],
        compiler_params=pltpu.CompilerParams(
            dimension_semantics=("arbitrary",), vmem_limit_bytes=VMEM_LIMIT),
        name="mixers",
    )(proj, cos, sin, *consts, *to_bf16)
    return outs[0], list(outs[1:])


def _route(hn, router_ref, ltri_ref, count_ref):
    r = router_ref[...]
    r_hi, r_lo = _split_bf16(r)
    h_hi, h_lo = _split_bf16(hn)
    logits = _dot(h_hi, r_hi) + (_dot(h_lo, r_hi) + _dot(h_hi, r_lo))
    lane_i = lax.broadcasted_iota(jnp.int32, logits.shape, 1)
    lane = lane_i.astype(F32)
    neg = jnp.float32(-jnp.inf)
    logits = jnp.where(lane_i < N_EXPERTS, logits, neg)
    m1 = jnp.max(logits, axis=-1, keepdims=True)
    i1 = jnp.min(jnp.where(logits == m1, lane, float(ROUTE_LANES)), axis=-1, keepdims=True)
    rest = jnp.where(lane == i1, neg, logits)
    m2 = jnp.max(rest, axis=-1, keepdims=True)
    i2 = jnp.min(jnp.where(rest == m2, lane, float(ROUTE_LANES)), axis=-1, keepdims=True)
    e2 = jnp.exp(m2 - m1)
    g1 = 1.0 / (1.0 + e2)
    g2 = e2 / (1.0 + e2)
    onehot = jnp.where((lane == i1) | (lane == i2), 1.0, 0.0)
    before = _dot(ltri_ref[...], onehot.astype(BF16)) + count_ref[...]
    r1 = jnp.sum(jnp.where(lane == i1, before, 0.0), axis=-1, keepdims=True)
    r2 = jnp.sum(jnp.where(lane == i2, before, 0.0), axis=-1, keepdims=True)
    count_ref[...] += jnp.sum(onehot, axis=0, keepdims=True)
    out = jnp.where(lane == 0, i1, 0.0)
    out = jnp.where(lane == 1, i2, out)
    out = jnp.where(lane == 2, g1, out)
    out = jnp.where(lane == 3, g2, out)
    out = jnp.where(lane == 4, r1, out)
    return jnp.where(lane == 5, r2, out)


def _out_proj_kernel(mixed_ref, h_ref, w_ref, g_ref, *rest, routed):
    if routed:
        router_ref, ltri_ref, h1_ref, hn_ref, route_ref, count_ref = rest

        @pl.when(pl.program_id(0) == 0)
        def _():
            count_ref[...] = jnp.zeros_like(count_ref)
    else:
        h1_ref, hn_ref = rest
    h1 = h_ref[...] + _dot(mixed_ref[...], w_ref[...])
    h1_ref[...] = h1
    hn = h1 * lax.rsqrt(jnp.mean(h1 * h1, axis=-1, keepdims=True) + EPS) * g_ref[...]
    hn_ref[...] = hn.astype(hn_ref.dtype)
    if routed:
        route_ref[...] = _route(hn, router_ref, ltri_ref, count_ref)


def _out_proj(mixed, h, w_bf16, g, router_pad=None):
    n, d = h.shape
    tm = _pick_tile(n, (640, 128))
    routed = router_pad is not None
    row = lambda i: (i, 0)
    fixed = lambda i: (0, 0)
    in_specs = [pl.BlockSpec((tm, D_MIX), row), pl.BlockSpec((tm, d), row),
                pl.BlockSpec(w_bf16.shape, fixed), pl.BlockSpec((1, d), fixed)]
    out_specs = [pl.BlockSpec((tm, d), row), pl.BlockSpec((tm, d), row)]
    out_shape = [jax.ShapeDtypeStruct((n, d), F32), jax.ShapeDtypeStruct((n, d), F32 if routed else BF16)]
    args = [mixed, h, w_bf16, g]
    if routed:
        ltri = jnp.asarray(np.tril(np.ones((tm, tm), np.float32), -1), BF16)
        in_specs += [pl.BlockSpec(router_pad.shape, fixed), pl.BlockSpec((tm, tm), fixed)]
        out_specs += [pl.BlockSpec((tm, ROUTE_LANES), row), pl.BlockSpec((1, ROUTE_LANES), fixed)]
        out_shape += [jax.ShapeDtypeStruct((n, ROUTE_LANES), F32),
                      jax.ShapeDtypeStruct((1, ROUTE_LANES), F32)]
        args += [router_pad, ltri]
    return pl.pallas_call(
        functools.partial(_out_proj_kernel, routed=routed),
        grid=(n // tm,),
        in_specs=in_specs,
        out_specs=out_specs,
        out_shape=out_shape,
        compiler_params=pltpu.CompilerParams(
            dimension_semantics=("arbitrary",), vmem_limit_bytes=VMEM_LIMIT),
        name="out_proj_routed" if routed else "out_proj",
    )(*args)


def _ffn_kernel(hn_ref, h1_ref, w1_ref, w3_ref, w2_ref, o_ref, acc_ref):
    f = pl.program_id(1)
    x = hn_ref[...]
    a = _dot(x, w1_ref[...])
    b = _dot(x, w3_ref[...])
    part = _dot((a * _sigmoid(a) * b).astype(BF16), w2_ref[...])

    @pl.when(f == 0)
    def _():
        acc_ref[...] = part

    @pl.when(f > 0)
    def _():
        acc_ref[...] += part

    @pl.when(f == pl.num_programs(1) - 1)
    def _():
        o_ref[...] = h1_ref[...] + acc_ref[...]


def _ffn(hn, h1, w1, w3, w2):
    n, d = h1.shape
    dff = w1.shape[1]
    tm, tf = _pick_tile(n, (640, 128)), _pick_tile(dff, (1408, 128))
    return pl.pallas_call(
        _ffn_kernel,
        grid=(n // tm, dff // tf),
        in_specs=[
            pl.BlockSpec((tm, d), lambda i, f: (i, 0)),
            pl.BlockSpec((tm, d), lambda i, f: (i, 0)),
            pl.BlockSpec((d, tf), lambda i, f: (0, f)),
            pl.BlockSpec((d, tf), lambda i, f: (0, f)),
            pl.BlockSpec((tf, d), lambda i, f: (f, 0)),
        ],
        out_specs=pl.BlockSpec((tm, d), lambda i, f: (i, 0)),
        out_shape=jax.ShapeDtypeStruct((n, d), F32),
        scratch_shapes=[pltpu.VMEM((tm, d), F32)],
        compiler_params=pltpu.CompilerParams(
            dimension_semantics=("arbitrary", "arbitrary"), vmem_limit_bytes=VMEM_LIMIT),
        name="ffn_dense",
    )(hn, h1, w1, w3, w2)


def _moe_kernel(te_ref, nv_ref, src_ref, tok_ref, hn_hbm, w1_ref, w3_ref, w2_ref, o_ref,
                xs_ref, xb_ref, acc_ref, sem, *, tm):
    i, f = pl.program_id(0), pl.program_id(1)
    nf = pl.num_programs(1)
    n_valid = nv_ref[0]
    share = tm // MOE_F_STEPS

    def row_copy(tile, r):
        return pltpu.make_async_copy(hn_hbm.at[pl.ds(tok_ref[src_ref[tile] + r], 1)],
                                     xs_ref.at[pl.ds(r, 1)], sem)

    @pl.when((i == 0) & (f == 0))
    def _():
        def body(r, carry):
            row_copy(0, r).start()
            return carry
        lax.fori_loop(0, tm, body, 0)

    @pl.when((f == 0) & (i <= n_valid))
    def _():
        pltpu.make_async_copy(xs_ref, xs_ref, sem).wait()
        xb_ref[...] = xs_ref[...].astype(BF16)

    @pl.when(i < n_valid)
    def _():
        for r in range(share):
            row_copy(i + 1, f * share + r).start(priority=r % MOE_DMA_QUEUES)
        x = xb_ref[...]
        a = _dot(x, w1_ref[0])
        b = _dot(x, w3_ref[0])
        part = _dot((a * _sigmoid(a) * b).astype(BF16), w2_ref[0])

        @pl.when(f == 0)
        def _():
            acc_ref[...] = part

        @pl.when(f > 0)
        def _():
            acc_ref[...] += part

    @pl.when(f == nf - 1)
    def _():
        o_ref[...] = jnp.where(i < n_valid, acc_ref[...], 0.0).astype(o_ref.dtype)


def _live_f(i, f, nv):
    return jnp.where(i < nv[0], f, 0)


def _moe_grouped(tile_expert, n_valid, tile_src, sorted_tok, hn, w1, w3, w2, tm):
    d = hn.shape[1]
    n_tiles = tile_expert.shape[0]
    dff = w1.shape[2]
    tf = dff // MOE_F_STEPS
    grid_spec = pltpu.PrefetchScalarGridSpec(
        num_scalar_prefetch=4,
        grid=(n_tiles, MOE_F_STEPS),
        in_specs=[
            pl.BlockSpec(memory_space=pl.ANY),
            pl.BlockSpec((1, d, tf), lambda i, f, te, nv, *_: (te[i], 0, _live_f(i, f, nv))),
            pl.BlockSpec((1, d, tf), lambda i, f, te, nv, *_: (te[i], 0, _live_f(i, f, nv))),
            pl.BlockSpec((1, tf, d), lambda i, f, te, nv, *_: (te[i], _live_f(i, f, nv), 0)),
        ],
        out_specs=pl.BlockSpec((tm, d), lambda i, f, *_: (i, 0)),
        scratch_shapes=[pltpu.VMEM((tm, d), F32), pltpu.VMEM((tm, d), BF16), pltpu.VMEM((tm, d), F32),
                        pltpu.SemaphoreType.DMA(())],
    )
    return pl.pallas_call(
        functools.partial(_moe_kernel, tm=tm),
        grid_spec=grid_spec,
        out_shape=jax.ShapeDtypeStruct((n_tiles * tm, d), BF16),
        compiler_params=pltpu.CompilerParams(
            dimension_semantics=("arbitrary", "arbitrary"), vmem_limit_bytes=VMEM_LIMIT),
        name="moe_grouped",
    )(tile_expert, n_valid, tile_src, sorted_tok, hn, w1, w3, w2)


def _moe_dispatch(route, counts_row, tm):
    n = route.shape[0]
    n_slots = n * TOP_K
    n_tiles = (n_slots + N_EXPERTS * tm + tm - 1) // tm
    counts = counts_row[0, :N_EXPERTS].astype(jnp.int32)
    padded = ((counts + tm - 1) // tm) * tm
    pend = jnp.cumsum(padded)
    pstart = pend - padded
    gstart = jnp.cumsum(counts) - counts
    eid = route[:, 0:TOP_K].astype(jnp.int32)
    rank = route[:, 2 * TOP_K:3 * TOP_K].astype(jnp.int32)
    pos = pstart[eid] + rank
    tok = jnp.broadcast_to(jnp.arange(n, dtype=jnp.int32)[:, None], (n, TOP_K))
    _, sorted_tok = lax.sort(((gstart[eid] + rank).reshape(-1), tok.reshape(-1)), num_keys=1)
    sorted_tok = jnp.concatenate([sorted_tok, jnp.zeros((tm,), jnp.int32)])
    tile_start = jnp.arange(n_tiles, dtype=jnp.int32) * tm
    tile_expert = jnp.minimum(jnp.sum((tile_start[:, None] >= pend[None, :]).astype(jnp.int32), axis=1),
                              N_EXPERTS - 1)
    tile_src = jnp.clip(tile_start - (pstart - gstart)[tile_expert], 0, n_slots)
    n_valid = (pend[-1] // tm).astype(jnp.int32).reshape(1)
    return sorted_tok, tile_src, pos, tile_expert, n_valid


def _final_kernel(h1_ref, y0_ref, y1_ref, route_ref, g_ref, o_ref):
    g0 = route_ref[0, :, TOP_K:TOP_K + 1]
    g1 = route_ref[0, :, TOP_K + 1:TOP_K + 2]
    h = h1_ref[0] + (g0 * y0_ref[0].astype(F32) + g1 * y1_ref[0].astype(F32))
    o_ref[0] = h * lax.rsqrt(jnp.mean(h * h, axis=-1, keepdims=True) + EPS) * g_ref[...]


def _final(h1, y0, y1, route, g, bsz, total):
    d = h1.shape[1]
    src = lambda b, c: (b, c + 1, 0)
    view = lambda a: a.reshape(bsz, total, a.shape[-1])
    return pl.pallas_call(
        _final_kernel,
        grid=(bsz, total // CHUNK - 1),
        in_specs=[pl.BlockSpec((1, CHUNK, d), src), pl.BlockSpec((1, CHUNK, d), src),
                  pl.BlockSpec((1, CHUNK, d), src), pl.BlockSpec((1, CHUNK, ROUTE_LANES), src),
                  pl.BlockSpec((1, d), lambda b, c: (0, 0))],
        out_specs=pl.BlockSpec((1, CHUNK, d), lambda b, c: (b, c, 0)),
        out_shape=jax.ShapeDtypeStruct((bsz, total - CHUNK, d), F32),
        compiler_params=pltpu.CompilerParams(
            dimension_semantics=("arbitrary", "arbitrary"), vmem_limit_bytes=VMEM_LIMIT),
        name="final_norm",
    )(view(h1), view(y0), view(y1), view(route), g)


def _rope_perm():
    half = RET_DK // 2
    idx = []
    for part in range(2):
        for h in range(RET_HEADS):
            for dd in range(half):
                idx.append(h * RET_DK + part * half + dd)
    return np.asarray(idx, np.int32)


def _in_col_perm():
    cols = np.arange(IN_COLS, dtype=np.int32)
    rp = _rope_perm()
    cols[C_RQ:C_RQ + RET_KEY_WIDTH] = C_RQ + rp
    cols[C_RK:C_RK + RET_KEY_WIDTH] = C_RK + rp
    return cols


def _hg_tables(nb):
    c = CHUNK
    i = np.arange(c)[:, None]
    t = np.arange(c)[None, :]
    blocks, masks, rsel = [], [], []
    for lvl in range(HG_LEVELS):
        s = 1 << lvl
        r = (i // (2 * s)) * (2 * s) + s - 1
        right = i > r
        m = np.where(right, (t > r) & (t <= i), (t > i) & (t <= r))
        blocks.append(m)
        j = t
        same = (i // (2 * s)) == (j // (2 * s))
        mk = same & right & (j <= r)
        masks.append(np.concatenate([mk, mk], axis=1))
        if s < 8:
            rsel.append(np.broadcast_to(right, (c, HG_WIDTH)))
    blocks.append(t <= i)
    blocks.append(t > i)
    eye = np.eye(nb, dtype=np.float32)
    mall = np.concatenate([np.kron(eye, blk.astype(np.float32)) for blk in blocks], axis=0)
    lvl = np.stack(masks, axis=0).astype(np.float32)
    ch = np.arange(HG_WIDTH)
    bones = (ch[:, None] // HG_DK == ch[None, :] // HG_DK).astype(np.float32)
    rsel = np.stack([np.tile(r, (nb, 1)) for r in rsel], axis=0).astype(np.float32)
    return mall, lvl, rsel, bones


def _ret_tables():
    f32 = jnp.float32
    log_gamma = jnp.log1p(-jnp.power(2.0, -5.0 - jnp.arange(RET_HEADS, dtype=f32)))
    n = jnp.arange(CHUNK, dtype=f32)
    lg = log_gamma[:, None]
    causal = jnp.tril(jnp.ones((CHUNK, CHUNK), dtype=bool))
    intra = jnp.exp(jnp.where(causal[None], (n[:, None] - n[None, :])[None] * lg[:, :, None], -jnp.inf))
    scale = RET_DK ** -0.5
    inter = jnp.exp((n[None, :] + 1.0) * lg)
    to_state = jnp.exp((CHUNK - 1.0 - n[None, :]) * lg)
    carry = jnp.exp(CHUNK * lg)[:, 0]
    head_of_v = np.arange(RET_WIDTH) // RET_DV
    head_of_k = (np.arange(RET_KEY_WIDTH) % (RET_KEY_WIDTH // 2)) // (RET_DK // 2)
    inter_t = (inter * scale).T[:, head_of_v]
    to_state_t = to_state.T[:, head_of_k]
    carry_row = carry[head_of_v][None, :]
    bmask = jnp.asarray((head_of_k[:, None] == head_of_v[None, :]).astype(np.float32))
    intra_pairs = (intra * scale).reshape(RET_HEADS // 2, 2, CHUNK, CHUNK)
    intra_pairs = jnp.concatenate([intra_pairs[:, 0], intra_pairs[:, 1]], axis=2)
    return intra_pairs, inter_t, to_state_t, carry_row, bmask


def _s5_tables(lam_re, lam_im, b_re, b_im, c_re, c_im, d_skip, log_step):
    f32 = jnp.float32
    lam = lax.complex(lam_re.astype(f32), lam_im.astype(f32))
    step = jnp.exp(log_step.astype(f32))[:, None]
    lam_dt = lam * step
    lam_bar = jnp.exp(lam_dt)
    b_bar = ((lam_bar - 1.0) / lam)[..., None] * lax.complex(b_re.astype(f32), b_im.astype(f32))
    eye = jnp.eye(S5_NGROUPS, dtype=f32)
    wb_re = jnp.einsum('gph,gk->ghkp', jnp.real(b_bar), eye).reshape(S5_WIDTH, S5_NSTATE)
    wb_im = jnp.einsum('gph,gk->ghkp', jnp.imag(b_bar), eye).reshape(S5_WIDTH, S5_NSTATE)
    wb = jnp.concatenate([wb_re, wb_im], axis=1)
    wc_re = jnp.einsum('ghp,gk->gpkh', c_re.astype(f32), eye).reshape(S5_NSTATE, S5_WIDTH)
    wc_im = jnp.einsum('ghp,gk->gpkh', c_im.astype(f32), eye).reshape(S5_NSTATE, S5_WIDTH)
    wc = jnp.concatenate([wc_re, -wc_im], axis=0)
    t = jnp.arange(CHUNK, dtype=f32)[:, None, None]
    pp = jnp.exp(lam_dt[None] * t).reshape(CHUNK, S5_NSTATE)
    pn = jnp.exp(-lam_dt[None] * t).reshape(CHUNK, S5_NSTATE)
    lam_rows = jnp.stack([jnp.real(lam_bar).reshape(-1), jnp.imag(lam_bar).reshape(-1)], axis=0)
    return (wb.astype(BF16), wc.astype(BF16), jnp.real(pn), jnp.imag(pn), jnp.real(pp), jnp.imag(pp),
            lam_rows, d_skip.astype(f32).reshape(1, S5_WIDTH))


def kernel(x, meta_tokens, norm_mix_g, w_in, s5_lam_re, s5_lam_im, s5_b_re, s5_b_im, s5_c_re, s5_c_im, s5_d, s5_log_step, s5_w_glu, s5_out_g, hg_conv_w, hg_lb_param, hg_out_g, ret_out_g, w_out, norm_ffn_g, ffn_w1, ffn_w3, ffn_w2, moe_router, moe_w1, moe_w3, moe_w2, final_norm_g):
    bsz, seq_len, d = x.shape
    depth = w_in.shape[0]
    total = seq_len + CHUNK
    n_chunks = total // CHUNK
    n = bsz * total

    meta = jnp.broadcast_to(meta_tokens.astype(F32)[None], (bsz, N_META, d))
    h = jnp.concatenate([jnp.zeros((bsz, PAD, d), F32), meta, x.astype(F32)], axis=1).reshape(n, d)

    pos = (jnp.arange(total) - PAD).astype(F32)
    half = RET_DK // 2
    inv_freq = ROPE_BASE ** (-jnp.arange(half, dtype=F32) / half)
    ang = pos[:, None] * inv_freq[None, :]
    cos_t = jnp.tile(jnp.cos(ang), (1, RET_HEADS))
    sin_t = jnp.tile(jnp.sin(ang), (1, RET_HEADS))

    lb_all = jnp.cumsum(jax.nn.softmax(hg_lb_param.astype(F32), axis=0), axis=0)
    lb_all = lb_all - lb_all[0]

    mall_np, lvl_np, rsel_np, bones_np = _hg_tables(bsz)
    mall = jnp.asarray(mall_np, BF16)
    lvl = jnp.asarray(lvl_np, F32)
    rsel = jnp.asarray(rsel_np, F32)
    bones = jnp.asarray(bones_np, BF16)
    avg = jnp.asarray(bones_np / HG_DK, BF16)
    ltri = jnp.asarray(np.kron(np.eye(bsz, dtype=np.float32),
                               np.tril(np.ones((CHUNK, CHUNK), np.float32))), BF16)
    intra, inter_t, to_state_t, carry_row, bmask = _ret_tables()
    col_perm = jnp.asarray(_in_col_perm())

    out = None
    for l in range(depth):
        w_in_l = jnp.take(w_in[l], col_perm, axis=1).astype(BF16)
        proj = _in_proj(h, norm_mix_g[l].astype(F32).reshape(1, d), w_in_l, total)

        s5c = _s5_tables(s5_lam_re[l], s5_lam_im[l], s5_b_re[l], s5_b_im[l], s5_c_re[l], s5_c_im[l],
                         s5_d[l], s5_log_step[l])
        lb = lb_all[l][None, :]
        lbv = jnp.concatenate([jnp.log(lb), jnp.log1p(-lb), 1.0 - lb], axis=0)
        consts = list(s5c) + [
            s5_w_glu[l].astype(BF16), s5_out_g[l].astype(F32).reshape(1, -1), ltri,
            hg_conv_w[l].astype(F32), lbv, mall, lvl, rsel, bones, avg, hg_out_g[l].astype(F32).reshape(1, -1),
            intra, inter_t, to_state_t, carry_row, bmask, ret_out_g[l].astype(F32).reshape(1, -1),
        ]
        j = l // 2
        if l % 2 == 0:
            ffn_f32 = [ffn_w1[j], ffn_w3[j], ffn_w2[j]]
        else:
            dff_e = moe_w1.shape[-1]
            ffn_f32 = [moe_w1[j].reshape(N_EXPERTS * d, dff_e), moe_w3[j].reshape(N_EXPERTS * d, dff_e),
                       moe_w2[j].reshape(N_EXPERTS * dff_e, d)]
        mixed, ffn_bf16 = _mixers(proj.reshape(bsz, total, IN_COLS), cos_t, sin_t, consts, bsz, n_chunks,
                                  to_bf16=[a.astype(F32) for a in ffn_f32])
        mixed = mixed.reshape(n, D_MIX)

        g_ffn = norm_ffn_g[l].astype(F32).reshape(1, d)
        w_out_l = w_out[l].astype(BF16)
        if l % 2 == 0:
            h1, hn = _out_proj(mixed, h, w_out_l, g_ffn)
            h = _ffn(hn, h1, *ffn_bf16)
            y0 = y1 = None
        else:
            router_pad = jnp.zeros((d, ROUTE_LANES), F32).at[:, :N_EXPERTS].set(moe_router[j].astype(F32))
            h1, hn, route, counts_row = _out_proj(mixed, h, w_out_l, g_ffn, router_pad)
            tm = MOE_TILE
            sorted_tok, tile_src, pos_of_slot, tile_expert, n_valid = _moe_dispatch(route, counts_row, tm)
            e_w1, e_w3, e_w2 = (ffn_bf16[0].reshape(N_EXPERTS, d, dff_e), ffn_bf16[1].reshape(N_EXPERTS, d, dff_e),
                                ffn_bf16[2].reshape(N_EXPERTS, dff_e, d))
            ys = _moe_grouped(tile_expert, n_valid, tile_src, sorted_tok, hn, e_w1, e_w3, e_w2, tm)
            y0 = jnp.take(ys, pos_of_slot[:, 0], axis=0)
            y1 = jnp.take(ys, pos_of_slot[:, 1], axis=0)
            if l < depth - 1:
                h = h1 + (route[:, TOP_K:TOP_K + 1] * y0.astype(F32)
                          + route[:, TOP_K + 1:TOP_K + 2] * y1.astype(F32))

        if l == depth - 1:
            if y0 is None:
                y0 = y1 = jnp.zeros((n, d), BF16)
                h1 = h
                route = jnp.zeros((n, ROUTE_LANES), F32)
            out = _final(h1, y0, y1, route, final_norm_g.astype(F32).reshape(1, d), bsz, total)

    return out.astype(x.dtype)
```

```python
import functools
import math

import jax
import jax.numpy as jnp
import numpy as np
from jax import lax
from jax.experimental import pallas as pl
from jax.experimental.pallas import tpu as pltpu

F32 = jnp.float32
BF16 = jnp.bfloat16

CHUNK = 128
N_META = 16
PAD = CHUNK - N_META
EPS = 1e-6

S5_WIDTH = 256
S5_GROUP = 16
S5_NGROUPS = 16
S5_STATE = 64
S5_NSTATE = S5_NGROUPS * S5_STATE

HG_HEADS = 4
HG_DK = 64
HG_WIDTH = 256
CONV_K = 4
HG_CONV_W = 3 * HG_WIDTH
HG_LEVELS = 7

RET_HEADS = 8
RET_DK = 32
RET_DV = 64
RET_KEY_WIDTH = 256
RET_WIDTH = 512
ROPE_BASE = 10000.0

D_MIX = 1024
IN_COLS = 2816
C_U, C_HQ, C_HGATE, C_RQ, C_RK, C_RV, C_RGATE = 0, 256, 1024, 1280, 1536, 1792, 2304

N_EXPERTS = 8
TOP_K = 2
ROUTE_LANES = 128
MOE_F_STEPS = 2
MOE_TILE = 448
MOE_CHUNK = 256
MOE_DMA_QUEUES = 2

VMEM_LIMIT = 56 * 1024 * 1024


def _sigmoid(x):
    return 1.0 / (1.0 + jnp.exp(-x))


def _split_bf16(x):
    hi = x.astype(BF16)
    lo = (x - hi.astype(F32)).astype(BF16)
    return hi, lo


def _pick_tile(n, candidates):
    for t in candidates:
        if n % t == 0:
            return t
    raise ValueError(f"no tile in {candidates} divides {n}")


def _dot(a, b):
    return jnp.dot(a, b, preferred_element_type=F32)


def _dot_nt(a, b):
    return lax.dot_general(a, b, (((1,), (1,)), ((), ())), preferred_element_type=F32)


def _dot_tn(a, b):
    return lax.dot_general(a, b, (((0,), (0,)), ((), ())), preferred_element_type=F32)


def _in_proj_kernel(h_ref, g_ref, w_ref, o_ref, *, tm, tiles_per_batch):
    x = h_ref[...]
    y = x * lax.rsqrt(jnp.mean(x * x, axis=-1, keepdims=True) + EPS) * g_ref[...]
    proj = _dot(y.astype(BF16), w_ref[...])
    row0 = (pl.program_id(0) % tiles_per_batch) * tm
    rows = row0 + lax.broadcasted_iota(jnp.int32, (tm, 1), 0)
    o_ref[...] = jnp.where(rows >= PAD, proj, 0.0)


def _in_proj(h, g, w_bf16, total):
    n = h.shape[0]
    tm = _pick_tile(total, (320, 128))
    return pl.pallas_call(
        functools.partial(_in_proj_kernel, tm=tm, tiles_per_batch=total // tm),
        grid=(n // tm,),
        in_specs=[
            pl.BlockSpec((tm, h.shape[1]), lambda i: (i, 0)),
            pl.BlockSpec((1, h.shape[1]), lambda i: (0, 0)),
            pl.BlockSpec(w_bf16.shape, lambda i: (0, 0)),
        ],
        out_specs=pl.BlockSpec((tm, IN_COLS), lambda i: (i, 0)),
        out_shape=jax.ShapeDtypeStruct((n, IN_COLS), F32),
        compiler_params=pltpu.CompilerParams(
            dimension_semantics=("arbitrary",), vmem_limit_bytes=VMEM_LIMIT),
        name="in_proj",
    )(h, g, w_bf16)


def _rows(parts):
    return parts[0] if len(parts) == 1 else jnp.concatenate(parts, axis=0)


def _s5_chunks(u, wb_ref, wc_ref, pn_re_ref, pn_im_ref, pp_re_ref, pp_im_ref, lam_ref,
               d_ref, wglu_ref, g_ref, ltri_ref, st_ref):
    ns = S5_NSTATE
    nb = st_ref.shape[0]
    bu = _dot(u.astype(BF16), wb_ref[...])
    pn_re, pn_im = pn_re_ref[...], pn_im_ref[...]
    w_re, w_im = [], []
    for b in range(nb):
        bu_re = bu[b * CHUNK:(b + 1) * CHUNK, :ns]
        bu_im = bu[b * CHUNK:(b + 1) * CHUNK, ns:]
        w_re.append(pn_re * bu_re - pn_im * bu_im)
        w_im.append(pn_re * bu_im + pn_im * bu_re)
    ltri = ltri_ref[...]
    c_re = _dot(ltri, _rows(w_re).astype(BF16))
    c_im = _dot(ltri, _rows(w_im).astype(BF16))
    lam_re, lam_im = lam_ref[0:1, :], lam_ref[1:2, :]
    pp_re, pp_im = pp_re_ref[...], pp_im_ref[...]
    st_re, st_im = [], []
    for b in range(nb):
        s_re, s_im = st_ref[b, 0:1, :], st_ref[b, 1:2, :]
        z_re = c_re[b * CHUNK:(b + 1) * CHUNK] + (lam_re * s_re - lam_im * s_im)
        z_im = c_im[b * CHUNK:(b + 1) * CHUNK] + (lam_re * s_im + lam_im * s_re)
        t_re = pp_re * z_re - pp_im * z_im
        t_im = pp_re * z_im + pp_im * z_re
        st_ref[b, 0:1, :] = t_re[CHUNK - 1:CHUNK, :]
        st_ref[b, 1:2, :] = t_im[CHUNK - 1:CHUNK, :]
        st_re.append(t_re)
        st_im.append(t_im)
    y = (_dot(_rows(st_re).astype(BF16), wc_ref[0:ns, :]) + _dot(_rows(st_im).astype(BF16), wc_ref[ns:2 * ns, :])
         + d_ref[...] * u)
    y = 0.5 * y * (1.0 + jnp.tanh(math.sqrt(2.0 / math.pi) * (y + 0.044715 * (y * y * y))))
    y = y * _sigmoid(_dot(y.astype(BF16), wglu_ref[...]))
    return y * lax.rsqrt(jnp.mean(y * y, axis=-1, keepdims=True) + EPS) * g_ref[...]


def _hgrn2_chunks(xc_ref, gate, convw_ref, lbv_ref, mall_ref, lvl_ref, rsel_ref, bones_ref, avg_ref,
                  g_ref, st_ref):
    w = HG_WIDTH
    nb = st_ref.shape[0]
    rows = nb * CHUNK
    convs = []
    for b in range(nb):
        conv = None
        for i in range(CONV_K):
            term = xc_ref[b, pl.ds(8 - (CONV_K - 1) + i, CHUNK), :] * convw_ref[i:i + 1, :]
            conv = term if conv is None else conv + term
        convs.append(conv)
    conv = _rows(convs)
    cq, cf, v = conv[:, :w], conv[:, w:2 * w], conv[:, 2 * w:]
    q = cq * _sigmoid(cq)
    log_lb, log_1m_lb, one_m_lb = lbv_ref[0:1, :], lbv_ref[1:2, :], lbv_ref[2:3, :]
    log_sig = jnp.minimum(cf, 0.0) - jnp.log(1.0 + jnp.exp(-jnp.abs(cf)))
    b_ = log_1m_lb + log_sig
    logf = jnp.maximum(log_lb, b_) + jnp.log(1.0 + jnp.exp(-jnp.abs(log_lb - b_)))
    kk = one_m_lb * _sigmoid(-cf)
    hi, lo = _split_bf16(logf)

    lvl_sums = _dot(mall_ref[0:HG_LEVELS * rows, :], hi)
    tail = mall_ref[HG_LEVELS * rows:(HG_LEVELS + 2) * rows, :]
    cum_suf = _dot(tail, hi) + _dot(tail, lo)
    g_cum, g_suffix = cum_suf[:rows], cum_suf[rows:]

    lane = lax.broadcasted_iota(jnp.int32, (1, 2 * HG_DK), 1)
    head_sel = [jnp.where(lane < HG_DK, 1.0, 0.0), jnp.where(lane >= HG_DK, 1.0, 0.0)]
    n_pairs = HG_HEADS // 2
    scores = [[None] * n_pairs for _ in range(nb)]
    for lvl in range(HG_LEVELS):
        s = 1 << lvl
        e = jnp.exp(lvl_sums[lvl * rows:(lvl + 1) * rows])
        if s >= 8:
            qk = jnp.concatenate([(q if (r // s) % 2 else kk)[r:r + s] for r in range(0, rows, s)], axis=0)
        else:
            qk = jnp.where(rsel_ref[lvl] > 0.5, q, kk)
        x = qk * e
        m = lvl_ref[lvl]
        for b in range(nb):
            for p in range(n_pairs):
                xp = x[b * CHUNK:(b + 1) * CHUNK, p * 128:(p + 1) * 128]
                rhs = jnp.concatenate([xp * head_sel[0], xp * head_sel[1]], axis=0).astype(BF16)
                sc = _dot_nt(xp.astype(BF16), rhs) * m
                scores[b][p] = sc if scores[b][p] is None else scores[b][p] + sc
    v_bf = v.astype(BF16)
    bones = bones_ref[...]
    qg = (q * jnp.exp(g_cum)).astype(BF16)
    kd = (kk * jnp.exp(g_suffix)).astype(BF16)
    o_rows = []
    for b in range(nb):
        sl = slice(b * CHUNK, (b + 1) * CHUNK)
        o_parts = []
        for p in range(n_pairs):
            vp = v[sl, p * 128:(p + 1) * 128]
            vv = jnp.concatenate([vp * head_sel[0], vp * head_sel[1]], axis=0).astype(BF16)
            o_parts.append(_dot(scores[b][p].astype(BF16), vv))
        st = st_ref[b]
        o_rows.append(jnp.concatenate(o_parts, axis=1) + _dot_nt(qg[sl], st.astype(BF16)))
        upd = _dot_tn(v_bf[sl], kd[sl]) * bones.astype(F32)
        st_ref[b] = st * jnp.exp(g_cum[(b + 1) * CHUNK - 1:(b + 1) * CHUNK, :]) + upd
    o = _rows(o_rows) + _dot((q * kk).astype(BF16), bones) * v
    ms = _dot((o * o).astype(BF16), avg_ref[...])
    return o * lax.rsqrt(ms + EPS) * g_ref[...] * (gate * _sigmoid(gate))


def _ret_chunk(rq, rk, v, cos, sin, intra_ref, inter_ref, tostate_ref, carry_ref, bmask_ref, st_ref):
    hw = RET_KEY_WIDTH // 2

    def rot(t):
        t1, t2 = t[:, :hw], t[:, hw:]
        return jnp.concatenate([t1 * cos - t2 * sin, t1 * sin + t2 * cos], axis=1)

    qr, kr = rot(rq), rot(rk)
    qr_bf, v_bf = qr.astype(BF16), v.astype(BF16)
    lane_k = lax.broadcasted_iota(jnp.int32, (1, RET_KEY_WIDTH), 1) % hw
    lane_v = lax.broadcasted_iota(jnp.int32, (1, 2 * RET_DV), 1)
    v_sel = [jnp.where(lane_v < RET_DV, 1.0, 0.0), jnp.where(lane_v >= RET_DV, 1.0, 0.0)]
    half = RET_DK // 2
    o_parts = []
    for p in range(RET_HEADS // 2):
        sel = [jnp.where((lane_k >= h * half) & (lane_k < (h + 1) * half), 1.0, 0.0)
               for h in (2 * p, 2 * p + 1)]
        rhs = jnp.concatenate([kr * sel[0], kr * sel[1]], axis=0).astype(BF16)
        sc = (_dot_nt(qr_bf, rhs) * intra_ref[p]).astype(BF16)
        vp = v[:, p * 128:(p + 1) * 128]
        vv = jnp.concatenate([vp * v_sel[0], vp * v_sel[1]], axis=0).astype(BF16)
        o_parts.append(_dot(sc, vv))
    o = jnp.concatenate(o_parts, axis=1)
    st = st_ref[...]
    o = o + _dot(qr_bf, st.astype(BF16)) * inter_ref[...]
    kd = (kr * tostate_ref[...]).astype(BF16)
    st_ref[...] = st * carry_ref[...] + _dot_tn(kd, v_bf) * bmask_ref[...]
    return o


def _ret_norm_gate(o, gate, avg_ref, g_ref):
    avg = avg_ref[...]
    outs = []
    for s in range(RET_WIDTH // 256):
        os_ = o[:, s * 256:(s + 1) * 256]
        c = os_ - _dot(os_.astype(BF16), avg)
        outs.append(c * lax.rsqrt(_dot((c * c).astype(BF16), avg) + EPS))
    return jnp.concatenate(outs, axis=1) * g_ref[...] * (gate * _sigmoid(gate))


def _mixer_kernel(proj_ref, cos_ref, sin_ref,
                  wb_ref, wc_ref, pn_re_ref, pn_im_ref, pp_re_ref, pp_im_ref, lam_ref, d_ref,
                  wglu_ref, s5g_ref, ltri_ref,
                  convw_ref, lbv_ref, mall_ref, lvl_ref, rsel_ref, bones_ref, avg_ref, hgg_ref,
                  intra_ref, inter_ref, tostate_ref, carry_ref, bmask_ref, retg_ref,
                  *rest, n_cast, cast_steps):
    cast_in, o_ref, cast_out = rest[:n_cast], rest[n_cast], rest[n_cast + 1:2 * n_cast + 1]
    s5_st, hg_xc, hg_st, ret_st = rest[2 * n_cast + 1:]

    @pl.when(pl.program_id(0) == 0)
    def _():
        s5_st[...] = jnp.zeros_like(s5_st)
        hg_xc[...] = jnp.zeros_like(hg_xc)
        hg_st[...] = jnp.zeros_like(hg_st)
        ret_st[...] = jnp.zeros_like(ret_st)

    if n_cast:
        @pl.when(pl.program_id(0) < cast_steps)
        def _():
            for src, dst in zip(cast_in, cast_out):
                if len(dst.shape) == 2:
                    dst[...] = src[...].astype(dst.dtype)
                else:
                    for k in range(dst.shape[0]):
                        dst[k] = src[:, k * MOE_CHUNK:(k + 1) * MOE_CHUNK].astype(dst.dtype)

    nb = proj_ref.shape[0]

    def cols(c0, width):
        return _rows([proj_ref[b, :, c0:c0 + width] for b in range(nb)])

    def emit(c0, y):
        for b in range(nb):
            o_ref[b, :, c0:c0 + y.shape[1]] = y[b * CHUNK:(b + 1) * CHUNK].astype(o_ref.dtype)

    y_a = _s5_chunks(cols(C_U, S5_WIDTH), wb_ref, wc_ref, pn_re_ref, pn_im_ref, pp_re_ref, pp_im_ref,
                     lam_ref, d_ref, wglu_ref, s5g_ref, ltri_ref, s5_st)
    emit(0, y_a)

    for b in range(nb):
        hg_xc[b, 8:8 + CHUNK, :] = proj_ref[b, :, C_HQ:C_HQ + HG_CONV_W]
    y_b = _hgrn2_chunks(hg_xc, cols(C_HGATE, HG_WIDTH), convw_ref, lbv_ref,
                        mall_ref, lvl_ref, rsel_ref, bones_ref, avg_ref, hgg_ref, hg_st)
    for b in range(nb):
        hg_xc[b, 0:8, :] = hg_xc[b, CHUNK:CHUNK + 8, :]
    emit(S5_WIDTH, y_b)

    o_c = [_ret_chunk(proj_ref[b, :, C_RQ:C_RQ + RET_KEY_WIDTH], proj_ref[b, :, C_RK:C_RK + RET_KEY_WIDTH],
                      proj_ref[b, :, C_RV:C_RV + RET_WIDTH], cos_ref[...], sin_ref[...],
                      intra_ref, inter_ref, tostate_ref, carry_ref, bmask_ref, ret_st.at[b])
           for b in range(nb)]
    y_c = _ret_norm_gate(_rows(o_c), cols(C_RGATE, RET_WIDTH), avg_ref, retg_ref)
    emit(S5_WIDTH + HG_WIDTH, y_c)


def _const_spec(a):
    nd = a.ndim
    return pl.BlockSpec(a.shape, lambda c, _nd=nd: (0,) * _nd)


def _cast_steps(n_chunks, arrays):
    for steps in range(n_chunks, 0, -1):
        if all(a.shape[0] % (steps * 16) == 0 for a in arrays):
            return steps
    raise ValueError("no slab split")


def _mixers(proj, cos, sin, consts, bsz, n_chunks, to_bf16=(), chunked=()):
    total = proj.shape[1]
    n_cast = len(to_bf16)
    chunked = tuple(chunked) or (False,) * n_cast
    cast_steps = _cast_steps(n_chunks, to_bf16) if n_cast else 0
    last = lambda c: jnp.minimum(c, cast_steps - 1)
    cast_in_specs, cast_out_specs, cast_shapes = [], [], []
    for a, ch in zip(to_bf16, chunked):
        r = a.shape[0] // cast_steps
        cast_in_specs.append(pl.BlockSpec((r, a.shape[1]), lambda c: (last(c), 0)))
        if ch:
            kc = a.shape[1] // MOE_CHUNK
            cast_out_specs.append(pl.BlockSpec((kc, r, MOE_CHUNK), lambda c: (0, last(c), 0)))
            cast_shapes.append(jax.ShapeDtypeStruct((kc, a.shape[0], MOE_CHUNK), BF16))
        else:
            cast_out_specs.append(pl.BlockSpec((r, a.shape[1]), lambda c: (last(c), 0)))
            cast_shapes.append(jax.ShapeDtypeStruct(a.shape, BF16))
    in_specs = [
        pl.BlockSpec((bsz, CHUNK, IN_COLS), lambda c: (0, c, 0)),
        pl.BlockSpec((CHUNK, 128), lambda c: (c, 0)),
        pl.BlockSpec((CHUNK, 128), lambda c: (c, 0)),
    ] + [_const_spec(a) for a in consts] + cast_in_specs
    outs = pl.pallas_call(
        functools.partial(_mixer_kernel, n_cast=n_cast, cast_steps=cast_steps),
        grid=(n_chunks,),
        in_specs=in_specs,
        out_specs=[pl.BlockSpec((bsz, CHUNK, D_MIX), lambda c: (0, c, 0))] + cast_out_specs,
        out_shape=[jax.ShapeDtypeStruct((bsz, total, D_MIX), BF16)] + cast_shapes,
        scratch_shapes=[
            pltpu.VMEM((bsz, 2, S5_NSTATE), F32),
            pltpu.VMEM((bsz, CHUNK + 8, HG_CONV_W), F32),
            pltpu.VMEM((bsz, HG_WIDTH, HG_WIDTH), F32),
            pltpu.VMEM((bsz, RET_KEY_WIDTH, RET_WIDTH), F32),
        ],
        compiler_params=pltpu.CompilerParams(
            dimension_semantics=("arbitrary",), vmem_limit_bytes=VMEM_LIMIT),
        name="mixers",
    )(proj, cos, sin, *consts, *to_bf16)
    return outs[0], list(outs[1:])


def _route(hn, router_ref, ltri_ref, count_ref):
    r = router_ref[...]
    r_hi, r_lo = _split_bf16(r)
    h_hi, h_lo = _split_bf16(hn)
    logits = _dot(h_hi, r_hi) + (_dot(h_lo, r_hi) + _dot(h_hi, r_lo))
    lane_i = lax.broadcasted_iota(jnp.int32, logits.shape, 1)
    lane = lane_i.astype(F32)
    neg = jnp.float32(-jnp.inf)
    logits = jnp.where(lane_i < N_EXPERTS, logits, neg)
    m1 = jnp.max(logits, axis=-1, keepdims=True)
    i1 = jnp.min(jnp.where(logits == m1, lane, float(ROUTE_LANES)), axis=-1, keepdims=True)
    rest = jnp.where(lane == i1, neg, logits)
    m2 = jnp.max(rest, axis=-1, keepdims=True)
    i2 = jnp.min(jnp.where(rest == m2, lane, float(ROUTE_LANES)), axis=-1, keepdims=True)
    e2 = jnp.exp(m2 - m1)
    g1 = 1.0 / (1.0 + e2)
    g2 = e2 / (1.0 + e2)
    onehot = jnp.where((lane == i1) | (lane == i2), 1.0, 0.0)
    before = _dot(ltri_ref[...], onehot.astype(BF16)) + count_ref[...]
    r1 = jnp.sum(jnp.where(lane == i1, before, 0.0), axis=-1, keepdims=True)
    r2 = jnp.sum(jnp.where(lane == i2, before, 0.0), axis=-1, keepdims=True)
    count_ref[...] += jnp.sum(onehot, axis=0, keepdims=True)
    out = jnp.where(lane == 0, i1, 0.0)
    out = jnp.where(lane == 1, i2, out)
    out = jnp.where(lane == 2, g1, out)
    out = jnp.where(lane == 3, g2, out)
    out = jnp.where(lane == 4, r1, out)
    return jnp.where(lane == 5, r2, out)


def _out_proj_kernel(mixed_ref, h_ref, w_ref, g_ref, *rest, routed):
    if routed:
        router_ref, ltri_ref, h1_ref, hn_ref, route_ref, count_ref = rest

        @pl.when(pl.program_id(0) == 0)
        def _():
            count_ref[...] = jnp.zeros_like(count_ref)
    else:
        h1_ref, hn_ref = rest
    h1 = h_ref[...] + _dot(mixed_ref[...], w_ref[...])
    h1_ref[...] = h1
    hn = h1 * lax.rsqrt(jnp.mean(h1 * h1, axis=-1, keepdims=True) + EPS) * g_ref[...]
    hn_ref[...] = hn.astype(hn_ref.dtype)
    if routed:
        route_ref[...] = _route(hn, router_ref, ltri_ref, count_ref)


def _out_proj(mixed, h, w_bf16, g, router_pad=None):
    n, d = h.shape
    tm = _pick_tile(n, (640, 128))
    routed = router_pad is not None
    row = lambda i: (i, 0)
    fixed = lambda i: (0, 0)
    in_specs = [pl.BlockSpec((tm, D_MIX), row), pl.BlockSpec((tm, d), row),
                pl.BlockSpec(w_bf16.shape, fixed), pl.BlockSpec((1, d), fixed)]
    out_specs = [pl.BlockSpec((tm, d), row), pl.BlockSpec((tm, d), row)]
    out_shape = [jax.ShapeDtypeStruct((n, d), F32), jax.ShapeDtypeStruct((n, d), F32 if routed else BF16)]
    args = [mixed, h, w_bf16, g]
    if routed:
        ltri = jnp.asarray(np.tril(np.ones((tm, tm), np.float32), -1), BF16)
        in_specs += [pl.BlockSpec(router_pad.shape, fixed), pl.BlockSpec((tm, tm), fixed)]
        out_specs += [pl.BlockSpec((tm, ROUTE_LANES), row), pl.BlockSpec((1, ROUTE_LANES), fixed)]
        out_shape += [jax.ShapeDtypeStruct((n, ROUTE_LANES), F32),
                      jax.ShapeDtypeStruct((1, ROUTE_LANES), F32)]
        args += [router_pad, ltri]
    return pl.pallas_call(
        functools.partial(_out_proj_kernel, routed=routed),
        grid=(n // tm,),
        in_specs=in_specs,
        out_specs=out_specs,
        out_shape=out_shape,
        compiler_params=pltpu.CompilerParams(
            dimension_semantics=("arbitrary",), vmem_limit_bytes=VMEM_LIMIT),
        name="out_proj_routed" if routed else "out_proj",
    )(*args)


def _ffn_kernel(hn_ref, h1_ref, w1_ref, w3_ref, w2_ref, o_ref, acc_ref):
    f = pl.program_id(1)
    x = hn_ref[...]
    a = _dot(x, w1_ref[...])
    b = _dot(x, w3_ref[...])
    part = _dot((a * _sigmoid(a) * b).astype(BF16), w2_ref[...])

    @pl.when(f == 0)
    def _():
        acc_ref[...] = part

    @pl.when(f > 0)
    def _():
        acc_ref[...] += part

    @pl.when(f == pl.num_programs(1) - 1)
    def _():
        o_ref[...] = h1_ref[...] + acc_ref[...]


def _ffn(hn, h1, w1, w3, w2):
    n, d = h1.shape
    dff = w1.shape[1]
    tm, tf = _pick_tile(n, (640, 128)), _pick_tile(dff, (1408, 128))
    return pl.pallas_call(
        _ffn_kernel,
        grid=(n // tm, dff // tf),
        in_specs=[
            pl.BlockSpec((tm, d), lambda i, f: (i, 0)),
            pl.BlockSpec((tm, d), lambda i, f: (i, 0)),
            pl.BlockSpec((d, tf), lambda i, f: (0, f)),
            pl.BlockSpec((d, tf), lambda i, f: (0, f)),
            pl.BlockSpec((tf, d), lambda i, f: (f, 0)),
        ],
        out_specs=pl.BlockSpec((tm, d), lambda i, f: (i, 0)),
        out_shape=jax.ShapeDtypeStruct((n, d), F32),
        scratch_shapes=[pltpu.VMEM((tm, d), F32)],
        compiler_params=pltpu.CompilerParams(
            dimension_semantics=("arbitrary", "arbitrary"), vmem_limit_bytes=VMEM_LIMIT),
        name="ffn_dense",
    )(hn, h1, w1, w3, w2)


def _moe_kernel(te_ref, nv_ref, src_ref, tok_ref, hn_hbm, w1_ref, w3_ref, w2_ref, o_ref,
                xs_ref, xb_ref, acc_ref, sem, *, tm):
    i, f = pl.program_id(0), pl.program_id(1)
    nf = pl.num_programs(1)
    n_valid = nv_ref[0]
    kc = w1_ref.shape[0]
    share = tm // (MOE_F_STEPS * kc)

    def row_copy(tile, r):
        return pltpu.make_async_copy(hn_hbm.at[pl.ds(tok_ref[src_ref[tile] + r], 1)],
                                     xs_ref.at[pl.ds(r, 1)], sem)

    @pl.when((i == 0) & (f == 0))
    def _():
        def body(r, carry):
            row_copy(0, r).start()
            return carry
        lax.fori_loop(0, tm, body, 0)

    @pl.when((f == 0) & (i <= n_valid))
    def _():
        pltpu.make_async_copy(xs_ref, xs_ref, sem).wait()
        xb_ref[...] = xs_ref[...].astype(BF16)

    @pl.when(i < n_valid)
    def _():
        @pl.when(f == 0)
        def _():
            acc_ref[...] = jnp.zeros_like(acc_ref)

        def chunk(k, carry):
            first = (f * kc + k) * share
            for r in range(share):
                row_copy(i + 1, first + r).start(priority=r % MOE_DMA_QUEUES)
            x = xb_ref[...]
            a = _dot(x, w1_ref[k])
            b = _dot(x, w3_ref[k])
            hid = (a * _sigmoid(a) * b).astype(BF16)
            rows = pl.ds(pl.multiple_of(k * MOE_CHUNK, MOE_CHUNK), MOE_CHUNK)
            acc_ref[...] += _dot(hid, w2_ref[rows, :])
            return carry

        lax.fori_loop(0, kc, chunk, 0)

    @pl.when(f == nf - 1)
    def _():
        o_ref[...] = jnp.where(i < n_valid, acc_ref[...], 0.0).astype(o_ref.dtype)


def _live_f(i, f, nv):
    return jnp.where(i < nv[0], f, 0)


def _moe_grouped(tile_expert, n_valid, tile_src, sorted_tok, hn, w1, w3, w2, tm):
    d = hn.shape[1]
    n_tiles = tile_expert.shape[0]
    kc_all = w1.shape[0]
    assert kc_all % MOE_F_STEPS == 0 and tm % kc_all == 0
    kc = kc_all // MOE_F_STEPS
    tf = kc * MOE_CHUNK
    grid_spec = pltpu.PrefetchScalarGridSpec(
        num_scalar_prefetch=4,
        grid=(n_tiles, MOE_F_STEPS),
        in_specs=[
            pl.BlockSpec(memory_space=pl.ANY),
            pl.BlockSpec((kc, d, MOE_CHUNK), lambda i, f, te, nv, *_: (_live_f(i, f, nv), te[i], 0)),
            pl.BlockSpec((kc, d, MOE_CHUNK), lambda i, f, te, nv, *_: (_live_f(i, f, nv), te[i], 0)),
            pl.BlockSpec((tf, d), lambda i, f, te, nv, *_: (te[i] * MOE_F_STEPS + _live_f(i, f, nv), 0)),
        ],
        out_specs=pl.BlockSpec((tm, d), lambda i, f, *_: (i, 0)),
        scratch_shapes=[pltpu.VMEM((tm, d), F32), pltpu.VMEM((tm, d), BF16), pltpu.VMEM((tm, d), F32),
                        pltpu.SemaphoreType.DMA(())],
    )
    return pl.pallas_call(
        functools.partial(_moe_kernel, tm=tm),
        grid_spec=grid_spec,
        out_shape=jax.ShapeDtypeStruct((n_tiles * tm, d), BF16),
        compiler_params=pltpu.CompilerParams(
            dimension_semantics=("arbitrary", "arbitrary"), vmem_limit_bytes=VMEM_LIMIT),
        name="moe_grouped",
    )(tile_expert, n_valid, tile_src, sorted_tok, hn, w1, w3, w2)


def _moe_dispatch(route, counts_row, tm):
    n = route.shape[0]
    n_slots = n * TOP_K
    n_tiles = (n_slots + N_EXPERTS * tm + tm - 1) // tm
    counts = counts_row[0, :N_EXPERTS].astype(jnp.int32)
    padded = ((counts + tm - 1) // tm) * tm
    pend = jnp.cumsum(padded)
    pstart = pend - padded
    gstart = jnp.cumsum(counts) - counts
    eid = route[:, 0:TOP_K].astype(jnp.int32)
    rank = route[:, 2 * TOP_K:3 * TOP_K].astype(jnp.int32)
    pos = pstart[eid] + rank
    tok = jnp.broadcast_to(jnp.arange(n, dtype=jnp.int32)[:, None], (n, TOP_K))
    _, sorted_tok = lax.sort(((gstart[eid] + rank).reshape(-1), tok.reshape(-1)), num_keys=1)
    sorted_tok = jnp.concatenate([sorted_tok, jnp.zeros((tm,), jnp.int32)])
    tile_start = jnp.arange(n_tiles, dtype=jnp.int32) * tm
    tile_expert = jnp.minimum(jnp.sum((tile_start[:, None] >= pend[None, :]).astype(jnp.int32), axis=1),
                              N_EXPERTS - 1)
    tile_src = jnp.clip(tile_start - (pstart - gstart)[tile_expert], 0, n_slots)
    n_valid = (pend[-1] // tm).astype(jnp.int32).reshape(1)
    return sorted_tok, tile_src, pos, tile_expert, n_valid


def _final_kernel(h1_ref, y0_ref, y1_ref, route_ref, g_ref, o_ref):
    g0 = route_ref[:, TOP_K:TOP_K + 1]
    g1 = route_ref[:, TOP_K + 1:TOP_K + 2]
    h = h1_ref[...] + (g0 * y0_ref[...].astype(F32) + g1 * y1_ref[...].astype(F32))
    o_ref[...] = h * lax.rsqrt(jnp.mean(h * h, axis=-1, keepdims=True) + EPS) * g_ref[...]


def _final(h1, y0, y1, route, g, bsz, total):
    d = h1.shape[1]
    seq = total - CHUNK
    tm = _pick_tile(seq, (1024, 128))
    per_seq = seq // tm
    src = lambda b, k: (pl.multiple_of(b * total + CHUNK + k * tm, CHUNK), 0)
    rows = lambda width: pl.BlockSpec((pl.Element(tm), pl.Element(width)), src)
    out = pl.pallas_call(
        _final_kernel,
        grid=(bsz, per_seq),
        in_specs=[rows(d), rows(d), rows(d), rows(ROUTE_LANES), pl.BlockSpec((1, d), lambda b, k: (0, 0))],
        out_specs=pl.BlockSpec((tm, d), lambda b, k: (b * per_seq + k, 0)),
        out_shape=jax.ShapeDtypeStruct((bsz * seq, d), F32),
        compiler_params=pltpu.CompilerParams(
            dimension_semantics=("arbitrary", "arbitrary"), vmem_limit_bytes=VMEM_LIMIT),
        name="final_norm",
    )(h1, y0, y1, route, g)
    return out.reshape(bsz, seq, d)


def _rope_perm():
    half = RET_DK // 2
    idx = []
    for part in range(2):
        for h in range(RET_HEADS):
            for dd in range(half):
                idx.append(h * RET_DK + part * half + dd)
    return np.asarray(idx, np.int32)


def _in_col_perm():
    cols = np.arange(IN_COLS, dtype=np.int32)
    rp = _rope_perm()
    cols[C_RQ:C_RQ + RET_KEY_WIDTH] = C_RQ + rp
    cols[C_RK:C_RK + RET_KEY_WIDTH] = C_RK + rp
    return cols


def _hg_tables(nb):
    c = CHUNK
    i = np.arange(c)[:, None]
    t = np.arange(c)[None, :]
    blocks, masks, rsel = [], [], []
    for lvl in range(HG_LEVELS):
        s = 1 << lvl
        r = (i // (2 * s)) * (2 * s) + s - 1
        right = i > r
        m = np.where(right, (t > r) & (t <= i), (t > i) & (t <= r))
        blocks.append(m)
        j = t
        same = (i // (2 * s)) == (j // (2 * s))
        mk = same & right & (j <= r)
        masks.append(np.concatenate([mk, mk], axis=1))
        if s < 8:
            rsel.append(np.broadcast_to(right, (c, HG_WIDTH)))
    blocks.append(t <= i)
    blocks.append(t > i)
    eye = np.eye(nb, dtype=np.float32)
    mall = np.concatenate([np.kron(eye, blk.astype(np.float32)) for blk in blocks], axis=0)
    lvl = np.stack(masks, axis=0).astype(np.float32)
    ch = np.arange(HG_WIDTH)
    bones = (ch[:, None] // HG_DK == ch[None, :] // HG_DK).astype(np.float32)
    rsel = np.stack([np.tile(r, (nb, 1)) for r in rsel], axis=0).astype(np.float32)
    return mall, lvl, rsel, bones


def _ret_tables():
    f32 = jnp.float32
    log_gamma = jnp.log1p(-jnp.power(2.0, -5.0 - jnp.arange(RET_HEADS, dtype=f32)))
    n = jnp.arange(CHUNK, dtype=f32)
    lg = log_gamma[:, None]
    causal = jnp.tril(jnp.ones((CHUNK, CHUNK), dtype=bool))
    intra = jnp.exp(jnp.where(causal[None], (n[:, None] - n[None, :])[None] * lg[:, :, None], -jnp.inf))
    scale = RET_DK ** -0.5
    inter = jnp.exp((n[None, :] + 1.0) * lg)
    to_state = jnp.exp((CHUNK - 1.0 - n[None, :]) * lg)
    carry = jnp.exp(CHUNK * lg)[:, 0]
    head_of_v = np.arange(RET_WIDTH) // RET_DV
    head_of_k = (np.arange(RET_KEY_WIDTH) % (RET_KEY_WIDTH // 2)) // (RET_DK // 2)
    inter_t = (inter * scale).T[:, head_of_v]
    to_state_t = to_state.T[:, head_of_k]
    carry_row = carry[head_of_v][None, :]
    bmask = jnp.asarray((head_of_k[:, None] == head_of_v[None, :]).astype(np.float32))
    intra_pairs = (intra * scale).reshape(RET_HEADS // 2, 2, CHUNK, CHUNK)
    intra_pairs = jnp.concatenate([intra_pairs[:, 0], intra_pairs[:, 1]], axis=2)
    return intra_pairs, inter_t, to_state_t, carry_row, bmask


def _s5_tables(lam_re, lam_im, b_re, b_im, c_re, c_im, d_skip, log_step):
    f32 = jnp.float32
    lam = lax.complex(lam_re.astype(f32), lam_im.astype(f32))
    step = jnp.exp(log_step.astype(f32))[:, None]
    lam_dt = lam * step
    lam_bar = jnp.exp(lam_dt)
    b_bar = ((lam_bar - 1.0) / lam)[..., None] * lax.complex(b_re.astype(f32), b_im.astype(f32))
    eye = jnp.eye(S5_NGROUPS, dtype=f32)
    wb_re = jnp.einsum('gph,gk->ghkp', jnp.real(b_bar), eye).reshape(S5_WIDTH, S5_NSTATE)
    wb_im = jnp.einsum('gph,gk->ghkp', jnp.imag(b_bar), eye).reshape(S5_WIDTH, S5_NSTATE)
    wb = jnp.concatenate([wb_re, wb_im], axis=1)
    wc_re = jnp.einsum('ghp,gk->gpkh', c_re.astype(f32), eye).reshape(S5_NSTATE, S5_WIDTH)
    wc_im = jnp.einsum('ghp,gk->gpkh', c_im.astype(f32), eye).reshape(S5_NSTATE, S5_WIDTH)
    wc = jnp.concatenate([wc_re, -wc_im], axis=0)
    t = jnp.arange(CHUNK, dtype=f32)[:, None, None]
    pp = jnp.exp(lam_dt[None] * t).reshape(CHUNK, S5_NSTATE)
    pn = jnp.exp(-lam_dt[None] * t).reshape(CHUNK, S5_NSTATE)
    lam_rows = jnp.stack([jnp.real(lam_bar).reshape(-1), jnp.imag(lam_bar).reshape(-1)], axis=0)
    return (wb.astype(BF16), wc.astype(BF16), jnp.real(pn), jnp.imag(pn), jnp.real(pp), jnp.imag(pp),
            lam_rows, d_skip.astype(f32).reshape(1, S5_WIDTH))


def kernel(x, meta_tokens, norm_mix_g, w_in, s5_lam_re, s5_lam_im, s5_b_re, s5_b_im, s5_c_re, s5_c_im, s5_d, s5_log_step, s5_w_glu, s5_out_g, hg_conv_w, hg_lb_param, hg_out_g, ret_out_g, w_out, norm_ffn_g, ffn_w1, ffn_w3, ffn_w2, moe_router, moe_w1, moe_w3, moe_w2, final_norm_g):
    bsz, seq_len, d = x.shape
    depth = w_in.shape[0]
    total = seq_len + CHUNK
    n_chunks = total // CHUNK
    n = bsz * total

    meta = jnp.broadcast_to(meta_tokens.astype(F32)[None], (bsz, N_META, d))
    h = jnp.concatenate([jnp.zeros((bsz, PAD, d), F32), meta, x.astype(F32)], axis=1).reshape(n, d)

    pos = (jnp.arange(total) - PAD).astype(F32)
    half = RET_DK // 2
    inv_freq = ROPE_BASE ** (-jnp.arange(half, dtype=F32) / half)
    ang = pos[:, None] * inv_freq[None, :]
    cos_t = jnp.tile(jnp.cos(ang), (1, RET_HEADS))
    sin_t = jnp.tile(jnp.sin(ang), (1, RET_HEADS))

    lb_all = jnp.cumsum(jax.nn.softmax(hg_lb_param.astype(F32), axis=0), axis=0)
    lb_all = lb_all - lb_all[0]

    mall_np, lvl_np, rsel_np, bones_np = _hg_tables(bsz)
    mall = jnp.asarray(mall_np, BF16)
    lvl = jnp.asarray(lvl_np, F32)
    rsel = jnp.asarray(rsel_np, F32)
    bones = jnp.asarray(bones_np, BF16)
    avg = jnp.asarray(bones_np / HG_DK, BF16)
    ltri = jnp.asarray(np.kron(np.eye(bsz, dtype=np.float32),
                               np.tril(np.ones((CHUNK, CHUNK), np.float32))), BF16)
    intra, inter_t, to_state_t, carry_row, bmask = _ret_tables()
    col_perm = jnp.asarray(_in_col_perm())

    out = None
    for l in range(depth):
        w_in_l = jnp.take(w_in[l], col_perm, axis=1).astype(BF16)
        proj = _in_proj(h, norm_mix_g[l].astype(F32).reshape(1, d), w_in_l, total)

        s5c = _s5_tables(s5_lam_re[l], s5_lam_im[l], s5_b_re[l], s5_b_im[l], s5_c_re[l], s5_c_im[l],
                         s5_d[l], s5_log_step[l])
        lb = lb_all[l][None, :]
        lbv = jnp.concatenate([jnp.log(lb), jnp.log1p(-lb), 1.0 - lb], axis=0)
        consts = list(s5c) + [
            s5_w_glu[l].astype(BF16), s5_out_g[l].astype(F32).reshape(1, -1), ltri,
            hg_conv_w[l].astype(F32), lbv, mall, lvl, rsel, bones, avg, hg_out_g[l].astype(F32).reshape(1, -1),
            intra, inter_t, to_state_t, carry_row, bmask, ret_out_g[l].astype(F32).reshape(1, -1),
        ]
        j = l // 2
        if l % 2 == 0:
            ffn_f32 = [ffn_w1[j], ffn_w3[j], ffn_w2[j]]
        else:
            dff_e = moe_w1.shape[-1]
            ffn_f32 = [moe_w1[j].reshape(N_EXPERTS * d, dff_e), moe_w3[j].reshape(N_EXPERTS * d, dff_e),
                       moe_w2[j].reshape(N_EXPERTS * dff_e, d)]
        mixed, ffn_bf16 = _mixers(proj.reshape(bsz, total, IN_COLS), cos_t, sin_t, consts, bsz, n_chunks,
                                  to_bf16=[a.astype(F32) for a in ffn_f32],
                                  chunked=(False,) * 3 if l % 2 == 0 else (True, True, False))
        mixed = mixed.reshape(n, D_MIX)

        g_ffn = norm_ffn_g[l].astype(F32).reshape(1, d)
        w_out_l = w_out[l].astype(BF16)
        if l % 2 == 0:
            h1, hn = _out_proj(mixed, h, w_out_l, g_ffn)
            h = _ffn(hn, h1, *ffn_bf16)
            y0 = y1 = None
        else:
            router_pad = jnp.zeros((d, ROUTE_LANES), F32).at[:, :N_EXPERTS].set(moe_router[j].astype(F32))
            h1, hn, route, counts_row = _out_proj(mixed, h, w_out_l, g_ffn, router_pad)
            tm = MOE_TILE
            sorted_tok, tile_src, pos_of_slot, tile_expert, n_valid = _moe_dispatch(route, counts_row, tm)
            ys = _moe_grouped(tile_expert, n_valid, tile_src, sorted_tok, hn, *ffn_bf16, tm)
            y0 = jnp.take(ys, pos_of_slot[:, 0], axis=0)
            y1 = jnp.take(ys, pos_of_slot[:, 1], axis=0)
            if l < depth - 1:
                h = h1 + (route[:, TOP_K:TOP_K + 1] * y0.astype(F32)
                          + route[:, TOP_K + 1:TOP_K + 2] * y1.astype(F32))

        if l == depth - 1:
            if y0 is None:
                y0 = y1 = jnp.zeros((n, d), BF16)
                h1 = h
                route = jnp.zeros((n, ROUTE_LANES), F32)
            out = _final(h1, y0, y1, route, final_norm_g.astype(F32).reshape(1, d), bsz, total)

    return out.astype(x.dtype)
```

```python
import functools
import math

import jax
import jax.numpy as jnp
import numpy as np
from jax import lax
from jax.experimental import pallas as pl
from jax.experimental.pallas import tpu as pltpu

F32 = jnp.float32
BF16 = jnp.bfloat16

CHUNK = 128
N_META = 16
PAD = CHUNK - N_META
EPS = 1e-6

S5_WIDTH = 256
S5_GROUP = 16
S5_NGROUPS = 16
S5_STATE = 64
S5_NSTATE = S5_NGROUPS * S5_STATE

HG_HEADS = 4
HG_DK = 64
HG_WIDTH = 256
CONV_K = 4
HG_CONV_W = 3 * HG_WIDTH
HG_LEVELS = 7

RET_HEADS = 8
RET_DK = 32
RET_DV = 64
RET_KEY_WIDTH = 256
RET_WIDTH = 512
ROPE_BASE = 10000.0

D_MIX = 1024
IN_COLS = 2816
C_U, C_HQ, C_HGATE, C_RQ, C_RK, C_RV, C_RGATE = 0, 256, 1024, 1280, 1536, 1792, 2304

N_EXPERTS = 8
TOP_K = 2
ROUTE_LANES = 128
MOE_F_STEPS = 2
MOE_TILE = 512
MOE_DMA_QUEUES = 2

VMEM_LIMIT = 56 * 1024 * 1024


def _sigmoid(x):
    return 1.0 / (1.0 + jnp.exp(-x))


def _split_bf16(x):
    hi = x.astype(BF16)
    lo = (x - hi.astype(F32)).astype(BF16)
    return hi, lo


def _pick_tile(n, candidates):
    for t in candidates:
        if n % t == 0:
            return t
    raise ValueError(f"no tile in {candidates} divides {n}")


def _dot(a, b):
    return jnp.dot(a, b, preferred_element_type=F32)


def _dot_nt(a, b):
    return lax.dot_general(a, b, (((1,), (1,)), ((), ())), preferred_element_type=F32)


def _dot_tn(a, b):
    return lax.dot_general(a, b, (((0,), (0,)), ((), ())), preferred_element_type=F32)


def _in_proj_kernel(h_ref, g_ref, w_ref, o_ref, *, tm, tiles_per_batch):
    x = h_ref[...]
    y = x * lax.rsqrt(jnp.mean(x * x, axis=-1, keepdims=True) + EPS) * g_ref[...]
    proj = _dot(y.astype(BF16), w_ref[...])
    row0 = (pl.program_id(0) % tiles_per_batch) * tm
    rows = row0 + lax.broadcasted_iota(jnp.int32, (tm, 1), 0)
    o_ref[...] = jnp.where(rows >= PAD, proj, 0.0)


def _in_proj(h, g, w_bf16, total):
    n = h.shape[0]
    tm = _pick_tile(total, (640, 128))
    return pl.pallas_call(
        functools.partial(_in_proj_kernel, tm=tm, tiles_per_batch=total // tm),
        grid=(n // tm,),
        in_specs=[
            pl.BlockSpec((tm, h.shape[1]), lambda i: (i, 0)),
            pl.BlockSpec((1, h.shape[1]), lambda i: (0, 0)),
            pl.BlockSpec(w_bf16.shape, lambda i: (0, 0), pipeline_mode=pl.Buffered(1)),
        ],
        out_specs=pl.BlockSpec((tm, IN_COLS), lambda i: (i, 0)),
        out_shape=jax.ShapeDtypeStruct((n, IN_COLS), F32),
        compiler_params=pltpu.CompilerParams(
            dimension_semantics=("arbitrary",), vmem_limit_bytes=VMEM_LIMIT),
        name="in_proj",
    )(h, g, w_bf16)


def _rows(parts):
    return parts[0] if len(parts) == 1 else jnp.concatenate(parts, axis=0)


def _s5_chunks(u, wb_ref, wc_ref, pn_re_ref, pn_im_ref, pp_re_ref, pp_im_ref, lam_ref,
               d_ref, wglu_ref, g_ref, ltri_ref, st_ref):
    ns = S5_NSTATE
    nb = st_ref.shape[0]
    bu = _dot(u.astype(BF16), wb_ref[...])
    pn_re, pn_im = pn_re_ref[...], pn_im_ref[...]
    w_re, w_im = [], []
    for b in range(nb):
        bu_re = bu[b * CHUNK:(b + 1) * CHUNK, :ns]
        bu_im = bu[b * CHUNK:(b + 1) * CHUNK, ns:]
        w_re.append(pn_re * bu_re - pn_im * bu_im)
        w_im.append(pn_re * bu_im + pn_im * bu_re)
    ltri = ltri_ref[...]
    c_re = _dot(ltri, _rows(w_re).astype(BF16))
    c_im = _dot(ltri, _rows(w_im).astype(BF16))
    lam_re, lam_im = lam_ref[0:1, :], lam_ref[1:2, :]
    pp_re, pp_im = pp_re_ref[...], pp_im_ref[...]
    st_re, st_im = [], []
    for b in range(nb):
        s_re, s_im = st_ref[b, 0:1, :], st_ref[b, 1:2, :]
        z_re = c_re[b * CHUNK:(b + 1) * CHUNK] + (lam_re * s_re - lam_im * s_im)
        z_im = c_im[b * CHUNK:(b + 1) * CHUNK] + (lam_re * s_im + lam_im * s_re)
        t_re = pp_re * z_re - pp_im * z_im
        t_im = pp_re * z_im + pp_im * z_re
        st_ref[b, 0:1, :] = t_re[CHUNK - 1:CHUNK, :]
        st_ref[b, 1:2, :] = t_im[CHUNK - 1:CHUNK, :]
        st_re.append(t_re)
        st_im.append(t_im)
    y = (_dot(_rows(st_re).astype(BF16), wc_ref[0:ns, :]) + _dot(_rows(st_im).astype(BF16), wc_ref[ns:2 * ns, :])
         + d_ref[...] * u)
    y = 0.5 * y * (1.0 + jnp.tanh(math.sqrt(2.0 / math.pi) * (y + 0.044715 * (y * y * y))))
    y = y * _sigmoid(_dot(y.astype(BF16), wglu_ref[...]))
    return y * lax.rsqrt(jnp.mean(y * y, axis=-1, keepdims=True) + EPS) * g_ref[...]


def _hgrn2_chunks(xc_ref, gate, convw_ref, lbv_ref, mall_ref, lvl_ref, rsel_ref, bones_ref, avg_ref,
                  g_ref, st_ref):
    w = HG_WIDTH
    nb = st_ref.shape[0]
    rows = nb * CHUNK
    convs = []
    for b in range(nb):
        conv = None
        for i in range(CONV_K):
            term = xc_ref[b, pl.ds(8 - (CONV_K - 1) + i, CHUNK), :] * convw_ref[i:i + 1, :]
            conv = term if conv is None else conv + term
        convs.append(conv)
    conv = _rows(convs)
    cq, cf, v = conv[:, :w], conv[:, w:2 * w], conv[:, 2 * w:]
    q = cq * _sigmoid(cq)
    log_lb, log_1m_lb, one_m_lb = lbv_ref[0:1, :], lbv_ref[1:2, :], lbv_ref[2:3, :]
    log_sig = jnp.minimum(cf, 0.0) - jnp.log(1.0 + jnp.exp(-jnp.abs(cf)))
    b_ = log_1m_lb + log_sig
    logf = jnp.maximum(log_lb, b_) + jnp.log(1.0 + jnp.exp(-jnp.abs(log_lb - b_)))
    kk = one_m_lb * _sigmoid(-cf)
    hi, lo = _split_bf16(logf)

    lvl_sums = _dot(mall_ref[0:HG_LEVELS * rows, :], hi)
    tail = mall_ref[HG_LEVELS * rows:(HG_LEVELS + 2) * rows, :]
    cum_suf = _dot(tail, hi) + _dot(tail, lo)
    g_cum, g_suffix = cum_suf[:rows], cum_suf[rows:]

    lane = lax.broadcasted_iota(jnp.int32, (1, 2 * HG_DK), 1)
    head_sel = [jnp.where(lane < HG_DK, 1.0, 0.0), jnp.where(lane >= HG_DK, 1.0, 0.0)]
    n_pairs = HG_HEADS // 2
    scores = [[None] * n_pairs for _ in range(nb)]
    for lvl in range(HG_LEVELS):
        s = 1 << lvl
        e = jnp.exp(lvl_sums[lvl * rows:(lvl + 1) * rows])
        if s >= 8:
            qk = jnp.concatenate([(q if (r // s) % 2 else kk)[r:r + s] for r in range(0, rows, s)], axis=0)
        else:
            qk = jnp.where(rsel_ref[lvl] > 0.5, q, kk)
        x = qk * e
        m = lvl_ref[lvl]
        for b in range(nb):
            for p in range(n_pairs):
                xp = x[b * CHUNK:(b + 1) * CHUNK, p * 128:(p + 1) * 128]
                rhs = jnp.concatenate([xp * head_sel[0], xp * head_sel[1]], axis=0).astype(BF16)
                sc = _dot_nt(xp.astype(BF16), rhs) * m
                scores[b][p] = sc if scores[b][p] is None else scores[b][p] + sc
    v_bf = v.astype(BF16)
    bones = bones_ref[...]
    qg = (q * jnp.exp(g_cum)).astype(BF16)
    kd = (kk * jnp.exp(g_suffix)).astype(BF16)
    o_rows = []
    for b in range(nb):
        sl = slice(b * CHUNK, (b + 1) * CHUNK)
        o_parts = []
        for p in range(n_pairs):
            vp = v[sl, p * 128:(p + 1) * 128]
            vv = jnp.concatenate([vp * head_sel[0], vp * head_sel[1]], axis=0).astype(BF16)
            o_parts.append(_dot(scores[b][p].astype(BF16), vv))
        st = st_ref[b]
        o_rows.append(jnp.concatenate(o_parts, axis=1) + _dot_nt(qg[sl], st.astype(BF16)))
        upd = _dot_tn(v_bf[sl], kd[sl]) * bones.astype(F32)
        st_ref[b] = st * jnp.exp(g_cum[(b + 1) * CHUNK - 1:(b + 1) * CHUNK, :]) + upd
    o = _rows(o_rows) + _dot((q * kk).astype(BF16), bones) * v
    ms = _dot((o * o).astype(BF16), avg_ref[...])
    return o * lax.rsqrt(ms + EPS) * g_ref[...] * (gate * _sigmoid(gate))


def _ret_chunk(rq, rk, v, cos, sin, intra_ref, inter_ref, tostate_ref, carry_ref, bmask_ref, st_ref):
    hw = RET_KEY_WIDTH // 2

    def rot(t):
        t1, t2 = t[:, :hw], t[:, hw:]
        return jnp.concatenate([t1 * cos - t2 * sin, t1 * sin + t2 * cos], axis=1)

    qr, kr = rot(rq), rot(rk)
    qr_bf, v_bf = qr.astype(BF16), v.astype(BF16)
    lane_k = lax.broadcasted_iota(jnp.int32, (1, RET_KEY_WIDTH), 1) % hw
    lane_v = lax.broadcasted_iota(jnp.int32, (1, 2 * RET_DV), 1)
    v_sel = [jnp.where(lane_v < RET_DV, 1.0, 0.0), jnp.where(lane_v >= RET_DV, 1.0, 0.0)]
    half = RET_DK // 2
    o_parts = []
    for p in range(RET_HEADS // 2):
        sel = [jnp.where((lane_k >= h * half) & (lane_k < (h + 1) * half), 1.0, 0.0)
               for h in (2 * p, 2 * p + 1)]
        rhs = jnp.concatenate([kr * sel[0], kr * sel[1]], axis=0).astype(BF16)
        sc = (_dot_nt(qr_bf, rhs) * intra_ref[p]).astype(BF16)
        vp = v[:, p * 128:(p + 1) * 128]
        vv = jnp.concatenate([vp * v_sel[0], vp * v_sel[1]], axis=0).astype(BF16)
        o_parts.append(_dot(sc, vv))
    o = jnp.concatenate(o_parts, axis=1)
    st = st_ref[...]
    o = o + _dot(qr_bf, st.astype(BF16)) * inter_ref[...]
    kd = (kr * tostate_ref[...]).astype(BF16)
    st_ref[...] = st * carry_ref[...] + _dot_tn(kd, v_bf) * bmask_ref[...]
    return o


def _ret_norm_gate(o, gate, avg_ref, g_ref):
    avg = avg_ref[...]
    outs = []
    for s in range(RET_WIDTH // 256):
        os_ = o[:, s * 256:(s + 1) * 256]
        c = os_ - _dot(os_.astype(BF16), avg)
        outs.append(c * lax.rsqrt(_dot((c * c).astype(BF16), avg) + EPS))
    return jnp.concatenate(outs, axis=1) * g_ref[...] * (gate * _sigmoid(gate))


def _mixer_kernel(proj_ref, cos_ref, sin_ref,
                  wb_ref, wc_ref, pn_re_ref, pn_im_ref, pp_re_ref, pp_im_ref, lam_ref, d_ref,
                  wglu_ref, s5g_ref, ltri_ref,
                  convw_ref, lbv_ref, mall_ref, lvl_ref, rsel_ref, bones_ref, avg_ref, hgg_ref,
                  intra_ref, inter_ref, tostate_ref, carry_ref, bmask_ref, retg_ref,
                  *rest, n_cast, cast_steps):
    cast_in, o_ref, cast_out = rest[:n_cast], rest[n_cast], rest[n_cast + 1:2 * n_cast + 1]
    s5_st, hg_xc, hg_st, ret_st = rest[2 * n_cast + 1:]

    @pl.when(pl.program_id(0) == 0)
    def _():
        s5_st[...] = jnp.zeros_like(s5_st)
        hg_xc[...] = jnp.zeros_like(hg_xc)
        hg_st[...] = jnp.zeros_like(hg_st)
        ret_st[...] = jnp.zeros_like(ret_st)

    if n_cast:
        @pl.when(pl.program_id(0) < cast_steps)
        def _():
            for src, dst in zip(cast_in, cast_out):
                dst[...] = src[...].astype(dst.dtype)

    nb = proj_ref.shape[0]

    def cols(c0, width):
        return _rows([proj_ref[b, :, c0:c0 + width] for b in range(nb)])

    def emit(c0, y):
        for b in range(nb):
            o_ref[b, :, c0:c0 + y.shape[1]] = y[b * CHUNK:(b + 1) * CHUNK].astype(o_ref.dtype)

    y_a = _s5_chunks(cols(C_U, S5_WIDTH), wb_ref, wc_ref, pn_re_ref, pn_im_ref, pp_re_ref, pp_im_ref,
                     lam_ref, d_ref, wglu_ref, s5g_ref, ltri_ref, s5_st)
    emit(0, y_a)

    for b in range(nb):
        hg_xc[b, 8:8 + CHUNK, :] = proj_ref[b, :, C_HQ:C_HQ + HG_CONV_W]
    y_b = _hgrn2_chunks(hg_xc, cols(C_HGATE, HG_WIDTH), convw_ref, lbv_ref,
                        mall_ref, lvl_ref, rsel_ref, bones_ref, avg_ref, hgg_ref, hg_st)
    for b in range(nb):
        hg_xc[b, 0:8, :] = hg_xc[b, CHUNK:CHUNK + 8, :]
    emit(S5_WIDTH, y_b)

    o_c = [_ret_chunk(proj_ref[b, :, C_RQ:C_RQ + RET_KEY_WIDTH], proj_ref[b, :, C_RK:C_RK + RET_KEY_WIDTH],
                      proj_ref[b, :, C_RV:C_RV + RET_WIDTH], cos_ref[...], sin_ref[...],
                      intra_ref, inter_ref, tostate_ref, carry_ref, bmask_ref, ret_st.at[b])
           for b in range(nb)]
    y_c = _ret_norm_gate(_rows(o_c), cols(C_RGATE, RET_WIDTH), avg_ref, retg_ref)
    emit(S5_WIDTH + HG_WIDTH, y_c)


def _const_spec(a):
    nd = a.ndim
    return pl.BlockSpec(a.shape, lambda c, _nd=nd: (0,) * _nd)


def _cast_steps(n_chunks, arrays):
    for steps in range(n_chunks, 0, -1):
        if all(a.shape[0] % (steps * 16) == 0 for a in arrays):
            return steps
    raise ValueError("no slab split")


def _mixers(proj, cos, sin, consts, bsz, n_chunks, to_bf16=()):
    total = proj.shape[1]
    n_cast = len(to_bf16)
    cast_steps = _cast_steps(n_chunks, to_bf16) if n_cast else 0
    slab = lambda c: (jnp.minimum(c, cast_steps - 1), 0)
    cast_specs = [pl.BlockSpec((a.shape[0] // cast_steps, a.shape[1]), slab) for a in to_bf16]
    in_specs = [
        pl.BlockSpec((bsz, CHUNK, IN_COLS), lambda c: (0, c, 0)),
        pl.BlockSpec((CHUNK, 128), lambda c: (c, 0)),
        pl.BlockSpec((CHUNK, 128), lambda c: (c, 0)),
    ] + [_const_spec(a) for a in consts] + cast_specs
    outs = pl.pallas_call(
        functools.partial(_mixer_kernel, n_cast=n_cast, cast_steps=cast_steps),
        grid=(n_chunks,),
        in_specs=in_specs,
        out_specs=[pl.BlockSpec((bsz, CHUNK, D_MIX), lambda c: (0, c, 0))] + cast_specs,
        out_shape=[jax.ShapeDtypeStruct((bsz, total, D_MIX), BF16)]
        + [jax.ShapeDtypeStruct(a.shape, BF16) for a in to_bf16],
        scratch_shapes=[
            pltpu.VMEM((bsz, 2, S5_NSTATE), F32),
            pltpu.VMEM((bsz, CHUNK + 8, HG_CONV_W), F32),
            pltpu.VMEM((bsz, HG_WIDTH, HG_WIDTH), F32),
            pltpu.VMEM((bsz, RET_KEY_WIDTH, RET_WIDTH), F32),
        ],
        compiler_params=pltpu.CompilerParams(
            dimension_semantics=("arbitrary",), vmem_limit_bytes=VMEM_LIMIT),
        name="mixers",
    )(proj, cos, sin, *consts, *to_bf16)
    return outs[0], list(outs[1:])


def _route(hn, router_ref, ltri_ref, count_ref):
    r = router_ref[...]
    r_hi, r_lo = _split_bf16(r)
    h_hi, h_lo = _split_bf16(hn)
    logits = _dot(h_hi, r_hi) + (_dot(h_lo, r_hi) + _dot(h_hi, r_lo))
    lane_i = lax.broadcasted_iota(jnp.int32, logits.shape, 1)
    lane = lane_i.astype(F32)
    neg = jnp.float32(-jnp.inf)
    logits = jnp.where(lane_i < N_EXPERTS, logits, neg)
    m1 = jnp.max(logits, axis=-1, keepdims=True)
    i1 = jnp.min(jnp.where(logits == m1, lane, float(ROUTE_LANES)), axis=-1, keepdims=True)
    rest = jnp.where(lane == i1, neg, logits)
    m2 = jnp.max(rest, axis=-1, keepdims=True)
    i2 = jnp.min(jnp.where(rest == m2, lane, float(ROUTE_LANES)), axis=-1, keepdims=True)
    e2 = jnp.exp(m2 - m1)
    g1 = 1.0 / (1.0 + e2)
    g2 = e2 / (1.0 + e2)
    onehot = jnp.where((lane == i1) | (lane == i2), 1.0, 0.0)
    before = _dot(ltri_ref[...], onehot.astype(BF16)) + count_ref[...]
    r1 = jnp.sum(jnp.where(lane == i1, before, 0.0), axis=-1, keepdims=True)
    r2 = jnp.sum(jnp.where(lane == i2, before, 0.0), axis=-1, keepdims=True)
    count_ref[...] += jnp.sum(onehot, axis=0, keepdims=True)
    out = jnp.where(lane == 0, i1, 0.0)
    out = jnp.where(lane == 1, i2, out)
    out = jnp.where(lane == 2, g1, out)
    out = jnp.where(lane == 3, g2, out)
    out = jnp.where(lane == 4, r1, out)
    return jnp.where(lane == 5, r2, out)


def _out_proj_kernel(mixed_ref, h_ref, w_ref, g_ref, *rest, routed):
    if routed:
        router_ref, ltri_ref, h1_ref, hn_ref, route_ref, count_ref = rest

        @pl.when(pl.program_id(0) == 0)
        def _():
            count_ref[...] = jnp.zeros_like(count_ref)
    else:
        h1_ref, hn_ref = rest
    h1 = h_ref[...] + _dot(mixed_ref[...], w_ref[...])
    h1_ref[...] = h1
    hn = h1 * lax.rsqrt(jnp.mean(h1 * h1, axis=-1, keepdims=True) + EPS) * g_ref[...]
    hn_ref[...] = hn.astype(hn_ref.dtype)
    if routed:
        route_ref[...] = _route(hn, router_ref, ltri_ref, count_ref)


def _out_proj(mixed, h, w_bf16, g, router_pad=None):
    n, d = h.shape
    tm = _pick_tile(n, (640, 128))
    routed = router_pad is not None
    row = lambda i: (i, 0)
    fixed = lambda i: (0, 0)
    in_specs = [pl.BlockSpec((tm, D_MIX), row), pl.BlockSpec((tm, d), row),
                pl.BlockSpec(w_bf16.shape, fixed), pl.BlockSpec((1, d), fixed)]
    out_specs = [pl.BlockSpec((tm, d), row), pl.BlockSpec((tm, d), row)]
    out_shape = [jax.ShapeDtypeStruct((n, d), F32), jax.ShapeDtypeStruct((n, d), F32 if routed else BF16)]
    args = [mixed, h, w_bf16, g]
    if routed:
        ltri = jnp.asarray(np.tril(np.ones((tm, tm), np.float32), -1), BF16)
        in_specs += [pl.BlockSpec(router_pad.shape, fixed), pl.BlockSpec((tm, tm), fixed)]
        out_specs += [pl.BlockSpec((tm, ROUTE_LANES), row), pl.BlockSpec((1, ROUTE_LANES), fixed)]
        out_shape += [jax.ShapeDtypeStruct((n, ROUTE_LANES), F32),
                      jax.ShapeDtypeStruct((1, ROUTE_LANES), F32)]
        args += [router_pad, ltri]
    return pl.pallas_call(
        functools.partial(_out_proj_kernel, routed=routed),
        grid=(n // tm,),
        in_specs=in_specs,
        out_specs=out_specs,
        out_shape=out_shape,
        compiler_params=pltpu.CompilerParams(
            dimension_semantics=("arbitrary",), vmem_limit_bytes=VMEM_LIMIT),
        name="out_proj_routed" if routed else "out_proj",
    )(*args)


def _ffn_kernel(hn_ref, h1_ref, w1_ref, w3_ref, w2_ref, o_ref, *, tf):
    x = hn_ref[...]
    acc = h1_ref[...]
    for c0 in range(0, w1_ref.shape[1], tf):
        a = _dot(x, w1_ref[:, c0:c0 + tf])
        b = _dot(x, w3_ref[:, c0:c0 + tf])
        acc = acc + _dot((a * _sigmoid(a) * b).astype(BF16), w2_ref[c0:c0 + tf, :])
    o_ref[...] = acc


def _ffn(hn, h1, w1, w3, w2):
    n, d = h1.shape
    dff = w1.shape[1]
    tm, tf = _pick_tile(n, (640, 128)), _pick_tile(dff, (1408, 128))
    resident = lambda a: pl.BlockSpec(a.shape, lambda i: (0, 0), pipeline_mode=pl.Buffered(1))
    return pl.pallas_call(
        functools.partial(_ffn_kernel, tf=tf),
        grid=(n // tm,),
        in_specs=[
            pl.BlockSpec((tm, d), lambda i: (i, 0)),
            pl.BlockSpec((tm, d), lambda i: (i, 0)),
            resident(w1), resident(w3), resident(w2),
        ],
        out_specs=pl.BlockSpec((tm, d), lambda i: (i, 0)),
        out_shape=jax.ShapeDtypeStruct((n, d), F32),
        compiler_params=pltpu.CompilerParams(
            dimension_semantics=("arbitrary",), vmem_limit_bytes=VMEM_LIMIT),
        name="ffn_dense",
    )(hn, h1, w1, w3, w2)


def _moe_kernel(te_ref, nv_ref, src_ref, tok_ref, hn_hbm, w1_ref, w3_ref, w2_ref, o_ref,
                xs_ref, xb_ref, acc_ref, sem, *, tm):
    i, f = pl.program_id(0), pl.program_id(1)
    nf = pl.num_programs(1)
    n_valid = nv_ref[0]
    share = tm // MOE_F_STEPS

    def row_copy(tile, r):
        return pltpu.make_async_copy(hn_hbm.at[pl.ds(tok_ref[src_ref[tile] + r], 1)],
                                     xs_ref.at[pl.ds(r, 1)], sem)

    @pl.when((i == 0) & (f == 0))
    def _():
        def body(r, carry):
            row_copy(0, r).start()
            return carry
        lax.fori_loop(0, tm, body, 0)

    @pl.when((f == 0) & (i <= n_valid))
    def _():
        pltpu.make_async_copy(xs_ref, xs_ref, sem).wait()
        xb_ref[...] = xs_ref[...].astype(BF16)

    @pl.when(i < n_valid)
    def _():
        for r in range(share):
            row_copy(i + 1, f * share + r).start(priority=r % MOE_DMA_QUEUES)
        x = xb_ref[...]
        a = _dot(x, w1_ref[0])
        b = _dot(x, w3_ref[0])
        part = _dot((a * _sigmoid(a) * b).astype(BF16), w2_ref[0])

        @pl.when(f == 0)
        def _():
            acc_ref[...] = part

        @pl.when(f > 0)
        def _():
            acc_ref[...] += part

    @pl.when(f == nf - 1)
    def _():
        o_ref[...] = jnp.where(i < n_valid, acc_ref[...], 0.0).astype(o_ref.dtype)


def _live_f(i, f, nv):
    return jnp.where(i < nv[0], f, 0)


def _moe_grouped(tile_expert, n_valid, tile_src, sorted_tok, hn, w1, w3, w2, tm):
    d = hn.shape[1]
    n_tiles = tile_expert.shape[0]
    dff = w1.shape[2]
    tf = dff // MOE_F_STEPS
    grid_spec = pltpu.PrefetchScalarGridSpec(
        num_scalar_prefetch=4,
        grid=(n_tiles, MOE_F_STEPS),
        in_specs=[
            pl.BlockSpec(memory_space=pl.ANY),
            pl.BlockSpec((1, d, tf), lambda i, f, te, nv, *_: (te[i], 0, _live_f(i, f, nv))),
            pl.BlockSpec((1, d, tf), lambda i, f, te, nv, *_: (te[i], 0, _live_f(i, f, nv))),
            pl.BlockSpec((1, tf, d), lambda i, f, te, nv, *_: (te[i], _live_f(i, f, nv), 0)),
        ],
        out_specs=pl.BlockSpec((tm, d), lambda i, f, *_: (i, 0)),
        scratch_shapes=[pltpu.VMEM((tm, d), F32), pltpu.VMEM((tm, d), BF16), pltpu.VMEM((tm, d), F32),
                        pltpu.SemaphoreType.DMA(())],
    )
    return pl.pallas_call(
        functools.partial(_moe_kernel, tm=tm),
        grid_spec=grid_spec,
        out_shape=jax.ShapeDtypeStruct((n_tiles * tm, d), BF16),
        compiler_params=pltpu.CompilerParams(
            dimension_semantics=("arbitrary", "arbitrary"), vmem_limit_bytes=VMEM_LIMIT),
        name="moe_grouped",
    )(tile_expert, n_valid, tile_src, sorted_tok, hn, w1, w3, w2)


def _moe_dispatch(route, counts_row, tm):
    n = route.shape[0]
    n_slots = n * TOP_K
    n_tiles = (n_slots + N_EXPERTS * tm + tm - 1) // tm
    counts = counts_row[0, :N_EXPERTS].astype(jnp.int32)
    padded = ((counts + tm - 1) // tm) * tm
    pend = jnp.cumsum(padded)
    pstart = pend - padded
    gstart = jnp.cumsum(counts) - counts
    eid = route[:, 0:TOP_K].astype(jnp.int32)
    rank = route[:, 2 * TOP_K:3 * TOP_K].astype(jnp.int32)
    pos = pstart[eid] + rank
    tok = jnp.broadcast_to(jnp.arange(n, dtype=jnp.int32)[:, None], (n, TOP_K))
    _, sorted_tok = lax.sort(((gstart[eid] + rank).reshape(-1), tok.reshape(-1)), num_keys=1)
    sorted_tok = jnp.concatenate([sorted_tok, jnp.zeros((tm,), jnp.int32)])
    tile_start = jnp.arange(n_tiles, dtype=jnp.int32) * tm
    tile_expert = jnp.minimum(jnp.sum((tile_start[:, None] >= pend[None, :]).astype(jnp.int32), axis=1),
                              N_EXPERTS - 1)
    tile_src = jnp.clip(tile_start - (pstart - gstart)[tile_expert], 0, n_slots)
    n_valid = (pend[-1] // tm).astype(jnp.int32).reshape(1)
    return sorted_tok, tile_src, pos, tile_expert, n_valid


def _final_kernel(h1_ref, y0_ref, y1_ref, route_ref, g_ref, o_ref):
    g0 = route_ref[:, TOP_K:TOP_K + 1]
    g1 = route_ref[:, TOP_K + 1:TOP_K + 2]
    h = h1_ref[...] + (g0 * y0_ref[...].astype(F32) + g1 * y1_ref[...].astype(F32))
    o_ref[...] = h * lax.rsqrt(jnp.mean(h * h, axis=-1, keepdims=True) + EPS) * g_ref[...]


def _final(h1, y0, y1, route, g, bsz, total):
    d = h1.shape[1]
    seq = total - CHUNK
    tm = _pick_tile(seq, (1024, 128))
    per_seq = seq // tm
    src = lambda b, k: (pl.multiple_of(b * total + CHUNK + k * tm, CHUNK), 0)
    rows = lambda width: pl.BlockSpec((pl.Element(tm), pl.Element(width)), src)
    out = pl.pallas_call(
        _final_kernel,
        grid=(bsz, per_seq),
        in_specs=[rows(d), rows(d), rows(d), rows(ROUTE_LANES), pl.BlockSpec((1, d), lambda b, k: (0, 0))],
        out_specs=pl.BlockSpec((tm, d), lambda b, k: (b * per_seq + k, 0)),
        out_shape=jax.ShapeDtypeStruct((bsz * seq, d), F32),
        compiler_params=pltpu.CompilerParams(
            dimension_semantics=("arbitrary", "arbitrary"), vmem_limit_bytes=VMEM_LIMIT),
        name="final_norm",
    )(h1, y0, y1, route, g)
    return out.reshape(bsz, seq, d)


def _rope_perm():
    half = RET_DK // 2
    idx = []
    for part in range(2):
        for h in range(RET_HEADS):
            for dd in range(half):
                idx.append(h * RET_DK + part * half + dd)
    return np.asarray(idx, np.int32)


def _in_col_perm():
    cols = np.arange(IN_COLS, dtype=np.int32)
    rp = _rope_perm()
    cols[C_RQ:C_RQ + RET_KEY_WIDTH] = C_RQ + rp
    cols[C_RK:C_RK + RET_KEY_WIDTH] = C_RK + rp
    return cols


def _hg_tables(nb):
    c = CHUNK
    i = np.arange(c)[:, None]
    t = np.arange(c)[None, :]
    blocks, masks, rsel = [], [], []
    for lvl in range(HG_LEVELS):
        s = 1 << lvl
        r = (i // (2 * s)) * (2 * s) + s - 1
        right = i > r
        m = np.where(right, (t > r) & (t <= i), (t > i) & (t <= r))
        blocks.append(m)
        j = t
        same = (i // (2 * s)) == (j // (2 * s))
        mk = same & right & (j <= r)
        masks.append(np.concatenate([mk, mk], axis=1))
        if s < 8:
            rsel.append(np.broadcast_to(right, (c, HG_WIDTH)))
    blocks.append(t <= i)
    blocks.append(t > i)
    eye = np.eye(nb, dtype=np.float32)
    mall = np.concatenate([np.kron(eye, blk.astype(np.float32)) for blk in blocks], axis=0)
    lvl = np.stack(masks, axis=0).astype(np.float32)
    ch = np.arange(HG_WIDTH)
    bones = (ch[:, None] // HG_DK == ch[None, :] // HG_DK).astype(np.float32)
    rsel = np.stack([np.tile(r, (nb, 1)) for r in rsel], axis=0).astype(np.float32)
    return mall, lvl, rsel, bones


def _ret_tables():
    f32 = jnp.float32
    log_gamma = jnp.log1p(-jnp.power(2.0, -5.0 - jnp.arange(RET_HEADS, dtype=f32)))
    n = jnp.arange(CHUNK, dtype=f32)
    lg = log_gamma[:, None]
    causal = jnp.tril(jnp.ones((CHUNK, CHUNK), dtype=bool))
    intra = jnp.exp(jnp.where(causal[None], (n[:, None] - n[None, :])[None] * lg[:, :, None], -jnp.inf))
    scale = RET_DK ** -0.5
    inter = jnp.exp((n[None, :] + 1.0) * lg)
    to_state = jnp.exp((CHUNK - 1.0 - n[None, :]) * lg)
    carry = jnp.exp(CHUNK * lg)[:, 0]
    head_of_v = np.arange(RET_WIDTH) // RET_DV
    head_of_k = (np.arange(RET_KEY_WIDTH) % (RET_KEY_WIDTH // 2)) // (RET_DK // 2)
    inter_t = (inter * scale).T[:, head_of_v]
    to_state_t = to_state.T[:, head_of_k]
    carry_row = carry[head_of_v][None, :]
    bmask = jnp.asarray((head_of_k[:, None] == head_of_v[None, :]).astype(np.float32))
    intra_pairs = (intra * scale).reshape(RET_HEADS // 2, 2, CHUNK, CHUNK)
    intra_pairs = jnp.concatenate([intra_pairs[:, 0], intra_pairs[:, 1]], axis=2)
    return intra_pairs, inter_t, to_state_t, carry_row, bmask


def _s5_tables(lam_re, lam_im, b_re, b_im, c_re, c_im, d_skip, log_step):
    f32 = jnp.float32
    lam = lax.complex(lam_re.astype(f32), lam_im.astype(f32))
    step = jnp.exp(log_step.astype(f32))[:, None]
    lam_dt = lam * step
    lam_bar = jnp.exp(lam_dt)
    b_bar = ((lam_bar - 1.0) / lam)[..., None] * lax.complex(b_re.astype(f32), b_im.astype(f32))
    eye = jnp.eye(S5_NGROUPS, dtype=f32)
    wb_re = jnp.einsum('gph,gk->ghkp', jnp.real(b_bar), eye).reshape(S5_WIDTH, S5_NSTATE)
    wb_im = jnp.einsum('gph,gk->ghkp', jnp.imag(b_bar), eye).reshape(S5_WIDTH, S5_NSTATE)
    wb = jnp.concatenate([wb_re, wb_im], axis=1)
    wc_re = jnp.einsum('ghp,gk->gpkh', c_re.astype(f32), eye).reshape(S5_NSTATE, S5_WIDTH)
    wc_im = jnp.einsum('ghp,gk->gpkh', c_im.astype(f32), eye).reshape(S5_NSTATE, S5_WIDTH)
    wc = jnp.concatenate([wc_re, -wc_im], axis=0)
    t = jnp.arange(CHUNK, dtype=f32)[:, None, None]
    pp = jnp.exp(lam_dt[None] * t).reshape(CHUNK, S5_NSTATE)
    pn = jnp.exp(-lam_dt[None] * t).reshape(CHUNK, S5_NSTATE)
    lam_rows = jnp.stack([jnp.real(lam_bar).reshape(-1), jnp.imag(lam_bar).reshape(-1)], axis=0)
    return (wb.astype(BF16), wc.astype(BF16), jnp.real(pn), jnp.imag(pn), jnp.real(pp), jnp.imag(pp),
            lam_rows, d_skip.astype(f32).reshape(1, S5_WIDTH))


def kernel(x, meta_tokens, norm_mix_g, w_in, s5_lam_re, s5_lam_im, s5_b_re, s5_b_im, s5_c_re, s5_c_im, s5_d, s5_log_step, s5_w_glu, s5_out_g, hg_conv_w, hg_lb_param, hg_out_g, ret_out_g, w_out, norm_ffn_g, ffn_w1, ffn_w3, ffn_w2, moe_router, moe_w1, moe_w3, moe_w2, final_norm_g):
    bsz, seq_len, d = x.shape
    depth = w_in.shape[0]
    total = seq_len + CHUNK
    n_chunks = total // CHUNK
    n = bsz * total

    meta = jnp.broadcast_to(meta_tokens.astype(F32)[None], (bsz, N_META, d))
    h = jnp.concatenate([jnp.zeros((bsz, PAD, d), F32), meta, x.astype(F32)], axis=1).reshape(n, d)

    pos = (jnp.arange(total) - PAD).astype(F32)
    half = RET_DK // 2
    inv_freq = ROPE_BASE ** (-jnp.arange(half, dtype=F32) / half)
    ang = pos[:, None] * inv_freq[None, :]
    cos_t = jnp.tile(jnp.cos(ang), (1, RET_HEADS))
    sin_t = jnp.tile(jnp.sin(ang), (1, RET_HEADS))

    lb_all = jnp.cumsum(jax.nn.softmax(hg_lb_param.astype(F32), axis=0), axis=0)
    lb_all = lb_all - lb_all[0]

    mall_np, lvl_np, rsel_np, bones_np = _hg_tables(bsz)
    mall = jnp.asarray(mall_np, BF16)
    lvl = jnp.asarray(lvl_np, F32)
    rsel = jnp.asarray(rsel_np, F32)
    bones = jnp.asarray(bones_np, BF16)
    avg = jnp.asarray(bones_np / HG_DK, BF16)
    ltri = jnp.asarray(np.kron(np.eye(bsz, dtype=np.float32),
                               np.tril(np.ones((CHUNK, CHUNK), np.float32))), BF16)
    intra, inter_t, to_state_t, carry_row, bmask = _ret_tables()
    col_perm = jnp.asarray(_in_col_perm())

    out = None
    for l in range(depth):
        w_in_l = jnp.take(w_in[l], col_perm, axis=1).astype(BF16)
        proj = _in_proj(h, norm_mix_g[l].astype(F32).reshape(1, d), w_in_l, total)

        s5c = _s5_tables(s5_lam_re[l], s5_lam_im[l], s5_b_re[l], s5_b_im[l], s5_c_re[l], s5_c_im[l],
                         s5_d[l], s5_log_step[l])
        lb = lb_all[l][None, :]
        lbv = jnp.concatenate([jnp.log(lb), jnp.log1p(-lb), 1.0 - lb], axis=0)
        consts = list(s5c) + [
            s5_w_glu[l].astype(BF16), s5_out_g[l].astype(F32).reshape(1, -1), ltri,
            hg_conv_w[l].astype(F32), lbv, mall, lvl, rsel, bones, avg, hg_out_g[l].astype(F32).reshape(1, -1),
            intra, inter_t, to_state_t, carry_row, bmask, ret_out_g[l].astype(F32).reshape(1, -1),
        ]
        j = l // 2
        if l % 2 == 0:
            ffn_f32 = [ffn_w1[j], ffn_w3[j], ffn_w2[j]]
        else:
            dff_e = moe_w1.shape[-1]
            ffn_f32 = [moe_w1[j].reshape(N_EXPERTS * d, dff_e), moe_w3[j].reshape(N_EXPERTS * d, dff_e),
                       moe_w2[j].reshape(N_EXPERTS * dff_e, d)]
        mixed, ffn_bf16 = _mixers(proj.reshape(bsz, total, IN_COLS), cos_t, sin_t, consts, bsz, n_chunks,
                                  to_bf16=[a.astype(F32) for a in ffn_f32])
        mixed = mixed.reshape(n, D_MIX)

        g_ffn = norm_ffn_g[l].astype(F32).reshape(1, d)
        w_out_l = w_out[l].astype(BF16)
        if l % 2 == 0:
            h1, hn = _out_proj(mixed, h, w_out_l, g_ffn)
            h = _ffn(hn, h1, *ffn_bf16)
            y0 = y1 = None
        else:
            router_pad = jnp.zeros((d, ROUTE_LANES), F32).at[:, :N_EXPERTS].set(moe_router[j].astype(F32))
            h1, hn, route, counts_row = _out_proj(mixed, h, w_out_l, g_ffn, router_pad)
            tm = MOE_TILE
            sorted_tok, tile_src, pos_of_slot, tile_expert, n_valid = _moe_dispatch(route, counts_row, tm)
            e_w1, e_w3, e_w2 = (ffn_bf16[0].reshape(N_EXPERTS, d, dff_e), ffn_bf16[1].reshape(N_EXPERTS, d, dff_e),
                                ffn_bf16[2].reshape(N_EXPERTS, dff_e, d))
            ys = _moe_grouped(tile_expert, n_valid, tile_src, sorted_tok, hn, e_w1, e_w3, e_w2, tm)
            y0 = jnp.take(ys, pos_of_slot[:, 0], axis=0)
            y1 = jnp.take(ys, pos_of_slot[:, 1], axis=0)
            if l < depth - 1:
                h = h1 + (route[:, TOP_K:TOP_K + 1] * y0.astype(F32)
                          + route[:, TOP_K + 1:TOP_K + 2] * y1.astype(F32))

        if l == depth - 1:
            if y0 is None:
                y0 = y1 = jnp.zeros((n, d), BF16)
                h1 = h
                route = jnp.zeros((n, ROUTE_LANES), F32)
            out = _final(h1, y0, y1, route, final_norm_g.astype(F32).reshape(1, d), bsz, total)

    return out.astype(x.dtype)
```

```python
import functools
import math

import jax
import jax.numpy as jnp
import numpy as np
from jax import lax
from jax.experimental import pallas as pl
from jax.experimental.pallas import tpu as pltpu

F32 = jnp.float32
BF16 = jnp.bfloat16

CHUNK = 128
N_META = 16
PAD = CHUNK - N_META
EPS = 1e-6

S5_WIDTH = 256
S5_GROUP = 16
S5_NGROUPS = 16
S5_STATE = 64
S5_NSTATE = S5_NGROUPS * S5_STATE

HG_HEADS = 4
HG_DK = 64
HG_WIDTH = 256
CONV_K = 4
HG_CONV_W = 3 * HG_WIDTH
HG_LEVELS = 7

RET_HEADS = 8
RET_DK = 32
RET_DV = 64
RET_KEY_WIDTH = 256
RET_WIDTH = 512
ROPE_BASE = 10000.0

D_MIX = 1024
IN_COLS = 2816
C_U, C_HQ, C_HGATE, C_RQ, C_RK, C_RV, C_RGATE = 0, 256, 1024, 1280, 1536, 1792, 2304

N_EXPERTS = 8
TOP_K = 2
ROUTE_LANES = 128
MOE_F_STEPS = 2
MOE_TILE = 512
ROW_TILE = 8
MOE_DMA_QUEUES = 2

VMEM_LIMIT = 56 * 1024 * 1024


def _sigmoid(x):
    return 1.0 / (1.0 + jnp.exp(-x))


def _split_bf16(x):
    hi = x.astype(BF16)
    lo = (x - hi.astype(F32)).astype(BF16)
    return hi, lo


def _pick_tile(n, candidates):
    for t in candidates:
        if n % t == 0:
            return t
    raise ValueError(f"no tile in {candidates} divides {n}")


def _dot(a, b):
    return jnp.dot(a, b, preferred_element_type=F32)


def _dot_nt(a, b):
    return lax.dot_general(a, b, (((1,), (1,)), ((), ())), preferred_element_type=F32)


def _dot_tn(a, b):
    return lax.dot_general(a, b, (((0,), (0,)), ((), ())), preferred_element_type=F32)


def _in_proj_kernel(h_ref, g_ref, w_ref, o_ref, *, tm, tiles_per_batch):
    x = h_ref[...]
    y = x * lax.rsqrt(jnp.mean(x * x, axis=-1, keepdims=True) + EPS) * g_ref[...]
    proj = _dot(y.astype(BF16), w_ref[...])
    row0 = (pl.program_id(0) % tiles_per_batch) * tm
    rows = row0 + lax.broadcasted_iota(jnp.int32, (tm, 1), 0)
    o_ref[...] = jnp.where(rows >= PAD, proj, 0.0)


def _in_proj(h, g, w_bf16, total):
    n = h.shape[0]
    tm = _pick_tile(total, (640, 128))
    return pl.pallas_call(
        functools.partial(_in_proj_kernel, tm=tm, tiles_per_batch=total // tm),
        grid=(n // tm,),
        in_specs=[
            pl.BlockSpec((tm, h.shape[1]), lambda i: (i, 0)),
            pl.BlockSpec((1, h.shape[1]), lambda i: (0, 0)),
            pl.BlockSpec(w_bf16.shape, lambda i: (0, 0), pipeline_mode=pl.Buffered(1)),
        ],
        out_specs=pl.BlockSpec((tm, IN_COLS), lambda i: (i, 0)),
        out_shape=jax.ShapeDtypeStruct((n, IN_COLS), F32),
        compiler_params=pltpu.CompilerParams(
            dimension_semantics=("arbitrary",), vmem_limit_bytes=VMEM_LIMIT),
        name="in_proj",
    )(h, g, w_bf16)


def _rows(parts):
    return parts[0] if len(parts) == 1 else jnp.concatenate(parts, axis=0)


def _s5_chunks(u, wb_ref, wc_ref, pn_re_ref, pn_im_ref, pp_re_ref, pp_im_ref, lam_ref,
               d_ref, wglu_ref, g_ref, ltri_ref, st_ref):
    ns = S5_NSTATE
    nb = st_ref.shape[0]
    bu = _dot(u.astype(BF16), wb_ref[...])
    pn_re, pn_im = pn_re_ref[...], pn_im_ref[...]
    w_re, w_im = [], []
    for b in range(nb):
        bu_re = bu[b * CHUNK:(b + 1) * CHUNK, :ns]
        bu_im = bu[b * CHUNK:(b + 1) * CHUNK, ns:]
        w_re.append(pn_re * bu_re - pn_im * bu_im)
        w_im.append(pn_re * bu_im + pn_im * bu_re)
    ltri = ltri_ref[...]
    c_re = _dot(ltri, _rows(w_re).astype(BF16))
    c_im = _dot(ltri, _rows(w_im).astype(BF16))
    lam_re, lam_im = lam_ref[0:1, :], lam_ref[1:2, :]
    pp_re, pp_im = pp_re_ref[...], pp_im_ref[...]
    st_re, st_im = [], []
    for b in range(nb):
        s_re, s_im = st_ref[b, 0:1, :], st_ref[b, 1:2, :]
        z_re = c_re[b * CHUNK:(b + 1) * CHUNK] + (lam_re * s_re - lam_im * s_im)
        z_im = c_im[b * CHUNK:(b + 1) * CHUNK] + (lam_re * s_im + lam_im * s_re)
        t_re = pp_re * z_re - pp_im * z_im
        t_im = pp_re * z_im + pp_im * z_re
        st_ref[b, 0:1, :] = t_re[CHUNK - 1:CHUNK, :]
        st_ref[b, 1:2, :] = t_im[CHUNK - 1:CHUNK, :]
        st_re.append(t_re)
        st_im.append(t_im)
    y = (_dot(_rows(st_re).astype(BF16), wc_ref[0:ns, :]) + _dot(_rows(st_im).astype(BF16), wc_ref[ns:2 * ns, :])
         + d_ref[...] * u)
    y = 0.5 * y * (1.0 + jnp.tanh(math.sqrt(2.0 / math.pi) * (y + 0.044715 * (y * y * y))))
    y = y * _sigmoid(_dot(y.astype(BF16), wglu_ref[...]))
    return y * lax.rsqrt(jnp.mean(y * y, axis=-1, keepdims=True) + EPS) * g_ref[...]


def _hgrn2_chunks(xc_ref, gate, convw_ref, lbv_ref, mall_ref, lvl_ref, rsel_ref, bones_ref, avg_ref,
                  g_ref, st_ref):
    w = HG_WIDTH
    nb = st_ref.shape[0]
    rows = nb * CHUNK
    convs = []
    for b in range(nb):
        conv = None
        for i in range(CONV_K):
            term = xc_ref[b, pl.ds(8 - (CONV_K - 1) + i, CHUNK), :] * convw_ref[i:i + 1, :]
            conv = term if conv is None else conv + term
        convs.append(conv)
    conv = _rows(convs)
    cq, cf, v = conv[:, :w], conv[:, w:2 * w], conv[:, 2 * w:]
    q = cq * _sigmoid(cq)
    log_lb, log_1m_lb, one_m_lb = lbv_ref[0:1, :], lbv_ref[1:2, :], lbv_ref[2:3, :]
    log_sig = jnp.minimum(cf, 0.0) - jnp.log(1.0 + jnp.exp(-jnp.abs(cf)))
    b_ = log_1m_lb + log_sig
    logf = jnp.maximum(log_lb, b_) + jnp.log(1.0 + jnp.exp(-jnp.abs(log_lb - b_)))
    kk = one_m_lb * _sigmoid(-cf)
    hi, lo = _split_bf16(logf)

    lvl_sums = _dot(mall_ref[0:HG_LEVELS * rows, :], hi)
    tail = mall_ref[HG_LEVELS * rows:(HG_LEVELS + 2) * rows, :]
    cum_suf = _dot(tail, hi) + _dot(tail, lo)
    g_cum, g_suffix = cum_suf[:rows], cum_suf[rows:]

    lane = lax.broadcasted_iota(jnp.int32, (1, 2 * HG_DK), 1)
    head_sel = [jnp.where(lane < HG_DK, 1.0, 0.0), jnp.where(lane >= HG_DK, 1.0, 0.0)]
    n_pairs = HG_HEADS // 2
    scores = [[None] * n_pairs for _ in range(nb)]
    for lvl in range(HG_LEVELS):
        s = 1 << lvl
        e = jnp.exp(lvl_sums[lvl * rows:(lvl + 1) * rows])
        if s >= 8:
            qk = jnp.concatenate([(q if (r // s) % 2 else kk)[r:r + s] for r in range(0, rows, s)], axis=0)
        else:
            qk = jnp.where(rsel_ref[lvl] > 0.5, q, kk)
        x = qk * e
        m = lvl_ref[lvl]
        for b in range(nb):
            for p in range(n_pairs):
                xp = x[b * CHUNK:(b + 1) * CHUNK, p * 128:(p + 1) * 128]
                rhs = jnp.concatenate([xp * head_sel[0], xp * head_sel[1]], axis=0).astype(BF16)
                sc = _dot_nt(xp.astype(BF16), rhs) * m
                scores[b][p] = sc if scores[b][p] is None else scores[b][p] + sc
    v_bf = v.astype(BF16)
    bones = bones_ref[...]
    qg = (q * jnp.exp(g_cum)).astype(BF16)
    kd = (kk * jnp.exp(g_suffix)).astype(BF16)
    o_rows = []
    for b in range(nb):
        sl = slice(b * CHUNK, (b + 1) * CHUNK)
        o_parts = []
        for p in range(n_pairs):
            vp = v[sl, p * 128:(p + 1) * 128]
            vv = jnp.concatenate([vp * head_sel[0], vp * head_sel[1]], axis=0).astype(BF16)
            o_parts.append(_dot(scores[b][p].astype(BF16), vv))
        st = st_ref[b]
        o_rows.append(jnp.concatenate(o_parts, axis=1) + _dot_nt(qg[sl], st.astype(BF16)))
        upd = _dot_tn(v_bf[sl], kd[sl]) * bones.astype(F32)
        st_ref[b] = st * jnp.exp(g_cum[(b + 1) * CHUNK - 1:(b + 1) * CHUNK, :]) + upd
    o = _rows(o_rows) + _dot((q * kk).astype(BF16), bones) * v
    ms = _dot((o * o).astype(BF16), avg_ref[...])
    return o * lax.rsqrt(ms + EPS) * g_ref[...] * (gate * _sigmoid(gate))


def _ret_chunk(rq, rk, v, cos, sin, intra_ref, inter_ref, tostate_ref, carry_ref, bmask_ref, st_ref):
    hw = RET_KEY_WIDTH // 2

    def rot(t):
        t1, t2 = t[:, :hw], t[:, hw:]
        return jnp.concatenate([t1 * cos - t2 * sin, t1 * sin + t2 * cos], axis=1)

    qr, kr = rot(rq), rot(rk)
    qr_bf, v_bf = qr.astype(BF16), v.astype(BF16)
    lane_k = lax.broadcasted_iota(jnp.int32, (1, RET_KEY_WIDTH), 1) % hw
    lane_v = lax.broadcasted_iota(jnp.int32, (1, 2 * RET_DV), 1)
    v_sel = [jnp.where(lane_v < RET_DV, 1.0, 0.0), jnp.where(lane_v >= RET_DV, 1.0, 0.0)]
    half = RET_DK // 2
    o_parts = []
    for p in range(RET_HEADS // 2):
        sel = [jnp.where((lane_k >= h * half) & (lane_k < (h + 1) * half), 1.0, 0.0)
               for h in (2 * p, 2 * p + 1)]
        rhs = jnp.concatenate([kr * sel[0], kr * sel[1]], axis=0).astype(BF16)
        sc = (_dot_nt(qr_bf, rhs) * intra_ref[p]).astype(BF16)
        vp = v[:, p * 128:(p + 1) * 128]
        vv = jnp.concatenate([vp * v_sel[0], vp * v_sel[1]], axis=0).astype(BF16)
        o_parts.append(_dot(sc, vv))
    o = jnp.concatenate(o_parts, axis=1)
    st = st_ref[...]
    o = o + _dot(qr_bf, st.astype(BF16)) * inter_ref[...]
    kd = (kr * tostate_ref[...]).astype(BF16)
    st_ref[...] = st * carry_ref[...] + _dot_tn(kd, v_bf) * bmask_ref[...]
    return o


def _ret_norm_gate(o, gate, avg_ref, g_ref):
    avg = avg_ref[...]
    outs = []
    for s in range(RET_WIDTH // 256):
        os_ = o[:, s * 256:(s + 1) * 256]
        c = os_ - _dot(os_.astype(BF16), avg)
        outs.append(c * lax.rsqrt(_dot((c * c).astype(BF16), avg) + EPS))
    return jnp.concatenate(outs, axis=1) * g_ref[...] * (gate * _sigmoid(gate))


def _mixer_kernel(proj_ref, cos_ref, sin_ref,
                  wb_ref, wc_ref, pn_re_ref, pn_im_ref, pp_re_ref, pp_im_ref, lam_ref, d_ref,
                  wglu_ref, s5g_ref, ltri_ref,
                  convw_ref, lbv_ref, mall_ref, lvl_ref, rsel_ref, bones_ref, avg_ref, hgg_ref,
                  intra_ref, inter_ref, tostate_ref, carry_ref, bmask_ref, retg_ref,
                  *rest, n_cast, cast_steps):
    cast_in, o_ref, cast_out = rest[:n_cast], rest[n_cast], rest[n_cast + 1:2 * n_cast + 1]
    s5_st, hg_xc, hg_st, ret_st = rest[2 * n_cast + 1:]

    @pl.when(pl.program_id(0) == 0)
    def _():
        s5_st[...] = jnp.zeros_like(s5_st)
        hg_xc[...] = jnp.zeros_like(hg_xc)
        hg_st[...] = jnp.zeros_like(hg_st)
        ret_st[...] = jnp.zeros_like(ret_st)

    if n_cast:
        @pl.when(pl.program_id(0) < cast_steps)
        def _():
            for src, dst in zip(cast_in, cast_out):
                dst[...] = src[...].astype(dst.dtype)

    nb = proj_ref.shape[0]

    def cols(c0, width):
        return _rows([proj_ref[b, :, c0:c0 + width] for b in range(nb)])

    def emit(c0, y):
        for b in range(nb):
            o_ref[b, :, c0:c0 + y.shape[1]] = y[b * CHUNK:(b + 1) * CHUNK].astype(o_ref.dtype)

    y_a = _s5_chunks(cols(C_U, S5_WIDTH), wb_ref, wc_ref, pn_re_ref, pn_im_ref, pp_re_ref, pp_im_ref,
                     lam_ref, d_ref, wglu_ref, s5g_ref, ltri_ref, s5_st)
    emit(0, y_a)

    for b in range(nb):
        hg_xc[b, 8:8 + CHUNK, :] = proj_ref[b, :, C_HQ:C_HQ + HG_CONV_W]
    y_b = _hgrn2_chunks(hg_xc, cols(C_HGATE, HG_WIDTH), convw_ref, lbv_ref,
                        mall_ref, lvl_ref, rsel_ref, bones_ref, avg_ref, hgg_ref, hg_st)
    for b in range(nb):
        hg_xc[b, 0:8, :] = hg_xc[b, CHUNK:CHUNK + 8, :]
    emit(S5_WIDTH, y_b)

    o_c = [_ret_chunk(proj_ref[b, :, C_RQ:C_RQ + RET_KEY_WIDTH], proj_ref[b, :, C_RK:C_RK + RET_KEY_WIDTH],
                      proj_ref[b, :, C_RV:C_RV + RET_WIDTH], cos_ref[...], sin_ref[...],
                      intra_ref, inter_ref, tostate_ref, carry_ref, bmask_ref, ret_st.at[b])
           for b in range(nb)]
    y_c = _ret_norm_gate(_rows(o_c), cols(C_RGATE, RET_WIDTH), avg_ref, retg_ref)
    emit(S5_WIDTH + HG_WIDTH, y_c)


def _const_spec(a):
    nd = a.ndim
    return pl.BlockSpec(a.shape, lambda c, _nd=nd: (0,) * _nd)


def _cast_steps(n_chunks, arrays):
    for steps in range(n_chunks, 0, -1):
        if all(a.shape[0] % (steps * 16) == 0 for a in arrays):
            return steps
    raise ValueError("no slab split")


def _mixers(proj, cos, sin, consts, bsz, n_chunks, to_bf16=()):
    total = proj.shape[1]
    n_cast = len(to_bf16)
    cast_steps = _cast_steps(n_chunks, to_bf16) if n_cast else 0
    slab = lambda c: (jnp.minimum(c, cast_steps - 1), 0)
    cast_specs = [pl.BlockSpec((a.shape[0] // cast_steps, a.shape[1]), slab) for a in to_bf16]
    in_specs = [
        pl.BlockSpec((bsz, CHUNK, IN_COLS), lambda c: (0, c, 0)),
        pl.BlockSpec((CHUNK, 128), lambda c: (c, 0)),
        pl.BlockSpec((CHUNK, 128), lambda c: (c, 0)),
    ] + [_const_spec(a) for a in consts] + cast_specs
    outs = pl.pallas_call(
        functools.partial(_mixer_kernel, n_cast=n_cast, cast_steps=cast_steps),
        grid=(n_chunks,),
        in_specs=in_specs,
        out_specs=[pl.BlockSpec((bsz, CHUNK, D_MIX), lambda c: (0, c, 0))] + cast_specs,
        out_shape=[jax.ShapeDtypeStruct((bsz, total, D_MIX), BF16)]
        + [jax.ShapeDtypeStruct(a.shape, BF16) for a in to_bf16],
        scratch_shapes=[
            pltpu.VMEM((bsz, 2, S5_NSTATE), F32),
            pltpu.VMEM((bsz, CHUNK + 8, HG_CONV_W), F32),
            pltpu.VMEM((bsz, HG_WIDTH, HG_WIDTH), F32),
            pltpu.VMEM((bsz, RET_KEY_WIDTH, RET_WIDTH), F32),
        ],
        compiler_params=pltpu.CompilerParams(
            dimension_semantics=("arbitrary",), vmem_limit_bytes=VMEM_LIMIT),
        name="mixers",
    )(proj, cos, sin, *consts, *to_bf16)
    return outs[0], list(outs[1:])


def _route(hn, router_ref, ltri_ref, count_ref):
    r = router_ref[...]
    r_hi, r_lo = _split_bf16(r)
    h_hi, h_lo = _split_bf16(hn)
    logits = _dot(h_hi, r_hi) + (_dot(h_lo, r_hi) + _dot(h_hi, r_lo))
    lane_i = lax.broadcasted_iota(jnp.int32, logits.shape, 1)
    lane = lane_i.astype(F32)
    neg = jnp.float32(-jnp.inf)
    logits = jnp.where(lane_i < N_EXPERTS, logits, neg)
    m1 = jnp.max(logits, axis=-1, keepdims=True)
    i1 = jnp.min(jnp.where(logits == m1, lane, float(ROUTE_LANES)), axis=-1, keepdims=True)
    rest = jnp.where(lane == i1, neg, logits)
    m2 = jnp.max(rest, axis=-1, keepdims=True)
    i2 = jnp.min(jnp.where(rest == m2, lane, float(ROUTE_LANES)), axis=-1, keepdims=True)
    e2 = jnp.exp(m2 - m1)
    g1 = 1.0 / (1.0 + e2)
    g2 = e2 / (1.0 + e2)
    onehot = jnp.where((lane == i1) | (lane == i2), 1.0, 0.0)
    before = _dot(ltri_ref[...], onehot.astype(BF16)) + count_ref[...]
    r1 = jnp.sum(jnp.where(lane == i1, before, 0.0), axis=-1, keepdims=True)
    r2 = jnp.sum(jnp.where(lane == i2, before, 0.0), axis=-1, keepdims=True)
    count_ref[...] += jnp.sum(onehot, axis=0, keepdims=True)
    out = jnp.where(lane == 0, i1, 0.0)
    out = jnp.where(lane == 1, i2, out)
    out = jnp.where(lane == 2, g1, out)
    out = jnp.where(lane == 3, g2, out)
    out = jnp.where(lane == 4, r1, out)
    return jnp.where(lane == 5, r2, out)


def _out_proj_kernel(mixed_ref, h_ref, w_ref, g_ref, *rest, routed):
    if routed:
        router_ref, ltri_ref, h1_ref, hn_ref, route_ref, count_ref = rest

        @pl.when(pl.program_id(0) == 0)
        def _():
            count_ref[...] = jnp.zeros_like(count_ref)
    else:
        h1_ref, hn_ref = rest
    h1 = h_ref[...] + _dot(mixed_ref[...], w_ref[...])
    h1_ref[...] = h1
    hn = h1 * lax.rsqrt(jnp.mean(h1 * h1, axis=-1, keepdims=True) + EPS) * g_ref[...]
    if routed:
        tm = hn.shape[0]
        for s in range(ROW_TILE):
            hn_ref[pl.ds(s, tm, stride=ROW_TILE), :] = hn[:, s * 128:(s + 1) * 128]
        route_ref[...] = _route(hn, router_ref, ltri_ref, count_ref)
    else:
        hn_ref[...] = hn.astype(hn_ref.dtype)


def _out_proj(mixed, h, w_bf16, g, router_pad=None):
    n, d = h.shape
    tm = _pick_tile(n, (640, 128))
    routed = router_pad is not None
    row = lambda i: (i, 0)
    fixed = lambda i: (0, 0)
    in_specs = [pl.BlockSpec((tm, D_MIX), row), pl.BlockSpec((tm, d), row),
                pl.BlockSpec(w_bf16.shape, fixed), pl.BlockSpec((1, d), fixed)]
    out_specs = [pl.BlockSpec((tm, d), row), pl.BlockSpec((tm, d), row)]
    out_shape = [jax.ShapeDtypeStruct((n, d), F32), jax.ShapeDtypeStruct((n, d), BF16)]
    args = [mixed, h, w_bf16, g]
    if routed:
        assert d == ROW_TILE * 128
        out_specs[1] = pl.BlockSpec((tm * ROW_TILE, 128), row)
        out_shape[1] = jax.ShapeDtypeStruct((n * ROW_TILE, 128), F32)
        ltri = jnp.asarray(np.tril(np.ones((tm, tm), np.float32), -1), BF16)
        in_specs += [pl.BlockSpec(router_pad.shape, fixed), pl.BlockSpec((tm, tm), fixed)]
        out_specs += [pl.BlockSpec((tm, ROUTE_LANES), row), pl.BlockSpec((1, ROUTE_LANES), fixed)]
        out_shape += [jax.ShapeDtypeStruct((n, ROUTE_LANES), F32),
                      jax.ShapeDtypeStruct((1, ROUTE_LANES), F32)]
        args += [router_pad, ltri]
    return pl.pallas_call(
        functools.partial(_out_proj_kernel, routed=routed),
        grid=(n // tm,),
        in_specs=in_specs,
        out_specs=out_specs,
        out_shape=out_shape,
        compiler_params=pltpu.CompilerParams(
            dimension_semantics=("arbitrary",), vmem_limit_bytes=VMEM_LIMIT),
        name="out_proj_routed" if routed else "out_proj",
    )(*args)


def _ffn_kernel(hn_ref, h1_ref, w1_ref, w3_ref, w2_ref, o_ref, *, tf):
    x = hn_ref[...]
    acc = h1_ref[...]
    for c0 in range(0, w1_ref.shape[1], tf):
        a = _dot(x, w1_ref[:, c0:c0 + tf])
        b = _dot(x, w3_ref[:, c0:c0 + tf])
        acc = acc + _dot((a * _sigmoid(a) * b).astype(BF16), w2_ref[c0:c0 + tf, :])
    o_ref[...] = acc


def _ffn(hn, h1, w1, w3, w2):
    n, d = h1.shape
    dff = w1.shape[1]
    tm, tf = _pick_tile(n, (640, 128)), _pick_tile(dff, (1408, 128))
    resident = lambda a: pl.BlockSpec(a.shape, lambda i: (0, 0), pipeline_mode=pl.Buffered(1))
    return pl.pallas_call(
        functools.partial(_ffn_kernel, tf=tf),
        grid=(n // tm,),
        in_specs=[
            pl.BlockSpec((tm, d), lambda i: (i, 0)),
            pl.BlockSpec((tm, d), lambda i: (i, 0)),
            resident(w1), resident(w3), resident(w2),
        ],
        out_specs=pl.BlockSpec((tm, d), lambda i: (i, 0)),
        out_shape=jax.ShapeDtypeStruct((n, d), F32),
        compiler_params=pltpu.CompilerParams(
            dimension_semantics=("arbitrary",), vmem_limit_bytes=VMEM_LIMIT),
        name="ffn_dense",
    )(hn, h1, w1, w3, w2)


def _moe_kernel(te_ref, nv_ref, src_ref, tok_ref, hn_hbm, w1_ref, w3_ref, w2_ref, o_ref,
                xs_ref, xb_ref, acc_ref, sem, *, tm):
    i, f = pl.program_id(0), pl.program_id(1)
    nf = pl.num_programs(1)
    n_valid = nv_ref[0]
    share = tm // MOE_F_STEPS

    def row_copy(tile, r):
        tok = tok_ref[src_ref[tile] + r]
        return pltpu.make_async_copy(hn_hbm.at[pl.ds(pl.multiple_of(tok * ROW_TILE, ROW_TILE), ROW_TILE)],
                                     xs_ref.at[pl.ds(pl.multiple_of(r * ROW_TILE, ROW_TILE), ROW_TILE)], sem)

    @pl.when((i == 0) & (f == 0))
    def _():
        def body(r, carry):
            row_copy(0, r).start()
            return carry
        lax.fori_loop(0, tm, body, 0)

    @pl.when((f == 0) & (i <= n_valid))
    def _():
        pltpu.make_async_copy(xs_ref, xs_ref, sem).wait()
        for s in range(ROW_TILE):
            xb_ref[:, s * 128:(s + 1) * 128] = xs_ref[pl.ds(s, tm, stride=ROW_TILE), :].astype(BF16)

    @pl.when(i < n_valid)
    def _():
        for r in range(share):
            row_copy(i + 1, f * share + r).start(priority=r % MOE_DMA_QUEUES)
        x = xb_ref[...]
        a = _dot(x, w1_ref[0])
        b = _dot(x, w3_ref[0])
        part = _dot((a * _sigmoid(a) * b).astype(BF16), w2_ref[0])

        @pl.when(f == 0)
        def _():
            acc_ref[...] = part

        @pl.when(f > 0)
        def _():
            acc_ref[...] += part

    @pl.when(f == nf - 1)
    def _():
        o_ref[...] = jnp.where(i < n_valid, acc_ref[...], 0.0).astype(o_ref.dtype)


def _live_f(i, f, nv):
    return jnp.where(i < nv[0], f, 0)


def _moe_grouped(tile_expert, n_valid, tile_src, sorted_tok, hn, w1, w3, w2, tm):
    d = ROW_TILE * hn.shape[1]
    n_tiles = tile_expert.shape[0]
    dff = w1.shape[2]
    tf = dff // MOE_F_STEPS
    grid_spec = pltpu.PrefetchScalarGridSpec(
        num_scalar_prefetch=4,
        grid=(n_tiles, MOE_F_STEPS),
        in_specs=[
            pl.BlockSpec(memory_space=pl.ANY),
            pl.BlockSpec((1, d, tf), lambda i, f, te, nv, *_: (te[i], 0, _live_f(i, f, nv))),
            pl.BlockSpec((1, d, tf), lambda i, f, te, nv, *_: (te[i], 0, _live_f(i, f, nv))),
            pl.BlockSpec((1, tf, d), lambda i, f, te, nv, *_: (te[i], _live_f(i, f, nv), 0)),
        ],
        out_specs=pl.BlockSpec((tm, d), lambda i, f, *_: (i, 0)),
        scratch_shapes=[pltpu.VMEM((tm * ROW_TILE, 128), F32), pltpu.VMEM((tm, d), BF16),
                        pltpu.VMEM((tm, d), F32), pltpu.SemaphoreType.DMA(())],
    )
    return pl.pallas_call(
        functools.partial(_moe_kernel, tm=tm),
        grid_spec=grid_spec,
        out_shape=jax.ShapeDtypeStruct((n_tiles * tm, d), BF16),
        compiler_params=pltpu.CompilerParams(
            dimension_semantics=("arbitrary", "arbitrary"), vmem_limit_bytes=VMEM_LIMIT),
        name="moe_grouped",
    )(tile_expert, n_valid, tile_src, sorted_tok, hn, w1, w3, w2)


def _moe_dispatch(route, counts_row, tm):
    n = route.shape[0]
    n_slots = n * TOP_K
    n_tiles = (n_slots + N_EXPERTS * tm + tm - 1) // tm
    counts = counts_row[0, :N_EXPERTS].astype(jnp.int32)
    padded = ((counts + tm - 1) // tm) * tm
    pend = jnp.cumsum(padded)
    pstart = pend - padded
    gstart = jnp.cumsum(counts) - counts
    eid = route[:, 0:TOP_K].astype(jnp.int32)
    rank = route[:, 2 * TOP_K:3 * TOP_K].astype(jnp.int32)
    pos = pstart[eid] + rank
    tok = jnp.broadcast_to(jnp.arange(n, dtype=jnp.int32)[:, None], (n, TOP_K))
    _, sorted_tok = lax.sort(((gstart[eid] + rank).reshape(-1), tok.reshape(-1)), num_keys=1)
    sorted_tok = jnp.concatenate([sorted_tok, jnp.zeros((tm,), jnp.int32)])
    tile_start = jnp.arange(n_tiles, dtype=jnp.int32) * tm
    tile_expert = jnp.minimum(jnp.sum((tile_start[:, None] >= pend[None, :]).astype(jnp.int32), axis=1),
                              N_EXPERTS - 1)
    tile_src = jnp.clip(tile_start - (pstart - gstart)[tile_expert], 0, n_slots)
    n_valid = (pend[-1] // tm).astype(jnp.int32).reshape(1)
    return sorted_tok, tile_src, pos, tile_expert, n_valid


def _final_kernel(h1_ref, y0_ref, y1_ref, route_ref, g_ref, o_ref):
    g0 = route_ref[:, TOP_K:TOP_K + 1]
    g1 = route_ref[:, TOP_K + 1:TOP_K + 2]
    h = h1_ref[...] + (g0 * y0_ref[...].astype(F32) + g1 * y1_ref[...].astype(F32))
    o_ref[...] = h * lax.rsqrt(jnp.mean(h * h, axis=-1, keepdims=True) + EPS) * g_ref[...]


def _final(h1, y0, y1, route, g, bsz, total):
    d = h1.shape[1]
    seq = total - CHUNK
    tm = _pick_tile(seq, (1024, 128))
    per_seq = seq // tm
    src = lambda b, k: (pl.multiple_of(b * total + CHUNK + k * tm, CHUNK), 0)
    rows = lambda width: pl.BlockSpec((pl.Element(tm), pl.Element(width)), src)
    out = pl.pallas_call(
        _final_kernel,
        grid=(bsz, per_seq),
        in_specs=[rows(d), rows(d), rows(d), rows(ROUTE_LANES), pl.BlockSpec((1, d), lambda b, k: (0, 0))],
        out_specs=pl.BlockSpec((tm, d), lambda b, k: (b * per_seq + k, 0)),
        out_shape=jax.ShapeDtypeStruct((bsz * seq, d), F32),
        compiler_params=pltpu.CompilerParams(
            dimension_semantics=("arbitrary", "arbitrary"), vmem_limit_bytes=VMEM_LIMIT),
        name="final_norm",
    )(h1, y0, y1, route, g)
    return out.reshape(bsz, seq, d)


def _rope_perm():
    half = RET_DK // 2
    idx = []
    for part in range(2):
        for h in range(RET_HEADS):
            for dd in range(half):
                idx.append(h * RET_DK + part * half + dd)
    return np.asarray(idx, np.int32)


def _in_col_perm():
    cols = np.arange(IN_COLS, dtype=np.int32)
    rp = _rope_perm()
    cols[C_RQ:C_RQ + RET_KEY_WIDTH] = C_RQ + rp
    cols[C_RK:C_RK + RET_KEY_WIDTH] = C_RK + rp
    return cols


def _hg_tables(nb):
    c = CHUNK
    i = np.arange(c)[:, None]
    t = np.arange(c)[None, :]
    blocks, masks, rsel = [], [], []
    for lvl in range(HG_LEVELS):
        s = 1 << lvl
        r = (i // (2 * s)) * (2 * s) + s - 1
        right = i > r
        m = np.where(right, (t > r) & (t <= i), (t > i) & (t <= r))
        blocks.append(m)
        j = t
        same = (i // (2 * s)) == (j // (2 * s))
        mk = same & right & (j <= r)
        masks.append(np.concatenate([mk, mk], axis=1))
        if s < 8:
            rsel.append(np.broadcast_to(right, (c, HG_WIDTH)))
    blocks.append(t <= i)
    blocks.append(t > i)
    eye = np.eye(nb, dtype=np.float32)
    mall = np.concatenate([np.kron(eye, blk.astype(np.float32)) for blk in blocks], axis=0)
    lvl = np.stack(masks, axis=0).astype(np.float32)
    ch = np.arange(HG_WIDTH)
    bones = (ch[:, None] // HG_DK == ch[None, :] // HG_DK).astype(np.float32)
    rsel = np.stack([np.tile(r, (nb, 1)) for r in rsel], axis=0).astype(np.float32)
    return mall, lvl, rsel, bones


def _ret_tables():
    f32 = jnp.float32
    log_gamma = jnp.log1p(-jnp.power(2.0, -5.0 - jnp.arange(RET_HEADS, dtype=f32)))
    n = jnp.arange(CHUNK, dtype=f32)
    lg = log_gamma[:, None]
    causal = jnp.tril(jnp.ones((CHUNK, CHUNK), dtype=bool))
    intra = jnp.exp(jnp.where(causal[None], (n[:, None] - n[None, :])[None] * lg[:, :, None], -jnp.inf))
    scale = RET_DK ** -0.5
    inter = jnp.exp((n[None, :] + 1.0) * lg)
    to_state = jnp.exp((CHUNK - 1.0 - n[None, :]) * lg)
    carry = jnp.exp(CHUNK * lg)[:, 0]
    head_of_v = np.arange(RET_WIDTH) // RET_DV
    head_of_k = (np.arange(RET_KEY_WIDTH) % (RET_KEY_WIDTH // 2)) // (RET_DK // 2)
    inter_t = (inter * scale).T[:, head_of_v]
    to_state_t = to_state.T[:, head_of_k]
    carry_row = carry[head_of_v][None, :]
    bmask = jnp.asarray((head_of_k[:, None] == head_of_v[None, :]).astype(np.float32))
    intra_pairs = (intra * scale).reshape(RET_HEADS // 2, 2, CHUNK, CHUNK)
    intra_pairs = jnp.concatenate([intra_pairs[:, 0], intra_pairs[:, 1]], axis=2)
    return intra_pairs, inter_t, to_state_t, carry_row, bmask


def _s5_tables(lam_re, lam_im, b_re, b_im, c_re, c_im, d_skip, log_step):
    f32 = jnp.float32
    lam = lax.complex(lam_re.astype(f32), lam_im.astype(f32))
    step = jnp.exp(log_step.astype(f32))[:, None]
    lam_dt = lam * step
    lam_bar = jnp.exp(lam_dt)
    b_bar = ((lam_bar - 1.0) / lam)[..., None] * lax.complex(b_re.astype(f32), b_im.astype(f32))
    eye = jnp.eye(S5_NGROUPS, dtype=f32)
    wb_re = jnp.einsum('gph,gk->ghkp', jnp.real(b_bar), eye).reshape(S5_WIDTH, S5_NSTATE)
    wb_im = jnp.einsum('gph,gk->ghkp', jnp.imag(b_bar), eye).reshape(S5_WIDTH, S5_NSTATE)
    wb = jnp.concatenate([wb_re, wb_im], axis=1)
    wc_re = jnp.einsum('ghp,gk->gpkh', c_re.astype(f32), eye).reshape(S5_NSTATE, S5_WIDTH)
    wc_im = jnp.einsum('ghp,gk->gpkh', c_im.astype(f32), eye).reshape(S5_NSTATE, S5_WIDTH)
    wc = jnp.concatenate([wc_re, -wc_im], axis=0)
    t = jnp.arange(CHUNK, dtype=f32)[:, None, None]
    pp = jnp.exp(lam_dt[None] * t).reshape(CHUNK, S5_NSTATE)
    pn = jnp.exp(-lam_dt[None] * t).reshape(CHUNK, S5_NSTATE)
    lam_rows = jnp.stack([jnp.real(lam_bar).reshape(-1), jnp.imag(lam_bar).reshape(-1)], axis=0)
    return (wb.astype(BF16), wc.astype(BF16), jnp.real(pn), jnp.imag(pn), jnp.real(pp), jnp.imag(pp),
            lam_rows, d_skip.astype(f32).reshape(1, S5_WIDTH))


def kernel(x, meta_tokens, norm_mix_g, w_in, s5_lam_re, s5_lam_im, s5_b_re, s5_b_im, s5_c_re, s5_c_im, s5_d, s5_log_step, s5_w_glu, s5_out_g, hg_conv_w, hg_lb_param, hg_out_g, ret_out_g, w_out, norm_ffn_g, ffn_w1, ffn_w3, ffn_w2, moe_router, moe_w1, moe_w3, moe_w2, final_norm_g):
    bsz, seq_len, d = x.shape
    depth = w_in.shape[0]
    total = seq_len + CHUNK
    n_chunks = total // CHUNK
    n = bsz * total

    meta = jnp.broadcast_to(meta_tokens.astype(F32)[None], (bsz, N_META, d))
    h = jnp.concatenate([jnp.zeros((bsz, PAD, d), F32), meta, x.astype(F32)], axis=1).reshape(n, d)

    pos = (jnp.arange(total) - PAD).astype(F32)
    half = RET_DK // 2
    inv_freq = ROPE_BASE ** (-jnp.arange(half, dtype=F32) / half)
    ang = pos[:, None] * inv_freq[None, :]
    cos_t = jnp.tile(jnp.cos(ang), (1, RET_HEADS))
    sin_t = jnp.tile(jnp.sin(ang), (1, RET_HEADS))

    lb_all = jnp.cumsum(jax.nn.softmax(hg_lb_param.astype(F32), axis=0), axis=0)
    lb_all = lb_all - lb_all[0]

    mall_np, lvl_np, rsel_np, bones_np = _hg_tables(bsz)
    mall = jnp.asarray(mall_np, BF16)
    lvl = jnp.asarray(lvl_np, F32)
    rsel = jnp.asarray(rsel_np, F32)
    bones = jnp.asarray(bones_np, BF16)
    avg = jnp.asarray(bones_np / HG_DK, BF16)
    ltri = jnp.asarray(np.kron(np.eye(bsz, dtype=np.float32),
                               np.tril(np.ones((CHUNK, CHUNK), np.float32))), BF16)
    intra, inter_t, to_state_t, carry_row, bmask = _ret_tables()
    col_perm = jnp.asarray(_in_col_perm())

    out = None
    for l in range(depth):
        w_in_l = jnp.take(w_in[l], col_perm, axis=1).astype(BF16)
        proj = _in_proj(h, norm_mix_g[l].astype(F32).reshape(1, d), w_in_l, total)

        s5c = _s5_tables(s5_lam_re[l], s5_lam_im[l], s5_b_re[l], s5_b_im[l], s5_c_re[l], s5_c_im[l],
                         s5_d[l], s5_log_step[l])
        lb = lb_all[l][None, :]
        lbv = jnp.concatenate([jnp.log(lb), jnp.log1p(-lb), 1.0 - lb], axis=0)
        consts = list(s5c) + [
            s5_w_glu[l].astype(BF16), s5_out_g[l].astype(F32).reshape(1, -1), ltri,
            hg_conv_w[l].astype(F32), lbv, mall, lvl, rsel, bones, avg, hg_out_g[l].astype(F32).reshape(1, -1),
            intra, inter_t, to_state_t, carry_row, bmask, ret_out_g[l].astype(F32).reshape(1, -1),
        ]
        j = l // 2
        if l % 2 == 0:
            ffn_f32 = [ffn_w1[j], ffn_w3[j], ffn_w2[j]]
        else:
            dff_e = moe_w1.shape[-1]
            ffn_f32 = [moe_w1[j].reshape(N_EXPERTS * d, dff_e), moe_w3[j].reshape(N_EXPERTS * d, dff_e),
                       moe_w2[j].reshape(N_EXPERTS * dff_e, d)]
        mixed, ffn_bf16 = _mixers(proj.reshape(bsz, total, IN_COLS), cos_t, sin_t, consts, bsz, n_chunks,
                                  to_bf16=[a.astype(F32) for a in ffn_f32])
        mixed = mixed.reshape(n, D_MIX)

        g_ffn = norm_ffn_g[l].astype(F32).reshape(1, d)
        w_out_l = w_out[l].astype(BF16)
        if l % 2 == 0:
            h1, hn = _out_proj(mixed, h, w_out_l, g_ffn)
            h = _ffn(hn, h1, *ffn_bf16)
            y0 = y1 = None
        else:
            router_pad = jnp.zeros((d, ROUTE_LANES), F32).at[:, :N_EXPERTS].set(moe_router[j].astype(F32))
            h1, hn, route, counts_row = _out_proj(mixed, h, w_out_l, g_ffn, router_pad)
            tm = MOE_TILE
            sorted_tok, tile_src, pos_of_slot, tile_expert, n_valid = _moe_dispatch(route, counts_row, tm)
            e_w1, e_w3, e_w2 = (ffn_bf16[0].reshape(N_EXPERTS, d, dff_e), ffn_bf16[1].reshape(N_EXPERTS, d, dff_e),
                                ffn_bf16[2].reshape(N_EXPERTS, dff_e, d))
            ys = _moe_grouped(tile_expert, n_valid, tile_src, sorted_tok, hn, e_w1, e_w3, e_w2, tm)
            y0 = jnp.take(ys, pos_of_slot[:, 0], axis=0)
            y1 = jnp.take(ys, pos_of_slot[:, 1], axis=0)
            if l < depth - 1:
                h = h1 + (route[:, TOP_K:TOP_K + 1] * y0.astype(F32)
                          + route[:, TOP_K + 1:TOP_K + 2] * y1.astype(F32))

        if l == depth - 1:
            if y0 is None:
                y0 = y1 = jnp.zeros((n, d), BF16)
                h1 = h
                route = jnp.zeros((n, ROUTE_LANES), F32)
            out = _final(h1, y0, y1, route, final_norm_g.astype(F32).reshape(1, d), bsz, total)

    return out.astype(x.dtype)
```

```python
import functools
import math

import jax
import jax.numpy as jnp
import numpy as np
from jax import lax
from jax.experimental import pallas as pl
from jax.experimental.pallas import tpu as pltpu

F32 = jnp.float32
BF16 = jnp.bfloat16

CHUNK = 128
N_META = 16
PAD = CHUNK - N_META
EPS = 1e-6

S5_WIDTH = 256
S5_GROUP = 16
S5_NGROUPS = 16
S5_STATE = 64
S5_NSTATE = S5_NGROUPS * S5_STATE

HG_HEADS = 4
HG_DK = 64
HG_WIDTH = 256
CONV_K = 4
HG_CONV_W = 3 * HG_WIDTH
HG_LEVELS = 7

RET_HEADS = 8
RET_DK = 32
RET_DV = 64
RET_KEY_WIDTH = 256
RET_WIDTH = 512
ROPE_BASE = 10000.0

D_MIX = 1024
IN_COLS = 2816
C_U, C_HQ, C_HGATE, C_RQ, C_RK, C_RV, C_RGATE = 0, 256, 1024, 1280, 1536, 1792, 2304

N_EXPERTS = 8
TOP_K = 2
ROUTE_LANES = 128
ROUTE_FIELDS = 8
MOE_F_STEPS = 2
MOE_TILE = 512
ROW_TILE = 8
MOE_DMA_QUEUES = 2

VMEM_LIMIT = 56 * 1024 * 1024


def _sigmoid(x):
    return 1.0 / (1.0 + jnp.exp(-x))


def _split_bf16(x):
    hi = x.astype(BF16)
    lo = (x - hi.astype(F32)).astype(BF16)
    return hi, lo


def _pick_tile(n, candidates):
    for t in candidates:
        if n % t == 0:
            return t
    raise ValueError(f"no tile in {candidates} divides {n}")


def _dot(a, b):
    return jnp.dot(a, b, preferred_element_type=F32)


def _dot_nt(a, b):
    return lax.dot_general(a, b, (((1,), (1,)), ((), ())), preferred_element_type=F32)


def _dot_tn(a, b):
    return lax.dot_general(a, b, (((0,), (0,)), ((), ())), preferred_element_type=F32)


TOKEN_TILE = 640


def _stream_specs(stream, tm, total):
    d = stream[0].shape[1]
    if len(stream) == 1:
        return [pl.BlockSpec((tm, d), lambda i: (i, 0))], list(stream)
    per, seq = total // tm, total - CHUNK
    x_rows = lambda i: (pl.multiple_of((i // per) * seq + jnp.maximum((i % per) * tm - CHUNK, 0), CHUNK), 0)
    return ([pl.BlockSpec((pl.Element(tm), pl.Element(d)), x_rows),
             pl.BlockSpec((1, tm, d), lambda i: (i // per, 0, 0))], list(stream))


def _stream_tile(refs, tiles_per_seq):
    if len(refs) == 1:
        return refs[0][...]
    x_ref, head_ref = refs
    return jnp.where(pl.program_id(0) % tiles_per_seq == 0, head_ref[0], x_ref[...])


def _in_proj_kernel(*refs, tm, tiles_per_seq):
    *stream, g_ref, w_ref, o_ref = refs
    x = _stream_tile(stream, tiles_per_seq)
    y = x * lax.rsqrt(jnp.mean(x * x, axis=-1, keepdims=True) + EPS) * g_ref[...]
    proj = _dot(y.astype(BF16), w_ref[...])
    row0 = (pl.program_id(0) % tiles_per_seq) * tm
    rows = row0 + lax.broadcasted_iota(jnp.int32, (tm, 1), 0)
    o_ref[...] = jnp.where(rows >= PAD, proj, 0.0)


def _in_proj(stream, g, w_bf16, bsz, total):
    n, d = bsz * total, stream[0].shape[1]
    tm = _pick_tile(total, (TOKEN_TILE, 128))
    specs, operands = _stream_specs(stream, tm, total)
    return pl.pallas_call(
        functools.partial(_in_proj_kernel, tm=tm, tiles_per_seq=total // tm),
        grid=(n // tm,),
        in_specs=specs + [
            pl.BlockSpec((1, d), lambda i: (0, 0)),
            pl.BlockSpec(w_bf16.shape, lambda i: (0, 0), pipeline_mode=pl.Buffered(1)),
        ],
        out_specs=pl.BlockSpec((tm, IN_COLS), lambda i: (i, 0)),
        out_shape=jax.ShapeDtypeStruct((n, IN_COLS), F32),
        compiler_params=pltpu.CompilerParams(
            dimension_semantics=("arbitrary",), vmem_limit_bytes=VMEM_LIMIT),
        name="in_proj",
    )(*operands, g, w_bf16)


def _rows(parts):
    return parts[0] if len(parts) == 1 else jnp.concatenate(parts, axis=0)


def _s5_chunks(u, wb_ref, wc_ref, pn_re_ref, pn_im_ref, pp_re_ref, pp_im_ref, lam_ref,
               d_ref, wglu_ref, g_ref, ltri_ref, st_ref):
    ns = S5_NSTATE
    nb = st_ref.shape[0]
    bu = _dot(u.astype(BF16), wb_ref[...])
    pn_re, pn_im = pn_re_ref[...], pn_im_ref[...]
    w_re, w_im = [], []
    for b in range(nb):
        bu_re = bu[b * CHUNK:(b + 1) * CHUNK, :ns]
        bu_im = bu[b * CHUNK:(b + 1) * CHUNK, ns:]
        w_re.append(pn_re * bu_re - pn_im * bu_im)
        w_im.append(pn_re * bu_im + pn_im * bu_re)
    ltri = ltri_ref[...]
    c_re = _dot(ltri, _rows(w_re).astype(BF16))
    c_im = _dot(ltri, _rows(w_im).astype(BF16))
    lam_re, lam_im = lam_ref[0:1, :], lam_ref[1:2, :]
    pp_re, pp_im = pp_re_ref[...], pp_im_ref[...]
    st_re, st_im = [], []
    for b in range(nb):
        s_re, s_im = st_ref[b, 0:1, :], st_ref[b, 1:2, :]
        z_re = c_re[b * CHUNK:(b + 1) * CHUNK] + (lam_re * s_re - lam_im * s_im)
        z_im = c_im[b * CHUNK:(b + 1) * CHUNK] + (lam_re * s_im + lam_im * s_re)
        t_re = pp_re * z_re - pp_im * z_im
        t_im = pp_re * z_im + pp_im * z_re
        st_ref[b, 0:1, :] = t_re[CHUNK - 1:CHUNK, :]
        st_ref[b, 1:2, :] = t_im[CHUNK - 1:CHUNK, :]
        st_re.append(t_re)
        st_im.append(t_im)
    y = (_dot(_rows(st_re).astype(BF16), wc_ref[0:ns, :]) + _dot(_rows(st_im).astype(BF16), wc_ref[ns:2 * ns, :])
         + d_ref[...] * u)
    y = 0.5 * y * (1.0 + jnp.tanh(math.sqrt(2.0 / math.pi) * (y + 0.044715 * (y * y * y))))
    y = y * _sigmoid(_dot(y.astype(BF16), wglu_ref[...]))
    return y * lax.rsqrt(jnp.mean(y * y, axis=-1, keepdims=True) + EPS) * g_ref[...]


def _hgrn2_chunks(xc_ref, gate, convw_ref, lbv_ref, mall_ref, lvl_ref, rsel_ref, bones_ref, avg_ref,
                  g_ref, st_ref):
    w = HG_WIDTH
    nb = st_ref.shape[0]
    rows = nb * CHUNK
    convs = []
    for b in range(nb):
        conv = None
        for i in range(CONV_K):
            term = xc_ref[b, pl.ds(8 - (CONV_K - 1) + i, CHUNK), :] * convw_ref[i:i + 1, :]
            conv = term if conv is None else conv + term
        convs.append(conv)
    conv = _rows(convs)
    cq, cf, v = conv[:, :w], conv[:, w:2 * w], conv[:, 2 * w:]
    q = cq * _sigmoid(cq)
    log_lb, log_1m_lb, one_m_lb = lbv_ref[0:1, :], lbv_ref[1:2, :], lbv_ref[2:3, :]
    log_sig = jnp.minimum(cf, 0.0) - jnp.log(1.0 + jnp.exp(-jnp.abs(cf)))
    b_ = log_1m_lb + log_sig
    logf = jnp.maximum(log_lb, b_) + jnp.log(1.0 + jnp.exp(-jnp.abs(log_lb - b_)))
    kk = one_m_lb * _sigmoid(-cf)
    hi, lo = _split_bf16(logf)

    lvl_sums = _dot(mall_ref[0:HG_LEVELS * rows, :], hi)
    tail = mall_ref[HG_LEVELS * rows:(HG_LEVELS + 2) * rows, :]
    cum_suf = _dot(tail, hi) + _dot(tail, lo)
    g_cum, g_suffix = cum_suf[:rows], cum_suf[rows:]

    lane = lax.broadcasted_iota(jnp.int32, (1, 2 * HG_DK), 1)
    head_sel = [jnp.where(lane < HG_DK, 1.0, 0.0), jnp.where(lane >= HG_DK, 1.0, 0.0)]
    n_pairs = HG_HEADS // 2
    scores = [[None] * n_pairs for _ in range(nb)]
    for lvl in range(HG_LEVELS):
        s = 1 << lvl
        e = jnp.exp(lvl_sums[lvl * rows:(lvl + 1) * rows])
        if s >= 8:
            qk = jnp.concatenate([(q if (r // s) % 2 else kk)[r:r + s] for r in range(0, rows, s)], axis=0)
        else:
            qk = jnp.where(rsel_ref[lvl] > 0.5, q, kk)
        x = qk * e
        m = lvl_ref[lvl]
        for b in range(nb):
            for p in range(n_pairs):
                xp = x[b * CHUNK:(b + 1) * CHUNK, p * 128:(p + 1) * 128]
                rhs = jnp.concatenate([xp * head_sel[0], xp * head_sel[1]], axis=0).astype(BF16)
                sc = _dot_nt(xp.astype(BF16), rhs) * m
                scores[b][p] = sc if scores[b][p] is None else scores[b][p] + sc
    v_bf = v.astype(BF16)
    bones = bones_ref[...]
    qg = (q * jnp.exp(g_cum)).astype(BF16)
    kd = (kk * jnp.exp(g_suffix)).astype(BF16)
    o_rows = []
    for b in range(nb):
        sl = slice(b * CHUNK, (b + 1) * CHUNK)
        o_parts = []
        for p in range(n_pairs):
            vp = v[sl, p * 128:(p + 1) * 128]
            vv = jnp.concatenate([vp * head_sel[0], vp * head_sel[1]], axis=0).astype(BF16)
            o_parts.append(_dot(scores[b][p].astype(BF16), vv))
        st = st_ref[b]
        o_rows.append(jnp.concatenate(o_parts, axis=1) + _dot_nt(qg[sl], st.astype(BF16)))
        upd = _dot_tn(v_bf[sl], kd[sl]) * bones.astype(F32)
        st_ref[b] = st * jnp.exp(g_cum[(b + 1) * CHUNK - 1:(b + 1) * CHUNK, :]) + upd
    o = _rows(o_rows) + _dot((q * kk).astype(BF16), bones) * v
    ms = _dot((o * o).astype(BF16), avg_ref[...])
    return o * lax.rsqrt(ms + EPS) * g_ref[...] * (gate * _sigmoid(gate))


def _ret_chunk(rq, rk, v, cos, sin, intra_ref, inter_ref, tostate_ref, carry_ref, bmask_ref, st_ref):
    hw = RET_KEY_WIDTH // 2

    def rot(t):
        t1, t2 = t[:, :hw], t[:, hw:]
        return jnp.concatenate([t1 * cos - t2 * sin, t1 * sin + t2 * cos], axis=1)

    qr, kr = rot(rq), rot(rk)
    qr_bf, v_bf = qr.astype(BF16), v.astype(BF16)
    lane_k = lax.broadcasted_iota(jnp.int32, (1, RET_KEY_WIDTH), 1) % hw
    lane_v = lax.broadcasted_iota(jnp.int32, (1, 2 * RET_DV), 1)
    v_sel = [jnp.where(lane_v < RET_DV, 1.0, 0.0), jnp.where(lane_v >= RET_DV, 1.0, 0.0)]
    half = RET_DK // 2
    o_parts = []
    for p in range(RET_HEADS // 2):
        sel = [jnp.where((lane_k >= h * half) & (lane_k < (h + 1) * half), 1.0, 0.0)
               for h in (2 * p, 2 * p + 1)]
        rhs = jnp.concatenate([kr * sel[0], kr * sel[1]], axis=0).astype(BF16)
        sc = (_dot_nt(qr_bf, rhs) * intra_ref[p]).astype(BF16)
        vp = v[:, p * 128:(p + 1) * 128]
        vv = jnp.concatenate([vp * v_sel[0], vp * v_sel[1]], axis=0).astype(BF16)
        o_parts.append(_dot(sc, vv))
    o = jnp.concatenate(o_parts, axis=1)
    st = st_ref[...]
    o = o + _dot(qr_bf, st.astype(BF16)) * inter_ref[...]
    kd = (kr * tostate_ref[...]).astype(BF16)
    st_ref[...] = st * carry_ref[...] + _dot_tn(kd, v_bf) * bmask_ref[...]
    return o


def _ret_norm_gate(o, gate, avg_ref, g_ref):
    avg = avg_ref[...]
    outs = []
    for s in range(RET_WIDTH // 256):
        os_ = o[:, s * 256:(s + 1) * 256]
        c = os_ - _dot(os_.astype(BF16), avg)
        outs.append(c * lax.rsqrt(_dot((c * c).astype(BF16), avg) + EPS))
    return jnp.concatenate(outs, axis=1) * g_ref[...] * (gate * _sigmoid(gate))


def _mixer_kernel(proj_ref, cos_ref, sin_ref,
                  wb_ref, wc_ref, pn_re_ref, pn_im_ref, pp_re_ref, pp_im_ref, lam_ref, d_ref,
                  wglu_ref, s5g_ref, ltri_ref,
                  convw_ref, lbv_ref, mall_ref, lvl_ref, rsel_ref, bones_ref, avg_ref, hgg_ref,
                  intra_ref, inter_ref, tostate_ref, carry_ref, bmask_ref, retg_ref,
                  *rest, n_cast, cast_steps):
    cast_in, o_ref, cast_out = rest[:n_cast], rest[n_cast], rest[n_cast + 1:2 * n_cast + 1]
    s5_st, hg_xc, hg_st, ret_st = rest[2 * n_cast + 1:]

    @pl.when(pl.program_id(0) == 0)
    def _():
        s5_st[...] = jnp.zeros_like(s5_st)
        hg_xc[...] = jnp.zeros_like(hg_xc)
        hg_st[...] = jnp.zeros_like(hg_st)
        ret_st[...] = jnp.zeros_like(ret_st)

    if n_cast:
        @pl.when(pl.program_id(0) < cast_steps)
        def _():
            for src, dst in zip(cast_in, cast_out):
                dst[...] = src[...].astype(dst.dtype)

    nb = proj_ref.shape[0]

    def cols(c0, width):
        return _rows([proj_ref[b, :, c0:c0 + width] for b in range(nb)])

    def emit(c0, y):
        for b in range(nb):
            o_ref[b, :, c0:c0 + y.shape[1]] = y[b * CHUNK:(b + 1) * CHUNK].astype(o_ref.dtype)

    y_a = _s5_chunks(cols(C_U, S5_WIDTH), wb_ref, wc_ref, pn_re_ref, pn_im_ref, pp_re_ref, pp_im_ref,
                     lam_ref, d_ref, wglu_ref, s5g_ref, ltri_ref, s5_st)
    emit(0, y_a)

    for b in range(nb):
        hg_xc[b, 8:8 + CHUNK, :] = proj_ref[b, :, C_HQ:C_HQ + HG_CONV_W]
    y_b = _hgrn2_chunks(hg_xc, cols(C_HGATE, HG_WIDTH), convw_ref, lbv_ref,
                        mall_ref, lvl_ref, rsel_ref, bones_ref, avg_ref, hgg_ref, hg_st)
    for b in range(nb):
        hg_xc[b, 0:8, :] = hg_xc[b, CHUNK:CHUNK + 8, :]
    emit(S5_WIDTH, y_b)

    o_c = [_ret_chunk(proj_ref[b, :, C_RQ:C_RQ + RET_KEY_WIDTH], proj_ref[b, :, C_RK:C_RK + RET_KEY_WIDTH],
                      proj_ref[b, :, C_RV:C_RV + RET_WIDTH], cos_ref[...], sin_ref[...],
                      intra_ref, inter_ref, tostate_ref, carry_ref, bmask_ref, ret_st.at[b])
           for b in range(nb)]
    y_c = _ret_norm_gate(_rows(o_c), cols(C_RGATE, RET_WIDTH), avg_ref, retg_ref)
    emit(S5_WIDTH + HG_WIDTH, y_c)


def _const_spec(a):
    nd = a.ndim
    return pl.BlockSpec(a.shape, lambda c, _nd=nd: (0,) * _nd)


def _cast_steps(n_chunks, arrays):
    for steps in range(n_chunks, 0, -1):
        if all(a.shape[0] % (steps * 16) == 0 for a in arrays):
            return steps
    raise ValueError("no slab split")


def _mixers(proj, cos, sin, consts, bsz, n_chunks, to_bf16=()):
    total = proj.shape[1]
    n_cast = len(to_bf16)
    cast_steps = _cast_steps(n_chunks, to_bf16) if n_cast else 0
    slab = lambda c: (jnp.minimum(c, cast_steps - 1), 0)
    cast_specs = [pl.BlockSpec((a.shape[0] // cast_steps, a.shape[1]), slab) for a in to_bf16]
    in_specs = [
        pl.BlockSpec((bsz, CHUNK, IN_COLS), lambda c: (0, c, 0)),
        pl.BlockSpec((CHUNK, 128), lambda c: (c, 0)),
        pl.BlockSpec((CHUNK, 128), lambda c: (c, 0)),
    ] + [_const_spec(a) for a in consts] + cast_specs
    outs = pl.pallas_call(
        functools.partial(_mixer_kernel, n_cast=n_cast, cast_steps=cast_steps),
        grid=(n_chunks,),
        in_specs=in_specs,
        out_specs=[pl.BlockSpec((bsz, CHUNK, D_MIX), lambda c: (0, c, 0))] + cast_specs,
        out_shape=[jax.ShapeDtypeStruct((bsz, total, D_MIX), BF16)]
        + [jax.ShapeDtypeStruct(a.shape, BF16) for a in to_bf16],
        scratch_shapes=[
            pltpu.VMEM((bsz, 2, S5_NSTATE), F32),
            pltpu.VMEM((bsz, CHUNK + 8, HG_CONV_W), F32),
            pltpu.VMEM((bsz, HG_WIDTH, HG_WIDTH), F32),
            pltpu.VMEM((bsz, RET_KEY_WIDTH, RET_WIDTH), F32),
        ],
        compiler_params=pltpu.CompilerParams(
            dimension_semantics=("arbitrary",), vmem_limit_bytes=VMEM_LIMIT),
        name="mixers",
    )(proj, cos, sin, *consts, *to_bf16)
    return outs[0], list(outs[1:])


def _route(hn, router_ref, ltri_ref, count_ref):
    r = router_ref[...]
    r_hi, r_lo = _split_bf16(r)
    h_hi, h_lo = _split_bf16(hn)
    logits = _dot(h_hi, r_hi) + (_dot(h_lo, r_hi) + _dot(h_hi, r_lo))
    lane_i = lax.broadcasted_iota(jnp.int32, logits.shape, 1)
    lane = lane_i.astype(F32)
    neg = jnp.float32(-jnp.inf)
    logits = jnp.where(lane_i < N_EXPERTS, logits, neg)
    m1 = jnp.max(logits, axis=-1, keepdims=True)
    i1 = jnp.min(jnp.where(logits == m1, lane, float(ROUTE_LANES)), axis=-1, keepdims=True)
    rest = jnp.where(lane == i1, neg, logits)
    m2 = jnp.max(rest, axis=-1, keepdims=True)
    i2 = jnp.min(jnp.where(rest == m2, lane, float(ROUTE_LANES)), axis=-1, keepdims=True)
    e2 = jnp.exp(m2 - m1)
    g1 = 1.0 / (1.0 + e2)
    g2 = e2 / (1.0 + e2)
    onehot = jnp.where((lane == i1) | (lane == i2), 1.0, 0.0)
    before = _dot(ltri_ref[...], onehot.astype(BF16)) + count_ref[...]
    r1 = jnp.sum(jnp.where(lane == i1, before, 0.0), axis=-1, keepdims=True)
    r2 = jnp.sum(jnp.where(lane == i2, before, 0.0), axis=-1, keepdims=True)
    count_ref[...] += jnp.sum(onehot, axis=0, keepdims=True)
    out = jnp.where(lane == 0, i1, 0.0)
    out = jnp.where(lane == 1, i2, out)
    out = jnp.where(lane == 2, g1, out)
    out = jnp.where(lane == 3, g2, out)
    out = jnp.where(lane == 4, r1, out)
    return jnp.where(lane == 5, r2, out)


def _out_proj_kernel(mixed_ref, *refs, routed, n_stream, tiles_per_seq):
    stream, (w_ref, g_ref, *rest) = refs[:n_stream], refs[n_stream:]
    if routed:
        router_ref, ltri_ref, h1_ref, hn_ref, route_ref, fields_ref, count_ref = rest

        @pl.when(pl.program_id(0) == 0)
        def _():
            count_ref[...] = jnp.zeros_like(count_ref)
    else:
        h1_ref, hn_ref = rest
    h1 = _stream_tile(stream, tiles_per_seq) + _dot(mixed_ref[...], w_ref[...])
    h1_ref[...] = h1
    hn = h1 * lax.rsqrt(jnp.mean(h1 * h1, axis=-1, keepdims=True) + EPS) * g_ref[...]
    if routed:
        tm = hn.shape[0]
        for s in range(ROW_TILE):
            hn_ref[pl.ds(s, tm, stride=ROW_TILE), :] = hn[:, s * 128:(s + 1) * 128]
        slab = _route(hn, router_ref, ltri_ref, count_ref)
        route_ref[...] = slab
        fields_ref[...] = slab.T[0:ROUTE_FIELDS, :]
    else:
        hn_ref[...] = hn.astype(hn_ref.dtype)


def _out_proj(mixed, stream, w_bf16, g, bsz, total, router_pad=None):
    n, d = bsz * total, stream[0].shape[1]
    tm = _pick_tile(total, (TOKEN_TILE, 128))
    routed = router_pad is not None
    row = lambda i: (i, 0)
    fixed = lambda i: (0, 0)
    stream_specs, stream_operands = _stream_specs(stream, tm, total)
    in_specs = [pl.BlockSpec((tm, D_MIX), row)] + stream_specs + [
        pl.BlockSpec(w_bf16.shape, fixed), pl.BlockSpec((1, d), fixed)]
    out_specs = [pl.BlockSpec((tm, d), row), pl.BlockSpec((tm, d), row)]
    out_shape = [jax.ShapeDtypeStruct((n, d), F32), jax.ShapeDtypeStruct((n, d), BF16)]
    args = [mixed] + stream_operands + [w_bf16, g]
    if routed:
        assert d == ROW_TILE * 128
        out_specs[1] = pl.BlockSpec((tm * ROW_TILE, 128), row)
        out_shape[1] = jax.ShapeDtypeStruct((n * ROW_TILE, 128), F32)
        ltri = jnp.asarray(np.tril(np.ones((tm, tm), np.float32), -1), BF16)
        in_specs += [pl.BlockSpec(router_pad.shape, fixed), pl.BlockSpec((tm, tm), fixed)]
        out_specs += [pl.BlockSpec((tm, ROUTE_LANES), row), pl.BlockSpec((ROUTE_FIELDS, tm), lambda i: (0, i)),
                      pl.BlockSpec((1, ROUTE_LANES), fixed)]
        out_shape += [jax.ShapeDtypeStruct((n, ROUTE_LANES), F32),
                      jax.ShapeDtypeStruct((ROUTE_FIELDS, n), F32),
                      jax.ShapeDtypeStruct((1, ROUTE_LANES), F32)]
        args += [router_pad, ltri]
    return pl.pallas_call(
        functools.partial(_out_proj_kernel, routed=routed, n_stream=len(stream), tiles_per_seq=total // tm),
        grid=(n // tm,),
        in_specs=in_specs,
        out_specs=out_specs,
        out_shape=out_shape,
        compiler_params=pltpu.CompilerParams(
            dimension_semantics=("arbitrary",), vmem_limit_bytes=VMEM_LIMIT),
        name="out_proj_routed" if routed else "out_proj",
    )(*args)


def _ffn_kernel(hn_ref, h1_ref, w1_ref, w3_ref, w2_ref, o_ref, *, tf):
    x = hn_ref[...]
    acc = h1_ref[...]
    for c0 in range(0, w1_ref.shape[1], tf):
        a = _dot(x, w1_ref[:, c0:c0 + tf])
        b = _dot(x, w3_ref[:, c0:c0 + tf])
        acc = acc + _dot((a * _sigmoid(a) * b).astype(BF16), w2_ref[c0:c0 + tf, :])
    o_ref[...] = acc


def _ffn(hn, h1, w1, w3, w2):
    n, d = h1.shape
    dff = w1.shape[1]
    tm, tf = _pick_tile(n, (640, 128)), _pick_tile(dff, (1408, 128))
    resident = lambda a: pl.BlockSpec(a.shape, lambda i: (0, 0), pipeline_mode=pl.Buffered(1))
    return pl.pallas_call(
        functools.partial(_ffn_kernel, tf=tf),
        grid=(n // tm,),
        in_specs=[
            pl.BlockSpec((tm, d), lambda i: (i, 0)),
            pl.BlockSpec((tm, d), lambda i: (i, 0)),
            resident(w1), resident(w3), resident(w2),
        ],
        out_specs=pl.BlockSpec((tm, d), lambda i: (i, 0)),
        out_shape=jax.ShapeDtypeStruct((n, d), F32),
        compiler_params=pltpu.CompilerParams(
            dimension_semantics=("arbitrary",), vmem_limit_bytes=VMEM_LIMIT),
        name="ffn_dense",
    )(hn, h1, w1, w3, w2)


def _moe_kernel(te_ref, nv_ref, src_ref, tok_ref, hn_hbm, w1_ref, w3_ref, w2_ref, o_ref,
                xs_ref, xb_ref, acc_ref, sem, *, tm):
    i, f = pl.program_id(0), pl.program_id(1)
    nf = pl.num_programs(1)
    n_valid = nv_ref[0]
    share = tm // MOE_F_STEPS

    def row_copy(tile, r):
        tok = tok_ref[src_ref[tile] + r]
        return pltpu.make_async_copy(hn_hbm.at[pl.ds(pl.multiple_of(tok * ROW_TILE, ROW_TILE), ROW_TILE)],
                                     xs_ref.at[pl.ds(pl.multiple_of(r * ROW_TILE, ROW_TILE), ROW_TILE)], sem)

    @pl.when((i == 0) & (f == 0))
    def _():
        def body(r, carry):
            row_copy(0, r).start()
            return carry
        lax.fori_loop(0, tm, body, 0)

    @pl.when((f == 0) & (i <= n_valid))
    def _():
        pltpu.make_async_copy(xs_ref, xs_ref, sem).wait()
        for s in range(ROW_TILE):
            xb_ref[:, s * 128:(s + 1) * 128] = xs_ref[pl.ds(s, tm, stride=ROW_TILE), :].astype(BF16)

    @pl.when(i < n_valid)
    def _():
        for r in range(share):
            row_copy(i + 1, f * share + r).start(priority=r % MOE_DMA_QUEUES)
        x = xb_ref[...]
        a = _dot(x, w1_ref[0])
        b = _dot(x, w3_ref[0])
        part = _dot((a * _sigmoid(a) * b).astype(BF16), w2_ref[0])

        @pl.when(f == 0)
        def _():
            acc_ref[...] = part

        @pl.when(f > 0)
        def _():
            acc_ref[...] += part

    @pl.when(f == nf - 1)
    def _():
        o_ref[...] = jnp.where(i < n_valid, acc_ref[...], 0.0).astype(o_ref.dtype)


def _live_f(i, f, nv):
    return jnp.where(i < nv[0], f, 0)


def _moe_grouped(tile_expert, n_valid, tile_src, sorted_tok, hn, w1, w3, w2, tm):
    d = ROW_TILE * hn.shape[1]
    n_tiles = tile_expert.shape[0]
    dff = w1.shape[2]
    tf = dff // MOE_F_STEPS
    grid_spec = pltpu.PrefetchScalarGridSpec(
        num_scalar_prefetch=4,
        grid=(n_tiles, MOE_F_STEPS),
        in_specs=[
            pl.BlockSpec(memory_space=pl.ANY),
            pl.BlockSpec((1, d, tf), lambda i, f, te, nv, *_: (te[i], 0, _live_f(i, f, nv))),
            pl.BlockSpec((1, d, tf), lambda i, f, te, nv, *_: (te[i], 0, _live_f(i, f, nv))),
            pl.BlockSpec((1, tf, d), lambda i, f, te, nv, *_: (te[i], _live_f(i, f, nv), 0)),
        ],
        out_specs=pl.BlockSpec((tm, d), lambda i, f, *_: (i, 0)),
        scratch_shapes=[pltpu.VMEM((tm * ROW_TILE, 128), F32), pltpu.VMEM((tm, d), BF16),
                        pltpu.VMEM((tm, d), F32), pltpu.SemaphoreType.DMA(())],
    )
    return pl.pallas_call(
        functools.partial(_moe_kernel, tm=tm),
        grid_spec=grid_spec,
        out_shape=jax.ShapeDtypeStruct((n_tiles * tm, d), BF16),
        compiler_params=pltpu.CompilerParams(
            dimension_semantics=("arbitrary", "arbitrary"), vmem_limit_bytes=VMEM_LIMIT),
        name="moe_grouped",
    )(tile_expert, n_valid, tile_src, sorted_tok, hn, w1, w3, w2)


def _moe_dispatch(fields, counts_row, tm):
    n = fields.shape[1]
    n_slots = n * TOP_K
    n_tiles = (n_slots + N_EXPERTS * tm + tm - 1) // tm
    counts = counts_row[0, :N_EXPERTS].astype(jnp.int32)
    padded = ((counts + tm - 1) // tm) * tm
    pend = jnp.cumsum(padded)
    pstart = pend - padded
    gstart = jnp.cumsum(counts) - counts
    eid = fields[0:TOP_K].astype(jnp.int32)
    rank = fields[2 * TOP_K:3 * TOP_K].astype(jnp.int32)
    is_e = [eid == e for e in range(N_EXPERTS)]
    lookup = lambda table: sum(jnp.where(m, table[e], 0) for e, m in enumerate(is_e))
    pos = lookup(pstart) + rank
    tok = jnp.broadcast_to(jnp.arange(n, dtype=jnp.int32)[None, :], (TOP_K, n))
    _, sorted_tok = lax.sort(((lookup(gstart) + rank).reshape(-1), tok.reshape(-1)), num_keys=1)
    sorted_tok = jnp.concatenate([sorted_tok, jnp.zeros((tm,), jnp.int32)])
    tile_start = jnp.arange(n_tiles, dtype=jnp.int32) * tm
    tile_expert = jnp.minimum(jnp.sum((tile_start[:, None] >= pend[None, :]).astype(jnp.int32), axis=1),
                              N_EXPERTS - 1)
    tile_src = jnp.clip(tile_start - (pstart - gstart)[tile_expert], 0, n_slots)
    n_valid = (pend[-1] // tm).astype(jnp.int32).reshape(1)
    return sorted_tok, tile_src, pos, tile_expert, n_valid


def _final_kernel(h1_ref, y0_ref, y1_ref, route_ref, g_ref, o_ref):
    g0 = route_ref[:, TOP_K:TOP_K + 1]
    g1 = route_ref[:, TOP_K + 1:TOP_K + 2]
    h = h1_ref[...] + (g0 * y0_ref[...].astype(F32) + g1 * y1_ref[...].astype(F32))
    o_ref[...] = h * lax.rsqrt(jnp.mean(h * h, axis=-1, keepdims=True) + EPS) * g_ref[...]


def _final(h1, y0, y1, route, g, bsz, total):
    d = h1.shape[1]
    seq = total - CHUNK
    tm = _pick_tile(seq, (1024, 128))
    per_seq = seq // tm
    src = lambda b, k: (pl.multiple_of(b * total + CHUNK + k * tm, CHUNK), 0)
    rows = lambda width: pl.BlockSpec((pl.Element(tm), pl.Element(width)), src)
    out = pl.pallas_call(
        _final_kernel,
        grid=(bsz, per_seq),
        in_specs=[rows(d), rows(d), rows(d), rows(ROUTE_LANES), pl.BlockSpec((1, d), lambda b, k: (0, 0))],
        out_specs=pl.BlockSpec((tm, d), lambda b, k: (b * per_seq + k, 0)),
        out_shape=jax.ShapeDtypeStruct((bsz * seq, d), F32),
        compiler_params=pltpu.CompilerParams(
            dimension_semantics=("arbitrary", "arbitrary"), vmem_limit_bytes=VMEM_LIMIT),
        name="final_norm",
    )(h1, y0, y1, route, g)
    return out.reshape(bsz, seq, d)


def _rope_perm():
    half = RET_DK // 2
    idx = []
    for part in range(2):
        for h in range(RET_HEADS):
            for dd in range(half):
                idx.append(h * RET_DK + part * half + dd)
    return np.asarray(idx, np.int32)


def _in_col_perm():
    cols = np.arange(IN_COLS, dtype=np.int32)
    rp = _rope_perm()
    cols[C_RQ:C_RQ + RET_KEY_WIDTH] = C_RQ + rp
    cols[C_RK:C_RK + RET_KEY_WIDTH] = C_RK + rp
    return cols


def _hg_tables(nb):
    c = CHUNK
    i = np.arange(c)[:, None]
    t = np.arange(c)[None, :]
    blocks, masks, rsel = [], [], []
    for lvl in range(HG_LEVELS):
        s = 1 << lvl
        r = (i // (2 * s)) * (2 * s) + s - 1
        right = i > r
        m = np.where(right, (t > r) & (t <= i), (t > i) & (t <= r))
        blocks.append(m)
        j = t
        same = (i // (2 * s)) == (j // (2 * s))
        mk = same & right & (j <= r)
        masks.append(np.concatenate([mk, mk], axis=1))
        if s < 8:
            rsel.append(np.broadcast_to(right, (c, HG_WIDTH)))
    blocks.append(t <= i)
    blocks.append(t > i)
    eye = np.eye(nb, dtype=np.float32)
    mall = np.concatenate([np.kron(eye, blk.astype(np.float32)) for blk in blocks], axis=0)
    lvl = np.stack(masks, axis=0).astype(np.float32)
    ch = np.arange(HG_WIDTH)
    bones = (ch[:, None] // HG_DK == ch[None, :] // HG_DK).astype(np.float32)
    rsel = np.stack([np.tile(r, (nb, 1)) for r in rsel], axis=0).astype(np.float32)
    return mall, lvl, rsel, bones


def _ret_tables():
    f32 = jnp.float32
    log_gamma = jnp.log1p(-jnp.power(2.0, -5.0 - jnp.arange(RET_HEADS, dtype=f32)))
    n = jnp.arange(CHUNK, dtype=f32)
    lg = log_gamma[:, None]
    causal = jnp.tril(jnp.ones((CHUNK, CHUNK), dtype=bool))
    intra = jnp.exp(jnp.where(causal[None], (n[:, None] - n[None, :])[None] * lg[:, :, None], -jnp.inf))
    scale = RET_DK ** -0.5
    inter = jnp.exp((n[None, :] + 1.0) * lg)
    to_state = jnp.exp((CHUNK - 1.0 - n[None, :]) * lg)
    carry = jnp.exp(CHUNK * lg)[:, 0]
    head_of_v = np.arange(RET_WIDTH) // RET_DV
    head_of_k = (np.arange(RET_KEY_WIDTH) % (RET_KEY_WIDTH // 2)) // (RET_DK // 2)
    inter_t = (inter * scale).T[:, head_of_v]
    to_state_t = to_state.T[:, head_of_k]
    carry_row = carry[head_of_v][None, :]
    bmask = jnp.asarray((head_of_k[:, None] == head_of_v[None, :]).astype(np.float32))
    intra_pairs = (intra * scale).reshape(RET_HEADS // 2, 2, CHUNK, CHUNK)
    intra_pairs = jnp.concatenate([intra_pairs[:, 0], intra_pairs[:, 1]], axis=2)
    return intra_pairs, inter_t, to_state_t, carry_row, bmask


def _s5_tables(lam_re, lam_im, b_re, b_im, c_re, c_im, d_skip, log_step):
    f32 = jnp.float32
    lam = lax.complex(lam_re.astype(f32), lam_im.astype(f32))
    step = jnp.exp(log_step.astype(f32))[:, None]
    lam_dt = lam * step
    lam_bar = jnp.exp(lam_dt)
    b_bar = ((lam_bar - 1.0) / lam)[..., None] * lax.complex(b_re.astype(f32), b_im.astype(f32))
    eye = jnp.eye(S5_NGROUPS, dtype=f32)
    wb_re = jnp.einsum('gph,gk->ghkp', jnp.real(b_bar), eye).reshape(S5_WIDTH, S5_NSTATE)
    wb_im = jnp.einsum('gph,gk->ghkp', jnp.imag(b_bar), eye).reshape(S5_WIDTH, S5_NSTATE)
    wb = jnp.concatenate([wb_re, wb_im], axis=1)
    wc_re = jnp.einsum('ghp,gk->gpkh', c_re.astype(f32), eye).reshape(S5_NSTATE, S5_WIDTH)
    wc_im = jnp.einsum('ghp,gk->gpkh', c_im.astype(f32), eye).reshape(S5_NSTATE, S5_WIDTH)
    wc = jnp.concatenate([wc_re, -wc_im], axis=0)
    t = jnp.arange(CHUNK, dtype=f32)[:, None, None]
    pp = jnp.exp(lam_dt[None] * t).reshape(CHUNK, S5_NSTATE)
    pn = jnp.exp(-lam_dt[None] * t).reshape(CHUNK, S5_NSTATE)
    lam_rows = jnp.stack([jnp.real(lam_bar).reshape(-1), jnp.imag(lam_bar).reshape(-1)], axis=0)
    return (wb.astype(BF16), wc.astype(BF16), jnp.real(pn), jnp.imag(pn), jnp.real(pp), jnp.imag(pp),
            lam_rows, d_skip.astype(f32).reshape(1, S5_WIDTH))


def kernel(x, meta_tokens, norm_mix_g, w_in, s5_lam_re, s5_lam_im, s5_b_re, s5_b_im, s5_c_re, s5_c_im, s5_d, s5_log_step, s5_w_glu, s5_out_g, hg_conv_w, hg_lb_param, hg_out_g, ret_out_g, w_out, norm_ffn_g, ffn_w1, ffn_w3, ffn_w2, moe_router, moe_w1, moe_w3, moe_w2, final_norm_g):
    bsz, seq_len, d = x.shape
    depth = w_in.shape[0]
    total = seq_len + CHUNK
    n_chunks = total // CHUNK
    n = bsz * total

    meta = jnp.broadcast_to(meta_tokens.astype(F32)[None], (bsz, N_META, d))
    tm0 = _pick_tile(total, (TOKEN_TILE, 128))
    head = jnp.concatenate([jnp.zeros((bsz, PAD, d), F32), meta, x[:, :tm0 - CHUNK].astype(F32)], axis=1)
    stream = (x.astype(F32).reshape(bsz * seq_len, d), head)

    pos = (jnp.arange(total) - PAD).astype(F32)
    half = RET_DK // 2
    inv_freq = ROPE_BASE ** (-jnp.arange(half, dtype=F32) / half)
    ang = pos[:, None] * inv_freq[None, :]
    cos_t = jnp.tile(jnp.cos(ang), (1, RET_HEADS))
    sin_t = jnp.tile(jnp.sin(ang), (1, RET_HEADS))

    lb_all = jnp.cumsum(jax.nn.softmax(hg_lb_param.astype(F32), axis=0), axis=0)
    lb_all = lb_all - lb_all[0]

    mall_np, lvl_np, rsel_np, bones_np = _hg_tables(bsz)
    mall = jnp.asarray(mall_np, BF16)
    lvl = jnp.asarray(lvl_np, F32)
    rsel = jnp.asarray(rsel_np, F32)
    bones = jnp.asarray(bones_np, BF16)
    avg = jnp.asarray(bones_np / HG_DK, BF16)
    ltri = jnp.asarray(np.kron(np.eye(bsz, dtype=np.float32),
                               np.tril(np.ones((CHUNK, CHUNK), np.float32))), BF16)
    intra, inter_t, to_state_t, carry_row, bmask = _ret_tables()
    col_perm = jnp.asarray(_in_col_perm())

    out = None
    for l in range(depth):
        w_in_l = jnp.take(w_in[l], col_perm, axis=1).astype(BF16)
        proj = _in_proj(stream, norm_mix_g[l].astype(F32).reshape(1, d), w_in_l, bsz, total)

        s5c = _s5_tables(s5_lam_re[l], s5_lam_im[l], s5_b_re[l], s5_b_im[l], s5_c_re[l], s5_c_im[l],
                         s5_d[l], s5_log_step[l])
        lb = lb_all[l][None, :]
        lbv = jnp.concatenate([jnp.log(lb), jnp.log1p(-lb), 1.0 - lb], axis=0)
        consts = list(s5c) + [
            s5_w_glu[l].astype(BF16), s5_out_g[l].astype(F32).reshape(1, -1), ltri,
            hg_conv_w[l].astype(F32), lbv, mall, lvl, rsel, bones, avg, hg_out_g[l].astype(F32).reshape(1, -1),
            intra, inter_t, to_state_t, carry_row, bmask, ret_out_g[l].astype(F32).reshape(1, -1),
        ]
        j = l // 2
        if l % 2 == 0:
            ffn_f32 = [ffn_w1[j], ffn_w3[j], ffn_w2[j]]
        else:
            dff_e = moe_w1.shape[-1]
            ffn_f32 = [moe_w1[j].reshape(N_EXPERTS * d, dff_e), moe_w3[j].reshape(N_EXPERTS * d, dff_e),
                       moe_w2[j].reshape(N_EXPERTS * dff_e, d)]
        mixed, ffn_bf16 = _mixers(proj.reshape(bsz, total, IN_COLS), cos_t, sin_t, consts, bsz, n_chunks,
                                  to_bf16=[a.astype(F32) for a in ffn_f32])
        mixed = mixed.reshape(n, D_MIX)

        g_ffn = norm_ffn_g[l].astype(F32).reshape(1, d)
        w_out_l = w_out[l].astype(BF16)
        if l % 2 == 0:
            h1, hn = _out_proj(mixed, stream, w_out_l, g_ffn, bsz, total)
            stream = (_ffn(hn, h1, *ffn_bf16),)
            y0 = y1 = None
        else:
            router_pad = jnp.zeros((d, ROUTE_LANES), F32).at[:, :N_EXPERTS].set(moe_router[j].astype(F32))
            h1, hn, route, fields, counts_row = _out_proj(mixed, stream, w_out_l, g_ffn, bsz, total, router_pad)
            tm = MOE_TILE
            sorted_tok, tile_src, pos_of_slot, tile_expert, n_valid = _moe_dispatch(fields, counts_row, tm)
            e_w1, e_w3, e_w2 = (ffn_bf16[0].reshape(N_EXPERTS, d, dff_e), ffn_bf16[1].reshape(N_EXPERTS, d, dff_e),
                                ffn_bf16[2].reshape(N_EXPERTS, dff_e, d))
            ys = _moe_grouped(tile_expert, n_valid, tile_src, sorted_tok, hn, e_w1, e_w3, e_w2, tm)
            y0 = jnp.take(ys, pos_of_slot[0], axis=0)
            y1 = jnp.take(ys, pos_of_slot[1], axis=0)
            if l < depth - 1:
                stream = (h1 + (route[:, TOP_K:TOP_K + 1] * y0.astype(F32)
                                + route[:, TOP_K + 1:TOP_K + 2] * y1.astype(F32)),)

        if l == depth - 1:
            if y0 is None:
                y0 = y1 = jnp.zeros((n, d), BF16)
                h1 = stream[0]
                route = jnp.zeros((n, ROUTE_LANES), F32)
            out = _final(h1, y0, y1, route, final_norm_g.astype(F32).reshape(1, d), bsz, total)

    return out.astype(x.dtype)
```

```python
import functools
import math

import jax
import jax.numpy as jnp
import numpy as np
from jax import lax
from jax.experimental import pallas as pl
from jax.experimental.pallas import tpu as pltpu

F32 = jnp.float32
BF16 = jnp.bfloat16

CHUNK = 128
N_META = 16
PAD = CHUNK - N_META
EPS = 1e-6

S5_WIDTH = 256
S5_GROUP = 16
S5_NGROUPS = 16
S5_STATE = 64
S5_NSTATE = S5_NGROUPS * S5_STATE

HG_HEADS = 4
HG_DK = 64
HG_WIDTH = 256
CONV_K = 4
HG_CONV_W = 3 * HG_WIDTH
HG_LEVELS = 7

RET_HEADS = 8
RET_DK = 32
RET_DV = 64
RET_KEY_WIDTH = 256
RET_WIDTH = 512
ROPE_BASE = 10000.0

D_MIX = 1024
IN_COLS = 2816
C_U, C_HQ, C_HGATE, C_RQ, C_RK, C_RV, C_RGATE = 0, 256, 1024, 1280, 1536, 1792, 2304

N_EXPERTS = 8
TOP_K = 2
ROUTE_LANES = 128
ROUTE_FIELDS = 8
MOE_F_STEPS = 2
MOE_TILE = 512
ROW_TILE = 8
MOE_DMA_QUEUES = 2

VMEM_LIMIT = 56 * 1024 * 1024


def _sigmoid(x):
    return 1.0 / (1.0 + jnp.exp(-x))


def _split_bf16(x):
    hi = x.astype(BF16)
    lo = (x - hi.astype(F32)).astype(BF16)
    return hi, lo


def _pick_tile(n, candidates):
    for t in candidates:
        if n % t == 0:
            return t
    raise ValueError(f"no tile in {candidates} divides {n}")


def _dot(a, b):
    return jnp.dot(a, b, preferred_element_type=F32)


def _dot_nt(a, b):
    return lax.dot_general(a, b, (((1,), (1,)), ((), ())), preferred_element_type=F32)


def _dot_tn(a, b):
    return lax.dot_general(a, b, (((0,), (0,)), ((), ())), preferred_element_type=F32)


TOKEN_TILE = 640


def _stream_specs(stream, tm, total):
    d = stream[0].shape[1]
    if len(stream) == 1:
        return [pl.BlockSpec((tm, d), lambda i: (i, 0))], list(stream)
    per, seq = total // tm, total - CHUNK
    x_rows = lambda i: (pl.multiple_of((i // per) * seq + jnp.maximum((i % per) * tm - CHUNK, 0), CHUNK), 0)
    return ([pl.BlockSpec((pl.Element(tm), pl.Element(d)), x_rows),
             pl.BlockSpec((1, tm, d), lambda i: (i // per, 0, 0))], list(stream))


def _stream_tile(refs, tiles_per_seq):
    if len(refs) == 1:
        return refs[0][...]
    x_ref, head_ref = refs
    return jnp.where(pl.program_id(0) % tiles_per_seq == 0, head_ref[0], x_ref[...])


def _in_proj_kernel(*refs, tm, tiles_per_seq):
    *stream, g_ref, w_ref, o_ref = refs
    x = _stream_tile(stream, tiles_per_seq)
    y = x * lax.rsqrt(jnp.mean(x * x, axis=-1, keepdims=True) + EPS) * g_ref[...]
    proj = _dot(y.astype(BF16), w_ref[...])
    row0 = (pl.program_id(0) % tiles_per_seq) * tm
    rows = row0 + lax.broadcasted_iota(jnp.int32, (tm, 1), 0)
    o_ref[...] = jnp.where(rows >= PAD, proj, 0.0)


def _in_proj(stream, g, w_bf16, bsz, total):
    n, d = bsz * total, stream[0].shape[1]
    tm = _pick_tile(total, (TOKEN_TILE, 128))
    specs, operands = _stream_specs(stream, tm, total)
    return pl.pallas_call(
        functools.partial(_in_proj_kernel, tm=tm, tiles_per_seq=total // tm),
        grid=(n // tm,),
        in_specs=specs + [
            pl.BlockSpec((1, d), lambda i: (0, 0)),
            pl.BlockSpec(w_bf16.shape, lambda i: (0, 0), pipeline_mode=pl.Buffered(1)),
        ],
        out_specs=pl.BlockSpec((tm, IN_COLS), lambda i: (i, 0)),
        out_shape=jax.ShapeDtypeStruct((n, IN_COLS), F32),
        compiler_params=pltpu.CompilerParams(
            dimension_semantics=("arbitrary",), vmem_limit_bytes=VMEM_LIMIT),
        name="in_proj",
    )(*operands, g, w_bf16)


def _rows(parts):
    return parts[0] if len(parts) == 1 else jnp.concatenate(parts, axis=0)


def _s5_chunks(u, wb_ref, wc_ref, pn_re_ref, pn_im_ref, pp_re_ref, pp_im_ref, lam_ref,
               d_ref, wglu_ref, g_ref, ltri_ref, st_ref):
    ns = S5_NSTATE
    nb = st_ref.shape[0]
    bu = _dot(u.astype(BF16), wb_ref[...])
    pn_re, pn_im = pn_re_ref[...], pn_im_ref[...]
    w_re, w_im = [], []
    for b in range(nb):
        bu_re = bu[b * CHUNK:(b + 1) * CHUNK, :ns]
        bu_im = bu[b * CHUNK:(b + 1) * CHUNK, ns:]
        w_re.append(pn_re * bu_re - pn_im * bu_im)
        w_im.append(pn_re * bu_im + pn_im * bu_re)
    ltri = ltri_ref[...]
    c_re = _dot(ltri, _rows(w_re).astype(BF16))
    c_im = _dot(ltri, _rows(w_im).astype(BF16))
    lam_re, lam_im = lam_ref[0:1, :], lam_ref[1:2, :]
    pp_re, pp_im = pp_re_ref[...], pp_im_ref[...]
    st_re, st_im = [], []
    for b in range(nb):
        s_re, s_im = st_ref[b, 0:1, :], st_ref[b, 1:2, :]
        z_re = c_re[b * CHUNK:(b + 1) * CHUNK] + (lam_re * s_re - lam_im * s_im)
        z_im = c_im[b * CHUNK:(b + 1) * CHUNK] + (lam_re * s_im + lam_im * s_re)
        t_re = pp_re * z_re - pp_im * z_im
        t_im = pp_re * z_im + pp_im * z_re
        st_ref[b, 0:1, :] = t_re[CHUNK - 1:CHUNK, :]
        st_ref[b, 1:2, :] = t_im[CHUNK - 1:CHUNK, :]
        st_re.append(t_re)
        st_im.append(t_im)
    y = (_dot(_rows(st_re).astype(BF16), wc_ref[0:ns, :]) + _dot(_rows(st_im).astype(BF16), wc_ref[ns:2 * ns, :])
         + d_ref[...] * u)
    y = 0.5 * y * (1.0 + jnp.tanh(math.sqrt(2.0 / math.pi) * (y + 0.044715 * (y * y * y))))
    y = y * _sigmoid(_dot(y.astype(BF16), wglu_ref[...]))
    return y * lax.rsqrt(jnp.mean(y * y, axis=-1, keepdims=True) + EPS) * g_ref[...]


def _hgrn2_chunks(xc_ref, gate, convw_ref, lbv_ref, mall_ref, lvl_ref, rsel_ref, bones_ref, avg_ref,
                  g_ref, st_ref):
    w = HG_WIDTH
    nb = st_ref.shape[0]
    rows = nb * CHUNK
    convs = []
    for b in range(nb):
        conv = None
        for i in range(CONV_K):
            term = xc_ref[b, pl.ds(8 - (CONV_K - 1) + i, CHUNK), :] * convw_ref[i:i + 1, :]
            conv = term if conv is None else conv + term
        convs.append(conv)
    conv = _rows(convs)
    cq, cf, v = conv[:, :w], conv[:, w:2 * w], conv[:, 2 * w:]
    q = cq * _sigmoid(cq)
    log_lb, log_1m_lb, one_m_lb = lbv_ref[0:1, :], lbv_ref[1:2, :], lbv_ref[2:3, :]
    log_sig = jnp.minimum(cf, 0.0) - jnp.log(1.0 + jnp.exp(-jnp.abs(cf)))
    b_ = log_1m_lb + log_sig
    logf = jnp.maximum(log_lb, b_) + jnp.log(1.0 + jnp.exp(-jnp.abs(log_lb - b_)))
    kk = one_m_lb * _sigmoid(-cf)
    hi, lo = _split_bf16(logf)

    lvl_sums = _dot(mall_ref[0:HG_LEVELS * rows, :], hi)
    tail = mall_ref[HG_LEVELS * rows:(HG_LEVELS + 2) * rows, :]
    cum_suf = _dot(tail, hi) + _dot(tail, lo)
    g_cum, g_suffix = cum_suf[:rows], cum_suf[rows:]

    lane = lax.broadcasted_iota(jnp.int32, (1, 2 * HG_DK), 1)
    head_sel = [jnp.where(lane < HG_DK, 1.0, 0.0), jnp.where(lane >= HG_DK, 1.0, 0.0)]
    n_pairs = HG_HEADS // 2
    scores = [[None] * n_pairs for _ in range(nb)]
    for lvl in range(HG_LEVELS):
        s = 1 << lvl
        e = jnp.exp(lvl_sums[lvl * rows:(lvl + 1) * rows])
        if s >= 8:
            qk = jnp.concatenate([(q if (r // s) % 2 else kk)[r:r + s] for r in range(0, rows, s)], axis=0)
        else:
            qk = jnp.where(rsel_ref[lvl] > 0.5, q, kk)
        x = qk * e
        m = lvl_ref[lvl]
        for b in range(nb):
            for p in range(n_pairs):
                xp = x[b * CHUNK:(b + 1) * CHUNK, p * 128:(p + 1) * 128]
                rhs = jnp.concatenate([xp * head_sel[0], xp * head_sel[1]], axis=0).astype(BF16)
                sc = _dot_nt(xp.astype(BF16), rhs) * m
                scores[b][p] = sc if scores[b][p] is None else scores[b][p] + sc
    v_bf = v.astype(BF16)
    bones = bones_ref[...]
    qg = (q * jnp.exp(g_cum)).astype(BF16)
    kd = (kk * jnp.exp(g_suffix)).astype(BF16)
    o_rows = []
    for b in range(nb):
        sl = slice(b * CHUNK, (b + 1) * CHUNK)
        o_parts = []
        for p in range(n_pairs):
            vp = v[sl, p * 128:(p + 1) * 128]
            vv = jnp.concatenate([vp * head_sel[0], vp * head_sel[1]], axis=0).astype(BF16)
            o_parts.append(_dot(scores[b][p].astype(BF16), vv))
        st = st_ref[b]
        o_rows.append(jnp.concatenate(o_parts, axis=1) + _dot_nt(qg[sl], st.astype(BF16)))
        upd = _dot_tn(v_bf[sl], kd[sl]) * bones.astype(F32)
        st_ref[b] = st * jnp.exp(g_cum[(b + 1) * CHUNK - 1:(b + 1) * CHUNK, :]) + upd
    o = _rows(o_rows) + _dot((q * kk).astype(BF16), bones) * v
    ms = _dot((o * o).astype(BF16), avg_ref[...])
    return o * lax.rsqrt(ms + EPS) * g_ref[...] * (gate * _sigmoid(gate))


def _ret_chunk(rq, rk, v, cos, sin, intra_ref, inter_ref, tostate_ref, carry_ref, bmask_ref, st_ref):
    hw = RET_KEY_WIDTH // 2

    def rot(t):
        t1, t2 = t[:, :hw], t[:, hw:]
        return jnp.concatenate([t1 * cos - t2 * sin, t1 * sin + t2 * cos], axis=1)

    qr, kr = rot(rq), rot(rk)
    qr_bf, v_bf = qr.astype(BF16), v.astype(BF16)
    lane_k = lax.broadcasted_iota(jnp.int32, (1, RET_KEY_WIDTH), 1) % hw
    lane_v = lax.broadcasted_iota(jnp.int32, (1, 2 * RET_DV), 1)
    v_sel = [jnp.where(lane_v < RET_DV, 1.0, 0.0), jnp.where(lane_v >= RET_DV, 1.0, 0.0)]
    half = RET_DK // 2
    o_parts = []
    for p in range(RET_HEADS // 2):
        sel = [jnp.where((lane_k >= h * half) & (lane_k < (h + 1) * half), 1.0, 0.0)
               for h in (2 * p, 2 * p + 1)]
        rhs = jnp.concatenate([kr * sel[0], kr * sel[1]], axis=0).astype(BF16)
        sc = (_dot_nt(qr_bf, rhs) * intra_ref[p]).astype(BF16)
        vp = v[:, p * 128:(p + 1) * 128]
        vv = jnp.concatenate([vp * v_sel[0], vp * v_sel[1]], axis=0).astype(BF16)
        o_parts.append(_dot(sc, vv))
    o = jnp.concatenate(o_parts, axis=1)
    st = st_ref[...]
    o = o + _dot(qr_bf, st.astype(BF16)) * inter_ref[...]
    kd = (kr * tostate_ref[...]).astype(BF16)
    st_ref[...] = st * carry_ref[...] + _dot_tn(kd, v_bf) * bmask_ref[...]
    return o


def _ret_norm_gate(o, gate, avg_ref, g_ref):
    avg = avg_ref[...]
    outs = []
    for s in range(RET_WIDTH // 256):
        os_ = o[:, s * 256:(s + 1) * 256]
        c = os_ - _dot(os_.astype(BF16), avg)
        outs.append(c * lax.rsqrt(_dot((c * c).astype(BF16), avg) + EPS))
    return jnp.concatenate(outs, axis=1) * g_ref[...] * (gate * _sigmoid(gate))


def _mixer_kernel(proj_ref, cos_ref, sin_ref,
                  wb_ref, wc_ref, pn_re_ref, pn_im_ref, pp_re_ref, pp_im_ref, lam_ref, d_ref,
                  wglu_ref, s5g_ref, ltri_ref,
                  convw_ref, lbv_ref, mall_ref, lvl_ref, rsel_ref, bones_ref, avg_ref, hgg_ref,
                  intra_ref, inter_ref, tostate_ref, carry_ref, bmask_ref, retg_ref,
                  *rest, n_cast, cast_steps):
    cast_in, o_ref, cast_out = rest[:n_cast], rest[n_cast], rest[n_cast + 1:2 * n_cast + 1]
    s5_st, hg_xc, hg_st, ret_st = rest[2 * n_cast + 1:]

    @pl.when(pl.program_id(0) == 0)
    def _():
        s5_st[...] = jnp.zeros_like(s5_st)
        hg_xc[...] = jnp.zeros_like(hg_xc)
        hg_st[...] = jnp.zeros_like(hg_st)
        ret_st[...] = jnp.zeros_like(ret_st)

    if n_cast:
        @pl.when(pl.program_id(0) < cast_steps)
        def _():
            for src, dst in zip(cast_in, cast_out):
                dst[...] = src[...].astype(dst.dtype)

    nb = proj_ref.shape[0]

    def cols(c0, width):
        return _rows([proj_ref[b, :, c0:c0 + width] for b in range(nb)])

    def emit(c0, y):
        for b in range(nb):
            o_ref[b, :, c0:c0 + y.shape[1]] = y[b * CHUNK:(b + 1) * CHUNK].astype(o_ref.dtype)

    y_a = _s5_chunks(cols(C_U, S5_WIDTH), wb_ref, wc_ref, pn_re_ref, pn_im_ref, pp_re_ref, pp_im_ref,
                     lam_ref, d_ref, wglu_ref, s5g_ref, ltri_ref, s5_st)
    emit(0, y_a)

    for b in range(nb):
        hg_xc[b, 8:8 + CHUNK, :] = proj_ref[b, :, C_HQ:C_HQ + HG_CONV_W]
    y_b = _hgrn2_chunks(hg_xc, cols(C_HGATE, HG_WIDTH), convw_ref, lbv_ref,
                        mall_ref, lvl_ref, rsel_ref, bones_ref, avg_ref, hgg_ref, hg_st)
    for b in range(nb):
        hg_xc[b, 0:8, :] = hg_xc[b, CHUNK:CHUNK + 8, :]
    emit(S5_WIDTH, y_b)

    o_c = [_ret_chunk(proj_ref[b, :, C_RQ:C_RQ + RET_KEY_WIDTH], proj_ref[b, :, C_RK:C_RK + RET_KEY_WIDTH],
                      proj_ref[b, :, C_RV:C_RV + RET_WIDTH], cos_ref[...], sin_ref[...],
                      intra_ref, inter_ref, tostate_ref, carry_ref, bmask_ref, ret_st.at[b])
           for b in range(nb)]
    y_c = _ret_norm_gate(_rows(o_c), cols(C_RGATE, RET_WIDTH), avg_ref, retg_ref)
    emit(S5_WIDTH + HG_WIDTH, y_c)


def _const_spec(a):
    nd = a.ndim
    return pl.BlockSpec(a.shape, lambda c, _nd=nd: (0,) * _nd)


def _cast_steps(n_chunks, arrays):
    for steps in range(n_chunks, 0, -1):
        if all(a.shape[0] % (steps * 16) == 0 for a in arrays):
            return steps
    raise ValueError("no slab split")


def _mixers(proj, cos, sin, consts, bsz, n_chunks, to_bf16=()):
    total = proj.shape[1]
    n_cast = len(to_bf16)
    cast_steps = _cast_steps(n_chunks, to_bf16) if n_cast else 0
    slab = lambda c: (jnp.minimum(c, cast_steps - 1), 0)
    cast_specs = [pl.BlockSpec((a.shape[0] // cast_steps, a.shape[1]), slab) for a in to_bf16]
    in_specs = [
        pl.BlockSpec((bsz, CHUNK, IN_COLS), lambda c: (0, c, 0)),
        pl.BlockSpec((CHUNK, 128), lambda c: (c, 0)),
        pl.BlockSpec((CHUNK, 128), lambda c: (c, 0)),
    ] + [_const_spec(a) for a in consts] + cast_specs
    outs = pl.pallas_call(
        functools.partial(_mixer_kernel, n_cast=n_cast, cast_steps=cast_steps),
        grid=(n_chunks,),
        in_specs=in_specs,
        out_specs=[pl.BlockSpec((bsz, CHUNK, D_MIX), lambda c: (0, c, 0))] + cast_specs,
        out_shape=[jax.ShapeDtypeStruct((bsz, total, D_MIX), BF16)]
        + [jax.ShapeDtypeStruct(a.shape, BF16) for a in to_bf16],
        scratch_shapes=[
            pltpu.VMEM((bsz, 2, S5_NSTATE), F32),
            pltpu.VMEM((bsz, CHUNK + 8, HG_CONV_W), F32),
            pltpu.VMEM((bsz, HG_WIDTH, HG_WIDTH), F32),
            pltpu.VMEM((bsz, RET_KEY_WIDTH, RET_WIDTH), F32),
        ],
        compiler_params=pltpu.CompilerParams(
            dimension_semantics=("arbitrary",), vmem_limit_bytes=VMEM_LIMIT),
        name="mixers",
    )(proj, cos, sin, *consts, *to_bf16)
    return outs[0], list(outs[1:])


def _route(hn, router_ref, ltri_ref, count_ref):
    r = router_ref[...]
    r_hi, r_lo = _split_bf16(r)
    h_hi, h_lo = _split_bf16(hn)
    logits = _dot(h_hi, r_hi) + (_dot(h_lo, r_hi) + _dot(h_hi, r_lo))
    lane_i = lax.broadcasted_iota(jnp.int32, logits.shape, 1)
    lane = lane_i.astype(F32)
    neg = jnp.float32(-jnp.inf)
    logits = jnp.where(lane_i < N_EXPERTS, logits, neg)
    m1 = jnp.max(logits, axis=-1, keepdims=True)
    i1 = jnp.min(jnp.where(logits == m1, lane, float(ROUTE_LANES)), axis=-1, keepdims=True)
    rest = jnp.where(lane == i1, neg, logits)
    m2 = jnp.max(rest, axis=-1, keepdims=True)
    i2 = jnp.min(jnp.where(rest == m2, lane, float(ROUTE_LANES)), axis=-1, keepdims=True)
    e2 = jnp.exp(m2 - m1)
    g1 = 1.0 / (1.0 + e2)
    g2 = e2 / (1.0 + e2)
    onehot = jnp.where((lane == i1) | (lane == i2), 1.0, 0.0)
    before = _dot(ltri_ref[...], onehot.astype(BF16)) + count_ref[...]
    r1 = jnp.sum(jnp.where(lane == i1, before, 0.0), axis=-1, keepdims=True)
    r2 = jnp.sum(jnp.where(lane == i2, before, 0.0), axis=-1, keepdims=True)
    count_ref[...] += jnp.sum(onehot, axis=0, keepdims=True)
    out = jnp.where(lane == 0, i1, 0.0)
    out = jnp.where(lane == 1, i2, out)
    out = jnp.where(lane == 2, g1, out)
    out = jnp.where(lane == 3, g2, out)
    out = jnp.where(lane == 4, r1, out)
    return jnp.where(lane == 5, r2, out)


def _out_proj_kernel(mixed_ref, *refs, routed, n_stream, tiles_per_seq):
    stream, (w_ref, g_ref, *rest) = refs[:n_stream], refs[n_stream:]
    if routed:
        router_ref, ltri_ref, h1_ref, hn_ref, route_ref, fields_ref, count_ref = rest

        @pl.when(pl.program_id(0) == 0)
        def _():
            count_ref[...] = jnp.zeros_like(count_ref)
    else:
        h1_ref, hn_ref = rest
    h1 = _stream_tile(stream, tiles_per_seq) + _dot(mixed_ref[...], w_ref[...])
    h1_ref[...] = h1
    hn = h1 * lax.rsqrt(jnp.mean(h1 * h1, axis=-1, keepdims=True) + EPS) * g_ref[...]
    if routed:
        tm = hn.shape[0]
        for s in range(ROW_TILE):
            hn_ref[pl.ds(s, tm, stride=ROW_TILE), :] = hn[:, s * 128:(s + 1) * 128]
        slab = _route(hn, router_ref, ltri_ref, count_ref)
        route_ref[...] = slab
        fields_ref[...] = slab.T[0:ROUTE_FIELDS, :]
    else:
        hn_ref[...] = hn.astype(hn_ref.dtype)


def _out_proj(mixed, stream, w_bf16, g, bsz, total, router_pad=None):
    n, d = bsz * total, stream[0].shape[1]
    tm = _pick_tile(total, (TOKEN_TILE, 128))
    routed = router_pad is not None
    row = lambda i: (i, 0)
    fixed = lambda i: (0, 0)
    stream_specs, stream_operands = _stream_specs(stream, tm, total)
    in_specs = [pl.BlockSpec((tm, D_MIX), row)] + stream_specs + [
        pl.BlockSpec(w_bf16.shape, fixed), pl.BlockSpec((1, d), fixed)]
    out_specs = [pl.BlockSpec((tm, d), row), pl.BlockSpec((tm, d), row)]
    out_shape = [jax.ShapeDtypeStruct((n, d), F32), jax.ShapeDtypeStruct((n, d), BF16)]
    args = [mixed] + stream_operands + [w_bf16, g]
    if routed:
        assert d == ROW_TILE * 128
        out_specs[1] = pl.BlockSpec((tm * ROW_TILE, 128), row)
        out_shape[1] = jax.ShapeDtypeStruct((n * ROW_TILE, 128), F32)
        ltri = jnp.asarray(np.tril(np.ones((tm, tm), np.float32), -1), BF16)
        in_specs += [pl.BlockSpec(router_pad.shape, fixed), pl.BlockSpec((tm, tm), fixed)]
        out_specs += [pl.BlockSpec((tm, ROUTE_LANES), row), pl.BlockSpec((ROUTE_FIELDS, tm), lambda i: (0, i)),
                      pl.BlockSpec((1, ROUTE_LANES), fixed)]
        out_shape += [jax.ShapeDtypeStruct((n, ROUTE_LANES), F32),
                      jax.ShapeDtypeStruct((ROUTE_FIELDS, n), F32),
                      jax.ShapeDtypeStruct((1, ROUTE_LANES), F32)]
        args += [router_pad, ltri]
    return pl.pallas_call(
        functools.partial(_out_proj_kernel, routed=routed, n_stream=len(stream), tiles_per_seq=total // tm),
        grid=(n // tm,),
        in_specs=in_specs,
        out_specs=out_specs,
        out_shape=out_shape,
        compiler_params=pltpu.CompilerParams(
            dimension_semantics=("arbitrary",), vmem_limit_bytes=VMEM_LIMIT),
        name="out_proj_routed" if routed else "out_proj",
    )(*args)


def _ffn_kernel(hn_ref, h1_ref, w1_ref, w3_ref, w2_ref, o_ref, *, tf):
    x = hn_ref[...]
    acc = h1_ref[...]
    for c0 in range(0, w1_ref.shape[1], tf):
        a = _dot(x, w1_ref[:, c0:c0 + tf])
        b = _dot(x, w3_ref[:, c0:c0 + tf])
        acc = acc + _dot((a * _sigmoid(a) * b).astype(BF16), w2_ref[c0:c0 + tf, :])
    o_ref[...] = acc


def _ffn(hn, h1, w1, w3, w2):
    n, d = h1.shape
    dff = w1.shape[1]
    tm, tf = _pick_tile(n, (640, 128)), _pick_tile(dff, (1408, 128))
    resident = lambda a: pl.BlockSpec(a.shape, lambda i: (0, 0), pipeline_mode=pl.Buffered(1))
    return pl.pallas_call(
        functools.partial(_ffn_kernel, tf=tf),
        grid=(n // tm,),
        in_specs=[
            pl.BlockSpec((tm, d), lambda i: (i, 0)),
            pl.BlockSpec((tm, d), lambda i: (i, 0)),
            resident(w1), resident(w3), resident(w2),
        ],
        out_specs=pl.BlockSpec((tm, d), lambda i: (i, 0)),
        out_shape=jax.ShapeDtypeStruct((n, d), F32),
        compiler_params=pltpu.CompilerParams(
            dimension_semantics=("arbitrary",), vmem_limit_bytes=VMEM_LIMIT),
        name="ffn_dense",
    )(hn, h1, w1, w3, w2)


def _moe_kernel(te_ref, nv_ref, src_ref, tok_ref, hn_hbm, w1_ref, w3_ref, w2_ref, o_ref,
                xs_ref, xb_ref, acc_ref, sem, *, tm):
    i, f = pl.program_id(0), pl.program_id(1)
    nf = pl.num_programs(1)
    n_valid = nv_ref[0]
    share = tm // MOE_F_STEPS

    def row_copy(tile, r):
        tok = tok_ref[src_ref[tile] + r]
        return pltpu.make_async_copy(hn_hbm.at[pl.ds(pl.multiple_of(tok * ROW_TILE, ROW_TILE), ROW_TILE)],
                                     xs_ref.at[pl.ds(pl.multiple_of(r * ROW_TILE, ROW_TILE), ROW_TILE)], sem)

    @pl.when((i == 0) & (f == 0))
    def _():
        def body(r, carry):
            row_copy(0, r).start()
            return carry
        lax.fori_loop(0, tm, body, 0)

    @pl.when((f == 0) & (i <= n_valid))
    def _():
        pltpu.make_async_copy(xs_ref, xs_ref, sem).wait()
        for s in range(ROW_TILE):
            xb_ref[:, s * 128:(s + 1) * 128] = xs_ref[pl.ds(s, tm, stride=ROW_TILE), :].astype(BF16)

    @pl.when(i < n_valid)
    def _():
        for r in range(share):
            row_copy(i + 1, f * share + r).start(priority=r % MOE_DMA_QUEUES)
        x = xb_ref[...]
        a = _dot(x, w1_ref[0])
        b = _dot(x, w3_ref[0])
        part = _dot((a * _sigmoid(a) * b).astype(BF16), w2_ref[0])

        @pl.when(f == 0)
        def _():
            acc_ref[...] = part

        @pl.when(f > 0)
        def _():
            acc_ref[...] += part

    @pl.when(f == nf - 1)
    def _():
        o_ref[...] = jnp.where(i < n_valid, acc_ref[...], 0.0).astype(o_ref.dtype)


def _live_f(i, f, nv):
    return jnp.where(i < nv[0], f, 0)


def _moe_grouped(tile_expert, n_valid, tile_src, sorted_tok, hn, w1, w3, w2, tm):
    d = ROW_TILE * hn.shape[1]
    n_tiles = tile_expert.shape[0]
    dff = w1.shape[2]
    tf = dff // MOE_F_STEPS
    grid_spec = pltpu.PrefetchScalarGridSpec(
        num_scalar_prefetch=4,
        grid=(n_tiles, MOE_F_STEPS),
        in_specs=[
            pl.BlockSpec(memory_space=pl.ANY),
            pl.BlockSpec((1, d, tf), lambda i, f, te, nv, *_: (te[i], 0, _live_f(i, f, nv))),
            pl.BlockSpec((1, d, tf), lambda i, f, te, nv, *_: (te[i], 0, _live_f(i, f, nv))),
            pl.BlockSpec((1, tf, d), lambda i, f, te, nv, *_: (te[i], _live_f(i, f, nv), 0)),
        ],
        out_specs=pl.BlockSpec((tm, d), lambda i, f, *_: (i, 0)),
        scratch_shapes=[pltpu.VMEM((tm * ROW_TILE, 128), F32), pltpu.VMEM((tm, d), BF16),
                        pltpu.VMEM((tm, d), F32), pltpu.SemaphoreType.DMA(())],
    )
    return pl.pallas_call(
        functools.partial(_moe_kernel, tm=tm),
        grid_spec=grid_spec,
        out_shape=jax.ShapeDtypeStruct((n_tiles * tm, d), BF16),
        compiler_params=pltpu.CompilerParams(
            dimension_semantics=("arbitrary", "arbitrary"), vmem_limit_bytes=VMEM_LIMIT),
        name="moe_grouped",
    )(tile_expert, n_valid, tile_src, sorted_tok, hn, w1, w3, w2)


def _moe_dispatch(fields, counts_row, tm):
    n = fields.shape[1]
    n_slots = n * TOP_K
    n_tiles = (n_slots + N_EXPERTS * tm + tm - 1) // tm
    counts = counts_row[0, :N_EXPERTS].astype(jnp.int32)
    padded = ((counts + tm - 1) // tm) * tm
    pend = jnp.cumsum(padded)
    pstart = pend - padded
    gstart = jnp.cumsum(counts) - counts
    eid = fields[0:TOP_K].astype(jnp.int32)
    rank = fields[2 * TOP_K:3 * TOP_K].astype(jnp.int32)
    is_e = [eid == e for e in range(N_EXPERTS)]
    lookup = lambda table: sum(jnp.where(m, table[e], 0) for e, m in enumerate(is_e))
    pos = lookup(pstart) + rank
    tok = jnp.broadcast_to(jnp.arange(n, dtype=jnp.int32)[None, :], (TOP_K, n))
    _, sorted_tok = lax.sort(((lookup(gstart) + rank).reshape(-1), tok.reshape(-1)), num_keys=1)
    sorted_tok = jnp.concatenate([sorted_tok, jnp.zeros((tm,), jnp.int32)])
    tile_start = jnp.arange(n_tiles, dtype=jnp.int32) * tm
    tile_expert = jnp.minimum(jnp.sum((tile_start[:, None] >= pend[None, :]).astype(jnp.int32), axis=1),
                              N_EXPERTS - 1)
    tile_src = jnp.clip(tile_start - (pstart - gstart)[tile_expert], 0, n_slots)
    n_valid = (pend[-1] // tm).astype(jnp.int32).reshape(1)
    return sorted_tok, tile_src, pos, tile_expert, n_valid


def _final_kernel(h1_ref, y0_ref, y1_ref, route_ref, g_ref, o_ref):
    g0 = route_ref[:, TOP_K:TOP_K + 1]
    g1 = route_ref[:, TOP_K + 1:TOP_K + 2]
    h = h1_ref[...] + (g0 * y0_ref[...].astype(F32) + g1 * y1_ref[...].astype(F32))
    o_ref[...] = h * lax.rsqrt(jnp.mean(h * h, axis=-1, keepdims=True) + EPS) * g_ref[...]


def _final(h1, y0, y1, route, g, bsz, total):
    d = h1.shape[1]
    seq = total - CHUNK
    tm = _pick_tile(seq, (1024, 128))
    per_seq = seq // tm
    src = lambda b, k: (pl.multiple_of(b * total + CHUNK + k * tm, CHUNK), 0)
    rows = lambda width: pl.BlockSpec((pl.Element(tm), pl.Element(width)), src)
    out = pl.pallas_call(
        _final_kernel,
        grid=(bsz, per_seq),
        in_specs=[rows(d), rows(d), rows(d), rows(ROUTE_LANES), pl.BlockSpec((1, d), lambda b, k: (0, 0))],
        out_specs=pl.BlockSpec((tm, d), lambda b, k: (b * per_seq + k, 0)),
        out_shape=jax.ShapeDtypeStruct((bsz * seq, d), F32),
        compiler_params=pltpu.CompilerParams(
            dimension_semantics=("arbitrary", "arbitrary"), vmem_limit_bytes=VMEM_LIMIT),
        name="final_norm",
    )(h1, y0, y1, route, g)
    return out.reshape(bsz, seq, d)


def _relayout_in_cols(w):
    d = w.shape[0]

    def halves_first(block):
        return block.reshape(d, RET_HEADS, 2, RET_DK // 2).transpose(0, 2, 1, 3).reshape(d, RET_KEY_WIDTH)

    return jnp.concatenate([w[:, :C_RQ], halves_first(w[:, C_RQ:C_RK]), halves_first(w[:, C_RK:C_RV]),
                            w[:, C_RV:]], axis=1)


def _hg_tables(nb):
    c = CHUNK
    i = np.arange(c)[:, None]
    t = np.arange(c)[None, :]
    blocks, masks, rsel = [], [], []
    for lvl in range(HG_LEVELS):
        s = 1 << lvl
        r = (i // (2 * s)) * (2 * s) + s - 1
        right = i > r
        m = np.where(right, (t > r) & (t <= i), (t > i) & (t <= r))
        blocks.append(m)
        j = t
        same = (i // (2 * s)) == (j // (2 * s))
        mk = same & right & (j <= r)
        masks.append(np.concatenate([mk, mk], axis=1))
        if s < 8:
            rsel.append(np.broadcast_to(right, (c, HG_WIDTH)))
    blocks.append(t <= i)
    blocks.append(t > i)
    eye = np.eye(nb, dtype=np.float32)
    mall = np.concatenate([np.kron(eye, blk.astype(np.float32)) for blk in blocks], axis=0)
    lvl = np.stack(masks, axis=0).astype(np.float32)
    ch = np.arange(HG_WIDTH)
    bones = (ch[:, None] // HG_DK == ch[None, :] // HG_DK).astype(np.float32)
    rsel = np.stack([np.tile(r, (nb, 1)) for r in rsel], axis=0).astype(np.float32)
    return mall, lvl, rsel, bones


def _ret_tables():
    f32 = jnp.float32
    log_gamma = jnp.log1p(-jnp.power(2.0, -5.0 - jnp.arange(RET_HEADS, dtype=f32)))
    n = jnp.arange(CHUNK, dtype=f32)
    lg = log_gamma[:, None]
    causal = jnp.tril(jnp.ones((CHUNK, CHUNK), dtype=bool))
    intra = jnp.exp(jnp.where(causal[None], (n[:, None] - n[None, :])[None] * lg[:, :, None], -jnp.inf))
    scale = RET_DK ** -0.5
    inter = jnp.exp((n[None, :] + 1.0) * lg)
    to_state = jnp.exp((CHUNK - 1.0 - n[None, :]) * lg)
    carry = jnp.exp(CHUNK * lg)[:, 0]
    head_of_v = np.arange(RET_WIDTH) // RET_DV
    head_of_k = (np.arange(RET_KEY_WIDTH) % (RET_KEY_WIDTH // 2)) // (RET_DK // 2)
    inter_t = (inter * scale).T[:, head_of_v]
    to_state_t = to_state.T[:, head_of_k]
    carry_row = carry[head_of_v][None, :]
    bmask = jnp.asarray((head_of_k[:, None] == head_of_v[None, :]).astype(np.float32))
    intra_pairs = (intra * scale).reshape(RET_HEADS // 2, 2, CHUNK, CHUNK)
    intra_pairs = jnp.concatenate([intra_pairs[:, 0], intra_pairs[:, 1]], axis=2)
    return intra_pairs, inter_t, to_state_t, carry_row, bmask


def _s5_tables(lam_re, lam_im, b_re, b_im, c_re, c_im, d_skip, log_step):
    f32 = jnp.float32
    lam = lax.complex(lam_re.astype(f32), lam_im.astype(f32))
    step = jnp.exp(log_step.astype(f32))[:, None]
    lam_dt = lam * step
    lam_bar = jnp.exp(lam_dt)
    b_bar = ((lam_bar - 1.0) / lam)[..., None] * lax.complex(b_re.astype(f32), b_im.astype(f32))
    eye = jnp.eye(S5_NGROUPS, dtype=f32)
    wb_re = jnp.einsum('gph,gk->ghkp', jnp.real(b_bar), eye).reshape(S5_WIDTH, S5_NSTATE)
    wb_im = jnp.einsum('gph,gk->ghkp', jnp.imag(b_bar), eye).reshape(S5_WIDTH, S5_NSTATE)
    wb = jnp.concatenate([wb_re, wb_im], axis=1)
    wc_re = jnp.einsum('ghp,gk->gpkh', c_re.astype(f32), eye).reshape(S5_NSTATE, S5_WIDTH)
    wc_im = jnp.einsum('ghp,gk->gpkh', c_im.astype(f32), eye).reshape(S5_NSTATE, S5_WIDTH)
    wc = jnp.concatenate([wc_re, -wc_im], axis=0)
    t = jnp.arange(CHUNK, dtype=f32)[:, None, None]
    pp = jnp.exp(lam_dt[None] * t).reshape(CHUNK, S5_NSTATE)
    pn = jnp.exp(-lam_dt[None] * t).reshape(CHUNK, S5_NSTATE)
    lam_rows = jnp.stack([jnp.real(lam_bar).reshape(-1), jnp.imag(lam_bar).reshape(-1)], axis=0)
    return (wb.astype(BF16), wc.astype(BF16), jnp.real(pn), jnp.imag(pn), jnp.real(pp), jnp.imag(pp),
            lam_rows, d_skip.astype(f32).reshape(1, S5_WIDTH))


def kernel(x, meta_tokens, norm_mix_g, w_in, s5_lam_re, s5_lam_im, s5_b_re, s5_b_im, s5_c_re, s5_c_im, s5_d, s5_log_step, s5_w_glu, s5_out_g, hg_conv_w, hg_lb_param, hg_out_g, ret_out_g, w_out, norm_ffn_g, ffn_w1, ffn_w3, ffn_w2, moe_router, moe_w1, moe_w3, moe_w2, final_norm_g):
    bsz, seq_len, d = x.shape
    depth = w_in.shape[0]
    total = seq_len + CHUNK
    n_chunks = total // CHUNK
    n = bsz * total

    meta = jnp.broadcast_to(meta_tokens.astype(F32)[None], (bsz, N_META, d))
    tm0 = _pick_tile(total, (TOKEN_TILE, 128))
    head = jnp.concatenate([jnp.zeros((bsz, PAD, d), F32), meta, x[:, :tm0 - CHUNK].astype(F32)], axis=1)
    stream = (x.astype(F32).reshape(bsz * seq_len, d), head)

    pos = (jnp.arange(total) - PAD).astype(F32)
    half = RET_DK // 2
    inv_freq = ROPE_BASE ** (-jnp.arange(half, dtype=F32) / half)
    ang = pos[:, None] * inv_freq[None, :]
    cos_t = jnp.tile(jnp.cos(ang), (1, RET_HEADS))
    sin_t = jnp.tile(jnp.sin(ang), (1, RET_HEADS))

    lb_all = jnp.cumsum(jax.nn.softmax(hg_lb_param.astype(F32), axis=0), axis=0)
    lb_all = lb_all - lb_all[0]

    mall_np, lvl_np, rsel_np, bones_np = _hg_tables(bsz)
    mall = jnp.asarray(mall_np, BF16)
    lvl = jnp.asarray(lvl_np, F32)
    rsel = jnp.asarray(rsel_np, F32)
    bones = jnp.asarray(bones_np, BF16)
    avg = jnp.asarray(bones_np / HG_DK, BF16)
    ltri = jnp.asarray(np.kron(np.eye(bsz, dtype=np.float32),
                               np.tril(np.ones((CHUNK, CHUNK), np.float32))), BF16)
    intra, inter_t, to_state_t, carry_row, bmask = _ret_tables()

    out = None
    for l in range(depth):
        w_in_l = _relayout_in_cols(w_in[l].astype(BF16))
        proj = _in_proj(stream, norm_mix_g[l].astype(F32).reshape(1, d), w_in_l, bsz, total)

        s5c = _s5_tables(s5_lam_re[l], s5_lam_im[l], s5_b_re[l], s5_b_im[l], s5_c_re[l], s5_c_im[l],
                         s5_d[l], s5_log_step[l])
        lb = lb_all[l][None, :]
        lbv = jnp.concatenate([jnp.log(lb), jnp.log1p(-lb), 1.0 - lb], axis=0)
        consts = list(s5c) + [
            s5_w_glu[l].astype(BF16), s5_out_g[l].astype(F32).reshape(1, -1), ltri,
            hg_conv_w[l].astype(F32), lbv, mall, lvl, rsel, bones, avg, hg_out_g[l].astype(F32).reshape(1, -1),
            intra, inter_t, to_state_t, carry_row, bmask, ret_out_g[l].astype(F32).reshape(1, -1),
        ]
        j = l // 2
        if l % 2 == 0:
            ffn_f32 = [ffn_w1[j], ffn_w3[j], ffn_w2[j]]
        else:
            dff_e = moe_w1.shape[-1]
            ffn_f32 = [moe_w1[j].reshape(N_EXPERTS * d, dff_e), moe_w3[j].reshape(N_EXPERTS * d, dff_e),
                       moe_w2[j].reshape(N_EXPERTS * dff_e, d)]
        mixed, ffn_bf16 = _mixers(proj.reshape(bsz, total, IN_COLS), cos_t, sin_t, consts, bsz, n_chunks,
                                  to_bf16=[a.astype(F32) for a in ffn_f32])
        mixed = mixed.reshape(n, D_MIX)

        g_ffn = norm_ffn_g[l].astype(F32).reshape(1, d)
        w_out_l = w_out[l].astype(BF16)
        if l % 2 == 0:
            h1, hn = _out_proj(mixed, stream, w_out_l, g_ffn, bsz, total)
            stream = (_ffn(hn, h1, *ffn_bf16),)
            y0 = y1 = None
        else:
            router_pad = jnp.zeros((d, ROUTE_LANES), F32).at[:, :N_EXPERTS].set(moe_router[j].astype(F32))
            h1, hn, route, fields, counts_row = _out_proj(mixed, stream, w_out_l, g_ffn, bsz, total, router_pad)
            tm = MOE_TILE
            sorted_tok, tile_src, pos_of_slot, tile_expert, n_valid = _moe_dispatch(fields, counts_row, tm)
            e_w1, e_w3, e_w2 = (ffn_bf16[0].reshape(N_EXPERTS, d, dff_e), ffn_bf16[1].reshape(N_EXPERTS, d, dff_e),
                                ffn_bf16[2].reshape(N_EXPERTS, dff_e, d))
            ys = _moe_grouped(tile_expert, n_valid, tile_src, sorted_tok, hn, e_w1, e_w3, e_w2, tm)
            y0 = ys.at[pos_of_slot[0]].get(mode='promise_in_bounds')
            y1 = ys.at[pos_of_slot[1]].get(mode='promise_in_bounds')
            if l < depth - 1:
                stream = (h1 + (route[:, TOP_K:TOP_K + 1] * y0.astype(F32)
                                + route[:, TOP_K + 1:TOP_K + 2] * y1.astype(F32)),)

        if l == depth - 1:
            if y0 is None:
                y0 = y1 = jnp.zeros((n, d), BF16)
                h1 = stream[0]
                route = jnp.zeros((n, ROUTE_LANES), F32)
            out = _final(h1, y0, y1, route, final_norm_g.astype(F32).reshape(1, d), bsz, total)

    return out.astype(x.dtype)
```

```python
import functools
import math

import jax
import jax.numpy as jnp
import numpy as np
from jax import lax
from jax.experimental import pallas as pl
from jax.experimental.pallas import tpu as pltpu

F32 = jnp.float32
BF16 = jnp.bfloat16

CHUNK = 128
N_META = 16
PAD = CHUNK - N_META
EPS = 1e-6

S5_WIDTH = 256
S5_GROUP = 16
S5_NGROUPS = 16
S5_STATE = 64
S5_NSTATE = S5_NGROUPS * S5_STATE

HG_HEADS = 4
HG_DK = 64
HG_WIDTH = 256
CONV_K = 4
HG_CONV_W = 3 * HG_WIDTH
HG_LEVELS = 7

RET_HEADS = 8
RET_DK = 32
RET_DV = 64
RET_KEY_WIDTH = 256
RET_WIDTH = 512
ROPE_BASE = 10000.0

D_MIX = 1024
IN_COLS = 2816
C_U, C_HQ, C_HGATE, C_RQ, C_RK, C_RV, C_RGATE = 0, 256, 1024, 1280, 1536, 1792, 2304

N_EXPERTS = 8
TOP_K = 2
ROUTE_LANES = 128
ROUTE_FIELDS = 8
MOE_F_STEPS = 2
MOE_TILE = 512
ROW_TILE = 8
MOE_DMA_QUEUES = 2

VMEM_LIMIT = 56 * 1024 * 1024


def _sigmoid(x):
    return 1.0 / (1.0 + jnp.exp(-x))


def _split_bf16(x):
    hi = x.astype(BF16)
    lo = (x - hi.astype(F32)).astype(BF16)
    return hi, lo


def _pick_tile(n, candidates):
    for t in candidates:
        if n % t == 0:
            return t
    raise ValueError(f"no tile in {candidates} divides {n}")


def _dot(a, b):
    return jnp.dot(a, b, preferred_element_type=F32)


def _dot_nt(a, b):
    return lax.dot_general(a, b, (((1,), (1,)), ((), ())), preferred_element_type=F32)


def _dot_tn(a, b):
    return lax.dot_general(a, b, (((0,), (0,)), ((), ())), preferred_element_type=F32)


TOKEN_TILE = 640


def _stream_specs(stream, tm, total):
    d = stream[0].shape[1]
    if len(stream) == 1:
        return [pl.BlockSpec((tm, d), lambda i: (i, 0))], list(stream)
    per, seq = total // tm, total - CHUNK
    x_rows = lambda i: (pl.multiple_of((i // per) * seq + jnp.maximum((i % per) * tm - CHUNK, 0), CHUNK), 0)
    return ([pl.BlockSpec((pl.Element(tm), pl.Element(d)), x_rows),
             pl.BlockSpec((1, tm, d), lambda i: (i // per, 0, 0))], list(stream))


def _stream_tile(refs, tiles_per_seq):
    if len(refs) == 1:
        return refs[0][...]
    x_ref, head_ref = refs
    return jnp.where(pl.program_id(0) % tiles_per_seq == 0, head_ref[0], x_ref[...])


def _in_proj_kernel(*refs, tm, tiles_per_seq):
    *stream, g_ref, w_ref, o_ref = refs
    x = _stream_tile(stream, tiles_per_seq)
    y = x * lax.rsqrt(jnp.mean(x * x, axis=-1, keepdims=True) + EPS) * g_ref[...]
    proj = _dot(y.astype(BF16), w_ref[...])
    row0 = (pl.program_id(0) % tiles_per_seq) * tm
    rows = row0 + lax.broadcasted_iota(jnp.int32, (tm, 1), 0)
    o_ref[...] = jnp.where(rows >= PAD, proj, 0.0)


def _in_proj(stream, g, w_bf16, bsz, total):
    n, d = bsz * total, stream[0].shape[1]
    tm = _pick_tile(total, (TOKEN_TILE, 128))
    specs, operands = _stream_specs(stream, tm, total)
    return pl.pallas_call(
        functools.partial(_in_proj_kernel, tm=tm, tiles_per_seq=total // tm),
        grid=(n // tm,),
        in_specs=specs + [
            pl.BlockSpec((1, d), lambda i: (0, 0)),
            pl.BlockSpec(w_bf16.shape, lambda i: (0, 0), pipeline_mode=pl.Buffered(1)),
        ],
        out_specs=pl.BlockSpec((tm, IN_COLS), lambda i: (i, 0)),
        out_shape=jax.ShapeDtypeStruct((n, IN_COLS), F32),
        compiler_params=pltpu.CompilerParams(
            dimension_semantics=("arbitrary",), vmem_limit_bytes=VMEM_LIMIT),
        name="in_proj",
    )(*operands, g, w_bf16)


def _rows(parts):
    return parts[0] if len(parts) == 1 else jnp.concatenate(parts, axis=0)


def _s5_chunks(u, wb_ref, wc_ref, pn_re_ref, pn_im_ref, pp_re_ref, pp_im_ref, lam_ref,
               d_ref, wglu_ref, g_ref, ltri_ref, st_ref):
    ns = S5_NSTATE
    nb = st_ref.shape[0]
    bu = _dot(u.astype(BF16), wb_ref[...])
    pn_re, pn_im = pn_re_ref[...], pn_im_ref[...]
    w_re, w_im = [], []
    for b in range(nb):
        bu_re = bu[b * CHUNK:(b + 1) * CHUNK, :ns]
        bu_im = bu[b * CHUNK:(b + 1) * CHUNK, ns:]
        w_re.append(pn_re * bu_re - pn_im * bu_im)
        w_im.append(pn_re * bu_im + pn_im * bu_re)
    ltri = ltri_ref[...]
    c_re = _dot(ltri, _rows(w_re).astype(BF16))
    c_im = _dot(ltri, _rows(w_im).astype(BF16))
    lam_re, lam_im = lam_ref[0:1, :], lam_ref[1:2, :]
    pp_re, pp_im = pp_re_ref[...], pp_im_ref[...]
    st_re, st_im = [], []
    for b in range(nb):
        s_re, s_im = st_ref[b, 0:1, :], st_ref[b, 1:2, :]
        z_re = c_re[b * CHUNK:(b + 1) * CHUNK] + (lam_re * s_re - lam_im * s_im)
        z_im = c_im[b * CHUNK:(b + 1) * CHUNK] + (lam_re * s_im + lam_im * s_re)
        t_re = pp_re * z_re - pp_im * z_im
        t_im = pp_re * z_im + pp_im * z_re
        st_ref[b, 0:1, :] = t_re[CHUNK - 1:CHUNK, :]
        st_ref[b, 1:2, :] = t_im[CHUNK - 1:CHUNK, :]
        st_re.append(t_re)
        st_im.append(t_im)
    y = (_dot(_rows(st_re).astype(BF16), wc_ref[0:ns, :]) + _dot(_rows(st_im).astype(BF16), wc_ref[ns:2 * ns, :])
         + d_ref[...] * u)
    y = 0.5 * y * (1.0 + jnp.tanh(math.sqrt(2.0 / math.pi) * (y + 0.044715 * (y * y * y))))
    y = y * _sigmoid(_dot(y.astype(BF16), wglu_ref[...]))
    return y * lax.rsqrt(jnp.mean(y * y, axis=-1, keepdims=True) + EPS) * g_ref[...]


def _hgrn2_chunks(xc_ref, gate, convw_ref, lbv_ref, mall_ref, lvl_ref, rsel_ref, bones_ref, avg_ref,
                  g_ref, st_ref):
    w = HG_WIDTH
    nb = st_ref.shape[0]
    rows = nb * CHUNK
    convs = []
    for b in range(nb):
        conv = None
        for i in range(CONV_K):
            term = xc_ref[b, pl.ds(8 - (CONV_K - 1) + i, CHUNK), :] * convw_ref[i:i + 1, :]
            conv = term if conv is None else conv + term
        convs.append(conv)
    conv = _rows(convs)
    cq, cf, v = conv[:, :w], conv[:, w:2 * w], conv[:, 2 * w:]
    q = cq * _sigmoid(cq)
    log_lb, log_1m_lb, one_m_lb = lbv_ref[0:1, :], lbv_ref[1:2, :], lbv_ref[2:3, :]
    log_sig = jnp.minimum(cf, 0.0) - jnp.log(1.0 + jnp.exp(-jnp.abs(cf)))
    b_ = log_1m_lb + log_sig
    logf = jnp.maximum(log_lb, b_) + jnp.log(1.0 + jnp.exp(-jnp.abs(log_lb - b_)))
    kk = one_m_lb * _sigmoid(-cf)
    hi, lo = _split_bf16(logf)

    lvl_sums = _dot(mall_ref[0:HG_LEVELS * rows, :], hi)
    tail = mall_ref[HG_LEVELS * rows:(HG_LEVELS + 2) * rows, :]
    cum_suf = _dot(tail, hi) + _dot(tail, lo)
    g_cum, g_suffix = cum_suf[:rows], cum_suf[rows:]

    lane = lax.broadcasted_iota(jnp.int32, (1, 2 * HG_DK), 1)
    head_sel = [jnp.where(lane < HG_DK, 1.0, 0.0), jnp.where(lane >= HG_DK, 1.0, 0.0)]
    n_pairs = HG_HEADS // 2
    scores = [[None] * n_pairs for _ in range(nb)]
    for lvl in range(HG_LEVELS):
        s = 1 << lvl
        e = jnp.exp(lvl_sums[lvl * rows:(lvl + 1) * rows])
        if s >= 8:
            qk = jnp.concatenate([(q if (r // s) % 2 else kk)[r:r + s] for r in range(0, rows, s)], axis=0)
        else:
            qk = jnp.where(rsel_ref[lvl] > 0.5, q, kk)
        x = qk * e
        m = lvl_ref[lvl]
        for b in range(nb):
            for p in range(n_pairs):
                xp = x[b * CHUNK:(b + 1) * CHUNK, p * 128:(p + 1) * 128]
                rhs = jnp.concatenate([xp * head_sel[0], xp * head_sel[1]], axis=0).astype(BF16)
                sc = _dot_nt(xp.astype(BF16), rhs) * m
                scores[b][p] = sc if scores[b][p] is None else scores[b][p] + sc
    v_bf = v.astype(BF16)
    bones = bones_ref[...]
    qg = (q * jnp.exp(g_cum)).astype(BF16)
    kd = (kk * jnp.exp(g_suffix)).astype(BF16)
    o_rows = []
    for b in range(nb):
        sl = slice(b * CHUNK, (b + 1) * CHUNK)
        o_parts = []
        for p in range(n_pairs):
            vp = v[sl, p * 128:(p + 1) * 128]
            vv = jnp.concatenate([vp * head_sel[0], vp * head_sel[1]], axis=0).astype(BF16)
            o_parts.append(_dot(scores[b][p].astype(BF16), vv))
        st = st_ref[b]
        o_rows.append(jnp.concatenate(o_parts, axis=1) + _dot_nt(qg[sl], st.astype(BF16)))
        upd = _dot_tn(v_bf[sl], kd[sl]) * bones.astype(F32)
        st_ref[b] = st * jnp.exp(g_cum[(b + 1) * CHUNK - 1:(b + 1) * CHUNK, :]) + upd
    o = _rows(o_rows) + _dot((q * kk).astype(BF16), bones) * v
    ms = _dot((o * o).astype(BF16), avg_ref[...])
    return o * lax.rsqrt(ms + EPS) * g_ref[...] * (gate * _sigmoid(gate))


def _ret_chunk(rq, rk, v, cos, sin, intra_ref, inter_ref, tostate_ref, carry_ref, bmask_ref, st_ref):
    hw = RET_KEY_WIDTH // 2

    def rot(t):
        t1, t2 = t[:, :hw], t[:, hw:]
        return jnp.concatenate([t1 * cos - t2 * sin, t1 * sin + t2 * cos], axis=1)

    qr, kr = rot(rq), rot(rk)
    qr_bf, v_bf = qr.astype(BF16), v.astype(BF16)
    lane_k = lax.broadcasted_iota(jnp.int32, (1, RET_KEY_WIDTH), 1) % hw
    lane_v = lax.broadcasted_iota(jnp.int32, (1, 2 * RET_DV), 1)
    v_sel = [jnp.where(lane_v < RET_DV, 1.0, 0.0), jnp.where(lane_v >= RET_DV, 1.0, 0.0)]
    half = RET_DK // 2
    o_parts = []
    for p in range(RET_HEADS // 2):
        sel = [jnp.where((lane_k >= h * half) & (lane_k < (h + 1) * half), 1.0, 0.0)
               for h in (2 * p, 2 * p + 1)]
        rhs = jnp.concatenate([kr * sel[0], kr * sel[1]], axis=0).astype(BF16)
        sc = (_dot_nt(qr_bf, rhs) * intra_ref[p]).astype(BF16)
        vp = v[:, p * 128:(p + 1) * 128]
        vv = jnp.concatenate([vp * v_sel[0], vp * v_sel[1]], axis=0).astype(BF16)
        o_parts.append(_dot(sc, vv))
    o = jnp.concatenate(o_parts, axis=1)
    st = st_ref[...]
    o = o + _dot(qr_bf, st.astype(BF16)) * inter_ref[...]
    kd = (kr * tostate_ref[...]).astype(BF16)
    st_ref[...] = st * carry_ref[...] + _dot_tn(kd, v_bf) * bmask_ref[...]
    return o


def _ret_norm_gate(o, gate, avg_ref, g_ref):
    avg = avg_ref[...]
    outs = []
    for s in range(RET_WIDTH // 256):
        os_ = o[:, s * 256:(s + 1) * 256]
        c = os_ - _dot(os_.astype(BF16), avg)
        outs.append(c * lax.rsqrt(_dot((c * c).astype(BF16), avg) + EPS))
    return jnp.concatenate(outs, axis=1) * g_ref[...] * (gate * _sigmoid(gate))


def _mixer_kernel(proj_ref, cos_ref, sin_ref,
                  wb_ref, wc_ref, pn_re_ref, pn_im_ref, pp_re_ref, pp_im_ref, lam_ref, d_ref,
                  wglu_ref, s5g_ref, ltri_ref,
                  convw_ref, lbv_ref, mall_ref, lvl_ref, rsel_ref, bones_ref, avg_ref, hgg_ref,
                  intra_ref, inter_ref, tostate_ref, carry_ref, bmask_ref, retg_ref,
                  *rest, n_cast, cast_steps):
    cast_in, o_ref, cast_out = rest[:n_cast], rest[n_cast], rest[n_cast + 1:2 * n_cast + 1]
    s5_st, hg_xc, hg_st, ret_st = rest[2 * n_cast + 1:]

    @pl.when(pl.program_id(0) == 0)
    def _():
        s5_st[...] = jnp.zeros_like(s5_st)
        hg_xc[...] = jnp.zeros_like(hg_xc)
        hg_st[...] = jnp.zeros_like(hg_st)
        ret_st[...] = jnp.zeros_like(ret_st)

    if n_cast:
        @pl.when(pl.program_id(0) < cast_steps)
        def _():
            for src, dst in zip(cast_in, cast_out):
                dst[...] = src[...].astype(dst.dtype)

    nb = proj_ref.shape[0]

    def cols(c0, width):
        return _rows([proj_ref[b, :, c0:c0 + width] for b in range(nb)])

    def emit(c0, y):
        for b in range(nb):
            o_ref[b, :, c0:c0 + y.shape[1]] = y[b * CHUNK:(b + 1) * CHUNK].astype(o_ref.dtype)

    y_a = _s5_chunks(cols(C_U, S5_WIDTH), wb_ref, wc_ref, pn_re_ref, pn_im_ref, pp_re_ref, pp_im_ref,
                     lam_ref, d_ref, wglu_ref, s5g_ref, ltri_ref, s5_st)
    emit(0, y_a)

    for b in range(nb):
        hg_xc[b, 8:8 + CHUNK, :] = proj_ref[b, :, C_HQ:C_HQ + HG_CONV_W]
    y_b = _hgrn2_chunks(hg_xc, cols(C_HGATE, HG_WIDTH), convw_ref, lbv_ref,
                        mall_ref, lvl_ref, rsel_ref, bones_ref, avg_ref, hgg_ref, hg_st)
    for b in range(nb):
        hg_xc[b, 0:8, :] = hg_xc[b, CHUNK:CHUNK + 8, :]
    emit(S5_WIDTH, y_b)

    o_c = [_ret_chunk(proj_ref[b, :, C_RQ:C_RQ + RET_KEY_WIDTH], proj_ref[b, :, C_RK:C_RK + RET_KEY_WIDTH],
                      proj_ref[b, :, C_RV:C_RV + RET_WIDTH], cos_ref[...], sin_ref[...],
                      intra_ref, inter_ref, tostate_ref, carry_ref, bmask_ref, ret_st.at[b])
           for b in range(nb)]
    y_c = _ret_norm_gate(_rows(o_c), cols(C_RGATE, RET_WIDTH), avg_ref, retg_ref)
    emit(S5_WIDTH + HG_WIDTH, y_c)


def _const_spec(a):
    nd = a.ndim
    return pl.BlockSpec(a.shape, lambda c, _nd=nd: (0,) * _nd)


def _cast_steps(n_chunks, arrays):
    for steps in range(n_chunks, 0, -1):
        if all(a.shape[0] % (steps * 16) == 0 for a in arrays):
            return steps
    raise ValueError("no slab split")


def _mixers(proj, cos, sin, consts, bsz, n_chunks, to_bf16=()):
    total = proj.shape[1]
    n_cast = len(to_bf16)
    cast_steps = _cast_steps(n_chunks, to_bf16) if n_cast else 0
    slab = lambda c: (jnp.minimum(c, cast_steps - 1), 0)
    cast_specs = [pl.BlockSpec((a.shape[0] // cast_steps, a.shape[1]), slab) for a in to_bf16]
    in_specs = [
        pl.BlockSpec((bsz, CHUNK, IN_COLS), lambda c: (0, c, 0)),
        pl.BlockSpec((CHUNK, 128), lambda c: (c, 0)),
        pl.BlockSpec((CHUNK, 128), lambda c: (c, 0)),
    ] + [_const_spec(a) for a in consts] + cast_specs
    outs = pl.pallas_call(
        functools.partial(_mixer_kernel, n_cast=n_cast, cast_steps=cast_steps),
        grid=(n_chunks,),
        in_specs=in_specs,
        out_specs=[pl.BlockSpec((bsz, CHUNK, D_MIX), lambda c: (0, c, 0))] + cast_specs,
        out_shape=[jax.ShapeDtypeStruct((bsz, total, D_MIX), BF16)]
        + [jax.ShapeDtypeStruct(a.shape, BF16) for a in to_bf16],
        scratch_shapes=[
            pltpu.VMEM((bsz, 2, S5_NSTATE), F32),
            pltpu.VMEM((bsz, CHUNK + 8, HG_CONV_W), F32),
            pltpu.VMEM((bsz, HG_WIDTH, HG_WIDTH), F32),
            pltpu.VMEM((bsz, RET_KEY_WIDTH, RET_WIDTH), F32),
        ],
        compiler_params=pltpu.CompilerParams(
            dimension_semantics=("arbitrary",), vmem_limit_bytes=VMEM_LIMIT),
        name="mixers",
    )(proj, cos, sin, *consts, *to_bf16)
    return outs[0], list(outs[1:])


def _route(hn, router_ref, ltri_ref, count_ref):
    r = router_ref[...]
    r_hi, r_lo = _split_bf16(r)
    h_hi, h_lo = _split_bf16(hn)
    logits = _dot(h_hi, r_hi) + (_dot(h_lo, r_hi) + _dot(h_hi, r_lo))
    lane_i = lax.broadcasted_iota(jnp.int32, logits.shape, 1)
    lane = lane_i.astype(F32)
    neg = jnp.float32(-jnp.inf)
    logits = jnp.where(lane_i < N_EXPERTS, logits, neg)
    m1 = jnp.max(logits, axis=-1, keepdims=True)
    i1 = jnp.min(jnp.where(logits == m1, lane, float(ROUTE_LANES)), axis=-1, keepdims=True)
    rest = jnp.where(lane == i1, neg, logits)
    m2 = jnp.max(rest, axis=-1, keepdims=True)
    i2 = jnp.min(jnp.where(rest == m2, lane, float(ROUTE_LANES)), axis=-1, keepdims=True)
    e2 = jnp.exp(m2 - m1)
    g1 = 1.0 / (1.0 + e2)
    g2 = e2 / (1.0 + e2)
    onehot = jnp.where((lane == i1) | (lane == i2), 1.0, 0.0)
    before = _dot(ltri_ref[...], onehot.astype(BF16)) + count_ref[...]
    r1 = jnp.sum(jnp.where(lane == i1, before, 0.0), axis=-1, keepdims=True)
    r2 = jnp.sum(jnp.where(lane == i2, before, 0.0), axis=-1, keepdims=True)
    count_ref[...] += jnp.sum(onehot, axis=0, keepdims=True)
    out = jnp.where(lane == 0, i1, 0.0)
    out = jnp.where(lane == 1, i2, out)
    out = jnp.where(lane == 2, g1, out)
    out = jnp.where(lane == 3, g2, out)
    out = jnp.where(lane == 4, r1, out)
    return jnp.where(lane == 5, r2, out)


def _out_proj_kernel(mixed_ref, *refs, routed, n_stream, tiles_per_seq):
    stream, (w_ref, g_ref, *rest) = refs[:n_stream], refs[n_stream:]
    if routed:
        router_ref, ltri_ref, h1_ref, hn_ref, route_ref, fields_ref, count_ref = rest

        @pl.when(pl.program_id(0) == 0)
        def _():
            count_ref[...] = jnp.zeros_like(count_ref)
    else:
        h1_ref, hn_ref = rest
    h1 = _stream_tile(stream, tiles_per_seq) + _dot(mixed_ref[...], w_ref[...])
    h1_ref[...] = h1
    hn = h1 * lax.rsqrt(jnp.mean(h1 * h1, axis=-1, keepdims=True) + EPS) * g_ref[...]
    if routed:
        tm = hn.shape[0]
        for s in range(ROW_TILE):
            hn_ref[pl.ds(s, tm, stride=ROW_TILE), :] = hn[:, s * 128:(s + 1) * 128]
        slab = _route(hn, router_ref, ltri_ref, count_ref)
        route_ref[...] = slab
        fields_ref[...] = slab.T[0:ROUTE_FIELDS, :]
    else:
        hn_ref[...] = hn.astype(hn_ref.dtype)


def _out_proj(mixed, stream, w_bf16, g, bsz, total, router_pad=None):
    n, d = bsz * total, stream[0].shape[1]
    tm = _pick_tile(total, (TOKEN_TILE, 128))
    routed = router_pad is not None
    row = lambda i: (i, 0)
    fixed = lambda i: (0, 0)
    stream_specs, stream_operands = _stream_specs(stream, tm, total)
    in_specs = [pl.BlockSpec((tm, D_MIX), row)] + stream_specs + [
        pl.BlockSpec(w_bf16.shape, fixed), pl.BlockSpec((1, d), fixed)]
    out_specs = [pl.BlockSpec((tm, d), row), pl.BlockSpec((tm, d), row)]
    out_shape = [jax.ShapeDtypeStruct((n, d), F32), jax.ShapeDtypeStruct((n, d), BF16)]
    args = [mixed] + stream_operands + [w_bf16, g]
    if routed:
        assert d == ROW_TILE * 128
        out_specs[1] = pl.BlockSpec((tm * ROW_TILE, 128), row)
        out_shape[1] = jax.ShapeDtypeStruct((n * ROW_TILE, 128), F32)
        ltri = jnp.asarray(np.tril(np.ones((tm, tm), np.float32), -1), BF16)
        in_specs += [pl.BlockSpec(router_pad.shape, fixed), pl.BlockSpec((tm, tm), fixed)]
        out_specs += [pl.BlockSpec((tm, ROUTE_LANES), row), pl.BlockSpec((ROUTE_FIELDS, tm), lambda i: (0, i)),
                      pl.BlockSpec((1, ROUTE_LANES), fixed)]
        out_shape += [jax.ShapeDtypeStruct((n, ROUTE_LANES), F32),
                      jax.ShapeDtypeStruct((ROUTE_FIELDS, n), F32),
                      jax.ShapeDtypeStruct((1, ROUTE_LANES), F32)]
        args += [router_pad, ltri]
    return pl.pallas_call(
        functools.partial(_out_proj_kernel, routed=routed, n_stream=len(stream), tiles_per_seq=total // tm),
        grid=(n // tm,),
        in_specs=in_specs,
        out_specs=out_specs,
        out_shape=out_shape,
        compiler_params=pltpu.CompilerParams(
            dimension_semantics=("arbitrary",), vmem_limit_bytes=VMEM_LIMIT),
        name="out_proj_routed" if routed else "out_proj",
    )(*args)


HIDDEN_CHUNK = 256


def _swiglu_hidden(x_ref, w1_ref, w3_ref, hid_ref):
    for c0 in range(0, hid_ref.shape[1], HIDDEN_CHUNK):
        cols = slice(c0, c0 + HIDDEN_CHUNK)
        a = _dot(x_ref[...], w1_ref[:, cols])
        b = _dot(x_ref[...], w3_ref[:, cols])
        hid_ref[:, cols] = (a * _sigmoid(a) * b).astype(BF16)


def _ffn_kernel(hn_ref, h1_ref, w1_ref, w3_ref, w2_ref, o_ref, hid_ref):
    _swiglu_hidden(hn_ref, w1_ref, w3_ref, hid_ref)
    o_ref[...] = h1_ref[...] + _dot(hid_ref[...], w2_ref[...])


def _ffn(hn, h1, w1, w3, w2):
    n, d = h1.shape
    dff = w1.shape[1]
    assert dff % HIDDEN_CHUNK == 0
    tm = _pick_tile(n, (TOKEN_TILE, 128))
    resident = lambda a: pl.BlockSpec(a.shape, lambda i: (0, 0), pipeline_mode=pl.Buffered(1))
    return pl.pallas_call(
        _ffn_kernel,
        grid=(n // tm,),
        in_specs=[
            pl.BlockSpec((tm, d), lambda i: (i, 0)),
            pl.BlockSpec((tm, d), lambda i: (i, 0)),
            resident(w1), resident(w3), resident(w2),
        ],
        out_specs=pl.BlockSpec((tm, d), lambda i: (i, 0)),
        out_shape=jax.ShapeDtypeStruct((n, d), F32),
        scratch_shapes=[pltpu.VMEM((tm, dff), BF16)],
        compiler_params=pltpu.CompilerParams(
            dimension_semantics=("arbitrary",), vmem_limit_bytes=VMEM_LIMIT),
        name="ffn_dense",
    )(hn, h1, w1, w3, w2)


def _moe_kernel(te_ref, nv_ref, src_ref, tok_ref, hn_hbm, w1_ref, w3_ref, w2_ref, o_ref,
                xs_ref, xb_ref, hid_ref, acc_ref, sem, *, tm):
    i, f = pl.program_id(0), pl.program_id(1)
    nf = pl.num_programs(1)
    n_valid = nv_ref[0]
    share = tm // MOE_F_STEPS

    def row_copy(tile, r):
        tok = tok_ref[src_ref[tile] + r]
        return pltpu.make_async_copy(hn_hbm.at[pl.ds(pl.multiple_of(tok * ROW_TILE, ROW_TILE), ROW_TILE)],
                                     xs_ref.at[pl.ds(pl.multiple_of(r * ROW_TILE, ROW_TILE), ROW_TILE)], sem)

    @pl.when((i == 0) & (f == 0))
    def _():
        def body(r, carry):
            row_copy(0, r).start()
            return carry
        lax.fori_loop(0, tm, body, 0)

    @pl.when((f == 0) & (i <= n_valid))
    def _():
        pltpu.make_async_copy(xs_ref, xs_ref, sem).wait()
        for s in range(ROW_TILE):
            xb_ref[:, s * 128:(s + 1) * 128] = xs_ref[pl.ds(s, tm, stride=ROW_TILE), :].astype(BF16)

    @pl.when(i < n_valid)
    def _():
        for r in range(share):
            row_copy(i + 1, f * share + r).start(priority=r % MOE_DMA_QUEUES)
        _swiglu_hidden(xb_ref, w1_ref.at[0], w3_ref.at[0], hid_ref)
        part = _dot(hid_ref[...], w2_ref[0])

        @pl.when(f == 0)
        def _():
            acc_ref[...] = part

        @pl.when(f > 0)
        def _():
            acc_ref[...] += part

    @pl.when(f == nf - 1)
    def _():
        o_ref[...] = jnp.where(i < n_valid, acc_ref[...], 0.0).astype(o_ref.dtype)


def _live_f(i, f, nv):
    return jnp.where(i < nv[0], f, 0)


def _moe_grouped(tile_expert, n_valid, tile_src, sorted_tok, hn, w1, w3, w2, tm):
    d = ROW_TILE * hn.shape[1]
    n_tiles = tile_expert.shape[0]
    dff = w1.shape[2]
    tf = dff // MOE_F_STEPS
    grid_spec = pltpu.PrefetchScalarGridSpec(
        num_scalar_prefetch=4,
        grid=(n_tiles, MOE_F_STEPS),
        in_specs=[
            pl.BlockSpec(memory_space=pl.ANY),
            pl.BlockSpec((1, d, tf), lambda i, f, te, nv, *_: (te[i], 0, _live_f(i, f, nv))),
            pl.BlockSpec((1, d, tf), lambda i, f, te, nv, *_: (te[i], 0, _live_f(i, f, nv))),
            pl.BlockSpec((1, tf, d), lambda i, f, te, nv, *_: (te[i], _live_f(i, f, nv), 0)),
        ],
        out_specs=pl.BlockSpec((tm, d), lambda i, f, *_: (i, 0)),
        scratch_shapes=[pltpu.VMEM((tm * ROW_TILE, 128), F32), pltpu.VMEM((tm, d), BF16),
                        pltpu.VMEM((tm, tf), BF16), pltpu.VMEM((tm, d), F32), pltpu.SemaphoreType.DMA(())],
    )
    return pl.pallas_call(
        functools.partial(_moe_kernel, tm=tm),
        grid_spec=grid_spec,
        out_shape=jax.ShapeDtypeStruct((n_tiles * tm, d), BF16),
        compiler_params=pltpu.CompilerParams(
            dimension_semantics=("arbitrary", "arbitrary"), vmem_limit_bytes=VMEM_LIMIT),
        name="moe_grouped",
    )(tile_expert, n_valid, tile_src, sorted_tok, hn, w1, w3, w2)


def _moe_dispatch(fields, counts_row, tm):
    n = fields.shape[1]
    n_slots = n * TOP_K
    n_tiles = (n_slots + N_EXPERTS * tm + tm - 1) // tm
    counts = counts_row[0, :N_EXPERTS].astype(jnp.int32)
    padded = ((counts + tm - 1) // tm) * tm
    pend = jnp.cumsum(padded)
    pstart = pend - padded
    gstart = jnp.cumsum(counts) - counts
    eid = fields[0:TOP_K].astype(jnp.int32)
    rank = fields[2 * TOP_K:3 * TOP_K].astype(jnp.int32)
    is_e = [eid == e for e in range(N_EXPERTS)]
    lookup = lambda table: sum(jnp.where(m, table[e], 0) for e, m in enumerate(is_e))
    pos = lookup(pstart) + rank
    tok = jnp.broadcast_to(jnp.arange(n, dtype=jnp.int32)[None, :], (TOP_K, n))
    _, sorted_tok = lax.sort(((lookup(gstart) + rank).reshape(-1), tok.reshape(-1)), num_keys=1)
    sorted_tok = jnp.concatenate([sorted_tok, jnp.zeros((tm,), jnp.int32)])
    tile_start = jnp.arange(n_tiles, dtype=jnp.int32) * tm
    tile_expert = jnp.minimum(jnp.sum((tile_start[:, None] >= pend[None, :]).astype(jnp.int32), axis=1),
                              N_EXPERTS - 1)
    tile_src = jnp.clip(tile_start - (pstart - gstart)[tile_expert], 0, n_slots)
    n_valid = (pend[-1] // tm).astype(jnp.int32).reshape(1)
    return sorted_tok, tile_src, pos, tile_expert, n_valid


def _final_kernel(h1_ref, y0_ref, y1_ref, route_ref, g_ref, o_ref):
    g0 = route_ref[:, TOP_K:TOP_K + 1]
    g1 = route_ref[:, TOP_K + 1:TOP_K + 2]
    h = h1_ref[...] + (g0 * y0_ref[...].astype(F32) + g1 * y1_ref[...].astype(F32))
    o_ref[...] = h * lax.rsqrt(jnp.mean(h * h, axis=-1, keepdims=True) + EPS) * g_ref[...]


def _final(h1, y0, y1, route, g, bsz, total):
    d = h1.shape[1]
    seq = total - CHUNK
    tm = _pick_tile(seq, (1024, 128))
    per_seq = seq // tm
    src = lambda b, k: (pl.multiple_of(b * total + CHUNK + k * tm, CHUNK), 0)
    rows = lambda width: pl.BlockSpec((pl.Element(tm), pl.Element(width)), src)
    out = pl.pallas_call(
        _final_kernel,
        grid=(bsz, per_seq),
        in_specs=[rows(d), rows(d), rows(d), rows(ROUTE_LANES), pl.BlockSpec((1, d), lambda b, k: (0, 0))],
        out_specs=pl.BlockSpec((tm, d), lambda b, k: (b * per_seq + k, 0)),
        out_shape=jax.ShapeDtypeStruct((bsz * seq, d), F32),
        compiler_params=pltpu.CompilerParams(
            dimension_semantics=("arbitrary", "arbitrary"), vmem_limit_bytes=VMEM_LIMIT),
        name="final_norm",
    )(h1, y0, y1, route, g)
    return out.reshape(bsz, seq, d)


def _relayout_in_cols(w):
    d = w.shape[0]

    def halves_first(block):
        return block.reshape(d, RET_HEADS, 2, RET_DK // 2).transpose(0, 2, 1, 3).reshape(d, RET_KEY_WIDTH)

    return jnp.concatenate([w[:, :C_RQ], halves_first(w[:, C_RQ:C_RK]), halves_first(w[:, C_RK:C_RV]),
                            w[:, C_RV:]], axis=1)


def _hg_tables(nb):
    c = CHUNK
    i = np.arange(c)[:, None]
    t = np.arange(c)[None, :]
    blocks, masks, rsel = [], [], []
    for lvl in range(HG_LEVELS):
        s = 1 << lvl
        r = (i // (2 * s)) * (2 * s) + s - 1
        right = i > r
        m = np.where(right, (t > r) & (t <= i), (t > i) & (t <= r))
        blocks.append(m)
        j = t
        same = (i // (2 * s)) == (j // (2 * s))
        mk = same & right & (j <= r)
        masks.append(np.concatenate([mk, mk], axis=1))
        if s < 8:
            rsel.append(np.broadcast_to(right, (c, HG_WIDTH)))
    blocks.append(t <= i)
    blocks.append(t > i)
    eye = np.eye(nb, dtype=np.float32)
    mall = np.concatenate([np.kron(eye, blk.astype(np.float32)) for blk in blocks], axis=0)
    lvl = np.stack(masks, axis=0).astype(np.float32)
    ch = np.arange(HG_WIDTH)
    bones = (ch[:, None] // HG_DK == ch[None, :] // HG_DK).astype(np.float32)
    rsel = np.stack([np.tile(r, (nb, 1)) for r in rsel], axis=0).astype(np.float32)
    return mall, lvl, rsel, bones


def _ret_tables():
    f32 = jnp.float32
    log_gamma = jnp.log1p(-jnp.power(2.0, -5.0 - jnp.arange(RET_HEADS, dtype=f32)))
    n = jnp.arange(CHUNK, dtype=f32)
    lg = log_gamma[:, None]
    causal = jnp.tril(jnp.ones((CHUNK, CHUNK), dtype=bool))
    intra = jnp.exp(jnp.where(causal[None], (n[:, None] - n[None, :])[None] * lg[:, :, None], -jnp.inf))
    scale = RET_DK ** -0.5
    inter = jnp.exp((n[None, :] + 1.0) * lg)
    to_state = jnp.exp((CHUNK - 1.0 - n[None, :]) * lg)
    carry = jnp.exp(CHUNK * lg)[:, 0]
    head_of_v = np.arange(RET_WIDTH) // RET_DV
    head_of_k = (np.arange(RET_KEY_WIDTH) % (RET_KEY_WIDTH // 2)) // (RET_DK // 2)
    inter_t = (inter * scale).T[:, head_of_v]
    to_state_t = to_state.T[:, head_of_k]
    carry_row = carry[head_of_v][None, :]
    bmask = jnp.asarray((head_of_k[:, None] == head_of_v[None, :]).astype(np.float32))
    intra_pairs = (intra * scale).reshape(RET_HEADS // 2, 2, CHUNK, CHUNK)
    intra_pairs = jnp.concatenate([intra_pairs[:, 0], intra_pairs[:, 1]], axis=2)
    return intra_pairs, inter_t, to_state_t, carry_row, bmask


def _s5_tables(lam_re, lam_im, b_re, b_im, c_re, c_im, d_skip, log_step):
    f32 = jnp.float32
    lam = lax.complex(lam_re.astype(f32), lam_im.astype(f32))
    step = jnp.exp(log_step.astype(f32))[:, None]
    lam_dt = lam * step
    lam_bar = jnp.exp(lam_dt)
    b_bar = ((lam_bar - 1.0) / lam)[..., None] * lax.complex(b_re.astype(f32), b_im.astype(f32))
    eye = jnp.eye(S5_NGROUPS, dtype=f32)
    wb_re = jnp.einsum('gph,gk->ghkp', jnp.real(b_bar), eye).reshape(S5_WIDTH, S5_NSTATE)
    wb_im = jnp.einsum('gph,gk->ghkp', jnp.imag(b_bar), eye).reshape(S5_WIDTH, S5_NSTATE)
    wb = jnp.concatenate([wb_re, wb_im], axis=1)
    wc_re = jnp.einsum('ghp,gk->gpkh', c_re.astype(f32), eye).reshape(S5_NSTATE, S5_WIDTH)
    wc_im = jnp.einsum('ghp,gk->gpkh', c_im.astype(f32), eye).reshape(S5_NSTATE, S5_WIDTH)
    wc = jnp.concatenate([wc_re, -wc_im], axis=0)
    t = jnp.arange(CHUNK, dtype=f32)[:, None, None]
    pp = jnp.exp(lam_dt[None] * t).reshape(CHUNK, S5_NSTATE)
    pn = jnp.exp(-lam_dt[None] * t).reshape(CHUNK, S5_NSTATE)
    lam_rows = jnp.stack([jnp.real(lam_bar).reshape(-1), jnp.imag(lam_bar).reshape(-1)], axis=0)
    return (wb.astype(BF16), wc.astype(BF16), jnp.real(pn), jnp.imag(pn), jnp.real(pp), jnp.imag(pp),
            lam_rows, d_skip.astype(f32).reshape(1, S5_WIDTH))


def kernel(x, meta_tokens, norm_mix_g, w_in, s5_lam_re, s5_lam_im, s5_b_re, s5_b_im, s5_c_re, s5_c_im, s5_d, s5_log_step, s5_w_glu, s5_out_g, hg_conv_w, hg_lb_param, hg_out_g, ret_out_g, w_out, norm_ffn_g, ffn_w1, ffn_w3, ffn_w2, moe_router, moe_w1, moe_w3, moe_w2, final_norm_g):
    bsz, seq_len, d = x.shape
    depth = w_in.shape[0]
    total = seq_len + CHUNK
    n_chunks = total // CHUNK
    n = bsz * total

    meta = jnp.broadcast_to(meta_tokens.astype(F32)[None], (bsz, N_META, d))
    tm0 = _pick_tile(total, (TOKEN_TILE, 128))
    head = jnp.concatenate([jnp.zeros((bsz, PAD, d), F32), meta, x[:, :tm0 - CHUNK].astype(F32)], axis=1)
    stream = (x.astype(F32).reshape(bsz * seq_len, d), head)

    pos = (jnp.arange(total) - PAD).astype(F32)
    half = RET_DK // 2
    inv_freq = ROPE_BASE ** (-jnp.arange(half, dtype=F32) / half)
    ang = pos[:, None] * inv_freq[None, :]
    cos_t = jnp.tile(jnp.cos(ang), (1, RET_HEADS))
    sin_t = jnp.tile(jnp.sin(ang), (1, RET_HEADS))

    lb_all = jnp.cumsum(jax.nn.softmax(hg_lb_param.astype(F32), axis=0), axis=0)
    lb_all = lb_all - lb_all[0]

    mall_np, lvl_np, rsel_np, bones_np = _hg_tables(bsz)
    mall = jnp.asarray(mall_np, BF16)
    lvl = jnp.asarray(lvl_np, F32)
    rsel = jnp.asarray(rsel_np, F32)
    bones = jnp.asarray(bones_np, BF16)
    avg = jnp.asarray(bones_np / HG_DK, BF16)
    ltri = jnp.asarray(np.kron(np.eye(bsz, dtype=np.float32),
                               np.tril(np.ones((CHUNK, CHUNK), np.float32))), BF16)
    intra, inter_t, to_state_t, carry_row, bmask = _ret_tables()

    out = None
    for l in range(depth):
        w_in_l = _relayout_in_cols(w_in[l].astype(BF16))
        proj = _in_proj(stream, norm_mix_g[l].astype(F32).reshape(1, d), w_in_l, bsz, total)

        s5c = _s5_tables(s5_lam_re[l], s5_lam_im[l], s5_b_re[l], s5_b_im[l], s5_c_re[l], s5_c_im[l],
                         s5_d[l], s5_log_step[l])
        lb = lb_all[l][None, :]
        lbv = jnp.concatenate([jnp.log(lb), jnp.log1p(-lb), 1.0 - lb], axis=0)
        consts = list(s5c) + [
            s5_w_glu[l].astype(BF16), s5_out_g[l].astype(F32).reshape(1, -1), ltri,
            hg_conv_w[l].astype(F32), lbv, mall, lvl, rsel, bones, avg, hg_out_g[l].astype(F32).reshape(1, -1),
            intra, inter_t, to_state_t, carry_row, bmask, ret_out_g[l].astype(F32).reshape(1, -1),
        ]
        j = l // 2
        if l % 2 == 0:
            ffn_f32 = [ffn_w1[j], ffn_w3[j], ffn_w2[j]]
        else:
            dff_e = moe_w1.shape[-1]
            ffn_f32 = [moe_w1[j].reshape(N_EXPERTS * d, dff_e), moe_w3[j].reshape(N_EXPERTS * d, dff_e),
                       moe_w2[j].reshape(N_EXPERTS * dff_e, d)]
        mixed, ffn_bf16 = _mixers(proj.reshape(bsz, total, IN_COLS), cos_t, sin_t, consts, bsz, n_chunks,
                                  to_bf16=[a.astype(F32) for a in ffn_f32])
        mixed = mixed.reshape(n, D_MIX)

        g_ffn = norm_ffn_g[l].astype(F32).reshape(1, d)
        w_out_l = w_out[l].astype(BF16)
        if l % 2 == 0:
            h1, hn = _out_proj(mixed, stream, w_out_l, g_ffn, bsz, total)
            stream = (_ffn(hn, h1, *ffn_bf16),)
            y0 = y1 = None
        else:
            router_pad = jnp.zeros((d, ROUTE_LANES), F32).at[:, :N_EXPERTS].set(moe_router[j].astype(F32))
            h1, hn, route, fields, counts_row = _out_proj(mixed, stream, w_out_l, g_ffn, bsz, total, router_pad)
            tm = MOE_TILE
            sorted_tok, tile_src, pos_of_slot, tile_expert, n_valid = _moe_dispatch(fields, counts_row, tm)
            e_w1, e_w3, e_w2 = (ffn_bf16[0].reshape(N_EXPERTS, d, dff_e), ffn_bf16[1].reshape(N_EXPERTS, d, dff_e),
                                ffn_bf16[2].reshape(N_EXPERTS, dff_e, d))
            ys = _moe_grouped(tile_expert, n_valid, tile_src, sorted_tok, hn, e_w1, e_w3, e_w2, tm)
            y0 = ys.at[pos_of_slot[0]].get(mode='promise_in_bounds')
            y1 = ys.at[pos_of_slot[1]].get(mode='promise_in_bounds')
            if l < depth - 1:
                stream = (h1 + (route[:, TOP_K:TOP_K + 1] * y0.astype(F32)
                                + route[:, TOP_K + 1:TOP_K + 2] * y1.astype(F32)),)

        if l == depth - 1:
            if y0 is None:
                y0 = y1 = jnp.zeros((n, d), BF16)
                h1 = stream[0]
                route = jnp.zeros((n, ROUTE_LANES), F32)
            out = _final(h1, y0, y1, route, final_norm_g.astype(F32).reshape(1, d), bsz, total)

    return out.astype(x.dtype)
```

```python
import functools
import math

import jax
import jax.numpy as jnp
import numpy as np
from jax import lax
from jax.experimental import pallas as pl
from jax.experimental.pallas import tpu as pltpu

F32 = jnp.float32
BF16 = jnp.bfloat16

CHUNK = 128
N_META = 16
PAD = CHUNK - N_META
EPS = 1e-6

S5_WIDTH = 256
S5_GROUP = 16
S5_NGROUPS = 16
S5_STATE = 64
S5_NSTATE = S5_NGROUPS * S5_STATE

HG_HEADS = 4
HG_DK = 64
HG_WIDTH = 256
CONV_K = 4
HG_CONV_W = 3 * HG_WIDTH
HG_LEVELS = 7

RET_HEADS = 8
RET_DK = 32
RET_DV = 64
RET_KEY_WIDTH = 256
RET_WIDTH = 512
ROPE_BASE = 10000.0

D_MIX = 1024
IN_COLS = 2816
C_U, C_HQ, C_HGATE, C_RQ, C_RK, C_RV, C_RGATE = 0, 256, 1024, 1280, 1536, 1792, 2304

N_EXPERTS = 8
TOP_K = 2
ROUTE_LANES = 128
ROUTE_FIELDS = 8
MOE_F_STEPS = 2
MOE_TILE = 512
ROW_TILE = 8
MOE_DMA_QUEUES = 2

VMEM_LIMIT = 56 * 1024 * 1024


def _sigmoid(x):
    return 1.0 / (1.0 + jnp.exp(-x))


def _split_bf16(x):
    hi = x.astype(BF16)
    lo = (x - hi.astype(F32)).astype(BF16)
    return hi, lo


def _pick_tile(n, candidates):
    for t in candidates:
        if n % t == 0:
            return t
    raise ValueError(f"no tile in {candidates} divides {n}")


def _dot(a, b):
    return jnp.dot(a, b, preferred_element_type=F32)


def _dot_nt(a, b):
    return lax.dot_general(a, b, (((1,), (1,)), ((), ())), preferred_element_type=F32)


def _dot_tn(a, b):
    return lax.dot_general(a, b, (((0,), (0,)), ((), ())), preferred_element_type=F32)


TOKEN_TILE = 640


def _stream_specs(stream, tm, total):
    d = stream[0].shape[1]
    if len(stream) == 1:
        return [pl.BlockSpec((tm, d), lambda i: (i, 0))], list(stream)
    per, seq = total // tm, total - CHUNK
    x_rows = lambda i: (pl.multiple_of((i // per) * seq + jnp.maximum((i % per) * tm - CHUNK, 0), CHUNK), 0)
    return ([pl.BlockSpec((pl.Element(tm), pl.Element(d)), x_rows),
             pl.BlockSpec((1, tm, d), lambda i: (i // per, 0, 0))], list(stream))


def _stream_tile(refs, tiles_per_seq):
    if len(refs) == 1:
        return refs[0][...]
    x_ref, head_ref = refs
    return jnp.where(pl.program_id(0) % tiles_per_seq == 0, head_ref[0], x_ref[...])


def _in_proj_kernel(*refs, tm, tiles_per_seq):
    *stream, g_ref, w_ref, o_ref = refs
    x = _stream_tile(stream, tiles_per_seq)
    y = x * lax.rsqrt(jnp.mean(x * x, axis=-1, keepdims=True) + EPS) * g_ref[...]
    proj = _dot(y.astype(BF16), w_ref[...])
    row0 = (pl.program_id(0) % tiles_per_seq) * tm
    rows = row0 + lax.broadcasted_iota(jnp.int32, (tm, 1), 0)
    o_ref[...] = jnp.where(rows >= PAD, proj, 0.0)


def _in_proj(stream, g, w_bf16, bsz, total):
    n, d = bsz * total, stream[0].shape[1]
    tm = _pick_tile(total, (TOKEN_TILE, 128))
    specs, operands = _stream_specs(stream, tm, total)
    return pl.pallas_call(
        functools.partial(_in_proj_kernel, tm=tm, tiles_per_seq=total // tm),
        grid=(n // tm,),
        in_specs=specs + [
            pl.BlockSpec((1, d), lambda i: (0, 0)),
            pl.BlockSpec(w_bf16.shape, lambda i: (0, 0), pipeline_mode=pl.Buffered(1)),
        ],
        out_specs=pl.BlockSpec((tm, IN_COLS), lambda i: (i, 0)),
        out_shape=jax.ShapeDtypeStruct((n, IN_COLS), F32),
        compiler_params=pltpu.CompilerParams(
            dimension_semantics=("arbitrary",), vmem_limit_bytes=VMEM_LIMIT),
        name="in_proj",
    )(*operands, g, w_bf16)


def _rows(parts):
    return parts[0] if len(parts) == 1 else jnp.concatenate(parts, axis=0)


def _s5_chunks(u, wb_ref, wc_ref, pn_re_ref, pn_im_ref, pp_re_ref, pp_im_ref, lam_ref,
               d_ref, wglu_ref, g_ref, ltri_ref, st_ref):
    ns = S5_NSTATE
    nb = st_ref.shape[0]
    bu = _dot(u.astype(BF16), wb_ref[...])
    pn_re, pn_im = pn_re_ref[...], pn_im_ref[...]
    w_re, w_im = [], []
    for b in range(nb):
        bu_re = bu[b * CHUNK:(b + 1) * CHUNK, :ns]
        bu_im = bu[b * CHUNK:(b + 1) * CHUNK, ns:]
        w_re.append(pn_re * bu_re - pn_im * bu_im)
        w_im.append(pn_re * bu_im + pn_im * bu_re)
    ltri = ltri_ref[...]
    c_re = _dot(ltri, _rows(w_re).astype(BF16))
    c_im = _dot(ltri, _rows(w_im).astype(BF16))
    lam_re, lam_im = lam_ref[0:1, :], lam_ref[1:2, :]
    pp_re, pp_im = pp_re_ref[...], pp_im_ref[...]
    st_re, st_im = [], []
    for b in range(nb):
        s_re, s_im = st_ref[b, 0:1, :], st_ref[b, 1:2, :]
        z_re = c_re[b * CHUNK:(b + 1) * CHUNK] + (lam_re * s_re - lam_im * s_im)
        z_im = c_im[b * CHUNK:(b + 1) * CHUNK] + (lam_re * s_im + lam_im * s_re)
        t_re = pp_re * z_re - pp_im * z_im
        t_im = pp_re * z_im + pp_im * z_re
        st_ref[b, 0:1, :] = t_re[CHUNK - 1:CHUNK, :]
        st_ref[b, 1:2, :] = t_im[CHUNK - 1:CHUNK, :]
        st_re.append(t_re)
        st_im.append(t_im)
    y = (_dot(_rows(st_re).astype(BF16), wc_ref[0:ns, :]) + _dot(_rows(st_im).astype(BF16), wc_ref[ns:2 * ns, :])
         + d_ref[...] * u)
    y = 0.5 * y * (1.0 + jnp.tanh(math.sqrt(2.0 / math.pi) * (y + 0.044715 * (y * y * y))))
    y = y * _sigmoid(_dot(y.astype(BF16), wglu_ref[...]))
    return y * lax.rsqrt(jnp.mean(y * y, axis=-1, keepdims=True) + EPS) * g_ref[...]


def _hgrn2_chunks(xc_ref, gate, convw_ref, lbv_ref, mall_ref, lvl_ref, rsel_ref, bones_ref, avg_ref,
                  g_ref, st_ref):
    w = HG_WIDTH
    nb = st_ref.shape[0]
    rows = nb * CHUNK
    convs = []
    for b in range(nb):
        conv = None
        for i in range(CONV_K):
            term = xc_ref[b, pl.ds(8 - (CONV_K - 1) + i, CHUNK), :] * convw_ref[i:i + 1, :]
            conv = term if conv is None else conv + term
        convs.append(conv)
    conv = _rows(convs)
    cq, cf, v = conv[:, :w], conv[:, w:2 * w], conv[:, 2 * w:]
    q = cq * _sigmoid(cq)
    log_lb, log_1m_lb, one_m_lb = lbv_ref[0:1, :], lbv_ref[1:2, :], lbv_ref[2:3, :]
    log_sig = jnp.minimum(cf, 0.0) - jnp.log(1.0 + jnp.exp(-jnp.abs(cf)))
    b_ = log_1m_lb + log_sig
    logf = jnp.maximum(log_lb, b_) + jnp.log(1.0 + jnp.exp(-jnp.abs(log_lb - b_)))
    kk = one_m_lb * _sigmoid(-cf)
    hi, lo = _split_bf16(logf)

    lvl_sums = _dot(mall_ref[0:HG_LEVELS * rows, :], hi)
    tail = mall_ref[HG_LEVELS * rows:(HG_LEVELS + 2) * rows, :]
    cum_suf = _dot(tail, hi) + _dot(tail, lo)
    g_cum, g_suffix = cum_suf[:rows], cum_suf[rows:]

    lane = lax.broadcasted_iota(jnp.int32, (1, 2 * HG_DK), 1)
    head_sel = [jnp.where(lane < HG_DK, 1.0, 0.0), jnp.where(lane >= HG_DK, 1.0, 0.0)]
    n_pairs = HG_HEADS // 2
    scores = [[None] * n_pairs for _ in range(nb)]
    for lvl in range(HG_LEVELS):
        s = 1 << lvl
        e = jnp.exp(lvl_sums[lvl * rows:(lvl + 1) * rows])
        if s >= 8:
            qk = jnp.concatenate([(q if (r // s) % 2 else kk)[r:r + s] for r in range(0, rows, s)], axis=0)
        else:
            qk = jnp.where(rsel_ref[lvl] > 0.5, q, kk)
        x = qk * e
        m = lvl_ref[lvl] > 0.5
        for b in range(nb):
            for p in range(n_pairs):
                xp = x[b * CHUNK:(b + 1) * CHUNK, p * 128:(p + 1) * 128]
                rhs = jnp.concatenate([xp * head_sel[0], xp * head_sel[1]], axis=0).astype(BF16)
                sc = _dot_nt(xp.astype(BF16), rhs)
                scores[b][p] = jnp.where(m, sc, 0.0 if scores[b][p] is None else scores[b][p])
    v_bf = v.astype(BF16)
    bones = bones_ref[...]
    qg = (q * jnp.exp(g_cum)).astype(BF16)
    kd = (kk * jnp.exp(g_suffix)).astype(BF16)
    o_rows = []
    for b in range(nb):
        sl = slice(b * CHUNK, (b + 1) * CHUNK)
        o_parts = []
        for p in range(n_pairs):
            vp = v[sl, p * 128:(p + 1) * 128]
            vv = jnp.concatenate([vp * head_sel[0], vp * head_sel[1]], axis=0).astype(BF16)
            o_parts.append(_dot(scores[b][p].astype(BF16), vv))
        st = st_ref[b]
        o_rows.append(jnp.concatenate(o_parts, axis=1) + _dot_nt(qg[sl], st.astype(BF16)))
        upd = _dot_tn(v_bf[sl], kd[sl]) * bones.astype(F32)
        st_ref[b] = st * jnp.exp(g_cum[(b + 1) * CHUNK - 1:(b + 1) * CHUNK, :]) + upd
    o = _rows(o_rows) + _dot((q * kk).astype(BF16), bones) * v
    ms = _dot((o * o).astype(BF16), avg_ref[...])
    return o * lax.rsqrt(ms + EPS) * g_ref[...] * (gate * _sigmoid(gate))


def _ret_chunk(rq, rk, v, cos, sin, intra_ref, inter_ref, tostate_ref, carry_ref, bmask_ref, st_ref):
    hw = RET_KEY_WIDTH // 2

    def rot(t):
        t1, t2 = t[:, :hw], t[:, hw:]
        return jnp.concatenate([t1 * cos - t2 * sin, t1 * sin + t2 * cos], axis=1)

    qr, kr = rot(rq), rot(rk)
    qr_bf, v_bf = qr.astype(BF16), v.astype(BF16)
    lane_k = lax.broadcasted_iota(jnp.int32, (1, RET_KEY_WIDTH), 1) % hw
    lane_v = lax.broadcasted_iota(jnp.int32, (1, 2 * RET_DV), 1)
    v_sel = [jnp.where(lane_v < RET_DV, 1.0, 0.0), jnp.where(lane_v >= RET_DV, 1.0, 0.0)]
    half = RET_DK // 2
    o_parts = []
    for p in range(RET_HEADS // 2):
        sel = [jnp.where((lane_k >= h * half) & (lane_k < (h + 1) * half), 1.0, 0.0)
               for h in (2 * p, 2 * p + 1)]
        rhs = jnp.concatenate([kr * sel[0], kr * sel[1]], axis=0).astype(BF16)
        sc = (_dot_nt(qr_bf, rhs) * intra_ref[p]).astype(BF16)
        vp = v[:, p * 128:(p + 1) * 128]
        vv = jnp.concatenate([vp * v_sel[0], vp * v_sel[1]], axis=0).astype(BF16)
        o_parts.append(_dot(sc, vv))
    o = jnp.concatenate(o_parts, axis=1)
    st = st_ref[...]
    o = o + _dot(qr_bf, st.astype(BF16)) * inter_ref[...]
    kd = (kr * tostate_ref[...]).astype(BF16)
    st_ref[...] = st * carry_ref[...] + _dot_tn(kd, v_bf) * bmask_ref[...]
    return o


def _ret_norm_gate(o, gate, avg_ref, g_ref):
    avg = avg_ref[...]
    outs = []
    for s in range(RET_WIDTH // 256):
        os_ = o[:, s * 256:(s + 1) * 256]
        c = os_ - _dot(os_.astype(BF16), avg)
        outs.append(c * lax.rsqrt(_dot((c * c).astype(BF16), avg) + EPS))
    return jnp.concatenate(outs, axis=1) * g_ref[...] * (gate * _sigmoid(gate))


def _mixer_kernel(proj_ref, cos_ref, sin_ref,
                  wb_ref, wc_ref, pn_re_ref, pn_im_ref, pp_re_ref, pp_im_ref, lam_ref, d_ref,
                  wglu_ref, s5g_ref, ltri_ref,
                  convw_ref, lbv_ref, mall_ref, lvl_ref, rsel_ref, bones_ref, avg_ref, hgg_ref,
                  intra_ref, inter_ref, tostate_ref, carry_ref, bmask_ref, retg_ref,
                  *rest, n_cast, cast_steps):
    cast_in, o_ref, cast_out = rest[:n_cast], rest[n_cast], rest[n_cast + 1:2 * n_cast + 1]
    s5_st, hg_xc, hg_st, ret_st = rest[2 * n_cast + 1:]

    @pl.when(pl.program_id(0) == 0)
    def _():
        s5_st[...] = jnp.zeros_like(s5_st)
        hg_xc[...] = jnp.zeros_like(hg_xc)
        hg_st[...] = jnp.zeros_like(hg_st)
        ret_st[...] = jnp.zeros_like(ret_st)

    if n_cast:
        @pl.when(pl.program_id(0) < cast_steps)
        def _():
            for src, dst in zip(cast_in, cast_out):
                dst[...] = src[...].astype(dst.dtype)

    nb = proj_ref.shape[0]

    def cols(c0, width):
        return _rows([proj_ref[b, :, c0:c0 + width] for b in range(nb)])

    def emit(c0, y):
        for b in range(nb):
            o_ref[b, :, c0:c0 + y.shape[1]] = y[b * CHUNK:(b + 1) * CHUNK].astype(o_ref.dtype)

    y_a = _s5_chunks(cols(C_U, S5_WIDTH), wb_ref, wc_ref, pn_re_ref, pn_im_ref, pp_re_ref, pp_im_ref,
                     lam_ref, d_ref, wglu_ref, s5g_ref, ltri_ref, s5_st)
    emit(0, y_a)

    for b in range(nb):
        hg_xc[b, 8:8 + CHUNK, :] = proj_ref[b, :, C_HQ:C_HQ + HG_CONV_W]
    y_b = _hgrn2_chunks(hg_xc, cols(C_HGATE, HG_WIDTH), convw_ref, lbv_ref,
                        mall_ref, lvl_ref, rsel_ref, bones_ref, avg_ref, hgg_ref, hg_st)
    for b in range(nb):
        hg_xc[b, 0:8, :] = hg_xc[b, CHUNK:CHUNK + 8, :]
    emit(S5_WIDTH, y_b)

    o_c = [_ret_chunk(proj_ref[b, :, C_RQ:C_RQ + RET_KEY_WIDTH], proj_ref[b, :, C_RK:C_RK + RET_KEY_WIDTH],
                      proj_ref[b, :, C_RV:C_RV + RET_WIDTH], cos_ref[...], sin_ref[...],
                      intra_ref, inter_ref, tostate_ref, carry_ref, bmask_ref, ret_st.at[b])
           for b in range(nb)]
    y_c = _ret_norm_gate(_rows(o_c), cols(C_RGATE, RET_WIDTH), avg_ref, retg_ref)
    emit(S5_WIDTH + HG_WIDTH, y_c)


def _const_spec(a):
    nd = a.ndim
    return pl.BlockSpec(a.shape, lambda c, _nd=nd: (0,) * _nd)


def _cast_steps(n_chunks, arrays):
    for steps in range(n_chunks, 0, -1):
        if all(a.shape[0] % (steps * 16) == 0 for a in arrays):
            return steps
    raise ValueError("no slab split")


def _mixers(proj, cos, sin, consts, bsz, n_chunks, to_bf16=()):
    total = proj.shape[1]
    n_cast = len(to_bf16)
    cast_steps = _cast_steps(n_chunks, to_bf16) if n_cast else 0
    slab = lambda c: (jnp.minimum(c, cast_steps - 1), 0)
    cast_specs = [pl.BlockSpec((a.shape[0] // cast_steps, a.shape[1]), slab) for a in to_bf16]
    in_specs = [
        pl.BlockSpec((bsz, CHUNK, IN_COLS), lambda c: (0, c, 0)),
        pl.BlockSpec((CHUNK, 128), lambda c: (c, 0)),
        pl.BlockSpec((CHUNK, 128), lambda c: (c, 0)),
    ] + [_const_spec(a) for a in consts] + cast_specs
    outs = pl.pallas_call(
        functools.partial(_mixer_kernel, n_cast=n_cast, cast_steps=cast_steps),
        grid=(n_chunks,),
        in_specs=in_specs,
        out_specs=[pl.BlockSpec((bsz, CHUNK, D_MIX), lambda c: (0, c, 0))] + cast_specs,
        out_shape=[jax.ShapeDtypeStruct((bsz, total, D_MIX), BF16)]
        + [jax.ShapeDtypeStruct(a.shape, BF16) for a in to_bf16],
        scratch_shapes=[
            pltpu.VMEM((bsz, 2, S5_NSTATE), F32),
            pltpu.VMEM((bsz, CHUNK + 8, HG_CONV_W), F32),
            pltpu.VMEM((bsz, HG_WIDTH, HG_WIDTH), F32),
            pltpu.VMEM((bsz, RET_KEY_WIDTH, RET_WIDTH), F32),
        ],
        compiler_params=pltpu.CompilerParams(
            dimension_semantics=("arbitrary",), vmem_limit_bytes=VMEM_LIMIT),
        name="mixers",
    )(proj, cos, sin, *consts, *to_bf16)
    return outs[0], list(outs[1:])


def _route(hn, router_ref, ltri_ref, count_ref):
    r = router_ref[...]
    r_hi, r_lo = _split_bf16(r)
    h_hi, h_lo = _split_bf16(hn)
    tm = hn.shape[0]
    parts = _dot(jnp.concatenate([h_hi, h_lo], axis=0), jnp.concatenate([r_hi, r_lo], axis=1))
    logits = (parts[:tm, :ROUTE_LANES] + parts[:tm, ROUTE_LANES:]) + (parts[tm:, :ROUTE_LANES] + parts[tm:, ROUTE_LANES:])
    lane_i = lax.broadcasted_iota(jnp.int32, logits.shape, 1)
    lane = lane_i.astype(F32)
    neg = jnp.float32(-jnp.inf)
    logits = jnp.where(lane_i < N_EXPERTS, logits, neg)
    m1 = jnp.max(logits, axis=-1, keepdims=True)
    i1 = jnp.min(jnp.where(logits == m1, lane, float(ROUTE_LANES)), axis=-1, keepdims=True)
    rest = jnp.where(lane == i1, neg, logits)
    m2 = jnp.max(rest, axis=-1, keepdims=True)
    i2 = jnp.min(jnp.where(rest == m2, lane, float(ROUTE_LANES)), axis=-1, keepdims=True)
    e2 = jnp.exp(m2 - m1)
    g1 = 1.0 / (1.0 + e2)
    g2 = e2 / (1.0 + e2)
    onehot = jnp.where((lane == i1) | (lane == i2), 1.0, 0.0)
    before = _dot(ltri_ref[...], onehot.astype(BF16)) + count_ref[...]
    r1 = jnp.sum(jnp.where(lane == i1, before, 0.0), axis=-1, keepdims=True)
    r2 = jnp.sum(jnp.where(lane == i2, before, 0.0), axis=-1, keepdims=True)
    count_ref[...] += jnp.sum(onehot, axis=0, keepdims=True)
    out = jnp.where(lane == 0, i1, 0.0)
    out = jnp.where(lane == 1, i2, out)
    out = jnp.where(lane == 2, g1, out)
    out = jnp.where(lane == 3, g2, out)
    out = jnp.where(lane == 4, r1, out)
    return jnp.where(lane == 5, r2, out)


def _out_proj_kernel(mixed_ref, *refs, routed, n_stream, tiles_per_seq):
    stream, (w_ref, g_ref, *rest) = refs[:n_stream], refs[n_stream:]
    if routed:
        router_ref, ltri_ref, h1_ref, hn_ref, route_ref, fields_ref, count_ref = rest

        @pl.when(pl.program_id(0) == 0)
        def _():
            count_ref[...] = jnp.zeros_like(count_ref)
    else:
        h1_ref, hn_ref = rest
    h1 = _stream_tile(stream, tiles_per_seq) + _dot(mixed_ref[...], w_ref[...])
    h1_ref[...] = h1
    hn = h1 * lax.rsqrt(jnp.mean(h1 * h1, axis=-1, keepdims=True) + EPS) * g_ref[...]
    if routed:
        tm = hn.shape[0]
        for s in range(ROW_TILE):
            hn_ref[pl.ds(s, tm, stride=ROW_TILE), :] = hn[:, s * 128:(s + 1) * 128]
        slab = _route(hn, router_ref, ltri_ref, count_ref)
        route_ref[...] = slab
        fields_ref[...] = slab.T[0:ROUTE_FIELDS, :]
    else:
        hn_ref[...] = hn.astype(hn_ref.dtype)


def _out_proj(mixed, stream, w_bf16, g, bsz, total, router_pad=None):
    n, d = bsz * total, stream[0].shape[1]
    tm = _pick_tile(total, (TOKEN_TILE, 128))
    routed = router_pad is not None
    row = lambda i: (i, 0)
    fixed = lambda i: (0, 0)
    stream_specs, stream_operands = _stream_specs(stream, tm, total)
    in_specs = [pl.BlockSpec((tm, D_MIX), row)] + stream_specs + [
        pl.BlockSpec(w_bf16.shape, fixed), pl.BlockSpec((1, d), fixed)]
    out_specs = [pl.BlockSpec((tm, d), row), pl.BlockSpec((tm, d), row)]
    out_shape = [jax.ShapeDtypeStruct((n, d), F32), jax.ShapeDtypeStruct((n, d), BF16)]
    args = [mixed] + stream_operands + [w_bf16, g]
    if routed:
        assert d == ROW_TILE * 128
        out_specs[1] = pl.BlockSpec((tm * ROW_TILE, 128), row)
        out_shape[1] = jax.ShapeDtypeStruct((n * ROW_TILE, 128), F32)
        ltri = jnp.asarray(np.tril(np.ones((tm, tm), np.float32), -1), BF16)
        in_specs += [pl.BlockSpec(router_pad.shape, fixed), pl.BlockSpec((tm, tm), fixed)]
        out_specs += [pl.BlockSpec((tm, ROUTE_LANES), row), pl.BlockSpec((ROUTE_FIELDS, tm), lambda i: (0, i)),
                      pl.BlockSpec((1, ROUTE_LANES), fixed)]
        out_shape += [jax.ShapeDtypeStruct((n, ROUTE_LANES), F32),
                      jax.ShapeDtypeStruct((ROUTE_FIELDS, n), F32),
                      jax.ShapeDtypeStruct((1, ROUTE_LANES), F32)]
        args += [router_pad, ltri]
    return pl.pallas_call(
        functools.partial(_out_proj_kernel, routed=routed, n_stream=len(stream), tiles_per_seq=total // tm),
        grid=(n // tm,),
        in_specs=in_specs,
        out_specs=out_specs,
        out_shape=out_shape,
        compiler_params=pltpu.CompilerParams(
            dimension_semantics=("arbitrary",), vmem_limit_bytes=VMEM_LIMIT),
        name="out_proj_routed" if routed else "out_proj",
    )(*args)


HIDDEN_CHUNK = 256


def _swiglu_hidden(x_ref, w1_ref, w3_ref, hid_ref):
    for c0 in range(0, hid_ref.shape[1], HIDDEN_CHUNK):
        cols = slice(c0, c0 + HIDDEN_CHUNK)
        a = _dot(x_ref[...], w1_ref[:, cols])
        b = _dot(x_ref[...], w3_ref[:, cols])
        hid_ref[:, cols] = (a * _sigmoid(a) * b).astype(BF16)


def _ffn_kernel(hn_ref, h1_ref, w1_ref, w3_ref, w2_ref, o_ref, hid_ref):
    _swiglu_hidden(hn_ref, w1_ref, w3_ref, hid_ref)
    o_ref[...] = h1_ref[...] + _dot(hid_ref[...], w2_ref[...])


def _ffn(hn, h1, w1, w3, w2):
    n, d = h1.shape
    dff = w1.shape[1]
    assert dff % HIDDEN_CHUNK == 0
    tm = _pick_tile(n, (TOKEN_TILE, 128))
    resident = lambda a: pl.BlockSpec(a.shape, lambda i: (0, 0), pipeline_mode=pl.Buffered(1))
    return pl.pallas_call(
        _ffn_kernel,
        grid=(n // tm,),
        in_specs=[
            pl.BlockSpec((tm, d), lambda i: (i, 0)),
            pl.BlockSpec((tm, d), lambda i: (i, 0)),
            resident(w1), resident(w3), resident(w2),
        ],
        out_specs=pl.BlockSpec((tm, d), lambda i: (i, 0)),
        out_shape=jax.ShapeDtypeStruct((n, d), F32),
        scratch_shapes=[pltpu.VMEM((tm, dff), BF16)],
        compiler_params=pltpu.CompilerParams(
            dimension_semantics=("arbitrary",), vmem_limit_bytes=VMEM_LIMIT),
        name="ffn_dense",
    )(hn, h1, w1, w3, w2)


def _moe_kernel(te_ref, nv_ref, src_ref, tok_ref, hn_hbm, w1_ref, w3_ref, w2_ref, o_ref,
                xs_ref, xb_ref, hid_ref, acc_ref, sem, *, tm):
    i, f = pl.program_id(0), pl.program_id(1)
    nf = pl.num_programs(1)
    n_valid = nv_ref[0]
    share = tm // MOE_F_STEPS

    def row_copy(tile, r):
        tok = tok_ref[src_ref[tile] + r]
        return pltpu.make_async_copy(hn_hbm.at[pl.ds(pl.multiple_of(tok * ROW_TILE, ROW_TILE), ROW_TILE)],
                                     xs_ref.at[pl.ds(pl.multiple_of(r * ROW_TILE, ROW_TILE), ROW_TILE)], sem)

    @pl.when((i == 0) & (f == 0))
    def _():
        def body(r, carry):
            row_copy(0, r).start()
            return carry
        lax.fori_loop(0, tm, body, 0)

    @pl.when((f == 0) & (i <= n_valid))
    def _():
        pltpu.make_async_copy(xs_ref, xs_ref, sem).wait()
        for s in range(ROW_TILE):
            xb_ref[:, s * 128:(s + 1) * 128] = xs_ref[pl.ds(s, tm, stride=ROW_TILE), :].astype(BF16)

    @pl.when(i < n_valid)
    def _():
        for r in range(share):
            row_copy(i + 1, f * share + r).start(priority=r % MOE_DMA_QUEUES)
        _swiglu_hidden(xb_ref, w1_ref.at[0], w3_ref.at[0], hid_ref)
        part = _dot(hid_ref[...], w2_ref[0])

        @pl.when(f == 0)
        def _():
            acc_ref[...] = part

        @pl.when(f > 0)
        def _():
            acc_ref[...] += part

    @pl.when(f == nf - 1)
    def _():
        o_ref[...] = jnp.where(i < n_valid, acc_ref[...], 0.0).astype(o_ref.dtype)


def _live_f(i, f, nv):
    return jnp.where(i < nv[0], f, 0)


def _moe_grouped(tile_expert, n_valid, tile_src, sorted_tok, hn, w1, w3, w2, tm):
    d = ROW_TILE * hn.shape[1]
    n_tiles = tile_expert.shape[0]
    dff = w1.shape[2]
    tf = dff // MOE_F_STEPS
    grid_spec = pltpu.PrefetchScalarGridSpec(
        num_scalar_prefetch=4,
        grid=(n_tiles, MOE_F_STEPS),
        in_specs=[
            pl.BlockSpec(memory_space=pl.ANY),
            pl.BlockSpec((1, d, tf), lambda i, f, te, nv, *_: (te[i], 0, _live_f(i, f, nv))),
            pl.BlockSpec((1, d, tf), lambda i, f, te, nv, *_: (te[i], 0, _live_f(i, f, nv))),
            pl.BlockSpec((1, tf, d), lambda i, f, te, nv, *_: (te[i], _live_f(i, f, nv), 0)),
        ],
        out_specs=pl.BlockSpec((tm, d), lambda i, f, *_: (i, 0)),
        scratch_shapes=[pltpu.VMEM((tm * ROW_TILE, 128), F32), pltpu.VMEM((tm, d), BF16),
                        pltpu.VMEM((tm, tf), BF16), pltpu.VMEM((tm, d), F32), pltpu.SemaphoreType.DMA(())],
    )
    return pl.pallas_call(
        functools.partial(_moe_kernel, tm=tm),
        grid_spec=grid_spec,
        out_shape=jax.ShapeDtypeStruct((n_tiles * tm, d), BF16),
        compiler_params=pltpu.CompilerParams(
            dimension_semantics=("arbitrary", "arbitrary"), vmem_limit_bytes=VMEM_LIMIT),
        name="moe_grouped",
    )(tile_expert, n_valid, tile_src, sorted_tok, hn, w1, w3, w2)


def _moe_dispatch(fields, counts_row, tm):
    n = fields.shape[1]
    n_slots = n * TOP_K
    n_tiles = (n_slots + N_EXPERTS * tm + tm - 1) // tm
    counts = counts_row[0, :N_EXPERTS].astype(jnp.int32)
    padded = ((counts + tm - 1) // tm) * tm
    pend = jnp.cumsum(padded)
    pstart = pend - padded
    gstart = jnp.cumsum(counts) - counts
    eid = fields[0:TOP_K].astype(jnp.int32)
    rank = fields[2 * TOP_K:3 * TOP_K].astype(jnp.int32)
    is_e = [eid == e for e in range(N_EXPERTS)]
    lookup = lambda table: sum(jnp.where(m, table[e], 0) for e, m in enumerate(is_e))
    pos = lookup(pstart) + rank
    tok_bits = max(1, (n - 1).bit_length())
    assert n_slots < (1 << (31 - tok_bits))
    tok = jnp.broadcast_to(jnp.arange(n, dtype=jnp.int32)[None, :], (TOP_K, n))
    packed = jnp.sort((((lookup(gstart) + rank) << tok_bits) | tok).reshape(-1))
    sorted_tok = packed & ((1 << tok_bits) - 1)
    sorted_tok = jnp.concatenate([sorted_tok, jnp.zeros((tm,), jnp.int32)])
    tile_start = jnp.arange(n_tiles, dtype=jnp.int32) * tm
    tile_expert = jnp.minimum(jnp.sum((tile_start[:, None] >= pend[None, :]).astype(jnp.int32), axis=1),
                              N_EXPERTS - 1)
    tile_src = jnp.clip(tile_start - (pstart - gstart)[tile_expert], 0, n_slots)
    n_valid = (pend[-1] // tm).astype(jnp.int32).reshape(1)
    return sorted_tok, tile_src, pos, tile_expert, n_valid


def _final_kernel(h1_ref, y0_ref, y1_ref, route_ref, g_ref, o_ref):
    g0 = route_ref[:, TOP_K:TOP_K + 1]
    g1 = route_ref[:, TOP_K + 1:TOP_K + 2]
    h = h1_ref[...] + (g0 * y0_ref[...].astype(F32) + g1 * y1_ref[...].astype(F32))
    o_ref[...] = h * lax.rsqrt(jnp.mean(h * h, axis=-1, keepdims=True) + EPS) * g_ref[...]


def _final(h1, y01, route, g, bsz, total):
    n, d = h1.shape
    seq = total - CHUNK
    tm = _pick_tile(seq, (1024, 128))
    per_seq = seq // tm
    rows = lambda width, base=0: pl.BlockSpec(
        (pl.Element(tm), pl.Element(width)),
        lambda b, k: (pl.multiple_of(base + b * total + CHUNK + k * tm, CHUNK), 0))
    out = pl.pallas_call(
        _final_kernel,
        grid=(bsz, per_seq),
        in_specs=[rows(d), rows(d), rows(d, n), rows(ROUTE_LANES), pl.BlockSpec((1, d), lambda b, k: (0, 0))],
        out_specs=pl.BlockSpec((tm, d), lambda b, k: (b * per_seq + k, 0)),
        out_shape=jax.ShapeDtypeStruct((bsz * seq, d), F32),
        compiler_params=pltpu.CompilerParams(
            dimension_semantics=("arbitrary", "arbitrary"), vmem_limit_bytes=VMEM_LIMIT),
        name="final_norm",
    )(h1, y01, y01, route, g)
    return out.reshape(bsz, seq, d)


def _relayout_in_cols(w):
    d = w.shape[0]

    def halves_first(block):
        return block.reshape(d, RET_HEADS, 2, RET_DK // 2).transpose(0, 2, 1, 3).reshape(d, RET_KEY_WIDTH)

    return jnp.concatenate([w[:, :C_RQ], halves_first(w[:, C_RQ:C_RK]), halves_first(w[:, C_RK:C_RV]),
                            w[:, C_RV:]], axis=1)


def _hg_tables(nb):
    c = CHUNK
    i = np.arange(c)[:, None]
    t = np.arange(c)[None, :]
    blocks, masks, rsel = [], [], []
    for lvl in range(HG_LEVELS):
        s = 1 << lvl
        r = (i // (2 * s)) * (2 * s) + s - 1
        right = i > r
        m = np.where(right, (t > r) & (t <= i), (t > i) & (t <= r))
        blocks.append(m)
        j = t
        same = (i // (2 * s)) == (j // (2 * s))
        mk = same & right & (j <= r)
        masks.append(np.concatenate([mk, mk], axis=1))
        if s < 8:
            rsel.append(np.broadcast_to(right, (c, HG_WIDTH)))
    blocks.append(t <= i)
    blocks.append(t > i)
    eye = np.eye(nb, dtype=np.float32)
    mall = np.concatenate([np.kron(eye, blk.astype(np.float32)) for blk in blocks], axis=0)
    lvl = np.stack(masks, axis=0).astype(np.float32)
    ch = np.arange(HG_WIDTH)
    bones = (ch[:, None] // HG_DK == ch[None, :] // HG_DK).astype(np.float32)
    rsel = np.stack([np.tile(r, (nb, 1)) for r in rsel], axis=0).astype(np.float32)
    return mall, lvl, rsel, bones


def _ret_tables():
    f32 = jnp.float32
    log_gamma = jnp.log1p(-jnp.power(2.0, -5.0 - jnp.arange(RET_HEADS, dtype=f32)))
    n = jnp.arange(CHUNK, dtype=f32)
    lg = log_gamma[:, None]
    causal = jnp.tril(jnp.ones((CHUNK, CHUNK), dtype=bool))
    intra = jnp.exp(jnp.where(causal[None], (n[:, None] - n[None, :])[None] * lg[:, :, None], -jnp.inf))
    scale = RET_DK ** -0.5
    inter = jnp.exp((n[None, :] + 1.0) * lg)
    to_state = jnp.exp((CHUNK - 1.0 - n[None, :]) * lg)
    carry = jnp.exp(CHUNK * lg)[:, 0]
    head_of_v = np.arange(RET_WIDTH) // RET_DV
    head_of_k = (np.arange(RET_KEY_WIDTH) % (RET_KEY_WIDTH // 2)) // (RET_DK // 2)
    inter_t = (inter * scale).T[:, head_of_v]
    to_state_t = to_state.T[:, head_of_k]
    carry_row = carry[head_of_v][None, :]
    bmask = jnp.asarray((head_of_k[:, None] == head_of_v[None, :]).astype(np.float32))
    intra_pairs = (intra * scale).reshape(RET_HEADS // 2, 2, CHUNK, CHUNK)
    intra_pairs = jnp.concatenate([intra_pairs[:, 0], intra_pairs[:, 1]], axis=2)
    return intra_pairs, inter_t, to_state_t, carry_row, bmask


def _s5_tables(lam_re, lam_im, b_re, b_im, c_re, c_im, d_skip, log_step):
    f32 = jnp.float32
    lam = lax.complex(lam_re.astype(f32), lam_im.astype(f32))
    step = jnp.exp(log_step.astype(f32))[:, None]
    lam_dt = lam * step
    lam_bar = jnp.exp(lam_dt)
    b_bar = ((lam_bar - 1.0) / lam)[..., None] * lax.complex(b_re.astype(f32), b_im.astype(f32))
    eye = jnp.eye(S5_NGROUPS, dtype=f32)
    wb_re = jnp.einsum('gph,gk->ghkp', jnp.real(b_bar), eye).reshape(S5_WIDTH, S5_NSTATE)
    wb_im = jnp.einsum('gph,gk->ghkp', jnp.imag(b_bar), eye).reshape(S5_WIDTH, S5_NSTATE)
    wb = jnp.concatenate([wb_re, wb_im], axis=1)
    wc_re = jnp.einsum('ghp,gk->gpkh', c_re.astype(f32), eye).reshape(S5_NSTATE, S5_WIDTH)
    wc_im = jnp.einsum('ghp,gk->gpkh', c_im.astype(f32), eye).reshape(S5_NSTATE, S5_WIDTH)
    wc = jnp.concatenate([wc_re, -wc_im], axis=0)
    t = jnp.arange(CHUNK, dtype=f32)[:, None, None]
    pp = jnp.exp(lam_dt[None] * t).reshape(CHUNK, S5_NSTATE)
    pn = jnp.exp(-lam_dt[None] * t).reshape(CHUNK, S5_NSTATE)
    lam_rows = jnp.stack([jnp.real(lam_bar).reshape(-1), jnp.imag(lam_bar).reshape(-1)], axis=0)
    return (wb.astype(BF16), wc.astype(BF16), jnp.real(pn), jnp.imag(pn), jnp.real(pp), jnp.imag(pp),
            lam_rows, d_skip.astype(f32).reshape(1, S5_WIDTH))


def kernel(x, meta_tokens, norm_mix_g, w_in, s5_lam_re, s5_lam_im, s5_b_re, s5_b_im, s5_c_re, s5_c_im, s5_d, s5_log_step, s5_w_glu, s5_out_g, hg_conv_w, hg_lb_param, hg_out_g, ret_out_g, w_out, norm_ffn_g, ffn_w1, ffn_w3, ffn_w2, moe_router, moe_w1, moe_w3, moe_w2, final_norm_g):
    bsz, seq_len, d = x.shape
    depth = w_in.shape[0]
    total = seq_len + CHUNK
    n_chunks = total // CHUNK
    n = bsz * total

    meta = jnp.broadcast_to(meta_tokens.astype(F32)[None], (bsz, N_META, d))
    tm0 = _pick_tile(total, (TOKEN_TILE, 128))
    head = jnp.concatenate([jnp.zeros((bsz, PAD, d), F32), meta, x[:, :tm0 - CHUNK].astype(F32)], axis=1)
    stream = (x.astype(F32).reshape(bsz * seq_len, d), head)

    pos = (jnp.arange(total) - PAD).astype(F32)
    half = RET_DK // 2
    inv_freq = ROPE_BASE ** (-jnp.arange(half, dtype=F32) / half)
    ang = pos[:, None] * inv_freq[None, :]
    cos_t = jnp.tile(jnp.cos(ang), (1, RET_HEADS))
    sin_t = jnp.tile(jnp.sin(ang), (1, RET_HEADS))

    lb_all = jnp.cumsum(jax.nn.softmax(hg_lb_param.astype(F32), axis=0), axis=0)
    lb_all = lb_all - lb_all[0]

    mall_np, lvl_np, rsel_np, bones_np = _hg_tables(bsz)
    mall = jnp.asarray(mall_np, BF16)
    lvl = jnp.asarray(lvl_np, F32)
    rsel = jnp.asarray(rsel_np, F32)
    bones = jnp.asarray(bones_np, BF16)
    avg = jnp.asarray(bones_np / HG_DK, BF16)
    ltri = jnp.asarray(np.kron(np.eye(bsz, dtype=np.float32),
                               np.tril(np.ones((CHUNK, CHUNK), np.float32))), BF16)
    intra, inter_t, to_state_t, carry_row, bmask = _ret_tables()

    out = None
    for l in range(depth):
        w_in_l = _relayout_in_cols(w_in[l].astype(BF16))
        proj = _in_proj(stream, norm_mix_g[l].astype(F32).reshape(1, d), w_in_l, bsz, total)

        s5c = _s5_tables(s5_lam_re[l], s5_lam_im[l], s5_b_re[l], s5_b_im[l], s5_c_re[l], s5_c_im[l],
                         s5_d[l], s5_log_step[l])
        lb = lb_all[l][None, :]
        lbv = jnp.concatenate([jnp.log(lb), jnp.log1p(-lb), 1.0 - lb], axis=0)
        consts = list(s5c) + [
            s5_w_glu[l].astype(BF16), s5_out_g[l].astype(F32).reshape(1, -1), ltri,
            hg_conv_w[l].astype(F32), lbv, mall, lvl, rsel, bones, avg, hg_out_g[l].astype(F32).reshape(1, -1),
            intra, inter_t, to_state_t, carry_row, bmask, ret_out_g[l].astype(F32).reshape(1, -1),
        ]
        j = l // 2
        if l % 2 == 0:
            ffn_f32 = [ffn_w1[j], ffn_w3[j], ffn_w2[j]]
        else:
            dff_e = moe_w1.shape[-1]
            ffn_f32 = [moe_w1[j].reshape(N_EXPERTS * d, dff_e), moe_w3[j].reshape(N_EXPERTS * d, dff_e),
                       moe_w2[j].reshape(N_EXPERTS * dff_e, d)]
        mixed, ffn_bf16 = _mixers(proj.reshape(bsz, total, IN_COLS), cos_t, sin_t, consts, bsz, n_chunks,
                                  to_bf16=[a.astype(F32) for a in ffn_f32])
        mixed = mixed.reshape(n, D_MIX)

        g_ffn = norm_ffn_g[l].astype(F32).reshape(1, d)
        w_out_l = w_out[l].astype(BF16)
        if l % 2 == 0:
            h1, hn = _out_proj(mixed, stream, w_out_l, g_ffn, bsz, total)
            stream = (_ffn(hn, h1, *ffn_bf16),)
            y01 = None
        else:
            router_pad = jnp.zeros((d, ROUTE_LANES), F32).at[:, :N_EXPERTS].set(moe_router[j].astype(F32))
            h1, hn, route, fields, counts_row = _out_proj(mixed, stream, w_out_l, g_ffn, bsz, total, router_pad)
            tm = MOE_TILE
            sorted_tok, tile_src, pos_of_slot, tile_expert, n_valid = _moe_dispatch(fields, counts_row, tm)
            e_w1, e_w3, e_w2 = (ffn_bf16[0].reshape(N_EXPERTS, d, dff_e), ffn_bf16[1].reshape(N_EXPERTS, d, dff_e),
                                ffn_bf16[2].reshape(N_EXPERTS, dff_e, d))
            ys = _moe_grouped(tile_expert, n_valid, tile_src, sorted_tok, hn, e_w1, e_w3, e_w2, tm)
            y01 = ys.at[pos_of_slot.reshape(-1)].get(mode='promise_in_bounds')
            if l < depth - 1:
                stream = (h1 + (route[:, TOP_K:TOP_K + 1] * y01[:n].astype(F32)
                                + route[:, TOP_K + 1:TOP_K + 2] * y01[n:].astype(F32)),)

        if l == depth - 1:
            if y01 is None:
                y01 = jnp.zeros((TOP_K * n, d), BF16)
                h1 = stream[0]
                route = jnp.zeros((n, ROUTE_LANES), F32)
            out = _final(h1, y01, route, final_norm_g.astype(F32).reshape(1, d), bsz, total)

    return out.astype(x.dtype)
```

```python
import functools
import math

import jax
import jax.numpy as jnp
import numpy as np
from jax import lax
from jax.experimental import pallas as pl
from jax.experimental.pallas import tpu as pltpu

F32 = jnp.float32
BF16 = jnp.bfloat16

CHUNK = 128
N_META = 16
PAD = CHUNK - N_META
EPS = 1e-6

S5_WIDTH = 256
S5_GROUP = 16
S5_NGROUPS = 16
S5_STATE = 64
S5_NSTATE = S5_NGROUPS * S5_STATE

HG_HEADS = 4
HG_DK = 64
HG_WIDTH = 256
CONV_K = 4
HG_CONV_W = 3 * HG_WIDTH
HG_LEVELS = 7

RET_HEADS = 8
RET_DK = 32
RET_DV = 64
RET_KEY_WIDTH = 256
RET_WIDTH = 512
ROPE_BASE = 10000.0

D_MIX = 1024
IN_COLS = 2816
C_U, C_HQ, C_HGATE, C_RQ, C_RK, C_RV, C_RGATE = 0, 256, 1024, 1280, 1536, 1792, 2304
M_U, M_Q, M_LOGF, M_KK, M_V, M_HGATE, M_QR, M_KR, M_RV, M_RGATE = (
    0, 256, 512, 768, 1024, 1280, 1536, 1792, 2048, 2560)
MIX_COLS = 3072

N_EXPERTS = 8
TOP_K = 2
ROUTE_LANES = 128
ROUTE_FIELDS = 8
MOE_F_STEPS = 2
MOE_TILE = 512
ROW_TILE = 8
MOE_DMA_QUEUES = 2

VMEM_LIMIT = 56 * 1024 * 1024


def _sigmoid(x):
    return 1.0 / (1.0 + jnp.exp(-x))


def _split_bf16(x):
    hi = x.astype(BF16)
    lo = (x - hi.astype(F32)).astype(BF16)
    return hi, lo


def _pick_tile(n, candidates):
    for t in candidates:
        if n % t == 0:
            return t
    raise ValueError(f"no tile in {candidates} divides {n}")


def _dot(a, b):
    return jnp.dot(a, b, preferred_element_type=F32)


def _dot_nt(a, b):
    return lax.dot_general(a, b, (((1,), (1,)), ((), ())), preferred_element_type=F32)


def _dot_tn(a, b):
    return lax.dot_general(a, b, (((0,), (0,)), ((), ())), preferred_element_type=F32)


TOKEN_TILE = 640


def _stream_specs(stream, tm, total):
    d = stream[0].shape[1]
    if len(stream) == 1:
        return [pl.BlockSpec((tm, d), lambda i: (i, 0))], list(stream)
    per, seq = total // tm, total - CHUNK
    x_rows = lambda i: (pl.multiple_of((i // per) * seq + jnp.maximum((i % per) * tm - CHUNK, 0), CHUNK), 0)
    return ([pl.BlockSpec((pl.Element(tm), pl.Element(d)), x_rows),
             pl.BlockSpec((1, tm, d), lambda i: (i // per, 0, 0))], list(stream))


def _stream_tile(refs, tiles_per_seq):
    if len(refs) == 1:
        return refs[0][...]
    x_ref, head_ref = refs
    return jnp.where(pl.program_id(0) % tiles_per_seq == 0, head_ref[0], x_ref[...])


def _in_proj_kernel(*refs, tm, tiles_per_seq):
    *stream, g_ref, w_ref, convw_ref, lbv_ref, cos_ref, sin_ref, o_ref, xc_ref = refs
    t = pl.program_id(0) % tiles_per_seq
    x = _stream_tile(stream, tiles_per_seq)
    y = x * lax.rsqrt(jnp.mean(x * x, axis=-1, keepdims=True) + EPS) * g_ref[...]
    proj = _dot(y.astype(BF16), w_ref[...])
    rows = t * tm + lax.broadcasted_iota(jnp.int32, (tm, 1), 0)
    proj = jnp.where(rows >= PAD, proj, 0.0)
    o_ref[:, M_U:M_U + S5_WIDTH] = proj[:, C_U:C_U + S5_WIDTH]

    @pl.when(t == 0)
    def _():
        xc_ref[0:8, :] = jnp.zeros((8, HG_CONV_W), F32)
    xc_ref[8:8 + tm, :] = proj[:, C_HQ:C_HQ + HG_CONV_W]
    conv = None
    for i in range(CONV_K):
        term = xc_ref[pl.ds(8 - (CONV_K - 1) + i, tm), :] * convw_ref[i:i + 1, :]
        conv = term if conv is None else conv + term
    xc_ref[0:8, :] = xc_ref[tm:tm + 8, :]
    w = HG_WIDTH
    cq, cf = conv[:, :w], conv[:, w:2 * w]
    log_lb, log_1m_lb, one_m_lb = lbv_ref[0:1, :], lbv_ref[1:2, :], lbv_ref[2:3, :]
    log_sig = jnp.minimum(cf, 0.0) - jnp.log(1.0 + jnp.exp(-jnp.abs(cf)))
    b_ = log_1m_lb + log_sig
    o_ref[:, M_Q:M_Q + w] = cq * _sigmoid(cq)
    o_ref[:, M_LOGF:M_LOGF + w] = jnp.maximum(log_lb, b_) + jnp.log(1.0 + jnp.exp(-jnp.abs(log_lb - b_)))
    o_ref[:, M_KK:M_KK + w] = one_m_lb * _sigmoid(-cf)
    o_ref[:, M_V:M_V + w] = conv[:, 2 * w:]
    hgate = proj[:, C_HGATE:C_HGATE + w]
    o_ref[:, M_HGATE:M_HGATE + w] = hgate * _sigmoid(hgate)

    hw = RET_KEY_WIDTH // 2
    cos, sin = cos_ref[...], sin_ref[...]
    for c_in, c_out in ((C_RQ, M_QR), (C_RK, M_KR)):
        t1, t2 = proj[:, c_in:c_in + hw], proj[:, c_in + hw:c_in + 2 * hw]
        o_ref[:, c_out:c_out + hw] = t1 * cos - t2 * sin
        o_ref[:, c_out + hw:c_out + 2 * hw] = t1 * sin + t2 * cos
    o_ref[:, M_RV:M_RV + RET_WIDTH] = proj[:, C_RV:C_RV + RET_WIDTH]
    rgate = proj[:, C_RGATE:C_RGATE + RET_WIDTH]
    o_ref[:, M_RGATE:M_RGATE + RET_WIDTH] = rgate * _sigmoid(rgate)


def _in_proj(stream, g, w_bf16, convw, lbv, cos, sin, bsz, total):
    n, d = bsz * total, stream[0].shape[1]
    tm = _pick_tile(total, (TOKEN_TILE, 128))
    per = total // tm
    specs, operands = _stream_specs(stream, tm, total)
    fixed = lambda a: pl.BlockSpec(a.shape, lambda i: (0, 0))
    return pl.pallas_call(
        functools.partial(_in_proj_kernel, tm=tm, tiles_per_seq=per),
        grid=(n // tm,),
        in_specs=specs + [
            pl.BlockSpec((1, d), lambda i: (0, 0)),
            pl.BlockSpec(w_bf16.shape, lambda i: (0, 0), pipeline_mode=pl.Buffered(1)),
            fixed(convw), fixed(lbv),
            pl.BlockSpec((tm, 128), lambda i: (i % per, 0)),
            pl.BlockSpec((tm, 128), lambda i: (i % per, 0)),
        ],
        out_specs=pl.BlockSpec((tm, MIX_COLS), lambda i: (i, 0)),
        out_shape=jax.ShapeDtypeStruct((n, MIX_COLS), F32),
        scratch_shapes=[pltpu.VMEM((tm + 8, HG_CONV_W), F32)],
        compiler_params=pltpu.CompilerParams(
            dimension_semantics=("arbitrary",), vmem_limit_bytes=VMEM_LIMIT),
        name="in_proj",
    )(*operands, g, w_bf16, convw, lbv, cos, sin)


def _rows(parts):
    return parts[0] if len(parts) == 1 else jnp.concatenate(parts, axis=0)


def _s5_chunks(u, wb_ref, wc_ref, pn_re_ref, pn_im_ref, pp_re_ref, pp_im_ref, lam_ref,
               d_ref, wglu_ref, g_ref, ltri_ref, st_ref):
    ns = S5_NSTATE
    nb = st_ref.shape[0]
    bu = _dot(u.astype(BF16), wb_ref[...])
    pn_re, pn_im = pn_re_ref[...], pn_im_ref[...]
    w_re, w_im = [], []
    for b in range(nb):
        bu_re = bu[b * CHUNK:(b + 1) * CHUNK, :ns]
        bu_im = bu[b * CHUNK:(b + 1) * CHUNK, ns:]
        w_re.append(pn_re * bu_re - pn_im * bu_im)
        w_im.append(pn_re * bu_im + pn_im * bu_re)
    ltri = ltri_ref[...]
    c_re = _dot(ltri, _rows(w_re).astype(BF16))
    c_im = _dot(ltri, _rows(w_im).astype(BF16))
    lam_re, lam_im = lam_ref[0:1, :], lam_ref[1:2, :]
    pp_re, pp_im = pp_re_ref[...], pp_im_ref[...]
    st_re, st_im = [], []
    for b in range(nb):
        s_re, s_im = st_ref[b, 0:1, :], st_ref[b, 1:2, :]
        z_re = c_re[b * CHUNK:(b + 1) * CHUNK] + (lam_re * s_re - lam_im * s_im)
        z_im = c_im[b * CHUNK:(b + 1) * CHUNK] + (lam_re * s_im + lam_im * s_re)
        t_re = pp_re * z_re - pp_im * z_im
        t_im = pp_re * z_im + pp_im * z_re
        st_ref[b, 0:1, :] = t_re[CHUNK - 1:CHUNK, :]
        st_ref[b, 1:2, :] = t_im[CHUNK - 1:CHUNK, :]
        st_re.append(t_re)
        st_im.append(t_im)
    y = (_dot(_rows(st_re).astype(BF16), wc_ref[0:ns, :]) + _dot(_rows(st_im).astype(BF16), wc_ref[ns:2 * ns, :])
         + d_ref[...] * u)
    y = 0.5 * y * (1.0 + jnp.tanh(math.sqrt(2.0 / math.pi) * (y + 0.044715 * (y * y * y))))
    y = y * _sigmoid(_dot(y.astype(BF16), wglu_ref[...]))
    return y * lax.rsqrt(jnp.mean(y * y, axis=-1, keepdims=True) + EPS) * g_ref[...]


def _hgrn2_chunks(q, logf, kk, v, gate, mall_ref, lvl_ref, rsel_ref, bones_ref, avg_ref, g_ref, st_ref):
    w = HG_WIDTH
    nb = st_ref.shape[0]
    rows = nb * CHUNK
    hi, lo = _split_bf16(logf)

    lvl_sums = _dot(mall_ref[0:HG_LEVELS * rows, :], hi)
    tail = mall_ref[HG_LEVELS * rows:(HG_LEVELS + 2) * rows, :]
    cum_suf = _dot(tail, hi) + _dot(tail, lo)
    g_cum, g_suffix = cum_suf[:rows], cum_suf[rows:]

    lane = lax.broadcasted_iota(jnp.int32, (1, 2 * HG_DK), 1)
    head_sel = [jnp.where(lane < HG_DK, 1.0, 0.0), jnp.where(lane >= HG_DK, 1.0, 0.0)]
    n_pairs = HG_HEADS // 2
    scores = [[None] * n_pairs for _ in range(nb)]
    for lvl in range(HG_LEVELS):
        s = 1 << lvl
        e = jnp.exp(lvl_sums[lvl * rows:(lvl + 1) * rows])
        if s >= 8:
            qk = jnp.concatenate([(q if (r // s) % 2 else kk)[r:r + s] for r in range(0, rows, s)], axis=0)
        else:
            qk = jnp.where(rsel_ref[lvl] > 0.5, q, kk)
        x = qk * e
        m = lvl_ref[lvl] > 0.5
        for b in range(nb):
            for p in range(n_pairs):
                xp = x[b * CHUNK:(b + 1) * CHUNK, p * 128:(p + 1) * 128]
                rhs = jnp.concatenate([xp * head_sel[0], xp * head_sel[1]], axis=0).astype(BF16)
                sc = _dot_nt(xp.astype(BF16), rhs)
                scores[b][p] = jnp.where(m, sc, 0.0 if scores[b][p] is None else scores[b][p])
    v_bf = v.astype(BF16)
    bones = bones_ref[...]
    qg = (q * jnp.exp(g_cum)).astype(BF16)
    kd = (kk * jnp.exp(g_suffix)).astype(BF16)
    o_rows = []
    for b in range(nb):
        sl = slice(b * CHUNK, (b + 1) * CHUNK)
        o_parts = []
        for p in range(n_pairs):
            vp = v[sl, p * 128:(p + 1) * 128]
            vv = jnp.concatenate([vp * head_sel[0], vp * head_sel[1]], axis=0).astype(BF16)
            o_parts.append(_dot(scores[b][p].astype(BF16), vv))
        st = st_ref[b]
        o_rows.append(jnp.concatenate(o_parts, axis=1) + _dot_nt(qg[sl], st.astype(BF16)))
        upd = _dot_tn(v_bf[sl], kd[sl]) * bones.astype(F32)
        st_ref[b] = st * jnp.exp(g_cum[(b + 1) * CHUNK - 1:(b + 1) * CHUNK, :]) + upd
    o = _rows(o_rows) + _dot((q * kk).astype(BF16), bones) * v
    ms = _dot((o * o).astype(BF16), avg_ref[...])
    return o * lax.rsqrt(ms + EPS) * g_ref[...] * gate


def _ret_chunk(qr, kr, v, intra_ref, inter_ref, tostate_ref, carry_ref, bmask_ref, st_ref):
    hw = RET_KEY_WIDTH // 2
    qr_bf, v_bf = qr.astype(BF16), v.astype(BF16)
    lane_k = lax.broadcasted_iota(jnp.int32, (1, RET_KEY_WIDTH), 1) % hw
    lane_v = lax.broadcasted_iota(jnp.int32, (1, 2 * RET_DV), 1)
    v_sel = [jnp.where(lane_v < RET_DV, 1.0, 0.0), jnp.where(lane_v >= RET_DV, 1.0, 0.0)]
    half = RET_DK // 2
    o_parts = []
    for p in range(RET_HEADS // 2):
        sel = [jnp.where((lane_k >= h * half) & (lane_k < (h + 1) * half), 1.0, 0.0)
               for h in (2 * p, 2 * p + 1)]
        rhs = jnp.concatenate([kr * sel[0], kr * sel[1]], axis=0).astype(BF16)
        sc = (_dot_nt(qr_bf, rhs) * intra_ref[p]).astype(BF16)
        vp = v[:, p * 128:(p + 1) * 128]
        vv = jnp.concatenate([vp * v_sel[0], vp * v_sel[1]], axis=0).astype(BF16)
        o_parts.append(_dot(sc, vv))
    o = jnp.concatenate(o_parts, axis=1)
    st = st_ref[...]
    o = o + _dot(qr_bf, st.astype(BF16)) * inter_ref[...]
    kd = (kr * tostate_ref[...]).astype(BF16)
    st_ref[...] = st * carry_ref[...] + _dot_tn(kd, v_bf) * bmask_ref[...]
    return o


def _ret_norm_gate(o, gate, avg_ref, g_ref):
    avg = avg_ref[...]
    outs = []
    for s in range(RET_WIDTH // 256):
        os_ = o[:, s * 256:(s + 1) * 256]
        c = os_ - _dot(os_.astype(BF16), avg)
        outs.append(c * lax.rsqrt(_dot((c * c).astype(BF16), avg) + EPS))
    return jnp.concatenate(outs, axis=1) * g_ref[...] * gate


def _mixer_kernel(proj_ref,
                  wb_ref, wc_ref, pn_re_ref, pn_im_ref, pp_re_ref, pp_im_ref, lam_ref, d_ref,
                  wglu_ref, s5g_ref, ltri_ref,
                  mall_ref, lvl_ref, rsel_ref, bones_ref, avg_ref, hgg_ref,
                  intra_ref, inter_ref, tostate_ref, carry_ref, bmask_ref, retg_ref,
                  *rest, n_cast, cast_steps):
    cast_in, o_ref, cast_out = rest[:n_cast], rest[n_cast], rest[n_cast + 1:2 * n_cast + 1]
    s5_st, hg_st, ret_st = rest[2 * n_cast + 1:]

    @pl.when(pl.program_id(0) == 0)
    def _():
        s5_st[...] = jnp.zeros_like(s5_st)
        hg_st[...] = jnp.zeros_like(hg_st)
        ret_st[...] = jnp.zeros_like(ret_st)

    if n_cast:
        @pl.when(pl.program_id(0) < cast_steps)
        def _():
            for src, dst in zip(cast_in, cast_out):
                dst[...] = src[...].astype(dst.dtype)

    nb = proj_ref.shape[0]

    def cols(c0, width):
        return _rows([proj_ref[b, :, c0:c0 + width] for b in range(nb)])

    def emit(c0, y):
        for b in range(nb):
            o_ref[b, :, c0:c0 + y.shape[1]] = y[b * CHUNK:(b + 1) * CHUNK].astype(o_ref.dtype)

    y_a = _s5_chunks(cols(M_U, S5_WIDTH), wb_ref, wc_ref, pn_re_ref, pn_im_ref, pp_re_ref, pp_im_ref,
                     lam_ref, d_ref, wglu_ref, s5g_ref, ltri_ref, s5_st)
    emit(0, y_a)

    w = HG_WIDTH
    y_b = _hgrn2_chunks(cols(M_Q, w), cols(M_LOGF, w), cols(M_KK, w), cols(M_V, w), cols(M_HGATE, w),
                        mall_ref, lvl_ref, rsel_ref, bones_ref, avg_ref, hgg_ref, hg_st)
    emit(S5_WIDTH, y_b)

    o_c = [_ret_chunk(proj_ref[b, :, M_QR:M_QR + RET_KEY_WIDTH], proj_ref[b, :, M_KR:M_KR + RET_KEY_WIDTH],
                      proj_ref[b, :, M_RV:M_RV + RET_WIDTH],
                      intra_ref, inter_ref, tostate_ref, carry_ref, bmask_ref, ret_st.at[b])
           for b in range(nb)]
    y_c = _ret_norm_gate(_rows(o_c), cols(M_RGATE, RET_WIDTH), avg_ref, retg_ref)
    emit(S5_WIDTH + HG_WIDTH, y_c)


def _const_spec(a):
    nd = a.ndim
    return pl.BlockSpec(a.shape, lambda c, _nd=nd: (0,) * _nd)


def _cast_steps(n_chunks, arrays):
    for steps in range(n_chunks, 0, -1):
        if all(a.shape[0] % (steps * 16) == 0 for a in arrays):
            return steps
    raise ValueError("no slab split")


def _mixers(proj, consts, bsz, n_chunks, to_bf16=()):
    total = proj.shape[1]
    n_cast = len(to_bf16)
    cast_steps = _cast_steps(n_chunks, to_bf16) if n_cast else 0
    slab = lambda c: (jnp.minimum(c, cast_steps - 1), 0)
    cast_specs = [pl.BlockSpec((a.shape[0] // cast_steps, a.shape[1]), slab) for a in to_bf16]
    in_specs = [
        pl.BlockSpec((bsz, CHUNK, MIX_COLS), lambda c: (0, c, 0)),
    ] + [_const_spec(a) for a in consts] + cast_specs
    outs = pl.pallas_call(
        functools.partial(_mixer_kernel, n_cast=n_cast, cast_steps=cast_steps),
        grid=(n_chunks,),
        in_specs=in_specs,
        out_specs=[pl.BlockSpec((bsz, CHUNK, D_MIX), lambda c: (0, c, 0))] + cast_specs,
        out_shape=[jax.ShapeDtypeStruct((bsz, total, D_MIX), BF16)]
        + [jax.ShapeDtypeStruct(a.shape, BF16) for a in to_bf16],
        scratch_shapes=[
            pltpu.VMEM((bsz, 2, S5_NSTATE), F32),
            pltpu.VMEM((bsz, HG_WIDTH, HG_WIDTH), F32),
            pltpu.VMEM((bsz, RET_KEY_WIDTH, RET_WIDTH), F32),
        ],
        compiler_params=pltpu.CompilerParams(
            dimension_semantics=("arbitrary",), vmem_limit_bytes=VMEM_LIMIT),
        name="mixers",
    )(proj, *consts, *to_bf16)
    return outs[0], list(outs[1:])


def _route(hn, router_ref, ltri_ref, count_ref):
    r = router_ref[...]
    r_hi, r_lo = _split_bf16(r)
    h_hi, h_lo = _split_bf16(hn)
    tm = hn.shape[0]
    parts = _dot(jnp.concatenate([h_hi, h_lo], axis=0), jnp.concatenate([r_hi, r_lo], axis=1))
    logits = (parts[:tm, :ROUTE_LANES] + parts[:tm, ROUTE_LANES:]) + (parts[tm:, :ROUTE_LANES] + parts[tm:, ROUTE_LANES:])
    lane_i = lax.broadcasted_iota(jnp.int32, logits.shape, 1)
    lane = lane_i.astype(F32)
    neg = jnp.float32(-jnp.inf)
    logits = jnp.where(lane_i < N_EXPERTS, logits, neg)
    m1 = jnp.max(logits, axis=-1, keepdims=True)
    i1 = jnp.min(jnp.where(logits == m1, lane, float(ROUTE_LANES)), axis=-1, keepdims=True)
    rest = jnp.where(lane == i1, neg, logits)
    m2 = jnp.max(rest, axis=-1, keepdims=True)
    i2 = jnp.min(jnp.where(rest == m2, lane, float(ROUTE_LANES)), axis=-1, keepdims=True)
    e2 = jnp.exp(m2 - m1)
    g1 = 1.0 / (1.0 + e2)
    g2 = e2 / (1.0 + e2)
    onehot = jnp.where((lane == i1) | (lane == i2), 1.0, 0.0)
    before = _dot(ltri_ref[...], onehot.astype(BF16)) + count_ref[...]
    r1 = jnp.sum(jnp.where(lane == i1, before, 0.0), axis=-1, keepdims=True)
    r2 = jnp.sum(jnp.where(lane == i2, before, 0.0), axis=-1, keepdims=True)
    count_ref[...] += jnp.sum(onehot, axis=0, keepdims=True)
    out = jnp.where(lane == 0, i1, 0.0)
    out = jnp.where(lane == 1, i2, out)
    out = jnp.where(lane == 2, g1, out)
    out = jnp.where(lane == 3, g2, out)
    out = jnp.where(lane == 4, r1, out)
    return jnp.where(lane == 5, r2, out)


def _out_proj_kernel(mixed_ref, *refs, routed, n_stream, tiles_per_seq):
    stream, (w_ref, g_ref, *rest) = refs[:n_stream], refs[n_stream:]
    if routed:
        router_ref, ltri_ref, h1_ref, hn_ref, route_ref, fields_ref, count_ref = rest

        @pl.when(pl.program_id(0) == 0)
        def _():
            count_ref[...] = jnp.zeros_like(count_ref)
    else:
        h1_ref, hn_ref = rest
    h1 = _stream_tile(stream, tiles_per_seq) + _dot(mixed_ref[...], w_ref[...])
    h1_ref[...] = h1
    hn = h1 * lax.rsqrt(jnp.mean(h1 * h1, axis=-1, keepdims=True) + EPS) * g_ref[...]
    if routed:
        tm = hn.shape[0]
        for s in range(ROW_TILE):
            hn_ref[pl.ds(s, tm, stride=ROW_TILE), :] = hn[:, s * 128:(s + 1) * 128]
        slab = _route(hn, router_ref, ltri_ref, count_ref)
        route_ref[...] = slab
        fields_ref[...] = slab.T[0:ROUTE_FIELDS, :]
    else:
        hn_ref[...] = hn.astype(hn_ref.dtype)


def _out_proj(mixed, stream, w_bf16, g, bsz, total, router_pad=None):
    n, d = bsz * total, stream[0].shape[1]
    tm = _pick_tile(total, (TOKEN_TILE, 128))
    routed = router_pad is not None
    row = lambda i: (i, 0)
    fixed = lambda i: (0, 0)
    stream_specs, stream_operands = _stream_specs(stream, tm, total)
    in_specs = [pl.BlockSpec((tm, D_MIX), row)] + stream_specs + [
        pl.BlockSpec(w_bf16.shape, fixed), pl.BlockSpec((1, d), fixed)]
    out_specs = [pl.BlockSpec((tm, d), row), pl.BlockSpec((tm, d), row)]
    out_shape = [jax.ShapeDtypeStruct((n, d), F32), jax.ShapeDtypeStruct((n, d), BF16)]
    args = [mixed] + stream_operands + [w_bf16, g]
    if routed:
        assert d == ROW_TILE * 128
        out_specs[1] = pl.BlockSpec((tm * ROW_TILE, 128), row)
        out_shape[1] = jax.ShapeDtypeStruct((n * ROW_TILE, 128), F32)
        ltri = jnp.asarray(np.tril(np.ones((tm, tm), np.float32), -1), BF16)
        in_specs += [pl.BlockSpec(router_pad.shape, fixed), pl.BlockSpec((tm, tm), fixed)]
        out_specs += [pl.BlockSpec((tm, ROUTE_LANES), row), pl.BlockSpec((ROUTE_FIELDS, tm), lambda i: (0, i)),
                      pl.BlockSpec((1, ROUTE_LANES), fixed)]
        out_shape += [jax.ShapeDtypeStruct((n, ROUTE_LANES), F32),
                      jax.ShapeDtypeStruct((ROUTE_FIELDS, n), F32),
                      jax.ShapeDtypeStruct((1, ROUTE_LANES), F32)]
        args += [router_pad, ltri]
    return pl.pallas_call(
        functools.partial(_out_proj_kernel, routed=routed, n_stream=len(stream), tiles_per_seq=total // tm),
        grid=(n // tm,),
        in_specs=in_specs,
        out_specs=out_specs,
        out_shape=out_shape,
        compiler_params=pltpu.CompilerParams(
            dimension_semantics=("arbitrary",), vmem_limit_bytes=VMEM_LIMIT),
        name="out_proj_routed" if routed else "out_proj",
    )(*args)


HIDDEN_CHUNK = 256


def _swiglu_hidden(x_ref, w1_ref, w3_ref, hid_ref):
    for c0 in range(0, hid_ref.shape[1], HIDDEN_CHUNK):
        cols = slice(c0, c0 + HIDDEN_CHUNK)
        a = _dot(x_ref[...], w1_ref[:, cols])
        b = _dot(x_ref[...], w3_ref[:, cols])
        hid_ref[:, cols] = (a * _sigmoid(a) * b).astype(BF16)


def _ffn_kernel(hn_ref, h1_ref, w1_ref, w3_ref, w2_ref, o_ref, hid_ref):
    _swiglu_hidden(hn_ref, w1_ref, w3_ref, hid_ref)
    o_ref[...] = h1_ref[...] + _dot(hid_ref[...], w2_ref[...])


def _ffn(hn, h1, w1, w3, w2):
    n, d = h1.shape
    dff = w1.shape[1]
    assert dff % HIDDEN_CHUNK == 0
    tm = _pick_tile(n, (TOKEN_TILE, 128))
    resident = lambda a: pl.BlockSpec(a.shape, lambda i: (0, 0), pipeline_mode=pl.Buffered(1))
    return pl.pallas_call(
        _ffn_kernel,
        grid=(n // tm,),
        in_specs=[
            pl.BlockSpec((tm, d), lambda i: (i, 0)),
            pl.BlockSpec((tm, d), lambda i: (i, 0)),
            resident(w1), resident(w3), resident(w2),
        ],
        out_specs=pl.BlockSpec((tm, d), lambda i: (i, 0)),
        out_shape=jax.ShapeDtypeStruct((n, d), F32),
        scratch_shapes=[pltpu.VMEM((tm, dff), BF16)],
        compiler_params=pltpu.CompilerParams(
            dimension_semantics=("arbitrary",), vmem_limit_bytes=VMEM_LIMIT),
        name="ffn_dense",
    )(hn, h1, w1, w3, w2)


def _moe_kernel(te_ref, nv_ref, src_ref, tok_ref, hn_hbm, w1_ref, w3_ref, w2_ref, o_ref,
                xs_ref, xb_ref, hid_ref, acc_ref, sem, *, tm):
    i, f = pl.program_id(0), pl.program_id(1)
    nf = pl.num_programs(1)
    n_valid = nv_ref[0]
    share = tm // MOE_F_STEPS

    def row_copy(tile, r):
        tok = tok_ref[src_ref[tile] + r]
        return pltpu.make_async_copy(hn_hbm.at[pl.ds(pl.multiple_of(tok * ROW_TILE, ROW_TILE), ROW_TILE)],
                                     xs_ref.at[pl.ds(pl.multiple_of(r * ROW_TILE, ROW_TILE), ROW_TILE)], sem)

    @pl.when((i == 0) & (f == 0))
    def _():
        def body(r, carry):
            row_copy(0, r).start()
            return carry
        lax.fori_loop(0, tm, body, 0)

    @pl.when((f == 0) & (i <= n_valid))
    def _():
        pltpu.make_async_copy(xs_ref, xs_ref, sem).wait()
        for s in range(ROW_TILE):
            xb_ref[:, s * 128:(s + 1) * 128] = xs_ref[pl.ds(s, tm, stride=ROW_TILE), :].astype(BF16)

    @pl.when(i < n_valid)
    def _():
        for r in range(share):
            row_copy(i + 1, f * share + r).start(priority=r % MOE_DMA_QUEUES)
        _swiglu_hidden(xb_ref, w1_ref.at[0], w3_ref.at[0], hid_ref)
        part = _dot(hid_ref[...], w2_ref[0])

        @pl.when(f == 0)
        def _():
            acc_ref[...] = part

        @pl.when(f > 0)
        def _():
            acc_ref[...] += part

    @pl.when(f == nf - 1)
    def _():
        o_ref[...] = jnp.where(i < n_valid, acc_ref[...], 0.0).astype(o_ref.dtype)


def _live_f(i, f, nv):
    return jnp.where(i < nv[0], f, 0)


def _moe_grouped(tile_expert, n_valid, tile_src, sorted_tok, hn, w1, w3, w2, tm):
    d = ROW_TILE * hn.shape[1]
    n_tiles = tile_expert.shape[0]
    dff = w1.shape[2]
    tf = dff // MOE_F_STEPS
    grid_spec = pltpu.PrefetchScalarGridSpec(
        num_scalar_prefetch=4,
        grid=(n_tiles, MOE_F_STEPS),
        in_specs=[
            pl.BlockSpec(memory_space=pl.ANY),
            pl.BlockSpec((1, d, tf), lambda i, f, te, nv, *_: (te[i], 0, _live_f(i, f, nv))),
            pl.BlockSpec((1, d, tf), lambda i, f, te, nv, *_: (te[i], 0, _live_f(i, f, nv))),
            pl.BlockSpec((1, tf, d), lambda i, f, te, nv, *_: (te[i], _live_f(i, f, nv), 0)),
        ],
        out_specs=pl.BlockSpec((tm, d), lambda i, f, *_: (i, 0)),
        scratch_shapes=[pltpu.VMEM((tm * ROW_TILE, 128), F32), pltpu.VMEM((tm, d), BF16),
                        pltpu.VMEM((tm, tf), BF16), pltpu.VMEM((tm, d), F32), pltpu.SemaphoreType.DMA(())],
    )
    return pl.pallas_call(
        functools.partial(_moe_kernel, tm=tm),
        grid_spec=grid_spec,
        out_shape=jax.ShapeDtypeStruct((n_tiles * tm, d), BF16),
        compiler_params=pltpu.CompilerParams(
            dimension_semantics=("arbitrary", "arbitrary"), vmem_limit_bytes=VMEM_LIMIT),
        name="moe_grouped",
    )(tile_expert, n_valid, tile_src, sorted_tok, hn, w1, w3, w2)


def _moe_dispatch(fields, counts_row, tm):
    n = fields.shape[1]
    n_slots = n * TOP_K
    n_tiles = (n_slots + N_EXPERTS * tm + tm - 1) // tm
    counts = counts_row[0, :N_EXPERTS].astype(jnp.int32)
    padded = ((counts + tm - 1) // tm) * tm
    pend = jnp.cumsum(padded)
    pstart = pend - padded
    gstart = jnp.cumsum(counts) - counts
    eid = fields[0:TOP_K].astype(jnp.int32)
    rank = fields[2 * TOP_K:3 * TOP_K].astype(jnp.int32)
    is_e = [eid == e for e in range(N_EXPERTS)]
    lookup = lambda table: sum(jnp.where(m, table[e], 0) for e, m in enumerate(is_e))
    pos = lookup(pstart) + rank
    tok_bits = max(1, (n - 1).bit_length())
    assert n_slots < (1 << (31 - tok_bits))
    tok = jnp.broadcast_to(jnp.arange(n, dtype=jnp.int32)[None, :], (TOP_K, n))
    packed = jnp.sort((((lookup(gstart) + rank) << tok_bits) | tok).reshape(-1))
    sorted_tok = packed & ((1 << tok_bits) - 1)
    sorted_tok = jnp.concatenate([sorted_tok, jnp.zeros((tm,), jnp.int32)])
    tile_start = jnp.arange(n_tiles, dtype=jnp.int32) * tm
    tile_expert = jnp.minimum(jnp.sum((tile_start[:, None] >= pend[None, :]).astype(jnp.int32), axis=1),
                              N_EXPERTS - 1)
    tile_src = jnp.clip(tile_start - (pstart - gstart)[tile_expert], 0, n_slots)
    n_valid = (pend[-1] // tm).astype(jnp.int32).reshape(1)
    return sorted_tok, tile_src, pos, tile_expert, n_valid


def _final_kernel(h1_ref, y0_ref, y1_ref, route_ref, g_ref, o_ref):
    g0 = route_ref[:, TOP_K:TOP_K + 1]
    g1 = route_ref[:, TOP_K + 1:TOP_K + 2]
    h = h1_ref[...] + (g0 * y0_ref[...].astype(F32) + g1 * y1_ref[...].astype(F32))
    o_ref[...] = h * lax.rsqrt(jnp.mean(h * h, axis=-1, keepdims=True) + EPS) * g_ref[...]


def _final(h1, y01, route, g, bsz, total):
    n, d = h1.shape
    seq = total - CHUNK
    tm = _pick_tile(seq, (1024, 128))
    per_seq = seq // tm
    rows = lambda width, base=0: pl.BlockSpec(
        (pl.Element(tm), pl.Element(width)),
        lambda b, k: (pl.multiple_of(base + b * total + CHUNK + k * tm, CHUNK), 0))
    out = pl.pallas_call(
        _final_kernel,
        grid=(bsz, per_seq),
        in_specs=[rows(d), rows(d), rows(d, n), rows(ROUTE_LANES), pl.BlockSpec((1, d), lambda b, k: (0, 0))],
        out_specs=pl.BlockSpec((tm, d), lambda b, k: (b * per_seq + k, 0)),
        out_shape=jax.ShapeDtypeStruct((bsz * seq, d), F32),
        compiler_params=pltpu.CompilerParams(
            dimension_semantics=("arbitrary", "arbitrary"), vmem_limit_bytes=VMEM_LIMIT),
        name="final_norm",
    )(h1, y01, y01, route, g)
    return out.reshape(bsz, seq, d)


def _relayout_in_cols(w):
    d = w.shape[0]

    def halves_first(block):
        return block.reshape(d, RET_HEADS, 2, RET_DK // 2).transpose(0, 2, 1, 3).reshape(d, RET_KEY_WIDTH)

    return jnp.concatenate([w[:, :C_RQ], halves_first(w[:, C_RQ:C_RK]), halves_first(w[:, C_RK:C_RV]),
                            w[:, C_RV:]], axis=1)


def _hg_tables(nb):
    c = CHUNK
    i = np.arange(c)[:, None]
    t = np.arange(c)[None, :]
    blocks, masks, rsel = [], [], []
    for lvl in range(HG_LEVELS):
        s = 1 << lvl
        r = (i // (2 * s)) * (2 * s) + s - 1
        right = i > r
        m = np.where(right, (t > r) & (t <= i), (t > i) & (t <= r))
        blocks.append(m)
        j = t
        same = (i // (2 * s)) == (j // (2 * s))
        mk = same & right & (j <= r)
        masks.append(np.concatenate([mk, mk], axis=1))
        if s < 8:
            rsel.append(np.broadcast_to(right, (c, HG_WIDTH)))
    blocks.append(t <= i)
    blocks.append(t > i)
    eye = np.eye(nb, dtype=np.float32)
    mall = np.concatenate([np.kron(eye, blk.astype(np.float32)) for blk in blocks], axis=0)
    lvl = np.stack(masks, axis=0).astype(np.float32)
    ch = np.arange(HG_WIDTH)
    bones = (ch[:, None] // HG_DK == ch[None, :] // HG_DK).astype(np.float32)
    rsel = np.stack([np.tile(r, (nb, 1)) for r in rsel], axis=0).astype(np.float32)
    return mall, lvl, rsel, bones


def _ret_tables():
    f32 = jnp.float32
    log_gamma = jnp.log1p(-jnp.power(2.0, -5.0 - jnp.arange(RET_HEADS, dtype=f32)))
    n = jnp.arange(CHUNK, dtype=f32)
    lg = log_gamma[:, None]
    causal = jnp.tril(jnp.ones((CHUNK, CHUNK), dtype=bool))
    intra = jnp.exp(jnp.where(causal[None], (n[:, None] - n[None, :])[None] * lg[:, :, None], -jnp.inf))
    scale = RET_DK ** -0.5
    inter = jnp.exp((n[None, :] + 1.0) * lg)
    to_state = jnp.exp((CHUNK - 1.0 - n[None, :]) * lg)
    carry = jnp.exp(CHUNK * lg)[:, 0]
    head_of_v = np.arange(RET_WIDTH) // RET_DV
    head_of_k = (np.arange(RET_KEY_WIDTH) % (RET_KEY_WIDTH // 2)) // (RET_DK // 2)
    inter_t = (inter * scale).T[:, head_of_v]
    to_state_t = to_state.T[:, head_of_k]
    carry_row = carry[head_of_v][None, :]
    bmask = jnp.asarray((head_of_k[:, None] == head_of_v[None, :]).astype(np.float32))
    intra_pairs = (intra * scale).reshape(RET_HEADS // 2, 2, CHUNK, CHUNK)
    intra_pairs = jnp.concatenate([intra_pairs[:, 0], intra_pairs[:, 1]], axis=2)
    return intra_pairs, inter_t, to_state_t, carry_row, bmask


def _s5_tables(lam_re, lam_im, b_re, b_im, c_re, c_im, d_skip, log_step):
    f32 = jnp.float32
    lam = lax.complex(lam_re.astype(f32), lam_im.astype(f32))
    step = jnp.exp(log_step.astype(f32))[:, None]
    lam_dt = lam * step
    lam_bar = jnp.exp(lam_dt)
    b_bar = ((lam_bar - 1.0) / lam)[..., None] * lax.complex(b_re.astype(f32), b_im.astype(f32))
    eye = jnp.eye(S5_NGROUPS, dtype=f32)
    wb_re = jnp.einsum('gph,gk->ghkp', jnp.real(b_bar), eye).reshape(S5_WIDTH, S5_NSTATE)
    wb_im = jnp.einsum('gph,gk->ghkp', jnp.imag(b_bar), eye).reshape(S5_WIDTH, S5_NSTATE)
    wb = jnp.concatenate([wb_re, wb_im], axis=1)
    wc_re = jnp.einsum('ghp,gk->gpkh', c_re.astype(f32), eye).reshape(S5_NSTATE, S5_WIDTH)
    wc_im = jnp.einsum('ghp,gk->gpkh', c_im.astype(f32), eye).reshape(S5_NSTATE, S5_WIDTH)
    wc = jnp.concatenate([wc_re, -wc_im], axis=0)
    t = jnp.arange(CHUNK, dtype=f32)[:, None, None]
    pp = jnp.exp(lam_dt[None] * t).reshape(CHUNK, S5_NSTATE)
    pn = jnp.exp(-lam_dt[None] * t).reshape(CHUNK, S5_NSTATE)
    lam_rows = jnp.stack([jnp.real(lam_bar).reshape(-1), jnp.imag(lam_bar).reshape(-1)], axis=0)
    return (wb.astype(BF16), wc.astype(BF16), jnp.real(pn), jnp.imag(pn), jnp.real(pp), jnp.imag(pp),
            lam_rows, d_skip.astype(f32).reshape(1, S5_WIDTH))


def kernel(x, meta_tokens, norm_mix_g, w_in, s5_lam_re, s5_lam_im, s5_b_re, s5_b_im, s5_c_re, s5_c_im, s5_d, s5_log_step, s5_w_glu, s5_out_g, hg_conv_w, hg_lb_param, hg_out_g, ret_out_g, w_out, norm_ffn_g, ffn_w1, ffn_w3, ffn_w2, moe_router, moe_w1, moe_w3, moe_w2, final_norm_g):
    bsz, seq_len, d = x.shape
    depth = w_in.shape[0]
    total = seq_len + CHUNK
    n_chunks = total // CHUNK
    n = bsz * total

    meta = jnp.broadcast_to(meta_tokens.astype(F32)[None], (bsz, N_META, d))
    tm0 = _pick_tile(total, (TOKEN_TILE, 128))
    head = jnp.concatenate([jnp.zeros((bsz, PAD, d), F32), meta, x[:, :tm0 - CHUNK].astype(F32)], axis=1)
    stream = (x.astype(F32).reshape(bsz * seq_len, d), head)

    pos = (jnp.arange(total) - PAD).astype(F32)
    half = RET_DK // 2
    inv_freq = ROPE_BASE ** (-jnp.arange(half, dtype=F32) / half)
    ang = pos[:, None] * inv_freq[None, :]
    cos_t = jnp.tile(jnp.cos(ang), (1, RET_HEADS))
    sin_t = jnp.tile(jnp.sin(ang), (1, RET_HEADS))

    lb_all = jnp.cumsum(jax.nn.softmax(hg_lb_param.astype(F32), axis=0), axis=0)
    lb_all = lb_all - lb_all[0]

    mall_np, lvl_np, rsel_np, bones_np = _hg_tables(bsz)
    mall = jnp.asarray(mall_np, BF16)
    lvl = jnp.asarray(lvl_np, F32)
    rsel = jnp.asarray(rsel_np, F32)
    bones = jnp.asarray(bones_np, BF16)
    avg = jnp.asarray(bones_np / HG_DK, BF16)
    ltri = jnp.asarray(np.kron(np.eye(bsz, dtype=np.float32),
                               np.tril(np.ones((CHUNK, CHUNK), np.float32))), BF16)
    intra, inter_t, to_state_t, carry_row, bmask = _ret_tables()

    out = None
    for l in range(depth):
        w_in_l = _relayout_in_cols(w_in[l].astype(BF16))
        lb = lb_all[l][None, :]
        lbv = jnp.concatenate([jnp.log(lb), jnp.log1p(-lb), 1.0 - lb], axis=0)
        proj = _in_proj(stream, norm_mix_g[l].astype(F32).reshape(1, d), w_in_l,
                        hg_conv_w[l].astype(F32), lbv, cos_t, sin_t, bsz, total)

        s5c = _s5_tables(s5_lam_re[l], s5_lam_im[l], s5_b_re[l], s5_b_im[l], s5_c_re[l], s5_c_im[l],
                         s5_d[l], s5_log_step[l])
        consts = list(s5c) + [
            s5_w_glu[l].astype(BF16), s5_out_g[l].astype(F32).reshape(1, -1), ltri,
            mall, lvl, rsel, bones, avg, hg_out_g[l].astype(F32).reshape(1, -1),
            intra, inter_t, to_state_t, carry_row, bmask, ret_out_g[l].astype(F32).reshape(1, -1),
        ]
        j = l // 2
        if l % 2 == 0:
            ffn_f32 = [ffn_w1[j], ffn_w3[j], ffn_w2[j]]
        else:
            dff_e = moe_w1.shape[-1]
            ffn_f32 = [moe_w1[j].reshape(N_EXPERTS * d, dff_e), moe_w3[j].reshape(N_EXPERTS * d, dff_e),
                       moe_w2[j].reshape(N_EXPERTS * dff_e, d)]
        mixed, ffn_bf16 = _mixers(proj.reshape(bsz, total, MIX_COLS), consts, bsz, n_chunks,
                                  to_bf16=[a.astype(F32) for a in ffn_f32])
        mixed = mixed.reshape(n, D_MIX)

        g_ffn = norm_ffn_g[l].astype(F32).reshape(1, d)
        w_out_l = w_out[l].astype(BF16)
        if l % 2 == 0:
            h1, hn = _out_proj(mixed, stream, w_out_l, g_ffn, bsz, total)
            stream = (_ffn(hn, h1, *ffn_bf16),)
            y01 = None
        else:
            router_pad = jnp.zeros((d, ROUTE_LANES), F32).at[:, :N_EXPERTS].set(moe_router[j].astype(F32))
            h1, hn, route, fields, counts_row = _out_proj(mixed, stream, w_out_l, g_ffn, bsz, total, router_pad)
            tm = MOE_TILE
            sorted_tok, tile_src, pos_of_slot, tile_expert, n_valid = _moe_dispatch(fields, counts_row, tm)
            e_w1, e_w3, e_w2 = (ffn_bf16[0].reshape(N_EXPERTS, d, dff_e), ffn_bf16[1].reshape(N_EXPERTS, d, dff_e),
                                ffn_bf16[2].reshape(N_EXPERTS, dff_e, d))
            ys = _moe_grouped(tile_expert, n_valid, tile_src, sorted_tok, hn, e_w1, e_w3, e_w2, tm)
            y01 = ys.at[pos_of_slot.reshape(-1)].get(mode='promise_in_bounds')
            if l < depth - 1:
                stream = (h1 + (route[:, TOP_K:TOP_K + 1] * y01[:n].astype(F32)
                                + route[:, TOP_K + 1:TOP_K + 2] * y01[n:].astype(F32)),)

        if l == depth - 1:
            if y01 is None:
                y01 = jnp.zeros((TOP_K * n, d), BF16)
                h1 = stream[0]
                route = jnp.zeros((n, ROUTE_LANES), F32)
            out = _final(h1, y01, route, final_norm_g.astype(F32).reshape(1, d), bsz, total)

    return out.astype(x.dtype)
```

```python
import functools
import math

import jax
import jax.numpy as jnp
import numpy as np
from jax import lax
from jax.experimental import pallas as pl
from jax.experimental.pallas import tpu as pltpu

F32 = jnp.float32
BF16 = jnp.bfloat16

CHUNK = 128
N_META = 16
PAD = CHUNK - N_META
EPS = 1e-6

S5_WIDTH = 256
S5_GROUP = 16
S5_NGROUPS = 16
S5_STATE = 64
S5_NSTATE = S5_NGROUPS * S5_STATE

HG_HEADS = 4
HG_DK = 64
HG_WIDTH = 256
CONV_K = 4
HG_CONV_W = 3 * HG_WIDTH
HG_LEVELS = 7

RET_HEADS = 8
RET_DK = 32
RET_DV = 64
RET_KEY_WIDTH = 256
RET_WIDTH = 512
ROPE_BASE = 10000.0

D_MIX = 1024
IN_COLS = 2816
C_U, C_HQ, C_HGATE, C_RQ, C_RK, C_RV, C_RGATE = 0, 256, 1024, 1280, 1536, 1792, 2304
M_U, M_Q, M_LOGF, M_KK, M_V, M_HGATE, M_QR, M_KR, M_RV, M_RGATE = (
    0, 256, 512, 768, 1024, 1280, 1536, 1792, 2048, 2560)
MIX_COLS = 3072

N_EXPERTS = 8
TOP_K = 2
ROUTE_LANES = 128
ROUTE_FIELDS = 8
MOE_F_STEPS = 2
MOE_TILE = 512
ROW_TILE = 8
MOE_DMA_QUEUES = 2

VMEM_LIMIT = 56 * 1024 * 1024


def _sigmoid(x):
    return 1.0 / (1.0 + jnp.exp(-x))


def _split_bf16(x):
    hi = x.astype(BF16)
    lo = (x - hi.astype(F32)).astype(BF16)
    return hi, lo


def _pick_tile(n, candidates):
    for t in candidates:
        if n % t == 0:
            return t
    raise ValueError(f"no tile in {candidates} divides {n}")


def _dot(a, b):
    return jnp.dot(a, b, preferred_element_type=F32)


def _dot_nt(a, b):
    return lax.dot_general(a, b, (((1,), (1,)), ((), ())), preferred_element_type=F32)


def _dot_tn(a, b):
    return lax.dot_general(a, b, (((0,), (0,)), ((), ())), preferred_element_type=F32)


TOKEN_TILE = 640


def _stream_specs(stream, tm, total):
    d = stream[0].shape[1]
    if len(stream) == 1:
        return [pl.BlockSpec((tm, d), lambda i: (i, 0))], list(stream)
    per, seq = total // tm, total - CHUNK
    x_rows = lambda i: (pl.multiple_of((i // per) * seq + jnp.maximum((i % per) * tm - CHUNK, 0), CHUNK), 0)
    return ([pl.BlockSpec((pl.Element(tm), pl.Element(d)), x_rows),
             pl.BlockSpec((1, tm, d), lambda i: (i // per, 0, 0))], list(stream))


def _stream_tile(refs, tiles_per_seq):
    if len(refs) == 1:
        return refs[0][...]
    x_ref, head_ref = refs
    return jnp.where(pl.program_id(0) % tiles_per_seq == 0, head_ref[0], x_ref[...])


def _in_proj_kernel(*refs, tm, tiles_per_seq):
    *stream, g_ref, w_ref, convw_ref, lbv_ref, cos_ref, sin_ref, o_ref, xc_ref = refs
    t = pl.program_id(0) % tiles_per_seq
    x = _stream_tile(stream, tiles_per_seq)
    y = (x * lax.rsqrt(jnp.mean(x * x, axis=-1, keepdims=True) + EPS) * g_ref[...]).astype(BF16)
    keep = (t * tm + lax.broadcasted_iota(jnp.int32, (tm, 1), 0)) >= PAD

    def project(c0, width):
        return jnp.where(keep, _dot(y, w_ref[:, c0:c0 + width]), 0.0)

    xc_ref[0:8, :] = jnp.where(t == 0, 0.0, xc_ref[0:8, :])
    xc_ref[8:8 + tm, :] = project(C_HQ, HG_CONV_W)
    conv = None
    for i in range(CONV_K):
        term = xc_ref[pl.ds(8 - (CONV_K - 1) + i, tm), :] * convw_ref[i:i + 1, :]
        conv = term if conv is None else conv + term
    xc_ref[0:8, :] = xc_ref[tm:tm + 8, :]
    w = HG_WIDTH
    cq, cf = conv[:, :w], conv[:, w:2 * w]
    log_lb, log_1m_lb, one_m_lb = lbv_ref[0:1, :], lbv_ref[1:2, :], lbv_ref[2:3, :]
    log_sig = jnp.minimum(cf, 0.0) - jnp.log(1.0 + jnp.exp(-jnp.abs(cf)))
    b_ = log_1m_lb + log_sig
    o_ref[:, M_Q:M_Q + w] = cq * _sigmoid(cq)
    o_ref[:, M_LOGF:M_LOGF + w] = jnp.maximum(log_lb, b_) + jnp.log(1.0 + jnp.exp(-jnp.abs(log_lb - b_)))
    o_ref[:, M_KK:M_KK + w] = one_m_lb * _sigmoid(-cf)
    o_ref[:, M_V:M_V + w] = conv[:, 2 * w:]
    hgate = project(C_HGATE, w)
    o_ref[:, M_HGATE:M_HGATE + w] = hgate * _sigmoid(hgate)

    hw = RET_KEY_WIDTH // 2
    cos, sin = cos_ref[...], sin_ref[...]
    for c_in, c_out in ((C_RQ, M_QR), (C_RK, M_KR)):
        qk = project(c_in, RET_KEY_WIDTH)
        t1, t2 = qk[:, :hw], qk[:, hw:]
        o_ref[:, c_out:c_out + hw] = t1 * cos - t2 * sin
        o_ref[:, c_out + hw:c_out + 2 * hw] = t1 * sin + t2 * cos
    o_ref[:, M_RV:M_RV + RET_WIDTH] = project(C_RV, RET_WIDTH)
    rgate = project(C_RGATE, RET_WIDTH)
    o_ref[:, M_RGATE:M_RGATE + RET_WIDTH] = rgate * _sigmoid(rgate)
    o_ref[:, M_U:M_U + S5_WIDTH] = project(C_U, S5_WIDTH)


def _in_proj(stream, g, w_bf16, convw, lbv, cos, sin, bsz, total):
    n, d = bsz * total, stream[0].shape[1]
    tm = _pick_tile(total, (TOKEN_TILE, 128))
    per = total // tm
    specs, operands = _stream_specs(stream, tm, total)
    fixed = lambda a: pl.BlockSpec(a.shape, lambda i: (0, 0))
    return pl.pallas_call(
        functools.partial(_in_proj_kernel, tm=tm, tiles_per_seq=per),
        grid=(n // tm,),
        in_specs=specs + [
            pl.BlockSpec((1, d), lambda i: (0, 0)),
            pl.BlockSpec(w_bf16.shape, lambda i: (0, 0), pipeline_mode=pl.Buffered(1)),
            fixed(convw), fixed(lbv),
            pl.BlockSpec((tm, 128), lambda i: (i % per, 0)),
            pl.BlockSpec((tm, 128), lambda i: (i % per, 0)),
        ],
        out_specs=pl.BlockSpec((tm, MIX_COLS), lambda i: (i, 0)),
        out_shape=jax.ShapeDtypeStruct((n, MIX_COLS), F32),
        scratch_shapes=[pltpu.VMEM((tm + 8, HG_CONV_W), F32)],
        compiler_params=pltpu.CompilerParams(
            dimension_semantics=("arbitrary",), vmem_limit_bytes=VMEM_LIMIT),
        name="in_proj",
    )(*operands, g, w_bf16, convw, lbv, cos, sin)


def _rows(parts):
    return parts[0] if len(parts) == 1 else jnp.concatenate(parts, axis=0)


def _s5_chunks(u, wb_ref, wc_ref, pn_re_ref, pn_im_ref, pp_re_ref, pp_im_ref, lam_ref,
               d_ref, wglu_ref, g_ref, ltri_ref, st_ref):
    ns = S5_NSTATE
    nb = st_ref.shape[0]
    bu = _dot(u.astype(BF16), wb_ref[...])
    pn_re, pn_im = pn_re_ref[...], pn_im_ref[...]
    w_re, w_im = [], []
    for b in range(nb):
        bu_re = bu[b * CHUNK:(b + 1) * CHUNK, :ns]
        bu_im = bu[b * CHUNK:(b + 1) * CHUNK, ns:]
        w_re.append(pn_re * bu_re - pn_im * bu_im)
        w_im.append(pn_re * bu_im + pn_im * bu_re)
    ltri = ltri_ref[...]
    c_re = _dot(ltri, _rows(w_re).astype(BF16))
    c_im = _dot(ltri, _rows(w_im).astype(BF16))
    lam_re, lam_im = lam_ref[0:1, :], lam_ref[1:2, :]
    pp_re, pp_im = pp_re_ref[...], pp_im_ref[...]
    st_re, st_im = [], []
    for b in range(nb):
        s_re, s_im = st_ref[b, 0:1, :], st_ref[b, 1:2, :]
        z_re = c_re[b * CHUNK:(b + 1) * CHUNK] + (lam_re * s_re - lam_im * s_im)
        z_im = c_im[b * CHUNK:(b + 1) * CHUNK] + (lam_re * s_im + lam_im * s_re)
        t_re = pp_re * z_re - pp_im * z_im
        t_im = pp_re * z_im + pp_im * z_re
        st_ref[b, 0:1, :] = t_re[CHUNK - 1:CHUNK, :]
        st_ref[b, 1:2, :] = t_im[CHUNK - 1:CHUNK, :]
        st_re.append(t_re)
        st_im.append(t_im)
    y = (_dot(_rows(st_re).astype(BF16), wc_ref[0:ns, :]) + _dot(_rows(st_im).astype(BF16), wc_ref[ns:2 * ns, :])
         + d_ref[...] * u)
    y = 0.5 * y * (1.0 + jnp.tanh(math.sqrt(2.0 / math.pi) * (y + 0.044715 * (y * y * y))))
    y = y * _sigmoid(_dot(y.astype(BF16), wglu_ref[...]))
    return y * lax.rsqrt(jnp.mean(y * y, axis=-1, keepdims=True) + EPS) * g_ref[...]


def _hgrn2_chunks(q, logf, kk, v, gate, mall_ref, lvl_ref, rsel_ref, bones_ref, avg_ref, g_ref, st_ref):
    w = HG_WIDTH
    nb = st_ref.shape[0]
    rows = nb * CHUNK
    hi, lo = _split_bf16(logf)

    lvl_sums = _dot(mall_ref[0:HG_LEVELS * rows, :], hi)
    tail = mall_ref[HG_LEVELS * rows:(HG_LEVELS + 2) * rows, :]
    cum_suf = _dot(tail, hi) + _dot(tail, lo)
    g_cum, g_suffix = cum_suf[:rows], cum_suf[rows:]

    lane = lax.broadcasted_iota(jnp.int32, (1, 2 * HG_DK), 1)
    head_sel = [jnp.where(lane < HG_DK, 1.0, 0.0), jnp.where(lane >= HG_DK, 1.0, 0.0)]
    n_pairs = HG_HEADS // 2
    scores = [[None] * n_pairs for _ in range(nb)]
    for lvl in range(HG_LEVELS):
        s = 1 << lvl
        e = jnp.exp(lvl_sums[lvl * rows:(lvl + 1) * rows])
        if s >= 8:
            qk = jnp.concatenate([(q if (r // s) % 2 else kk)[r:r + s] for r in range(0, rows, s)], axis=0)
        else:
            qk = jnp.where(rsel_ref[lvl] > 0.5, q, kk)
        x = qk * e
        m = lvl_ref[lvl] > 0.5
        for b in range(nb):
            for p in range(n_pairs):
                xp = x[b * CHUNK:(b + 1) * CHUNK, p * 128:(p + 1) * 128]
                rhs = jnp.concatenate([xp * head_sel[0], xp * head_sel[1]], axis=0).astype(BF16)
                sc = _dot_nt(xp.astype(BF16), rhs)
                scores[b][p] = jnp.where(m, sc, 0.0 if scores[b][p] is None else scores[b][p])
    v_bf = v.astype(BF16)
    bones = bones_ref[...]
    qg = (q * jnp.exp(g_cum)).astype(BF16)
    kd = (kk * jnp.exp(g_suffix)).astype(BF16)
    o_rows = []
    for b in range(nb):
        sl = slice(b * CHUNK, (b + 1) * CHUNK)
        o_parts = []
        for p in range(n_pairs):
            vp = v[sl, p * 128:(p + 1) * 128]
            vv = jnp.concatenate([vp * head_sel[0], vp * head_sel[1]], axis=0).astype(BF16)
            o_parts.append(_dot(scores[b][p].astype(BF16), vv))
        st = st_ref[b]
        o_rows.append(jnp.concatenate(o_parts, axis=1) + _dot_nt(qg[sl], st.astype(BF16)))
        upd = _dot_tn(v_bf[sl], kd[sl]) * bones.astype(F32)
        st_ref[b] = st * jnp.exp(g_cum[(b + 1) * CHUNK - 1:(b + 1) * CHUNK, :]) + upd
    o = _rows(o_rows) + _dot((q * kk).astype(BF16), bones) * v
    ms = _dot((o * o).astype(BF16), avg_ref[...])
    return o * lax.rsqrt(ms + EPS) * g_ref[...] * gate


def _ret_chunk(qr, kr, v, intra_ref, inter_ref, tostate_ref, carry_ref, bmask_ref, st_ref):
    hw = RET_KEY_WIDTH // 2
    qr_bf, v_bf = qr.astype(BF16), v.astype(BF16)
    lane_k = lax.broadcasted_iota(jnp.int32, (1, RET_KEY_WIDTH), 1) % hw
    lane_v = lax.broadcasted_iota(jnp.int32, (1, 2 * RET_DV), 1)
    v_sel = [jnp.where(lane_v < RET_DV, 1.0, 0.0), jnp.where(lane_v >= RET_DV, 1.0, 0.0)]
    half = RET_DK // 2
    o_parts = []
    for p in range(RET_HEADS // 2):
        sel = [jnp.where((lane_k >= h * half) & (lane_k < (h + 1) * half), 1.0, 0.0)
               for h in (2 * p, 2 * p + 1)]
        rhs = jnp.concatenate([kr * sel[0], kr * sel[1]], axis=0).astype(BF16)
        sc = (_dot_nt(qr_bf, rhs) * intra_ref[p]).astype(BF16)
        vp = v[:, p * 128:(p + 1) * 128]
        vv = jnp.concatenate([vp * v_sel[0], vp * v_sel[1]], axis=0).astype(BF16)
        o_parts.append(_dot(sc, vv))
    o = jnp.concatenate(o_parts, axis=1)
    st = st_ref[...]
    o = o + _dot(qr_bf, st.astype(BF16)) * inter_ref[...]
    kd = (kr * tostate_ref[...]).astype(BF16)
    st_ref[...] = st * carry_ref[...] + _dot_tn(kd, v_bf) * bmask_ref[...]
    return o


def _ret_norm_gate(o, gate, avg_ref, g_ref):
    avg = avg_ref[...]
    outs = []
    for s in range(RET_WIDTH // 256):
        os_ = o[:, s * 256:(s + 1) * 256]
        c = os_ - _dot(os_.astype(BF16), avg)
        outs.append(c * lax.rsqrt(_dot((c * c).astype(BF16), avg) + EPS))
    return jnp.concatenate(outs, axis=1) * g_ref[...] * gate


def _mixer_kernel(proj_ref,
                  wb_ref, wc_ref, pn_re_ref, pn_im_ref, pp_re_ref, pp_im_ref, lam_ref, d_ref,
                  wglu_ref, s5g_ref, ltri_ref,
                  mall_ref, lvl_ref, rsel_ref, bones_ref, avg_ref, hgg_ref,
                  intra_ref, inter_ref, tostate_ref, carry_ref, bmask_ref, retg_ref,
                  *rest, n_cast, cast_steps):
    cast_in, o_ref, cast_out = rest[:n_cast], rest[n_cast], rest[n_cast + 1:2 * n_cast + 1]
    s5_st, hg_st, ret_st = rest[2 * n_cast + 1:]

    @pl.when(pl.program_id(0) == 0)
    def _():
        s5_st[...] = jnp.zeros_like(s5_st)
        hg_st[...] = jnp.zeros_like(hg_st)
        ret_st[...] = jnp.zeros_like(ret_st)

    if n_cast:
        @pl.when(pl.program_id(0) < cast_steps)
        def _():
            for src, dst in zip(cast_in, cast_out):
                dst[...] = src[...].astype(dst.dtype)

    nb = proj_ref.shape[0]

    def cols(c0, width):
        return _rows([proj_ref[b, :, c0:c0 + width] for b in range(nb)])

    def emit(c0, y):
        for b in range(nb):
            o_ref[b, :, c0:c0 + y.shape[1]] = y[b * CHUNK:(b + 1) * CHUNK].astype(o_ref.dtype)

    y_a = _s5_chunks(cols(M_U, S5_WIDTH), wb_ref, wc_ref, pn_re_ref, pn_im_ref, pp_re_ref, pp_im_ref,
                     lam_ref, d_ref, wglu_ref, s5g_ref, ltri_ref, s5_st)
    emit(0, y_a)

    w = HG_WIDTH
    y_b = _hgrn2_chunks(cols(M_Q, w), cols(M_LOGF, w), cols(M_KK, w), cols(M_V, w), cols(M_HGATE, w),
                        mall_ref, lvl_ref, rsel_ref, bones_ref, avg_ref, hgg_ref, hg_st)
    emit(S5_WIDTH, y_b)

    o_c = [_ret_chunk(proj_ref[b, :, M_QR:M_QR + RET_KEY_WIDTH], proj_ref[b, :, M_KR:M_KR + RET_KEY_WIDTH],
                      proj_ref[b, :, M_RV:M_RV + RET_WIDTH],
                      intra_ref, inter_ref, tostate_ref, carry_ref, bmask_ref, ret_st.at[b])
           for b in range(nb)]
    y_c = _ret_norm_gate(_rows(o_c), cols(M_RGATE, RET_WIDTH), avg_ref, retg_ref)
    emit(S5_WIDTH + HG_WIDTH, y_c)


def _const_spec(a):
    nd = a.ndim
    return pl.BlockSpec(a.shape, lambda c, _nd=nd: (0,) * _nd)


def _cast_steps(n_chunks, arrays):
    for steps in range(n_chunks, 0, -1):
        if all(a.shape[0] % (steps * 16) == 0 for a in arrays):
            return steps
    raise ValueError("no slab split")


def _mixers(proj, consts, bsz, n_chunks, to_bf16=()):
    total = proj.shape[1]
    n_cast = len(to_bf16)
    cast_steps = _cast_steps(n_chunks, to_bf16) if n_cast else 0
    slab = lambda c: (jnp.minimum(c, cast_steps - 1), 0)
    cast_specs = [pl.BlockSpec((a.shape[0] // cast_steps, a.shape[1]), slab) for a in to_bf16]
    in_specs = [
        pl.BlockSpec((bsz, CHUNK, MIX_COLS), lambda c: (0, c, 0)),
    ] + [_const_spec(a) for a in consts] + cast_specs
    outs = pl.pallas_call(
        functools.partial(_mixer_kernel, n_cast=n_cast, cast_steps=cast_steps),
        grid=(n_chunks,),
        in_specs=in_specs,
        out_specs=[pl.BlockSpec((bsz, CHUNK, D_MIX), lambda c: (0, c, 0))] + cast_specs,
        out_shape=[jax.ShapeDtypeStruct((bsz, total, D_MIX), BF16)]
        + [jax.ShapeDtypeStruct(a.shape, BF16) for a in to_bf16],
        scratch_shapes=[
            pltpu.VMEM((bsz, 2, S5_NSTATE), F32),
            pltpu.VMEM((bsz, HG_WIDTH, HG_WIDTH), F32),
            pltpu.VMEM((bsz, RET_KEY_WIDTH, RET_WIDTH), F32),
        ],
        compiler_params=pltpu.CompilerParams(
            dimension_semantics=("arbitrary",), vmem_limit_bytes=VMEM_LIMIT),
        name="mixers",
    )(proj, *consts, *to_bf16)
    return outs[0], list(outs[1:])


def _route(hn, router_ref, ltri_ref, count_ref):
    r = router_ref[...]
    r_hi, r_lo = _split_bf16(r)
    h_hi, h_lo = _split_bf16(hn)
    tm = hn.shape[0]
    parts = _dot(jnp.concatenate([h_hi, h_lo], axis=0), jnp.concatenate([r_hi, r_lo], axis=1))
    logits = (parts[:tm, :ROUTE_LANES] + parts[:tm, ROUTE_LANES:]) + (parts[tm:, :ROUTE_LANES] + parts[tm:, ROUTE_LANES:])
    lane_i = lax.broadcasted_iota(jnp.int32, logits.shape, 1)
    lane = lane_i.astype(F32)
    neg = jnp.float32(-jnp.inf)
    logits = jnp.where(lane_i < N_EXPERTS, logits, neg)
    m1 = jnp.max(logits, axis=-1, keepdims=True)
    i1 = jnp.min(jnp.where(logits == m1, lane, float(ROUTE_LANES)), axis=-1, keepdims=True)
    rest = jnp.where(lane == i1, neg, logits)
    m2 = jnp.max(rest, axis=-1, keepdims=True)
    i2 = jnp.min(jnp.where(rest == m2, lane, float(ROUTE_LANES)), axis=-1, keepdims=True)
    e2 = jnp.exp(m2 - m1)
    g1 = 1.0 / (1.0 + e2)
    g2 = e2 / (1.0 + e2)
    onehot = jnp.where((lane == i1) | (lane == i2), 1.0, 0.0)
    before = _dot(ltri_ref[...], onehot.astype(BF16)) + count_ref[...]
    r1 = jnp.sum(jnp.where(lane == i1, before, 0.0), axis=-1, keepdims=True)
    r2 = jnp.sum(jnp.where(lane == i2, before, 0.0), axis=-1, keepdims=True)
    count_ref[...] += jnp.sum(onehot, axis=0, keepdims=True)
    out = jnp.where(lane == 0, i1, 0.0)
    out = jnp.where(lane == 1, i2, out)
    out = jnp.where(lane == 2, g1, out)
    out = jnp.where(lane == 3, g2, out)
    out = jnp.where(lane == 4, r1, out)
    return jnp.where(lane == 5, r2, out)


def _out_proj_kernel(mixed_ref, *refs, routed, n_stream, tiles_per_seq):
    stream, (w_ref, g_ref, *rest) = refs[:n_stream], refs[n_stream:]
    if routed:
        router_ref, ltri_ref, h1_ref, hn_ref, route_ref, fields_ref, count_ref = rest

        @pl.when(pl.program_id(0) == 0)
        def _():
            count_ref[...] = jnp.zeros_like(count_ref)
    else:
        h1_ref, hn_ref = rest
    h1 = _stream_tile(stream, tiles_per_seq) + _dot(mixed_ref[...], w_ref[...])
    h1_ref[...] = h1
    hn = h1 * lax.rsqrt(jnp.mean(h1 * h1, axis=-1, keepdims=True) + EPS) * g_ref[...]
    if routed:
        tm = hn.shape[0]
        for s in range(ROW_TILE):
            hn_ref[pl.ds(s, tm, stride=ROW_TILE), :] = hn[:, s * 128:(s + 1) * 128]
        slab = _route(hn, router_ref, ltri_ref, count_ref)
        route_ref[...] = slab
        fields_ref[...] = slab.T[0:ROUTE_FIELDS, :]
    else:
        hn_ref[...] = hn.astype(hn_ref.dtype)


def _out_proj(mixed, stream, w_bf16, g, bsz, total, router_pad=None):
    n, d = bsz * total, stream[0].shape[1]
    tm = _pick_tile(total, (TOKEN_TILE, 128))
    routed = router_pad is not None
    row = lambda i: (i, 0)
    fixed = lambda i: (0, 0)
    stream_specs, stream_operands = _stream_specs(stream, tm, total)
    in_specs = [pl.BlockSpec((tm, D_MIX), row)] + stream_specs + [
        pl.BlockSpec(w_bf16.shape, fixed), pl.BlockSpec((1, d), fixed)]
    out_specs = [pl.BlockSpec((tm, d), row), pl.BlockSpec((tm, d), row)]
    out_shape = [jax.ShapeDtypeStruct((n, d), F32), jax.ShapeDtypeStruct((n, d), BF16)]
    args = [mixed] + stream_operands + [w_bf16, g]
    if routed:
        assert d == ROW_TILE * 128
        out_specs[1] = pl.BlockSpec((tm * ROW_TILE, 128), row)
        out_shape[1] = jax.ShapeDtypeStruct((n * ROW_TILE, 128), F32)
        ltri = jnp.asarray(np.tril(np.ones((tm, tm), np.float32), -1), BF16)
        in_specs += [pl.BlockSpec(router_pad.shape, fixed), pl.BlockSpec((tm, tm), fixed)]
        out_specs += [pl.BlockSpec((tm, ROUTE_LANES), row), pl.BlockSpec((ROUTE_FIELDS, tm), lambda i: (0, i)),
                      pl.BlockSpec((1, ROUTE_LANES), fixed)]
        out_shape += [jax.ShapeDtypeStruct((n, ROUTE_LANES), F32),
                      jax.ShapeDtypeStruct((ROUTE_FIELDS, n), F32),
                      jax.ShapeDtypeStruct((1, ROUTE_LANES), F32)]
        args += [router_pad, ltri]
    return pl.pallas_call(
        functools.partial(_out_proj_kernel, routed=routed, n_stream=len(stream), tiles_per_seq=total // tm),
        grid=(n // tm,),
        in_specs=in_specs,
        out_specs=out_specs,
        out_shape=out_shape,
        compiler_params=pltpu.CompilerParams(
            dimension_semantics=("arbitrary",), vmem_limit_bytes=VMEM_LIMIT),
        name="out_proj_routed" if routed else "out_proj",
    )(*args)


HIDDEN_CHUNK = 256


def _swiglu_hidden(x_ref, w1_ref, w3_ref, hid_ref):
    for c0 in range(0, hid_ref.shape[1], HIDDEN_CHUNK):
        cols = slice(c0, c0 + HIDDEN_CHUNK)
        a = _dot(x_ref[...], w1_ref[:, cols])
        b = _dot(x_ref[...], w3_ref[:, cols])
        hid_ref[:, cols] = (a * _sigmoid(a) * b).astype(BF16)


def _ffn_kernel(hn_ref, h1_ref, w1_ref, w3_ref, w2_ref, o_ref, hid_ref):
    _swiglu_hidden(hn_ref, w1_ref, w3_ref, hid_ref)
    o_ref[...] = h1_ref[...] + _dot(hid_ref[...], w2_ref[...])


def _ffn(hn, h1, w1, w3, w2):
    n, d = h1.shape
    dff = w1.shape[1]
    assert dff % HIDDEN_CHUNK == 0
    tm = _pick_tile(n, (TOKEN_TILE, 128))
    resident = lambda a: pl.BlockSpec(a.shape, lambda i: (0, 0), pipeline_mode=pl.Buffered(1))
    return pl.pallas_call(
        _ffn_kernel,
        grid=(n // tm,),
        in_specs=[
            pl.BlockSpec((tm, d), lambda i: (i, 0)),
            pl.BlockSpec((tm, d), lambda i: (i, 0)),
            resident(w1), resident(w3), resident(w2),
        ],
        out_specs=pl.BlockSpec((tm, d), lambda i: (i, 0)),
        out_shape=jax.ShapeDtypeStruct((n, d), F32),
        scratch_shapes=[pltpu.VMEM((tm, dff), BF16)],
        compiler_params=pltpu.CompilerParams(
            dimension_semantics=("arbitrary",), vmem_limit_bytes=VMEM_LIMIT),
        name="ffn_dense",
    )(hn, h1, w1, w3, w2)


def _moe_kernel(te_ref, nv_ref, src_ref, tok_ref, hn_hbm, w1_ref, w3_ref, w2_ref, o_ref,
                xs_ref, xb_ref, hid_ref, acc_ref, sem, *, tm):
    i, f = pl.program_id(0), pl.program_id(1)
    nf = pl.num_programs(1)
    n_valid = nv_ref[0]
    share = tm // MOE_F_STEPS

    def row_copy(tile, r):
        tok = tok_ref[src_ref[tile] + r]
        return pltpu.make_async_copy(hn_hbm.at[pl.ds(pl.multiple_of(tok * ROW_TILE, ROW_TILE), ROW_TILE)],
                                     xs_ref.at[pl.ds(pl.multiple_of(r * ROW_TILE, ROW_TILE), ROW_TILE)], sem)

    @pl.when((i == 0) & (f == 0))
    def _():
        def body(r, carry):
            row_copy(0, r).start()
            return carry
        lax.fori_loop(0, tm, body, 0)

    @pl.when((f == 0) & (i <= n_valid))
    def _():
        pltpu.make_async_copy(xs_ref, xs_ref, sem).wait()
        for s in range(ROW_TILE):
            xb_ref[:, s * 128:(s + 1) * 128] = xs_ref[pl.ds(s, tm, stride=ROW_TILE), :].astype(BF16)

    @pl.when(i < n_valid)
    def _():
        for r in range(share):
            row_copy(i + 1, f * share + r).start(priority=r % MOE_DMA_QUEUES)
        _swiglu_hidden(xb_ref, w1_ref.at[0], w3_ref.at[0], hid_ref)
        part = _dot(hid_ref[...], w2_ref[0])

        @pl.when(f == 0)
        def _():
            acc_ref[...] = part

        @pl.when(f > 0)
        def _():
            acc_ref[...] += part

    @pl.when(f == nf - 1)
    def _():
        o_ref[...] = jnp.where(i < n_valid, acc_ref[...], 0.0).astype(o_ref.dtype)


def _live_f(i, f, nv):
    return jnp.where(i < nv[0], f, 0)


def _moe_grouped(tile_expert, n_valid, tile_src, sorted_tok, hn, w1, w3, w2, tm):
    d = ROW_TILE * hn.shape[1]
    n_tiles = tile_expert.shape[0]
    dff = w1.shape[2]
    tf = dff // MOE_F_STEPS
    grid_spec = pltpu.PrefetchScalarGridSpec(
        num_scalar_prefetch=4,
        grid=(n_tiles, MOE_F_STEPS),
        in_specs=[
            pl.BlockSpec(memory_space=pl.ANY),
            pl.BlockSpec((1, d, tf), lambda i, f, te, nv, *_: (te[i], 0, _live_f(i, f, nv))),
            pl.BlockSpec((1, d, tf), lambda i, f, te, nv, *_: (te[i], 0, _live_f(i, f, nv))),
            pl.BlockSpec((1, tf, d), lambda i, f, te, nv, *_: (te[i], _live_f(i, f, nv), 0)),
        ],
        out_specs=pl.BlockSpec((tm, d), lambda i, f, *_: (i, 0)),
        scratch_shapes=[pltpu.VMEM((tm * ROW_TILE, 128), F32), pltpu.VMEM((tm, d), BF16),
                        pltpu.VMEM((tm, tf), BF16), pltpu.VMEM((tm, d), F32), pltpu.SemaphoreType.DMA(())],
    )
    return pl.pallas_call(
        functools.partial(_moe_kernel, tm=tm),
        grid_spec=grid_spec,
        out_shape=jax.ShapeDtypeStruct((n_tiles * tm, d), BF16),
        compiler_params=pltpu.CompilerParams(
            dimension_semantics=("arbitrary", "arbitrary"), vmem_limit_bytes=VMEM_LIMIT),
        name="moe_grouped",
    )(tile_expert, n_valid, tile_src, sorted_tok, hn, w1, w3, w2)


def _moe_dispatch(fields, counts_row, tm):
    n = fields.shape[1]
    n_slots = n * TOP_K
    n_tiles = (n_slots + N_EXPERTS * tm + tm - 1) // tm
    counts = counts_row[0, :N_EXPERTS].astype(jnp.int32)
    padded = ((counts + tm - 1) // tm) * tm
    pend = jnp.cumsum(padded)
    pstart = pend - padded
    gstart = jnp.cumsum(counts) - counts
    eid = fields[0:TOP_K].astype(jnp.int32)
    rank = fields[2 * TOP_K:3 * TOP_K].astype(jnp.int32)
    is_e = [eid == e for e in range(N_EXPERTS)]
    lookup = lambda table: sum(jnp.where(m, table[e], 0) for e, m in enumerate(is_e))
    pos = lookup(pstart) + rank
    tok_bits = max(1, (n - 1).bit_length())
    assert n_slots < (1 << (31 - tok_bits))
    tok = jnp.broadcast_to(jnp.arange(n, dtype=jnp.int32)[None, :], (TOP_K, n))
    packed = jnp.sort((((lookup(gstart) + rank) << tok_bits) | tok).reshape(-1))
    sorted_tok = packed & ((1 << tok_bits) - 1)
    sorted_tok = jnp.concatenate([sorted_tok, jnp.zeros((tm,), jnp.int32)])
    tile_start = jnp.arange(n_tiles, dtype=jnp.int32) * tm
    tile_expert = jnp.minimum(jnp.sum((tile_start[:, None] >= pend[None, :]).astype(jnp.int32), axis=1),
                              N_EXPERTS - 1)
    tile_src = jnp.clip(tile_start - (pstart - gstart)[tile_expert], 0, n_slots)
    n_valid = (pend[-1] // tm).astype(jnp.int32).reshape(1)
    return sorted_tok, tile_src, pos, tile_expert, n_valid


def _final_kernel(h1_ref, y0_ref, y1_ref, route_ref, g_ref, o_ref):
    g0 = route_ref[:, TOP_K:TOP_K + 1]
    g1 = route_ref[:, TOP_K + 1:TOP_K + 2]
    h = h1_ref[...] + (g0 * y0_ref[...].astype(F32) + g1 * y1_ref[...].astype(F32))
    o_ref[...] = h * lax.rsqrt(jnp.mean(h * h, axis=-1, keepdims=True) + EPS) * g_ref[...]


def _final(h1, y01, route, g, bsz, total):
    n, d = h1.shape
    seq = total - CHUNK
    tm = _pick_tile(seq, (1024, 128))
    per_seq = seq // tm
    rows = lambda width, base=0: pl.BlockSpec(
        (pl.Element(tm), pl.Element(width)),
        lambda b, k: (pl.multiple_of(base + b * total + CHUNK + k * tm, CHUNK), 0))
    out = pl.pallas_call(
        _final_kernel,
        grid=(bsz, per_seq),
        in_specs=[rows(d), rows(d), rows(d, n), rows(ROUTE_LANES), pl.BlockSpec((1, d), lambda b, k: (0, 0))],
        out_specs=pl.BlockSpec((tm, d), lambda b, k: (b * per_seq + k, 0)),
        out_shape=jax.ShapeDtypeStruct((bsz * seq, d), F32),
        compiler_params=pltpu.CompilerParams(
            dimension_semantics=("arbitrary", "arbitrary"), vmem_limit_bytes=VMEM_LIMIT),
        name="final_norm",
    )(h1, y01, y01, route, g)
    return out.reshape(bsz, seq, d)


def _relayout_in_cols(w):
    d = w.shape[0]

    def halves_first(block):
        return block.reshape(d, RET_HEADS, 2, RET_DK // 2).transpose(0, 2, 1, 3).reshape(d, RET_KEY_WIDTH)

    return jnp.concatenate([w[:, :C_RQ], halves_first(w[:, C_RQ:C_RK]), halves_first(w[:, C_RK:C_RV]),
                            w[:, C_RV:]], axis=1)


def _hg_tables(nb):
    c = CHUNK
    i = np.arange(c)[:, None]
    t = np.arange(c)[None, :]
    blocks, masks, rsel = [], [], []
    for lvl in range(HG_LEVELS):
        s = 1 << lvl
        r = (i // (2 * s)) * (2 * s) + s - 1
        right = i > r
        m = np.where(right, (t > r) & (t <= i), (t > i) & (t <= r))
        blocks.append(m)
        j = t
        same = (i // (2 * s)) == (j // (2 * s))
        mk = same & right & (j <= r)
        masks.append(np.concatenate([mk, mk], axis=1))
        if s < 8:
            rsel.append(np.broadcast_to(right, (c, HG_WIDTH)))
    blocks.append(t <= i)
    blocks.append(t > i)
    eye = np.eye(nb, dtype=np.float32)
    mall = np.concatenate([np.kron(eye, blk.astype(np.float32)) for blk in blocks], axis=0)
    lvl = np.stack(masks, axis=0).astype(np.float32)
    ch = np.arange(HG_WIDTH)
    bones = (ch[:, None] // HG_DK == ch[None, :] // HG_DK).astype(np.float32)
    rsel = np.stack([np.tile(r, (nb, 1)) for r in rsel], axis=0).astype(np.float32)
    return mall, lvl, rsel, bones


def _ret_tables():
    f32 = jnp.float32
    log_gamma = jnp.log1p(-jnp.power(2.0, -5.0 - jnp.arange(RET_HEADS, dtype=f32)))
    n = jnp.arange(CHUNK, dtype=f32)
    lg = log_gamma[:, None]
    causal = jnp.tril(jnp.ones((CHUNK, CHUNK), dtype=bool))
    intra = jnp.exp(jnp.where(causal[None], (n[:, None] - n[None, :])[None] * lg[:, :, None], -jnp.inf))
    scale = RET_DK ** -0.5
    inter = jnp.exp((n[None, :] + 1.0) * lg)
    to_state = jnp.exp((CHUNK - 1.0 - n[None, :]) * lg)
    carry = jnp.exp(CHUNK * lg)[:, 0]
    head_of_v = np.arange(RET_WIDTH) // RET_DV
    head_of_k = (np.arange(RET_KEY_WIDTH) % (RET_KEY_WIDTH // 2)) // (RET_DK // 2)
    inter_t = (inter * scale).T[:, head_of_v]
    to_state_t = to_state.T[:, head_of_k]
    carry_row = carry[head_of_v][None, :]
    bmask = jnp.asarray((head_of_k[:, None] == head_of_v[None, :]).astype(np.float32))
    intra_pairs = (intra * scale).reshape(RET_HEADS // 2, 2, CHUNK, CHUNK)
    intra_pairs = jnp.concatenate([intra_pairs[:, 0], intra_pairs[:, 1]], axis=2)
    return intra_pairs, inter_t, to_state_t, carry_row, bmask


def _s5_tables(lam_re, lam_im, b_re, b_im, c_re, c_im, d_skip, log_step):
    f32 = jnp.float32
    lam = lax.complex(lam_re.astype(f32), lam_im.astype(f32))
    step = jnp.exp(log_step.astype(f32))[:, None]
    lam_dt = lam * step
    lam_bar = jnp.exp(lam_dt)
    b_bar = ((lam_bar - 1.0) / lam)[..., None] * lax.complex(b_re.astype(f32), b_im.astype(f32))
    eye = jnp.eye(S5_NGROUPS, dtype=f32)
    wb_re = jnp.einsum('gph,gk->ghkp', jnp.real(b_bar), eye).reshape(S5_WIDTH, S5_NSTATE)
    wb_im = jnp.einsum('gph,gk->ghkp', jnp.imag(b_bar), eye).reshape(S5_WIDTH, S5_NSTATE)
    wb = jnp.concatenate([wb_re, wb_im], axis=1)
    wc_re = jnp.einsum('ghp,gk->gpkh', c_re.astype(f32), eye).reshape(S5_NSTATE, S5_WIDTH)
    wc_im = jnp.einsum('ghp,gk->gpkh', c_im.astype(f32), eye).reshape(S5_NSTATE, S5_WIDTH)
    wc = jnp.concatenate([wc_re, -wc_im], axis=0)
    t = jnp.arange(CHUNK, dtype=f32)[:, None, None]
    pp = jnp.exp(lam_dt[None] * t).reshape(CHUNK, S5_NSTATE)
    pn = jnp.exp(-lam_dt[None] * t).reshape(CHUNK, S5_NSTATE)
    lam_rows = jnp.stack([jnp.real(lam_bar).reshape(-1), jnp.imag(lam_bar).reshape(-1)], axis=0)
    return (wb.astype(BF16), wc.astype(BF16), jnp.real(pn), jnp.imag(pn), jnp.real(pp), jnp.imag(pp),
            lam_rows, d_skip.astype(f32).reshape(1, S5_WIDTH))


def kernel(x, meta_tokens, norm_mix_g, w_in, s5_lam_re, s5_lam_im, s5_b_re, s5_b_im, s5_c_re, s5_c_im, s5_d, s5_log_step, s5_w_glu, s5_out_g, hg_conv_w, hg_lb_param, hg_out_g, ret_out_g, w_out, norm_ffn_g, ffn_w1, ffn_w3, ffn_w2, moe_router, moe_w1, moe_w3, moe_w2, final_norm_g):
    bsz, seq_len, d = x.shape
    depth = w_in.shape[0]
    total = seq_len + CHUNK
    n_chunks = total // CHUNK
    n = bsz * total

    meta = jnp.broadcast_to(meta_tokens.astype(F32)[None], (bsz, N_META, d))
    tm0 = _pick_tile(total, (TOKEN_TILE, 128))
    head = jnp.concatenate([jnp.zeros((bsz, PAD, d), F32), meta, x[:, :tm0 - CHUNK].astype(F32)], axis=1)
    stream = (x.astype(F32).reshape(bsz * seq_len, d), head)

    pos = (jnp.arange(total) - PAD).astype(F32)
    half = RET_DK // 2
    inv_freq = ROPE_BASE ** (-jnp.arange(half, dtype=F32) / half)
    ang = pos[:, None] * inv_freq[None, :]
    cos_t = jnp.tile(jnp.cos(ang), (1, RET_HEADS))
    sin_t = jnp.tile(jnp.sin(ang), (1, RET_HEADS))

    lb_all = jnp.cumsum(jax.nn.softmax(hg_lb_param.astype(F32), axis=0), axis=0)
    lb_all = lb_all - lb_all[0]

    mall_np, lvl_np, rsel_np, bones_np = _hg_tables(bsz)
    mall = jnp.asarray(mall_np, BF16)
    lvl = jnp.asarray(lvl_np, F32)
    rsel = jnp.asarray(rsel_np, F32)
    bones = jnp.asarray(bones_np, BF16)
    avg = jnp.asarray(bones_np / HG_DK, BF16)
    ltri = jnp.asarray(np.kron(np.eye(bsz, dtype=np.float32),
                               np.tril(np.ones((CHUNK, CHUNK), np.float32))), BF16)
    intra, inter_t, to_state_t, carry_row, bmask = _ret_tables()

    out = None
    for l in range(depth):
        w_in_l = _relayout_in_cols(w_in[l].astype(BF16))
        lb = lb_all[l][None, :]
        lbv = jnp.concatenate([jnp.log(lb), jnp.log1p(-lb), 1.0 - lb], axis=0)
        proj = _in_proj(stream, norm_mix_g[l].astype(F32).reshape(1, d), w_in_l,
                        hg_conv_w[l].astype(F32), lbv, cos_t, sin_t, bsz, total)

        s5c = _s5_tables(s5_lam_re[l], s5_lam_im[l], s5_b_re[l], s5_b_im[l], s5_c_re[l], s5_c_im[l],
                         s5_d[l], s5_log_step[l])
        consts = list(s5c) + [
            s5_w_glu[l].astype(BF16), s5_out_g[l].astype(F32).reshape(1, -1), ltri,
            mall, lvl, rsel, bones, avg, hg_out_g[l].astype(F32).reshape(1, -1),
            intra, inter_t, to_state_t, carry_row, bmask, ret_out_g[l].astype(F32).reshape(1, -1),
        ]
        j = l // 2
        if l % 2 == 0:
            ffn_f32 = [ffn_w1[j], ffn_w3[j], ffn_w2[j]]
        else:
            dff_e = moe_w1.shape[-1]
            ffn_f32 = [moe_w1[j].reshape(N_EXPERTS * d, dff_e), moe_w3[j].reshape(N_EXPERTS * d, dff_e),
                       moe_w2[j].reshape(N_EXPERTS * dff_e, d)]
        mixed, ffn_bf16 = _mixers(proj.reshape(bsz, total, MIX_COLS), consts, bsz, n_chunks,
                                  to_bf16=[a.astype(F32) for a in ffn_f32])
        mixed = mixed.reshape(n, D_MIX)

        g_ffn = norm_ffn_g[l].astype(F32).reshape(1, d)
        w_out_l = w_out[l].astype(BF16)
        if l % 2 == 0:
            h1, hn = _out_proj(mixed, stream, w_out_l, g_ffn, bsz, total)
            stream = (_ffn(hn, h1, *ffn_bf16),)
            y01 = None
        else:
            router_pad = jnp.zeros((d, ROUTE_LANES), F32).at[:, :N_EXPERTS].set(moe_router[j].astype(F32))
            h1, hn, route, fields, counts_row = _out_proj(mixed, stream, w_out_l, g_ffn, bsz, total, router_pad)
            tm = MOE_TILE
            sorted_tok, tile_src, pos_of_slot, tile_expert, n_valid = _moe_dispatch(fields, counts_row, tm)
            e_w1, e_w3, e_w2 = (ffn_bf16[0].reshape(N_EXPERTS, d, dff_e), ffn_bf16[1].reshape(N_EXPERTS, d, dff_e),
                                ffn_bf16[2].reshape(N_EXPERTS, dff_e, d))
            ys = _moe_grouped(tile_expert, n_valid, tile_src, sorted_tok, hn, e_w1, e_w3, e_w2, tm)
            y01 = ys.at[pos_of_slot.reshape(-1)].get(mode='promise_in_bounds')
            if l < depth - 1:
                stream = (h1 + (route[:, TOP_K:TOP_K + 1] * y01[:n].astype(F32)
                                + route[:, TOP_K + 1:TOP_K + 2] * y01[n:].astype(F32)),)

        if l == depth - 1:
            if y01 is None:
                y01 = jnp.zeros((TOP_K * n, d), BF16)
                h1 = stream[0]
                route = jnp.zeros((n, ROUTE_LANES), F32)
            out = _final(h1, y01, route, final_norm_g.astype(F32).reshape(1, d), bsz, total)

    return out.astype(x.dtype)
```

```python
import functools
import math

import jax
import jax.numpy as jnp
import numpy as np
from jax import lax
from jax.experimental import pallas as pl
from jax.experimental.pallas import tpu as pltpu

F32 = jnp.float32
BF16 = jnp.bfloat16

CHUNK = 128
N_META = 16
PAD = CHUNK - N_META
EPS = 1e-6

S5_WIDTH = 256
S5_GROUP = 16
S5_NGROUPS = 16
S5_STATE = 64
S5_NSTATE = S5_NGROUPS * S5_STATE

HG_HEADS = 4
HG_DK = 64
HG_WIDTH = 256
CONV_K = 4
HG_CONV_W = 3 * HG_WIDTH
HG_LEVELS = 7

RET_HEADS = 8
RET_DK = 32
RET_DV = 64
RET_KEY_WIDTH = 256
RET_WIDTH = 512
ROPE_BASE = 10000.0

D_MIX = 1024
IN_COLS = 2816
C_U, C_HQ, C_HGATE, C_RQ, C_RK, C_RV, C_RGATE = 0, 256, 1024, 1280, 1536, 1792, 2304

N_EXPERTS = 8
TOP_K = 2
ROUTE_LANES = 128
ROUTE_FIELDS = 8
MOE_F_STEPS = 2
MOE_TILE = 512
ROW_TILE = 8
MOE_DMA_QUEUES = 2

VMEM_LIMIT = 56 * 1024 * 1024


def _sigmoid(x):
    return 1.0 / (1.0 + jnp.exp(-x))


def _split_bf16(x):
    hi = x.astype(BF16)
    lo = (x - hi.astype(F32)).astype(BF16)
    return hi, lo


def _pick_tile(n, candidates):
    for t in candidates:
        if n % t == 0:
            return t
    raise ValueError(f"no tile in {candidates} divides {n}")


def _dot(a, b):
    return jnp.dot(a, b, preferred_element_type=F32)


def _dot_nt(a, b):
    return lax.dot_general(a, b, (((1,), (1,)), ((), ())), preferred_element_type=F32)


def _dot_tn(a, b):
    return lax.dot_general(a, b, (((0,), (0,)), ((), ())), preferred_element_type=F32)


TOKEN_TILE = 640


def _stream_specs(stream, tm, total):
    d = stream[0].shape[1]
    if len(stream) == 1:
        return [pl.BlockSpec((tm, d), lambda i: (i, 0))], list(stream)
    per, seq = total // tm, total - CHUNK
    x_rows = lambda i: (pl.multiple_of((i // per) * seq + jnp.maximum((i % per) * tm - CHUNK, 0), CHUNK), 0)
    return ([pl.BlockSpec((pl.Element(tm), pl.Element(d)), x_rows),
             pl.BlockSpec((1, tm, d), lambda i: (i // per, 0, 0))], list(stream))


def _stream_tile(refs, tiles_per_seq):
    if len(refs) == 1:
        return refs[0][...]
    x_ref, head_ref = refs
    return jnp.where(pl.program_id(0) % tiles_per_seq == 0, head_ref[0], x_ref[...])


def _in_proj_kernel(*refs, tm, tiles_per_seq):
    *stream, g_ref, w_ref, o_ref = refs
    x = _stream_tile(stream, tiles_per_seq)
    row0 = (pl.program_id(0) % tiles_per_seq) * tm
    for r0 in range(0, tm, tm // 2):
        xh = x[r0:r0 + tm // 2]
        y = xh * lax.rsqrt(jnp.mean(xh * xh, axis=-1, keepdims=True) + EPS) * g_ref[...]
        proj = _dot(y.astype(BF16), w_ref[...])
        rows = row0 + r0 + lax.broadcasted_iota(jnp.int32, (tm // 2, 1), 0)
        o_ref[r0:r0 + tm // 2, :] = jnp.where(rows >= PAD, proj, 0.0)


def _in_proj(stream, g, w_bf16, bsz, total):
    n, d = bsz * total, stream[0].shape[1]
    tm = _pick_tile(total, (TOKEN_TILE, 128))
    specs, operands = _stream_specs(stream, tm, total)
    return pl.pallas_call(
        functools.partial(_in_proj_kernel, tm=tm, tiles_per_seq=total // tm),
        grid=(n // tm,),
        in_specs=specs + [
            pl.BlockSpec((1, d), lambda i: (0, 0)),
            pl.BlockSpec(w_bf16.shape, lambda i: (0, 0), pipeline_mode=pl.Buffered(1)),
        ],
        out_specs=pl.BlockSpec((tm, IN_COLS), lambda i: (i, 0)),
        out_shape=jax.ShapeDtypeStruct((n, IN_COLS), F32),
        compiler_params=pltpu.CompilerParams(
            dimension_semantics=("arbitrary",), vmem_limit_bytes=VMEM_LIMIT),
        name="in_proj",
    )(*operands, g, w_bf16)


def _rows(parts):
    return parts[0] if len(parts) == 1 else jnp.concatenate(parts, axis=0)


def _s5_chunks(u, wb_ref, wc_ref, pn_re_ref, pn_im_ref, pp_re_ref, pp_im_ref, lam_ref,
               d_ref, wglu_ref, g_ref, ltri_ref, st_ref):
    ns = S5_NSTATE
    nb = st_ref.shape[0]
    bu = _dot(u.astype(BF16), wb_ref[...])
    pn_re, pn_im = pn_re_ref[...], pn_im_ref[...]
    w_re, w_im = [], []
    for b in range(nb):
        bu_re = bu[b * CHUNK:(b + 1) * CHUNK, :ns]
        bu_im = bu[b * CHUNK:(b + 1) * CHUNK, ns:]
        w_re.append(pn_re * bu_re - pn_im * bu_im)
        w_im.append(pn_re * bu_im + pn_im * bu_re)
    ltri = ltri_ref[...]
    c_re = _dot(ltri, _rows(w_re).astype(BF16))
    c_im = _dot(ltri, _rows(w_im).astype(BF16))
    lam_re, lam_im = lam_ref[0:1, :], lam_ref[1:2, :]
    pp_re, pp_im = pp_re_ref[...], pp_im_ref[...]
    st_re, st_im = [], []
    for b in range(nb):
        s_re, s_im = st_ref[b, 0:1, :], st_ref[b, 1:2, :]
        z_re = c_re[b * CHUNK:(b + 1) * CHUNK] + (lam_re * s_re - lam_im * s_im)
        z_im = c_im[b * CHUNK:(b + 1) * CHUNK] + (lam_re * s_im + lam_im * s_re)
        t_re = pp_re * z_re - pp_im * z_im
        t_im = pp_re * z_im + pp_im * z_re
        st_ref[b, 0:1, :] = t_re[CHUNK - 1:CHUNK, :]
        st_ref[b, 1:2, :] = t_im[CHUNK - 1:CHUNK, :]
        st_re.append(t_re)
        st_im.append(t_im)
    y = (_dot(_rows(st_re).astype(BF16), wc_ref[0:ns, :]) + _dot(_rows(st_im).astype(BF16), wc_ref[ns:2 * ns, :])
         + d_ref[...] * u)
    y = 0.5 * y * (1.0 + jnp.tanh(math.sqrt(2.0 / math.pi) * (y + 0.044715 * (y * y * y))))
    y = y * _sigmoid(_dot(y.astype(BF16), wglu_ref[...]))
    return y * lax.rsqrt(jnp.mean(y * y, axis=-1, keepdims=True) + EPS) * g_ref[...]


def _hgrn2_chunks(xc_ref, gate, convw_ref, lbv_ref, mall_ref, lvl_ref, rsel_ref, bones_ref, avg_ref,
                  g_ref, st_ref):
    w = HG_WIDTH
    nb = st_ref.shape[0]
    rows = nb * CHUNK
    convs = []
    for b in range(nb):
        conv = None
        for i in range(CONV_K):
            term = xc_ref[b, pl.ds(8 - (CONV_K - 1) + i, CHUNK), :] * convw_ref[i:i + 1, :]
            conv = term if conv is None else conv + term
        convs.append(conv)
    conv = _rows(convs)
    cq, cf, v = conv[:, :w], conv[:, w:2 * w], conv[:, 2 * w:]
    q = cq * _sigmoid(cq)
    log_lb, log_1m_lb, one_m_lb = lbv_ref[0:1, :], lbv_ref[1:2, :], lbv_ref[2:3, :]
    log_sig = jnp.minimum(cf, 0.0) - jnp.log(1.0 + jnp.exp(-jnp.abs(cf)))
    b_ = log_1m_lb + log_sig
    logf = jnp.maximum(log_lb, b_) + jnp.log(1.0 + jnp.exp(-jnp.abs(log_lb - b_)))
    kk = one_m_lb * _sigmoid(-cf)
    hi, lo = _split_bf16(logf)

    lvl_sums = _dot(mall_ref[0:HG_LEVELS * rows, :], hi)
    tail = mall_ref[HG_LEVELS * rows:(HG_LEVELS + 2) * rows, :]
    cum_suf = _dot(tail, hi) + _dot(tail, lo)
    g_cum, g_suffix = cum_suf[:rows], cum_suf[rows:]

    lane = lax.broadcasted_iota(jnp.int32, (1, 2 * HG_DK), 1)
    head_sel = [jnp.where(lane < HG_DK, 1.0, 0.0), jnp.where(lane >= HG_DK, 1.0, 0.0)]
    n_pairs = HG_HEADS // 2
    scores = [[None] * n_pairs for _ in range(nb)]
    for lvl in range(HG_LEVELS):
        s = 1 << lvl
        e = jnp.exp(lvl_sums[lvl * rows:(lvl + 1) * rows])
        if s >= 8:
            qk = jnp.concatenate([(q if (r // s) % 2 else kk)[r:r + s] for r in range(0, rows, s)], axis=0)
        else:
            qk = jnp.where(rsel_ref[lvl] > 0.5, q, kk)
        x = qk * e
        m = lvl_ref[lvl] > 0.5
        for b in range(nb):
            for p in range(n_pairs):
                xp = x[b * CHUNK:(b + 1) * CHUNK, p * 128:(p + 1) * 128]
                rhs = jnp.concatenate([xp * head_sel[0], xp * head_sel[1]], axis=0).astype(BF16)
                sc = _dot_nt(xp.astype(BF16), rhs)
                scores[b][p] = jnp.where(m, sc, 0.0 if scores[b][p] is None else scores[b][p])
    v_bf = v.astype(BF16)
    bones = bones_ref[...]
    qg = (q * jnp.exp(g_cum)).astype(BF16)
    kd = (kk * jnp.exp(g_suffix)).astype(BF16)
    o_rows = []
    for b in range(nb):
        sl = slice(b * CHUNK, (b + 1) * CHUNK)
        o_parts = []
        for p in range(n_pairs):
            vp = v[sl, p * 128:(p + 1) * 128]
            vv = jnp.concatenate([vp * head_sel[0], vp * head_sel[1]], axis=0).astype(BF16)
            o_parts.append(_dot(scores[b][p].astype(BF16), vv))
        st = st_ref[b]
        o_rows.append(jnp.concatenate(o_parts, axis=1) + _dot_nt(qg[sl], st.astype(BF16)))
        upd = _dot_tn(v_bf[sl], kd[sl]) * bones.astype(F32)
        st_ref[b] = st * jnp.exp(g_cum[(b + 1) * CHUNK - 1:(b + 1) * CHUNK, :]) + upd
    o = _rows(o_rows) + _dot((q * kk).astype(BF16), bones) * v
    ms = _dot((o * o).astype(BF16), avg_ref[...])
    return o * lax.rsqrt(ms + EPS) * g_ref[...] * (gate * _sigmoid(gate))


def _ret_chunk(rq, rk, v, cos, sin, intra_ref, inter_ref, tostate_ref, carry_ref, bmask_ref, st_ref):
    hw = RET_KEY_WIDTH // 2

    def rot(t):
        t1, t2 = t[:, :hw], t[:, hw:]
        return jnp.concatenate([t1 * cos - t2 * sin, t1 * sin + t2 * cos], axis=1)

    qr, kr = rot(rq), rot(rk)
    qr_bf, v_bf = qr.astype(BF16), v.astype(BF16)
    lane_k = lax.broadcasted_iota(jnp.int32, (1, RET_KEY_WIDTH), 1) % hw
    lane_v = lax.broadcasted_iota(jnp.int32, (1, 2 * RET_DV), 1)
    v_sel = [jnp.where(lane_v < RET_DV, 1.0, 0.0), jnp.where(lane_v >= RET_DV, 1.0, 0.0)]
    half = RET_DK // 2
    o_parts = []
    for p in range(RET_HEADS // 2):
        sel = [jnp.where((lane_k >= h * half) & (lane_k < (h + 1) * half), 1.0, 0.0)
               for h in (2 * p, 2 * p + 1)]
        rhs = jnp.concatenate([kr * sel[0], kr * sel[1]], axis=0).astype(BF16)
        sc = (_dot_nt(qr_bf, rhs) * intra_ref[p]).astype(BF16)
        vp = v[:, p * 128:(p + 1) * 128]
        vv = jnp.concatenate([vp * v_sel[0], vp * v_sel[1]], axis=0).astype(BF16)
        o_parts.append(_dot(sc, vv))
    o = jnp.concatenate(o_parts, axis=1)
    st = st_ref[...]
    o = o + _dot(qr_bf, st.astype(BF16)) * inter_ref[...]
    kd = (kr * tostate_ref[...]).astype(BF16)
    st_ref[...] = st * carry_ref[...] + _dot_tn(kd, v_bf) * bmask_ref[...]
    return o


def _ret_norm_gate(o, gate, avg_ref, g_ref):
    avg = avg_ref[...]
    outs = []
    for s in range(RET_WIDTH // 256):
        os_ = o[:, s * 256:(s + 1) * 256]
        c = os_ - _dot(os_.astype(BF16), avg)
        outs.append(c * lax.rsqrt(_dot((c * c).astype(BF16), avg) + EPS))
    return jnp.concatenate(outs, axis=1) * g_ref[...] * (gate * _sigmoid(gate))


def _mixer_kernel(proj_ref, cos_ref, sin_ref,
                  wb_ref, wc_ref, pn_re_ref, pn_im_ref, pp_re_ref, pp_im_ref, lam_ref, d_ref,
                  wglu_ref, s5g_ref, ltri_ref,
                  convw_ref, lbv_ref, mall_ref, lvl_ref, rsel_ref, bones_ref, avg_ref, hgg_ref,
                  intra_ref, inter_ref, tostate_ref, carry_ref, bmask_ref, retg_ref,
                  *rest, n_cast, cast_steps):
    cast_in, o_ref, cast_out = rest[:n_cast], rest[n_cast], rest[n_cast + 1:2 * n_cast + 1]
    s5_st, hg_xc, hg_st, ret_st = rest[2 * n_cast + 1:]

    @pl.when(pl.program_id(0) == 0)
    def _():
        s5_st[...] = jnp.zeros_like(s5_st)
        hg_xc[...] = jnp.zeros_like(hg_xc)
        hg_st[...] = jnp.zeros_like(hg_st)
        ret_st[...] = jnp.zeros_like(ret_st)

    if n_cast:
        @pl.when(pl.program_id(0) < cast_steps)
        def _():
            for src, dst in zip(cast_in, cast_out):
                dst[...] = src[...].astype(dst.dtype)

    nb = proj_ref.shape[0]

    def cols(c0, width):
        return _rows([proj_ref[b, :, c0:c0 + width] for b in range(nb)])

    def emit(c0, y):
        for b in range(nb):
            o_ref[b, :, c0:c0 + y.shape[1]] = y[b * CHUNK:(b + 1) * CHUNK].astype(o_ref.dtype)

    y_a = _s5_chunks(cols(C_U, S5_WIDTH), wb_ref, wc_ref, pn_re_ref, pn_im_ref, pp_re_ref, pp_im_ref,
                     lam_ref, d_ref, wglu_ref, s5g_ref, ltri_ref, s5_st)
    emit(0, y_a)

    for b in range(nb):
        hg_xc[b, 8:8 + CHUNK, :] = proj_ref[b, :, C_HQ:C_HQ + HG_CONV_W]
    y_b = _hgrn2_chunks(hg_xc, cols(C_HGATE, HG_WIDTH), convw_ref, lbv_ref,
                        mall_ref, lvl_ref, rsel_ref, bones_ref, avg_ref, hgg_ref, hg_st)
    for b in range(nb):
        hg_xc[b, 0:8, :] = hg_xc[b, CHUNK:CHUNK + 8, :]
    emit(S5_WIDTH, y_b)

    o_c = [_ret_chunk(proj_ref[b, :, C_RQ:C_RQ + RET_KEY_WIDTH], proj_ref[b, :, C_RK:C_RK + RET_KEY_WIDTH],
                      proj_ref[b, :, C_RV:C_RV + RET_WIDTH], cos_ref[...], sin_ref[...],
                      intra_ref, inter_ref, tostate_ref, carry_ref, bmask_ref, ret_st.at[b])
           for b in range(nb)]
    y_c = _ret_norm_gate(_rows(o_c), cols(C_RGATE, RET_WIDTH), avg_ref, retg_ref)
    emit(S5_WIDTH + HG_WIDTH, y_c)


def _const_spec(a):
    nd = a.ndim
    return pl.BlockSpec(a.shape, lambda c, _nd=nd: (0,) * _nd)


def _cast_steps(n_chunks, arrays):
    for steps in range(n_chunks, 0, -1):
        if all(a.shape[0] % (steps * 16) == 0 for a in arrays):
            return steps
    raise ValueError("no slab split")


def _mixers(proj, cos, sin, consts, bsz, n_chunks, to_bf16=()):
    total = proj.shape[1]
    n_cast = len(to_bf16)
    cast_steps = _cast_steps(n_chunks, to_bf16) if n_cast else 0
    slab = lambda c: (jnp.minimum(c, cast_steps - 1), 0)
    cast_specs = [pl.BlockSpec((a.shape[0] // cast_steps, a.shape[1]), slab) for a in to_bf16]
    in_specs = [
        pl.BlockSpec((bsz, CHUNK, IN_COLS), lambda c: (0, c, 0)),
        pl.BlockSpec((CHUNK, 128), lambda c: (c, 0)),
        pl.BlockSpec((CHUNK, 128), lambda c: (c, 0)),
    ] + [_const_spec(a) for a in consts] + cast_specs
    outs = pl.pallas_call(
        functools.partial(_mixer_kernel, n_cast=n_cast, cast_steps=cast_steps),
        grid=(n_chunks,),
        in_specs=in_specs,
        out_specs=[pl.BlockSpec((bsz, CHUNK, D_MIX), lambda c: (0, c, 0))] + cast_specs,
        out_shape=[jax.ShapeDtypeStruct((bsz, total, D_MIX), BF16)]
        + [jax.ShapeDtypeStruct(a.shape, BF16) for a in to_bf16],
        scratch_shapes=[
            pltpu.VMEM((bsz, 2, S5_NSTATE), F32),
            pltpu.VMEM((bsz, CHUNK + 8, HG_CONV_W), F32),
            pltpu.VMEM((bsz, HG_WIDTH, HG_WIDTH), F32),
            pltpu.VMEM((bsz, RET_KEY_WIDTH, RET_WIDTH), F32),
        ],
        compiler_params=pltpu.CompilerParams(
            dimension_semantics=("arbitrary",), vmem_limit_bytes=VMEM_LIMIT),
        name="mixers",
    )(proj, cos, sin, *consts, *to_bf16)
    return outs[0], list(outs[1:])


def _route(hn, router_ref, ltri_ref, count_ref):
    r = router_ref[...]
    r_hi, r_lo = _split_bf16(r)
    h_hi, h_lo = _split_bf16(hn)
    tm = hn.shape[0]
    parts = _dot(jnp.concatenate([h_hi, h_lo], axis=0), jnp.concatenate([r_hi, r_lo], axis=1))
    logits = (parts[:tm, :ROUTE_LANES] + parts[:tm, ROUTE_LANES:]) + (parts[tm:, :ROUTE_LANES] + parts[tm:, ROUTE_LANES:])
    lane_i = lax.broadcasted_iota(jnp.int32, logits.shape, 1)
    lane = lane_i.astype(F32)
    neg = jnp.float32(-jnp.inf)
    logits = jnp.where(lane_i < N_EXPERTS, logits, neg)
    m1 = jnp.max(logits, axis=-1, keepdims=True)
    i1 = jnp.min(jnp.where(logits == m1, lane, float(ROUTE_LANES)), axis=-1, keepdims=True)
    rest = jnp.where(lane == i1, neg, logits)
    m2 = jnp.max(rest, axis=-1, keepdims=True)
    i2 = jnp.min(jnp.where(rest == m2, lane, float(ROUTE_LANES)), axis=-1, keepdims=True)
    e2 = jnp.exp(m2 - m1)
    g1 = 1.0 / (1.0 + e2)
    g2 = e2 / (1.0 + e2)
    onehot = jnp.where((lane == i1) | (lane == i2), 1.0, 0.0)
    before = _dot(ltri_ref[...], onehot.astype(BF16)) + count_ref[...]
    r1 = jnp.sum(jnp.where(lane == i1, before, 0.0), axis=-1, keepdims=True)
    r2 = jnp.sum(jnp.where(lane == i2, before, 0.0), axis=-1, keepdims=True)
    count_ref[...] += jnp.sum(onehot, axis=0, keepdims=True)
    out = jnp.where(lane == 0, i1, 0.0)
    out = jnp.where(lane == 1, i2, out)
    out = jnp.where(lane == 2, g1, out)
    out = jnp.where(lane == 3, g2, out)
    out = jnp.where(lane == 4, r1, out)
    return jnp.where(lane == 5, r2, out)


def _out_proj_kernel(mixed_ref, *refs, routed, n_stream, tiles_per_seq):
    stream, (w_ref, g_ref, *rest) = refs[:n_stream], refs[n_stream:]
    if routed:
        router_ref, ltri_ref, h1_ref, hn_ref, route_ref, fields_ref, count_ref = rest

        @pl.when(pl.program_id(0) == 0)
        def _():
            count_ref[...] = jnp.zeros_like(count_ref)
    else:
        h1_ref, hn_ref = rest
    res = _stream_tile(stream, tiles_per_seq)
    half = res.shape[0] // 2
    hn = []
    for r0 in (0, half):
        h1 = res[r0:r0 + half] + _dot(mixed_ref[r0:r0 + half, :], w_ref[...])
        h1_ref[r0:r0 + half, :] = h1
        hn.append(h1 * lax.rsqrt(jnp.mean(h1 * h1, axis=-1, keepdims=True) + EPS) * g_ref[...])
    hn = jnp.concatenate(hn, axis=0)
    if routed:
        tm = hn.shape[0]
        for s in range(ROW_TILE):
            hn_ref[pl.ds(s, tm, stride=ROW_TILE), :] = hn[:, s * 128:(s + 1) * 128]
        slab = _route(hn, router_ref, ltri_ref, count_ref)
        route_ref[...] = slab
        fields_ref[...] = slab.T[0:ROUTE_FIELDS, :]
    else:
        hn_ref[...] = hn.astype(hn_ref.dtype)


def _out_proj(mixed, stream, w_bf16, g, bsz, total, router_pad=None):
    n, d = bsz * total, stream[0].shape[1]
    tm = _pick_tile(total, (TOKEN_TILE, 128))
    routed = router_pad is not None
    row = lambda i: (i, 0)
    fixed = lambda i: (0, 0)
    stream_specs, stream_operands = _stream_specs(stream, tm, total)
    in_specs = [pl.BlockSpec((tm, D_MIX), row)] + stream_specs + [
        pl.BlockSpec(w_bf16.shape, fixed), pl.BlockSpec((1, d), fixed)]
    out_specs = [pl.BlockSpec((tm, d), row), pl.BlockSpec((tm, d), row)]
    out_shape = [jax.ShapeDtypeStruct((n, d), F32), jax.ShapeDtypeStruct((n, d), BF16)]
    args = [mixed] + stream_operands + [w_bf16, g]
    if routed:
        assert d == ROW_TILE * 128
        out_specs[1] = pl.BlockSpec((tm * ROW_TILE, 128), row)
        out_shape[1] = jax.ShapeDtypeStruct((n * ROW_TILE, 128), F32)
        ltri = jnp.asarray(np.tril(np.ones((tm, tm), np.float32), -1), BF16)
        in_specs += [pl.BlockSpec(router_pad.shape, fixed), pl.BlockSpec((tm, tm), fixed)]
        out_specs += [pl.BlockSpec((tm, ROUTE_LANES), row), pl.BlockSpec((ROUTE_FIELDS, tm), lambda i: (0, i)),
                      pl.BlockSpec((1, ROUTE_LANES), fixed)]
        out_shape += [jax.ShapeDtypeStruct((n, ROUTE_LANES), F32),
                      jax.ShapeDtypeStruct((ROUTE_FIELDS, n), F32),
                      jax.ShapeDtypeStruct((1, ROUTE_LANES), F32)]
        args += [router_pad, ltri]
    return pl.pallas_call(
        functools.partial(_out_proj_kernel, routed=routed, n_stream=len(stream), tiles_per_seq=total // tm),
        grid=(n // tm,),
        in_specs=in_specs,
        out_specs=out_specs,
        out_shape=out_shape,
        compiler_params=pltpu.CompilerParams(
            dimension_semantics=("arbitrary",), vmem_limit_bytes=VMEM_LIMIT),
        name="out_proj_routed" if routed else "out_proj",
    )(*args)


HIDDEN_CHUNK = 256


def _swiglu_hidden(x_ref, w1_ref, w3_ref, hid_ref):
    for c0 in range(0, hid_ref.shape[1], HIDDEN_CHUNK):
        cols = slice(c0, c0 + HIDDEN_CHUNK)
        a = _dot(x_ref[...], w1_ref[:, cols])
        b = _dot(x_ref[...], w3_ref[:, cols])
        hid_ref[:, cols] = (a * _sigmoid(a) * b).astype(BF16)


def _ffn_kernel(hn_ref, h1_ref, w1_ref, w3_ref, w2_ref, o_ref, hid_ref):
    _swiglu_hidden(hn_ref, w1_ref, w3_ref, hid_ref)
    o_ref[...] = h1_ref[...] + _dot(hid_ref[...], w2_ref[...])


def _ffn(hn, h1, w1, w3, w2):
    n, d = h1.shape
    dff = w1.shape[1]
    assert dff % HIDDEN_CHUNK == 0
    tm = _pick_tile(n, (TOKEN_TILE, 128))
    resident = lambda a: pl.BlockSpec(a.shape, lambda i: (0, 0), pipeline_mode=pl.Buffered(1))
    return pl.pallas_call(
        _ffn_kernel,
        grid=(n // tm,),
        in_specs=[
            pl.BlockSpec((tm, d), lambda i: (i, 0)),
            pl.BlockSpec((tm, d), lambda i: (i, 0)),
            resident(w1), resident(w3), resident(w2),
        ],
        out_specs=pl.BlockSpec((tm, d), lambda i: (i, 0)),
        out_shape=jax.ShapeDtypeStruct((n, d), F32),
        scratch_shapes=[pltpu.VMEM((tm, dff), BF16)],
        compiler_params=pltpu.CompilerParams(
            dimension_semantics=("arbitrary",), vmem_limit_bytes=VMEM_LIMIT),
        name="ffn_dense",
    )(hn, h1, w1, w3, w2)


def _moe_kernel(te_ref, nv_ref, src_ref, tok_ref, hn_hbm, w1_ref, w3_ref, w2_ref, o_ref,
                xs_ref, xb_ref, hid_ref, acc_ref, sem, *, tm):
    i, f = pl.program_id(0), pl.program_id(1)
    nf = pl.num_programs(1)
    n_valid = nv_ref[0]
    share = tm // MOE_F_STEPS

    def row_copy(tile, r):
        tok = tok_ref[src_ref[tile] + r]
        return pltpu.make_async_copy(hn_hbm.at[pl.ds(pl.multiple_of(tok * ROW_TILE, ROW_TILE), ROW_TILE)],
                                     xs_ref.at[pl.ds(pl.multiple_of(r * ROW_TILE, ROW_TILE), ROW_TILE)], sem)

    @pl.when((i == 0) & (f == 0))
    def _():
        def body(r, carry):
            row_copy(0, r).start()
            return carry
        lax.fori_loop(0, tm, body, 0)

    @pl.when((f == 0) & (i <= n_valid))
    def _():
        pltpu.make_async_copy(xs_ref, xs_ref, sem).wait()
        for s in range(ROW_TILE):
            xb_ref[:, s * 128:(s + 1) * 128] = xs_ref[pl.ds(s, tm, stride=ROW_TILE), :].astype(BF16)

    @pl.when(i < n_valid)
    def _():
        for r in range(share):
            row_copy(i + 1, f * share + r).start(priority=r % MOE_DMA_QUEUES)
        _swiglu_hidden(xb_ref, w1_ref.at[0], w3_ref.at[0], hid_ref)
        part = _dot(hid_ref[...], w2_ref[0])

        @pl.when(f == 0)
        def _():
            acc_ref[...] = part

        @pl.when(f > 0)
        def _():
            acc_ref[...] += part

    @pl.when(f == nf - 1)
    def _():
        o_ref[...] = jnp.where(i < n_valid, acc_ref[...], 0.0).astype(o_ref.dtype)


def _live_f(i, f, nv):
    return jnp.where(i < nv[0], f, 0)


def _moe_grouped(tile_expert, n_valid, tile_src, sorted_tok, hn, w1, w3, w2, tm):
    d = ROW_TILE * hn.shape[1]
    n_tiles = tile_expert.shape[0]
    dff = w1.shape[2]
    tf = dff // MOE_F_STEPS
    grid_spec = pltpu.PrefetchScalarGridSpec(
        num_scalar_prefetch=4,
        grid=(n_tiles, MOE_F_STEPS),
        in_specs=[
            pl.BlockSpec(memory_space=pl.ANY),
            pl.BlockSpec((1, d, tf), lambda i, f, te, nv, *_: (te[i], 0, _live_f(i, f, nv))),
            pl.BlockSpec((1, d, tf), lambda i, f, te, nv, *_: (te[i], 0, _live_f(i, f, nv))),
            pl.BlockSpec((1, tf, d), lambda i, f, te, nv, *_: (te[i], _live_f(i, f, nv), 0)),
        ],
        out_specs=pl.BlockSpec((tm, d), lambda i, f, *_: (i, 0)),
        scratch_shapes=[pltpu.VMEM((tm * ROW_TILE, 128), F32), pltpu.VMEM((tm, d), BF16),
                        pltpu.VMEM((tm, tf), BF16), pltpu.VMEM((tm, d), F32), pltpu.SemaphoreType.DMA(())],
    )
    return pl.pallas_call(
        functools.partial(_moe_kernel, tm=tm),
        grid_spec=grid_spec,
        out_shape=jax.ShapeDtypeStruct((n_tiles * tm, d), BF16),
        compiler_params=pltpu.CompilerParams(
            dimension_semantics=("arbitrary", "arbitrary"), vmem_limit_bytes=VMEM_LIMIT),
        name="moe_grouped",
    )(tile_expert, n_valid, tile_src, sorted_tok, hn, w1, w3, w2)


def _moe_dispatch(fields, counts_row, tm):
    n = fields.shape[1]
    n_slots = n * TOP_K
    n_tiles = (n_slots + N_EXPERTS * tm + tm - 1) // tm
    counts = counts_row[0, :N_EXPERTS].astype(jnp.int32)
    padded = ((counts + tm - 1) // tm) * tm
    pend = jnp.cumsum(padded)
    pstart = pend - padded
    gstart = jnp.cumsum(counts) - counts
    eid = fields[0:TOP_K].astype(jnp.int32)
    rank = fields[2 * TOP_K:3 * TOP_K].astype(jnp.int32)
    is_e = [eid == e for e in range(N_EXPERTS)]
    lookup = lambda table: sum(jnp.where(m, table[e], 0) for e, m in enumerate(is_e))
    pos = lookup(pstart) + rank
    tok_bits = max(1, (n - 1).bit_length())
    assert n_slots < (1 << (31 - tok_bits))
    tok = jnp.broadcast_to(jnp.arange(n, dtype=jnp.int32)[None, :], (TOP_K, n))
    packed = jnp.sort((((lookup(gstart) + rank) << tok_bits) | tok).reshape(-1))
    sorted_tok = packed & ((1 << tok_bits) - 1)
    sorted_tok = jnp.concatenate([sorted_tok, jnp.zeros((tm,), jnp.int32)])
    tile_start = jnp.arange(n_tiles, dtype=jnp.int32) * tm
    tile_expert = jnp.minimum(jnp.sum((tile_start[:, None] >= pend[None, :]).astype(jnp.int32), axis=1),
                              N_EXPERTS - 1)
    tile_src = jnp.clip(tile_start - (pstart - gstart)[tile_expert], 0, n_slots)
    n_valid = (pend[-1] // tm).astype(jnp.int32).reshape(1)
    return sorted_tok, tile_src, pos, tile_expert, n_valid


def _final_kernel(h1_ref, y0_ref, y1_ref, route_ref, g_ref, o_ref):
    g0 = route_ref[:, TOP_K:TOP_K + 1]
    g1 = route_ref[:, TOP_K + 1:TOP_K + 2]
    h = h1_ref[...] + (g0 * y0_ref[...].astype(F32) + g1 * y1_ref[...].astype(F32))
    o_ref[...] = h * lax.rsqrt(jnp.mean(h * h, axis=-1, keepdims=True) + EPS) * g_ref[...]


def _final(h1, y01, route, g, bsz, total):
    n, d = h1.shape
    seq = total - CHUNK
    tm = _pick_tile(seq, (1024, 128))
    per_seq = seq // tm
    rows = lambda width, base=0: pl.BlockSpec(
        (pl.Element(tm), pl.Element(width)),
        lambda b, k: (pl.multiple_of(base + b * total + CHUNK + k * tm, CHUNK), 0))
    out = pl.pallas_call(
        _final_kernel,
        grid=(bsz, per_seq),
        in_specs=[rows(d), rows(d), rows(d, n), rows(ROUTE_LANES), pl.BlockSpec((1, d), lambda b, k: (0, 0))],
        out_specs=pl.BlockSpec((tm, d), lambda b, k: (b * per_seq + k, 0)),
        out_shape=jax.ShapeDtypeStruct((bsz * seq, d), F32),
        compiler_params=pltpu.CompilerParams(
            dimension_semantics=("arbitrary", "arbitrary"), vmem_limit_bytes=VMEM_LIMIT),
        name="final_norm",
    )(h1, y01, y01, route, g)
    return out.reshape(bsz, seq, d)


def _relayout_in_cols(w):
    d = w.shape[0]

    def halves_first(block):
        return block.reshape(d, RET_HEADS, 2, RET_DK // 2).transpose(0, 2, 1, 3).reshape(d, RET_KEY_WIDTH)

    return jnp.concatenate([w[:, :C_RQ], halves_first(w[:, C_RQ:C_RK]), halves_first(w[:, C_RK:C_RV]),
                            w[:, C_RV:]], axis=1)


def _hg_tables(nb):
    c = CHUNK
    i = np.arange(c)[:, None]
    t = np.arange(c)[None, :]
    blocks, masks, rsel = [], [], []
    for lvl in range(HG_LEVELS):
        s = 1 << lvl
        r = (i // (2 * s)) * (2 * s) + s - 1
        right = i > r
        m = np.where(right, (t > r) & (t <= i), (t > i) & (t <= r))
        blocks.append(m)
        j = t
        same = (i // (2 * s)) == (j // (2 * s))
        mk = same & right & (j <= r)
        masks.append(np.concatenate([mk, mk], axis=1))
        if s < 8:
            rsel.append(np.broadcast_to(right, (c, HG_WIDTH)))
    blocks.append(t <= i)
    blocks.append(t > i)
    eye = np.eye(nb, dtype=np.float32)
    mall = np.concatenate([np.kron(eye, blk.astype(np.float32)) for blk in blocks], axis=0)
    lvl = np.stack(masks, axis=0).astype(np.float32)
    ch = np.arange(HG_WIDTH)
    bones = (ch[:, None] // HG_DK == ch[None, :] // HG_DK).astype(np.float32)
    rsel = np.stack([np.tile(r, (nb, 1)) for r in rsel], axis=0).astype(np.float32)
    return mall, lvl, rsel, bones


def _ret_tables():
    f32 = jnp.float32
    log_gamma = jnp.log1p(-jnp.power(2.0, -5.0 - jnp.arange(RET_HEADS, dtype=f32)))
    n = jnp.arange(CHUNK, dtype=f32)
    lg = log_gamma[:, None]
    causal = jnp.tril(jnp.ones((CHUNK, CHUNK), dtype=bool))
    intra = jnp.exp(jnp.where(causal[None], (n[:, None] - n[None, :])[None] * lg[:, :, None], -jnp.inf))
    scale = RET_DK ** -0.5
    inter = jnp.exp((n[None, :] + 1.0) * lg)
    to_state = jnp.exp((CHUNK - 1.0 - n[None, :]) * lg)
    carry = jnp.exp(CHUNK * lg)[:, 0]
    head_of_v = np.arange(RET_WIDTH) // RET_DV
    head_of_k = (np.arange(RET_KEY_WIDTH) % (RET_KEY_WIDTH // 2)) // (RET_DK // 2)
    inter_t = (inter * scale).T[:, head_of_v]
    to_state_t = to_state.T[:, head_of_k]
    carry_row = carry[head_of_v][None, :]
    bmask = jnp.asarray((head_of_k[:, None] == head_of_v[None, :]).astype(np.float32))
    intra_pairs = (intra * scale).reshape(RET_HEADS // 2, 2, CHUNK, CHUNK)
    intra_pairs = jnp.concatenate([intra_pairs[:, 0], intra_pairs[:, 1]], axis=2)
    return intra_pairs, inter_t, to_state_t, carry_row, bmask


def _s5_tables(lam_re, lam_im, b_re, b_im, c_re, c_im, d_skip, log_step):
    f32 = jnp.float32
    lam = lax.complex(lam_re.astype(f32), lam_im.astype(f32))
    step = jnp.exp(log_step.astype(f32))[:, None]
    lam_dt = lam * step
    lam_bar = jnp.exp(lam_dt)
    b_bar = ((lam_bar - 1.0) / lam)[..., None] * lax.complex(b_re.astype(f32), b_im.astype(f32))
    eye = jnp.eye(S5_NGROUPS, dtype=f32)
    wb_re = jnp.einsum('gph,gk->ghkp', jnp.real(b_bar), eye).reshape(S5_WIDTH, S5_NSTATE)
    wb_im = jnp.einsum('gph,gk->ghkp', jnp.imag(b_bar), eye).reshape(S5_WIDTH, S5_NSTATE)
    wb = jnp.concatenate([wb_re, wb_im], axis=1)
    wc_re = jnp.einsum('ghp,gk->gpkh', c_re.astype(f32), eye).reshape(S5_NSTATE, S5_WIDTH)
    wc_im = jnp.einsum('ghp,gk->gpkh', c_im.astype(f32), eye).reshape(S5_NSTATE, S5_WIDTH)
    wc = jnp.concatenate([wc_re, -wc_im], axis=0)
    t = jnp.arange(CHUNK, dtype=f32)[:, None, None]
    pp = jnp.exp(lam_dt[None] * t).reshape(CHUNK, S5_NSTATE)
    pn = jnp.exp(-lam_dt[None] * t).reshape(CHUNK, S5_NSTATE)
    lam_rows = jnp.stack([jnp.real(lam_bar).reshape(-1), jnp.imag(lam_bar).reshape(-1)], axis=0)
    return (wb.astype(BF16), wc.astype(BF16), jnp.real(pn), jnp.imag(pn), jnp.real(pp), jnp.imag(pp),
            lam_rows, d_skip.astype(f32).reshape(1, S5_WIDTH))


def kernel(x, meta_tokens, norm_mix_g, w_in, s5_lam_re, s5_lam_im, s5_b_re, s5_b_im, s5_c_re, s5_c_im, s5_d, s5_log_step, s5_w_glu, s5_out_g, hg_conv_w, hg_lb_param, hg_out_g, ret_out_g, w_out, norm_ffn_g, ffn_w1, ffn_w3, ffn_w2, moe_router, moe_w1, moe_w3, moe_w2, final_norm_g):
    bsz, seq_len, d = x.shape
    depth = w_in.shape[0]
    total = seq_len + CHUNK
    n_chunks = total // CHUNK
    n = bsz * total

    meta = jnp.broadcast_to(meta_tokens.astype(F32)[None], (bsz, N_META, d))
    tm0 = _pick_tile(total, (TOKEN_TILE, 128))
    head = jnp.concatenate([jnp.zeros((bsz, PAD, d), F32), meta, x[:, :tm0 - CHUNK].astype(F32)], axis=1)
    stream = (x.astype(F32).reshape(bsz * seq_len, d), head)

    pos = (jnp.arange(total) - PAD).astype(F32)
    half = RET_DK // 2
    inv_freq = ROPE_BASE ** (-jnp.arange(half, dtype=F32) / half)
    ang = pos[:, None] * inv_freq[None, :]
    cos_t = jnp.tile(jnp.cos(ang), (1, RET_HEADS))
    sin_t = jnp.tile(jnp.sin(ang), (1, RET_HEADS))

    lb_all = jnp.cumsum(jax.nn.softmax(hg_lb_param.astype(F32), axis=0), axis=0)
    lb_all = lb_all - lb_all[0]

    mall_np, lvl_np, rsel_np, bones_np = _hg_tables(bsz)
    mall = jnp.asarray(mall_np, BF16)
    lvl = jnp.asarray(lvl_np, F32)
    rsel = jnp.asarray(rsel_np, F32)
    bones = jnp.asarray(bones_np, BF16)
    avg = jnp.asarray(bones_np / HG_DK, BF16)
    ltri = jnp.asarray(np.kron(np.eye(bsz, dtype=np.float32),
                               np.tril(np.ones((CHUNK, CHUNK), np.float32))), BF16)
    intra, inter_t, to_state_t, carry_row, bmask = _ret_tables()
    s5_all = jax.vmap(_s5_tables)(s5_lam_re, s5_lam_im, s5_b_re, s5_b_im, s5_c_re, s5_c_im, s5_d, s5_log_step)

    out = None
    for l in range(depth):
        w_in_l = _relayout_in_cols(w_in[l].astype(BF16))
        proj = _in_proj(stream, norm_mix_g[l].astype(F32).reshape(1, d), w_in_l, bsz, total)

        s5c = [t[l] for t in s5_all]
        lb = lb_all[l][None, :]
        lbv = jnp.concatenate([jnp.log(lb), jnp.log1p(-lb), 1.0 - lb], axis=0)
        consts = list(s5c) + [
            s5_w_glu[l].astype(BF16), s5_out_g[l].astype(F32).reshape(1, -1), ltri,
            hg_conv_w[l].astype(F32), lbv, mall, lvl, rsel, bones, avg, hg_out_g[l].astype(F32).reshape(1, -1),
            intra, inter_t, to_state_t, carry_row, bmask, ret_out_g[l].astype(F32).reshape(1, -1),
        ]
        j = l // 2
        if l % 2 == 0:
            ffn_f32 = [ffn_w1[j], ffn_w3[j], ffn_w2[j]]
        else:
            dff_e = moe_w1.shape[-1]
            ffn_f32 = [moe_w1[j].reshape(N_EXPERTS * d, dff_e), moe_w3[j].reshape(N_EXPERTS * d, dff_e),
                       moe_w2[j].reshape(N_EXPERTS * dff_e, d)]
        mixed, ffn_bf16 = _mixers(proj.reshape(bsz, total, IN_COLS), cos_t, sin_t, consts, bsz, n_chunks,
                                  to_bf16=[a.astype(F32) for a in ffn_f32])
        mixed = mixed.reshape(n, D_MIX)

        g_ffn = norm_ffn_g[l].astype(F32).reshape(1, d)
        w_out_l = w_out[l].astype(BF16)
        if l % 2 == 0:
            h1, hn = _out_proj(mixed, stream, w_out_l, g_ffn, bsz, total)
            stream = (_ffn(hn, h1, *ffn_bf16),)
            y01 = None
        else:
            router_pad = jnp.zeros((d, ROUTE_LANES), F32).at[:, :N_EXPERTS].set(moe_router[j].astype(F32))
            h1, hn, route, fields, counts_row = _out_proj(mixed, stream, w_out_l, g_ffn, bsz, total, router_pad)
            tm = MOE_TILE
            sorted_tok, tile_src, pos_of_slot, tile_expert, n_valid = _moe_dispatch(fields, counts_row, tm)
            e_w1, e_w3, e_w2 = (ffn_bf16[0].reshape(N_EXPERTS, d, dff_e), ffn_bf16[1].reshape(N_EXPERTS, d, dff_e),
                                ffn_bf16[2].reshape(N_EXPERTS, dff_e, d))
            ys = _moe_grouped(tile_expert, n_valid, tile_src, sorted_tok, hn, e_w1, e_w3, e_w2, tm)
            y01 = ys.at[pos_of_slot.reshape(-1)].get(mode='promise_in_bounds')
            if l < depth - 1:
                stream = (h1 + (route[:, TOP_K:TOP_K + 1] * y01[:n].astype(F32)
                                + route[:, TOP_K + 1:TOP_K + 2] * y01[n:].astype(F32)),)

        if l == depth - 1:
            if y01 is None:
                y01 = jnp.zeros((TOP_K * n, d), BF16)
                h1 = stream[0]
                route = jnp.zeros((n, ROUTE_LANES), F32)
            out = _final(h1, y01, route, final_norm_g.astype(F32).reshape(1, d), bsz, total)

    return out.astype(x.dtype)
```

```python
import functools
import math

import jax
import jax.numpy as jnp
import numpy as np
from jax import lax
from jax.experimental import pallas as pl
from jax.experimental.pallas import tpu as pltpu

F32 = jnp.float32
BF16 = jnp.bfloat16

CHUNK = 128
N_META = 16
PAD = CHUNK - N_META
EPS = 1e-6

S5_WIDTH = 256
S5_GROUP = 16
S5_NGROUPS = 16
S5_STATE = 64
S5_NSTATE = S5_NGROUPS * S5_STATE

HG_HEADS = 4
HG_DK = 64
HG_WIDTH = 256
CONV_K = 4
HG_CONV_W = 3 * HG_WIDTH
HG_LEVELS = 7

RET_HEADS = 8
RET_DK = 32
RET_DV = 64
RET_KEY_WIDTH = 256
RET_WIDTH = 512
ROPE_BASE = 10000.0

D_MIX = 1024
IN_COLS = 2816
C_U, C_HQ, C_HGATE, C_RQ, C_RK, C_RV, C_RGATE = 0, 256, 1024, 1280, 1536, 1792, 2304

N_EXPERTS = 8
TOP_K = 2
ROUTE_LANES = 128
ROUTE_FIELDS = 8
MOE_F_STEPS = 2
MOE_TILE = 512
ROW_TILE = 8
MOE_DMA_QUEUES = 2

VMEM_LIMIT = 56 * 1024 * 1024


def _sigmoid(x):
    return 1.0 / (1.0 + jnp.exp(-x))


def _split_bf16(x):
    hi = x.astype(BF16)
    lo = (x - hi.astype(F32)).astype(BF16)
    return hi, lo


def _pick_tile(n, candidates):
    for t in candidates:
        if n % t == 0:
            return t
    raise ValueError(f"no tile in {candidates} divides {n}")


def _dot(a, b):
    return jnp.dot(a, b, preferred_element_type=F32)


def _dot_nt(a, b):
    return lax.dot_general(a, b, (((1,), (1,)), ((), ())), preferred_element_type=F32)


def _dot_tn(a, b):
    return lax.dot_general(a, b, (((0,), (0,)), ((), ())), preferred_element_type=F32)


TOKEN_TILE = 640


def _stream_specs(stream, tm, total):
    d = stream[0].shape[1]
    if len(stream) == 1:
        return [pl.BlockSpec((tm, d), lambda i: (i, 0))], list(stream)
    per, seq = total // tm, total - CHUNK
    x_rows = lambda i: (pl.multiple_of((i // per) * seq + jnp.maximum((i % per) * tm - CHUNK, 0), CHUNK), 0)
    return ([pl.BlockSpec((pl.Element(tm), pl.Element(d)), x_rows),
             pl.BlockSpec((1, tm, d), lambda i: (i // per, 0, 0))], list(stream))


def _stream_tile(refs, tiles_per_seq):
    if len(refs) == 1:
        return refs[0][...]
    x_ref, head_ref = refs
    return jnp.where(pl.program_id(0) % tiles_per_seq == 0, head_ref[0], x_ref[...])


def _in_proj_kernel(*refs, tm, tiles_per_seq):
    *stream, g_ref, w_ref, o_ref = refs
    x = _stream_tile(stream, tiles_per_seq)
    y = x * lax.rsqrt(jnp.mean(x * x, axis=-1, keepdims=True) + EPS) * g_ref[...]
    proj = _dot(y.astype(BF16), w_ref[...])
    row0 = (pl.program_id(0) % tiles_per_seq) * tm
    rows = row0 + lax.broadcasted_iota(jnp.int32, (tm, 1), 0)
    o_ref[...] = jnp.where(rows >= PAD, proj, 0.0)


def _in_proj(stream, g, w_bf16, bsz, total):
    n, d = bsz * total, stream[0].shape[1]
    tm = _pick_tile(total, (TOKEN_TILE, 128))
    specs, operands = _stream_specs(stream, tm, total)
    return pl.pallas_call(
        functools.partial(_in_proj_kernel, tm=tm, tiles_per_seq=total // tm),
        grid=(n // tm,),
        in_specs=specs + [
            pl.BlockSpec((1, d), lambda i: (0, 0)),
            pl.BlockSpec(w_bf16.shape, lambda i: (0, 0), pipeline_mode=pl.Buffered(1)),
        ],
        out_specs=pl.BlockSpec((tm, IN_COLS), lambda i: (i, 0)),
        out_shape=jax.ShapeDtypeStruct((n, IN_COLS), F32),
        compiler_params=pltpu.CompilerParams(
            dimension_semantics=("arbitrary",), vmem_limit_bytes=VMEM_LIMIT),
        name="in_proj",
    )(*operands, g, w_bf16)


def _rows(parts):
    return parts[0] if len(parts) == 1 else jnp.concatenate(parts, axis=0)


def _s5_chunks(u, wb_ref, wc_ref, pn_re_ref, pn_im_ref, pp_re_ref, pp_im_ref, lam_ref,
               d_ref, wglu_ref, g_ref, ltri_ref, st_ref):
    ns = S5_NSTATE
    nb = st_ref.shape[0]
    bu = _dot(u.astype(BF16), wb_ref[...])
    pn_re, pn_im = pn_re_ref[...], pn_im_ref[...]
    w_re, w_im = [], []
    for b in range(nb):
        bu_re = bu[b * CHUNK:(b + 1) * CHUNK, :ns]
        bu_im = bu[b * CHUNK:(b + 1) * CHUNK, ns:]
        w_re.append(pn_re * bu_re - pn_im * bu_im)
        w_im.append(pn_re * bu_im + pn_im * bu_re)
    ltri = ltri_ref[...]
    c_re = _dot(ltri, _rows(w_re).astype(BF16))
    c_im = _dot(ltri, _rows(w_im).astype(BF16))
    lam_re, lam_im = lam_ref[0:1, :], lam_ref[1:2, :]
    pp_re, pp_im = pp_re_ref[...], pp_im_ref[...]
    st_re, st_im = [], []
    for b in range(nb):
        s_re, s_im = st_ref[b, 0:1, :], st_ref[b, 1:2, :]
        z_re = c_re[b * CHUNK:(b + 1) * CHUNK] + (lam_re * s_re - lam_im * s_im)
        z_im = c_im[b * CHUNK:(b + 1) * CHUNK] + (lam_re * s_im + lam_im * s_re)
        t_re = pp_re * z_re - pp_im * z_im
        t_im = pp_re * z_im + pp_im * z_re
        st_ref[b, 0:1, :] = t_re[CHUNK - 1:CHUNK, :]
        st_ref[b, 1:2, :] = t_im[CHUNK - 1:CHUNK, :]
        st_re.append(t_re)
        st_im.append(t_im)
    y = (_dot(_rows(st_re).astype(BF16), wc_ref[0:ns, :]) + _dot(_rows(st_im).astype(BF16), wc_ref[ns:2 * ns, :])
         + d_ref[...] * u)
    y = 0.5 * y * (1.0 + jnp.tanh(math.sqrt(2.0 / math.pi) * (y + 0.044715 * (y * y * y))))
    y = y * _sigmoid(_dot(y.astype(BF16), wglu_ref[...]))
    return y * lax.rsqrt(jnp.mean(y * y, axis=-1, keepdims=True) + EPS) * g_ref[...]


def _hgrn2_chunks(xc_ref, gate, convw_ref, lbv_ref, mall_ref, lvl_ref, rsel_ref, bones_ref, avg_ref,
                  g_ref, st_ref):
    w = HG_WIDTH
    nb = st_ref.shape[0]
    rows = nb * CHUNK
    convs = []
    for b in range(nb):
        conv = None
        for i in range(CONV_K):
            term = xc_ref[b, pl.ds(8 - (CONV_K - 1) + i, CHUNK), :] * convw_ref[i:i + 1, :]
            conv = term if conv is None else conv + term
        convs.append(conv)
    conv = _rows(convs)
    cq, cf, v = conv[:, :w], conv[:, w:2 * w], conv[:, 2 * w:]
    q = cq * _sigmoid(cq)
    log_lb, log_1m_lb, one_m_lb = lbv_ref[0:1, :], lbv_ref[1:2, :], lbv_ref[2:3, :]
    log_sig = jnp.minimum(cf, 0.0) - jnp.log(1.0 + jnp.exp(-jnp.abs(cf)))
    b_ = log_1m_lb + log_sig
    logf = jnp.maximum(log_lb, b_) + jnp.log(1.0 + jnp.exp(-jnp.abs(log_lb - b_)))
    kk = one_m_lb * _sigmoid(-cf)
    hi, lo = _split_bf16(logf)

    lvl_sums = _dot(mall_ref[0:HG_LEVELS * rows, :], hi)
    tail = mall_ref[HG_LEVELS * rows:(HG_LEVELS + 2) * rows, :]
    cum_suf = _dot(tail, hi) + _dot(tail, lo)
    g_cum, g_suffix = cum_suf[:rows], cum_suf[rows:]

    lane = lax.broadcasted_iota(jnp.int32, (1, 2 * HG_DK), 1)
    head_sel = [jnp.where(lane < HG_DK, 1.0, 0.0), jnp.where(lane >= HG_DK, 1.0, 0.0)]
    n_pairs = HG_HEADS // 2
    scores = [[None] * n_pairs for _ in range(nb)]
    for lvl in range(HG_LEVELS):
        s = 1 << lvl
        e = jnp.exp(lvl_sums[lvl * rows:(lvl + 1) * rows])
        if s >= 8:
            qk = jnp.concatenate([(q if (r // s) % 2 else kk)[r:r + s] for r in range(0, rows, s)], axis=0)
        else:
            qk = jnp.where(rsel_ref[lvl] > 0.5, q, kk)
        x = qk * e
        m = lvl_ref[lvl] > 0.5
        for b in range(nb):
            for p in range(n_pairs):
                xp = x[b * CHUNK:(b + 1) * CHUNK, p * 128:(p + 1) * 128]
                rhs = jnp.concatenate([xp * head_sel[0], xp * head_sel[1]], axis=0).astype(BF16)
                sc = _dot_nt(xp.astype(BF16), rhs)
                scores[b][p] = jnp.where(m, sc, 0.0 if scores[b][p] is None else scores[b][p])
    v_bf = v.astype(BF16)
    bones = bones_ref[...]
    qg = (q * jnp.exp(g_cum)).astype(BF16)
    kd = (kk * jnp.exp(g_suffix)).astype(BF16)
    o_rows = []
    for b in range(nb):
        sl = slice(b * CHUNK, (b + 1) * CHUNK)
        o_parts = []
        for p in range(n_pairs):
            vp = v[sl, p * 128:(p + 1) * 128]
            vv = jnp.concatenate([vp * head_sel[0], vp * head_sel[1]], axis=0).astype(BF16)
            o_parts.append(_dot(scores[b][p].astype(BF16), vv))
        st = st_ref[b]
        o_rows.append(jnp.concatenate(o_parts, axis=1) + _dot_nt(qg[sl], st.astype(BF16)))
        upd = _dot_tn(v_bf[sl], kd[sl]) * bones.astype(F32)
        st_ref[b] = st * jnp.exp(g_cum[(b + 1) * CHUNK - 1:(b + 1) * CHUNK, :]) + upd
    o = _rows(o_rows) + _dot((q * kk).astype(BF16), bones) * v
    ms = _dot((o * o).astype(BF16), avg_ref[...])
    return o * lax.rsqrt(ms + EPS) * g_ref[...] * (gate * _sigmoid(gate))


def _ret_chunk(rq, rk, v, cos, sin, intra_ref, inter_ref, tostate_ref, carry_ref, bmask_ref, st_ref):
    hw = RET_KEY_WIDTH // 2

    def rot(t):
        t1, t2 = t[:, :hw], t[:, hw:]
        return jnp.concatenate([t1 * cos - t2 * sin, t1 * sin + t2 * cos], axis=1)

    qr, kr = rot(rq), rot(rk)
    qr_bf, v_bf = qr.astype(BF16), v.astype(BF16)
    lane_k = lax.broadcasted_iota(jnp.int32, (1, RET_KEY_WIDTH), 1) % hw
    lane_v = lax.broadcasted_iota(jnp.int32, (1, 2 * RET_DV), 1)
    v_sel = [jnp.where(lane_v < RET_DV, 1.0, 0.0), jnp.where(lane_v >= RET_DV, 1.0, 0.0)]
    half = RET_DK // 2
    o_parts = []
    for p in range(RET_HEADS // 2):
        sel = [jnp.where((lane_k >= h * half) & (lane_k < (h + 1) * half), 1.0, 0.0)
               for h in (2 * p, 2 * p + 1)]
        rhs = jnp.concatenate([kr * sel[0], kr * sel[1]], axis=0).astype(BF16)
        sc = (_dot_nt(qr_bf, rhs) * intra_ref[p]).astype(BF16)
        vp = v[:, p * 128:(p + 1) * 128]
        vv = jnp.concatenate([vp * v_sel[0], vp * v_sel[1]], axis=0).astype(BF16)
        o_parts.append(_dot(sc, vv))
    o = jnp.concatenate(o_parts, axis=1)
    st = st_ref[...]
    o = o + _dot(qr_bf, st.astype(BF16)) * inter_ref[...]
    kd = (kr * tostate_ref[...]).astype(BF16)
    st_ref[...] = st * carry_ref[...] + _dot_tn(kd, v_bf) * bmask_ref[...]
    return o


def _ret_norm_gate(o, gate, avg_ref, g_ref):
    avg = avg_ref[...]
    outs = []
    for s in range(RET_WIDTH // 256):
        os_ = o[:, s * 256:(s + 1) * 256]
        c = os_ - _dot(os_.astype(BF16), avg)
        outs.append(c * lax.rsqrt(_dot((c * c).astype(BF16), avg) + EPS))
    return jnp.concatenate(outs, axis=1) * g_ref[...] * (gate * _sigmoid(gate))


def _mixer_kernel(proj_ref, cos_ref, sin_ref,
                  wb_ref, wc_ref, pn_re_ref, pn_im_ref, pp_re_ref, pp_im_ref, lam_ref, d_ref,
                  wglu_ref, s5g_ref, ltri_ref,
                  convw_ref, lbv_ref, mall_ref, lvl_ref, rsel_ref, bones_ref, avg_ref, hgg_ref,
                  intra_ref, inter_ref, tostate_ref, carry_ref, bmask_ref, retg_ref,
                  *rest, n_cast, cast_steps):
    cast_in, o_ref, cast_out = rest[:n_cast], rest[n_cast], rest[n_cast + 1:2 * n_cast + 1]
    s5_st, hg_xc, hg_st, ret_st = rest[2 * n_cast + 1:]

    @pl.when(pl.program_id(0) == 0)
    def _():
        s5_st[...] = jnp.zeros_like(s5_st)
        hg_xc[...] = jnp.zeros_like(hg_xc)
        hg_st[...] = jnp.zeros_like(hg_st)
        ret_st[...] = jnp.zeros_like(ret_st)

    if n_cast:
        @pl.when(pl.program_id(0) < cast_steps)
        def _():
            for src, dst in zip(cast_in, cast_out):
                dst[...] = src[...].astype(dst.dtype)

    nb = proj_ref.shape[0]

    def cols(c0, width):
        return _rows([proj_ref[b, :, c0:c0 + width] for b in range(nb)])

    def emit(c0, y):
        for b in range(nb):
            o_ref[b, :, c0:c0 + y.shape[1]] = y[b * CHUNK:(b + 1) * CHUNK].astype(o_ref.dtype)

    y_a = _s5_chunks(cols(C_U, S5_WIDTH), wb_ref, wc_ref, pn_re_ref, pn_im_ref, pp_re_ref, pp_im_ref,
                     lam_ref, d_ref, wglu_ref, s5g_ref, ltri_ref, s5_st)
    emit(0, y_a)

    for b in range(nb):
        hg_xc[b, 8:8 + CHUNK, :] = proj_ref[b, :, C_HQ:C_HQ + HG_CONV_W]
    y_b = _hgrn2_chunks(hg_xc, cols(C_HGATE, HG_WIDTH), convw_ref, lbv_ref,
                        mall_ref, lvl_ref, rsel_ref, bones_ref, avg_ref, hgg_ref, hg_st)
    for b in range(nb):
        hg_xc[b, 0:8, :] = hg_xc[b, CHUNK:CHUNK + 8, :]
    emit(S5_WIDTH, y_b)

    o_c = [_ret_chunk(proj_ref[b, :, C_RQ:C_RQ + RET_KEY_WIDTH], proj_ref[b, :, C_RK:C_RK + RET_KEY_WIDTH],
                      proj_ref[b, :, C_RV:C_RV + RET_WIDTH], cos_ref[...], sin_ref[...],
                      intra_ref, inter_ref, tostate_ref, carry_ref, bmask_ref, ret_st.at[b])
           for b in range(nb)]
    y_c = _ret_norm_gate(_rows(o_c), cols(C_RGATE, RET_WIDTH), avg_ref, retg_ref)
    emit(S5_WIDTH + HG_WIDTH, y_c)


def _const_spec(a):
    nd = a.ndim
    return pl.BlockSpec(a.shape, lambda c, _nd=nd: (0,) * _nd)


def _cast_steps(n_chunks, arrays):
    for steps in range(n_chunks, 0, -1):
        if all(a.shape[0] % (steps * 16) == 0 for a in arrays):
            return steps
    raise ValueError("no slab split")


def _mixers(proj, cos, sin, consts, bsz, n_chunks, to_bf16=()):
    total = proj.shape[1]
    n_cast = len(to_bf16)
    cast_steps = _cast_steps(n_chunks, to_bf16) if n_cast else 0
    slab = lambda c: (jnp.minimum(c, cast_steps - 1), 0)
    cast_specs = [pl.BlockSpec((a.shape[0] // cast_steps, a.shape[1]), slab) for a in to_bf16]
    in_specs = [
        pl.BlockSpec((bsz, CHUNK, IN_COLS), lambda c: (0, c, 0)),
        pl.BlockSpec((CHUNK, 128), lambda c: (c, 0)),
        pl.BlockSpec((CHUNK, 128), lambda c: (c, 0)),
    ] + [_const_spec(a) for a in consts] + cast_specs
    outs = pl.pallas_call(
        functools.partial(_mixer_kernel, n_cast=n_cast, cast_steps=cast_steps),
        grid=(n_chunks,),
        in_specs=in_specs,
        out_specs=[pl.BlockSpec((bsz, CHUNK, D_MIX), lambda c: (0, c, 0))] + cast_specs,
        out_shape=[jax.ShapeDtypeStruct((bsz, total, D_MIX), BF16)]
        + [jax.ShapeDtypeStruct(a.shape, BF16) for a in to_bf16],
        scratch_shapes=[
            pltpu.VMEM((bsz, 2, S5_NSTATE), F32),
            pltpu.VMEM((bsz, CHUNK + 8, HG_CONV_W), F32),
            pltpu.VMEM((bsz, HG_WIDTH, HG_WIDTH), F32),
            pltpu.VMEM((bsz, RET_KEY_WIDTH, RET_WIDTH), F32),
        ],
        compiler_params=pltpu.CompilerParams(
            dimension_semantics=("arbitrary",), vmem_limit_bytes=VMEM_LIMIT),
        name="mixers",
    )(proj, cos, sin, *consts, *to_bf16)
    return outs[0], list(outs[1:])


def _route(hn, router_ref, ltri_ref, count_ref):
    r = router_ref[...]
    r_hi, r_lo = _split_bf16(r)
    h_hi, h_lo = _split_bf16(hn)
    tm = hn.shape[0]
    parts = _dot(jnp.concatenate([h_hi, h_lo], axis=0), jnp.concatenate([r_hi, r_lo], axis=1))
    logits = (parts[:tm, :ROUTE_LANES] + parts[:tm, ROUTE_LANES:]) + (parts[tm:, :ROUTE_LANES] + parts[tm:, ROUTE_LANES:])
    lane_i = lax.broadcasted_iota(jnp.int32, logits.shape, 1)
    lane = lane_i.astype(F32)
    neg = jnp.float32(-jnp.inf)
    logits = jnp.where(lane_i < N_EXPERTS, logits, neg)
    m1 = jnp.max(logits, axis=-1, keepdims=True)
    i1 = jnp.min(jnp.where(logits == m1, lane, float(ROUTE_LANES)), axis=-1, keepdims=True)
    rest = jnp.where(lane == i1, neg, logits)
    m2 = jnp.max(rest, axis=-1, keepdims=True)
    i2 = jnp.min(jnp.where(rest == m2, lane, float(ROUTE_LANES)), axis=-1, keepdims=True)
    e2 = jnp.exp(m2 - m1)
    g1 = 1.0 / (1.0 + e2)
    g2 = e2 / (1.0 + e2)
    onehot = jnp.where((lane == i1) | (lane == i2), 1.0, 0.0)
    before = _dot(ltri_ref[...], onehot.astype(BF16)) + count_ref[...]
    r1 = jnp.sum(jnp.where(lane == i1, before, 0.0), axis=-1, keepdims=True)
    r2 = jnp.sum(jnp.where(lane == i2, before, 0.0), axis=-1, keepdims=True)
    count_ref[...] += jnp.sum(onehot, axis=0, keepdims=True)
    out = jnp.where(lane == 0, i1, 0.0)
    out = jnp.where(lane == 1, i2, out)
    out = jnp.where(lane == 2, g1, out)
    out = jnp.where(lane == 3, g2, out)
    out = jnp.where(lane == 4, r1, out)
    return jnp.where(lane == 5, r2, out)


def _out_proj_kernel(mixed_ref, *refs, routed, n_stream, tiles_per_seq):
    stream, (w_ref, g_ref, *rest) = refs[:n_stream], refs[n_stream:]
    if routed:
        router_ref, ltri_ref, h1_ref, hn_ref, route_ref, fields_ref, count_ref = rest

        @pl.when(pl.program_id(0) == 0)
        def _():
            count_ref[...] = jnp.zeros_like(count_ref)
    else:
        h1_ref, hn_ref = rest
    h1 = _stream_tile(stream, tiles_per_seq) + _dot(mixed_ref[...], w_ref[...])
    h1_ref[...] = h1
    hn = h1 * lax.rsqrt(jnp.mean(h1 * h1, axis=-1, keepdims=True) + EPS) * g_ref[...]
    if routed:
        tm = hn.shape[0]
        for s in range(ROW_TILE):
            hn_ref[pl.ds(s, tm, stride=ROW_TILE), :] = hn[:, s * 128:(s + 1) * 128]
        slab = _route(hn, router_ref, ltri_ref, count_ref)
        route_ref[...] = slab
        fields_ref[...] = slab.T[0:ROUTE_FIELDS, :]
    else:
        hn_ref[...] = hn.astype(hn_ref.dtype)


def _out_proj(mixed, stream, w_bf16, g, bsz, total, router_pad=None):
    n, d = bsz * total, stream[0].shape[1]
    tm = _pick_tile(total, (TOKEN_TILE, 128))
    routed = router_pad is not None
    row = lambda i: (i, 0)
    fixed = lambda i: (0, 0)
    stream_specs, stream_operands = _stream_specs(stream, tm, total)
    in_specs = [pl.BlockSpec((tm, D_MIX), row)] + stream_specs + [
        pl.BlockSpec(w_bf16.shape, fixed), pl.BlockSpec((1, d), fixed)]
    out_specs = [pl.BlockSpec((tm, d), row), pl.BlockSpec((tm, d), row)]
    out_shape = [jax.ShapeDtypeStruct((n, d), F32), jax.ShapeDtypeStruct((n, d), BF16)]
    args = [mixed] + stream_operands + [w_bf16, g]
    if routed:
        assert d == ROW_TILE * 128
        out_specs[1] = pl.BlockSpec((tm * ROW_TILE, 128), row)
        out_shape[1] = jax.ShapeDtypeStruct((n * ROW_TILE, 128), F32)
        ltri = jnp.asarray(np.tril(np.ones((tm, tm), np.float32), -1), BF16)
        in_specs += [pl.BlockSpec(router_pad.shape, fixed), pl.BlockSpec((tm, tm), fixed)]
        out_specs += [pl.BlockSpec((tm, ROUTE_LANES), row), pl.BlockSpec((ROUTE_FIELDS, tm), lambda i: (0, i)),
                      pl.BlockSpec((1, ROUTE_LANES), fixed)]
        out_shape += [jax.ShapeDtypeStruct((n, ROUTE_LANES), F32),
                      jax.ShapeDtypeStruct((ROUTE_FIELDS, n), F32),
                      jax.ShapeDtypeStruct((1, ROUTE_LANES), F32)]
        args += [router_pad, ltri]
    return pl.pallas_call(
        functools.partial(_out_proj_kernel, routed=routed, n_stream=len(stream), tiles_per_seq=total // tm),
        grid=(n // tm,),
        in_specs=in_specs,
        out_specs=out_specs,
        out_shape=out_shape,
        compiler_params=pltpu.CompilerParams(
            dimension_semantics=("arbitrary",), vmem_limit_bytes=VMEM_LIMIT),
        name="out_proj_routed" if routed else "out_proj",
    )(*args)


HIDDEN_CHUNK = 256


def _swiglu_hidden(x_ref, w1_ref, w3_ref, hid_ref, before_chunk=None):
    n_chunks = hid_ref.shape[1] // HIDDEN_CHUNK
    for k in range(n_chunks):
        if before_chunk is not None:
            before_chunk(k, n_chunks)
        c0 = k * HIDDEN_CHUNK
        cols = slice(c0, c0 + HIDDEN_CHUNK)
        a = _dot(x_ref[...], w1_ref[:, cols])
        b = _dot(x_ref[...], w3_ref[:, cols])
        hid_ref[:, cols] = (a * _sigmoid(a) * b).astype(BF16)


def _ffn_kernel(hn_ref, h1_ref, w1_ref, w3_ref, w2_ref, o_ref, hid_ref):
    _swiglu_hidden(hn_ref, w1_ref, w3_ref, hid_ref)
    o_ref[...] = h1_ref[...] + _dot(hid_ref[...], w2_ref[...])


def _ffn(hn, h1, w1, w3, w2):
    n, d = h1.shape
    dff = w1.shape[1]
    assert dff % HIDDEN_CHUNK == 0
    tm = _pick_tile(n, (TOKEN_TILE, 128))
    resident = lambda a: pl.BlockSpec(a.shape, lambda i: (0, 0), pipeline_mode=pl.Buffered(1))
    return pl.pallas_call(
        _ffn_kernel,
        grid=(n // tm,),
        in_specs=[
            pl.BlockSpec((tm, d), lambda i: (i, 0)),
            pl.BlockSpec((tm, d), lambda i: (i, 0)),
            resident(w1), resident(w3), resident(w2),
        ],
        out_specs=pl.BlockSpec((tm, d), lambda i: (i, 0)),
        out_shape=jax.ShapeDtypeStruct((n, d), F32),
        scratch_shapes=[pltpu.VMEM((tm, dff), BF16)],
        compiler_params=pltpu.CompilerParams(
            dimension_semantics=("arbitrary",), vmem_limit_bytes=VMEM_LIMIT),
        name="ffn_dense",
    )(hn, h1, w1, w3, w2)


def _moe_kernel(te_ref, nv_ref, src_ref, tok_ref, hn_hbm, w1_ref, w3_ref, w2_ref, o_ref,
                xs_ref, xb_ref, hid_ref, acc_ref, sem, *, tm):
    i, f = pl.program_id(0), pl.program_id(1)
    nf = pl.num_programs(1)
    n_valid = nv_ref[0]
    share = tm // MOE_F_STEPS

    def row_copy(tile, r):
        tok = tok_ref[src_ref[tile] + r]
        return pltpu.make_async_copy(hn_hbm.at[pl.ds(pl.multiple_of(tok * ROW_TILE, ROW_TILE), ROW_TILE)],
                                     xs_ref.at[pl.ds(pl.multiple_of(r * ROW_TILE, ROW_TILE), ROW_TILE)], sem)

    @pl.when((i == 0) & (f == 0))
    def _():
        def body(r, carry):
            row_copy(0, r).start()
            return carry
        lax.fori_loop(0, tm, body, 0)

    @pl.when((f == 0) & (i <= n_valid))
    def _():
        pltpu.make_async_copy(xs_ref, xs_ref, sem).wait()
        for s in range(ROW_TILE):
            xb_ref[:, s * 128:(s + 1) * 128] = xs_ref[pl.ds(s, tm, stride=ROW_TILE), :].astype(BF16)

    @pl.when(i < n_valid)
    def _():
        def request_rows(k, n_chunks):
            for r in range(share * k // n_chunks, share * (k + 1) // n_chunks):
                row_copy(i + 1, f * share + r).start(priority=r % MOE_DMA_QUEUES)

        _swiglu_hidden(xb_ref, w1_ref.at[0], w3_ref.at[0], hid_ref, before_chunk=request_rows)
        part = _dot(hid_ref[...], w2_ref[0])

        @pl.when(f == 0)
        def _():
            acc_ref[...] = part

        @pl.when(f > 0)
        def _():
            acc_ref[...] += part

    @pl.when(f == nf - 1)
    def _():
        o_ref[...] = jnp.where(i < n_valid, acc_ref[...], 0.0).astype(o_ref.dtype)


def _live_f(i, f, nv):
    return jnp.where(i < nv[0], f, 0)


def _moe_grouped(tile_expert, n_valid, tile_src, sorted_tok, hn, w1, w3, w2, tm):
    d = ROW_TILE * hn.shape[1]
    n_tiles = tile_expert.shape[0]
    dff = w1.shape[2]
    tf = dff // MOE_F_STEPS
    grid_spec = pltpu.PrefetchScalarGridSpec(
        num_scalar_prefetch=4,
        grid=(n_tiles, MOE_F_STEPS),
        in_specs=[
            pl.BlockSpec(memory_space=pl.ANY),
            pl.BlockSpec((1, d, tf), lambda i, f, te, nv, *_: (te[i], 0, _live_f(i, f, nv))),
            pl.BlockSpec((1, d, tf), lambda i, f, te, nv, *_: (te[i], 0, _live_f(i, f, nv))),
            pl.BlockSpec((1, tf, d), lambda i, f, te, nv, *_: (te[i], _live_f(i, f, nv), 0)),
        ],
        out_specs=pl.BlockSpec((tm, d), lambda i, f, *_: (i, 0)),
        scratch_shapes=[pltpu.VMEM((tm * ROW_TILE, 128), F32), pltpu.VMEM((tm, d), BF16),
                        pltpu.VMEM((tm, tf), BF16), pltpu.VMEM((tm, d), F32), pltpu.SemaphoreType.DMA(())],
    )
    return pl.pallas_call(
        functools.partial(_moe_kernel, tm=tm),
        grid_spec=grid_spec,
        out_shape=jax.ShapeDtypeStruct((n_tiles * tm, d), BF16),
        compiler_params=pltpu.CompilerParams(
            dimension_semantics=("arbitrary", "arbitrary"), vmem_limit_bytes=VMEM_LIMIT),
        name="moe_grouped",
    )(tile_expert, n_valid, tile_src, sorted_tok, hn, w1, w3, w2)


def _moe_dispatch(fields, counts_row, tm):
    n = fields.shape[1]
    n_slots = n * TOP_K
    n_tiles = (n_slots + N_EXPERTS * tm + tm - 1) // tm
    counts = counts_row[0, :N_EXPERTS].astype(jnp.int32)
    padded = ((counts + tm - 1) // tm) * tm
    pend = jnp.cumsum(padded)
    pstart = pend - padded
    gstart = jnp.cumsum(counts) - counts
    eid = fields[0:TOP_K].astype(jnp.int32)
    rank = fields[2 * TOP_K:3 * TOP_K].astype(jnp.int32)
    is_e = [eid == e for e in range(N_EXPERTS)]
    lookup = lambda table: sum(jnp.where(m, table[e], 0) for e, m in enumerate(is_e))
    pos = lookup(pstart) + rank
    tok_bits = max(1, (n - 1).bit_length())
    assert n_slots < (1 << (31 - tok_bits))
    tok = jnp.broadcast_to(jnp.arange(n, dtype=jnp.int32)[None, :], (TOP_K, n))
    packed = jnp.sort((((lookup(gstart) + rank) << tok_bits) | tok).reshape(-1))
    sorted_tok = packed & ((1 << tok_bits) - 1)
    sorted_tok = jnp.concatenate([sorted_tok, jnp.zeros((tm,), jnp.int32)])
    tile_start = jnp.arange(n_tiles, dtype=jnp.int32) * tm
    tile_expert = jnp.minimum(jnp.sum((tile_start[:, None] >= pend[None, :]).astype(jnp.int32), axis=1),
                              N_EXPERTS - 1)
    tile_src = jnp.clip(tile_start - (pstart - gstart)[tile_expert], 0, n_slots)
    n_valid = (pend[-1] // tm).astype(jnp.int32).reshape(1)
    return sorted_tok, tile_src, pos, tile_expert, n_valid


def _final_kernel(h1_ref, y0_ref, y1_ref, route_ref, g_ref, o_ref):
    g0 = route_ref[:, TOP_K:TOP_K + 1]
    g1 = route_ref[:, TOP_K + 1:TOP_K + 2]
    h = h1_ref[...] + (g0 * y0_ref[...].astype(F32) + g1 * y1_ref[...].astype(F32))
    o_ref[...] = h * lax.rsqrt(jnp.mean(h * h, axis=-1, keepdims=True) + EPS) * g_ref[...]


def _final(h1, y01, route, g, bsz, total):
    n, d = h1.shape
    seq = total - CHUNK
    tm = _pick_tile(seq, (1024, 128))
    per_seq = seq // tm
    rows = lambda width, base=0: pl.BlockSpec(
        (pl.Element(tm), pl.Element(width)),
        lambda b, k: (pl.multiple_of(base + b * total + CHUNK + k * tm, CHUNK), 0))
    out = pl.pallas_call(
        _final_kernel,
        grid=(bsz, per_seq),
        in_specs=[rows(d), rows(d), rows(d, n), rows(ROUTE_LANES), pl.BlockSpec((1, d), lambda b, k: (0, 0))],
        out_specs=pl.BlockSpec((tm, d), lambda b, k: (b * per_seq + k, 0)),
        out_shape=jax.ShapeDtypeStruct((bsz * seq, d), F32),
        compiler_params=pltpu.CompilerParams(
            dimension_semantics=("arbitrary", "arbitrary"), vmem_limit_bytes=VMEM_LIMIT),
        name="final_norm",
    )(h1, y01, y01, route, g)
    return out.reshape(bsz, seq, d)


def _relayout_in_cols(w):
    d = w.shape[0]

    def halves_first(block):
        return block.reshape(d, RET_HEADS, 2, RET_DK // 2).transpose(0, 2, 1, 3).reshape(d, RET_KEY_WIDTH)

    return jnp.concatenate([w[:, :C_RQ], halves_first(w[:, C_RQ:C_RK]), halves_first(w[:, C_RK:C_RV]),
                            w[:, C_RV:]], axis=1)


def _hg_tables(nb):
    c = CHUNK
    i = np.arange(c)[:, None]
    t = np.arange(c)[None, :]
    blocks, masks, rsel = [], [], []
    for lvl in range(HG_LEVELS):
        s = 1 << lvl
        r = (i // (2 * s)) * (2 * s) + s - 1
        right = i > r
        m = np.where(right, (t > r) & (t <= i), (t > i) & (t <= r))
        blocks.append(m)
        j = t
        same = (i // (2 * s)) == (j // (2 * s))
        mk = same & right & (j <= r)
        masks.append(np.concatenate([mk, mk], axis=1))
        if s < 8:
            rsel.append(np.broadcast_to(right, (c, HG_WIDTH)))
    blocks.append(t <= i)
    blocks.append(t > i)
    eye = np.eye(nb, dtype=np.float32)
    mall = np.concatenate([np.kron(eye, blk.astype(np.float32)) for blk in blocks], axis=0)
    lvl = np.stack(masks, axis=0).astype(np.float32)
    ch = np.arange(HG_WIDTH)
    bones = (ch[:, None] // HG_DK == ch[None, :] // HG_DK).astype(np.float32)
    rsel = np.stack([np.tile(r, (nb, 1)) for r in rsel], axis=0).astype(np.float32)
    return mall, lvl, rsel, bones


def _ret_tables():
    f32 = jnp.float32
    log_gamma = jnp.log1p(-jnp.power(2.0, -5.0 - jnp.arange(RET_HEADS, dtype=f32)))
    n = jnp.arange(CHUNK, dtype=f32)
    lg = log_gamma[:, None]
    causal = jnp.tril(jnp.ones((CHUNK, CHUNK), dtype=bool))
    intra = jnp.exp(jnp.where(causal[None], (n[:, None] - n[None, :])[None] * lg[:, :, None], -jnp.inf))
    scale = RET_DK ** -0.5
    inter = jnp.exp((n[None, :] + 1.0) * lg)
    to_state = jnp.exp((CHUNK - 1.0 - n[None, :]) * lg)
    carry = jnp.exp(CHUNK * lg)[:, 0]
    head_of_v = np.arange(RET_WIDTH) // RET_DV
    head_of_k = (np.arange(RET_KEY_WIDTH) % (RET_KEY_WIDTH // 2)) // (RET_DK // 2)
    inter_t = (inter * scale).T[:, head_of_v]
    to_state_t = to_state.T[:, head_of_k]
    carry_row = carry[head_of_v][None, :]
    bmask = jnp.asarray((head_of_k[:, None] == head_of_v[None, :]).astype(np.float32))
    intra_pairs = (intra * scale).reshape(RET_HEADS // 2, 2, CHUNK, CHUNK)
    intra_pairs = jnp.concatenate([intra_pairs[:, 0], intra_pairs[:, 1]], axis=2)
    return intra_pairs, inter_t, to_state_t, carry_row, bmask


def _s5_tables(lam_re, lam_im, b_re, b_im, c_re, c_im, d_skip, log_step):
    f32 = jnp.float32
    lam = lax.complex(lam_re.astype(f32), lam_im.astype(f32))
    step = jnp.exp(log_step.astype(f32))[:, None]
    lam_dt = lam * step
    lam_bar = jnp.exp(lam_dt)
    b_bar = ((lam_bar - 1.0) / lam)[..., None] * lax.complex(b_re.astype(f32), b_im.astype(f32))
    eye = jnp.eye(S5_NGROUPS, dtype=f32)
    wb_re = jnp.einsum('gph,gk->ghkp', jnp.real(b_bar), eye).reshape(S5_WIDTH, S5_NSTATE)
    wb_im = jnp.einsum('gph,gk->ghkp', jnp.imag(b_bar), eye).reshape(S5_WIDTH, S5_NSTATE)
    wb = jnp.concatenate([wb_re, wb_im], axis=1)
    wc_re = jnp.einsum('ghp,gk->gpkh', c_re.astype(f32), eye).reshape(S5_NSTATE, S5_WIDTH)
    wc_im = jnp.einsum('ghp,gk->gpkh', c_im.astype(f32), eye).reshape(S5_NSTATE, S5_WIDTH)
    wc = jnp.concatenate([wc_re, -wc_im], axis=0)
    t = jnp.arange(CHUNK, dtype=f32)[:, None, None]
    pp = jnp.exp(lam_dt[None] * t).reshape(CHUNK, S5_NSTATE)
    pn = jnp.exp(-lam_dt[None] * t).reshape(CHUNK, S5_NSTATE)
    lam_rows = jnp.stack([jnp.real(lam_bar).reshape(-1), jnp.imag(lam_bar).reshape(-1)], axis=0)
    return (wb.astype(BF16), wc.astype(BF16), jnp.real(pn), jnp.imag(pn), jnp.real(pp), jnp.imag(pp),
            lam_rows, d_skip.astype(f32).reshape(1, S5_WIDTH))


def kernel(x, meta_tokens, norm_mix_g, w_in, s5_lam_re, s5_lam_im, s5_b_re, s5_b_im, s5_c_re, s5_c_im, s5_d, s5_log_step, s5_w_glu, s5_out_g, hg_conv_w, hg_lb_param, hg_out_g, ret_out_g, w_out, norm_ffn_g, ffn_w1, ffn_w3, ffn_w2, moe_router, moe_w1, moe_w3, moe_w2, final_norm_g):
    bsz, seq_len, d = x.shape
    depth = w_in.shape[0]
    total = seq_len + CHUNK
    n_chunks = total // CHUNK
    n = bsz * total

    meta = jnp.broadcast_to(meta_tokens.astype(F32)[None], (bsz, N_META, d))
    tm0 = _pick_tile(total, (TOKEN_TILE, 128))
    head = jnp.concatenate([jnp.zeros((bsz, PAD, d), F32), meta, x[:, :tm0 - CHUNK].astype(F32)], axis=1)
    stream = (x.astype(F32).reshape(bsz * seq_len, d), head)

    pos = (jnp.arange(total) - PAD).astype(F32)
    half = RET_DK // 2
    inv_freq = ROPE_BASE ** (-jnp.arange(half, dtype=F32) / half)
    ang = pos[:, None] * inv_freq[None, :]
    cos_t = jnp.tile(jnp.cos(ang), (1, RET_HEADS))
    sin_t = jnp.tile(jnp.sin(ang), (1, RET_HEADS))

    lb_all = jnp.cumsum(jax.nn.softmax(hg_lb_param.astype(F32), axis=0), axis=0)
    lb_all = lb_all - lb_all[0]

    mall_np, lvl_np, rsel_np, bones_np = _hg_tables(bsz)
    mall = jnp.asarray(mall_np, BF16)
    lvl = jnp.asarray(lvl_np, F32)
    rsel = jnp.asarray(rsel_np, F32)
    bones = jnp.asarray(bones_np, BF16)
    avg = jnp.asarray(bones_np / HG_DK, BF16)
    ltri = jnp.asarray(np.kron(np.eye(bsz, dtype=np.float32),
                               np.tril(np.ones((CHUNK, CHUNK), np.float32))), BF16)
    intra, inter_t, to_state_t, carry_row, bmask = _ret_tables()

    out = None
    for l in range(depth):
        w_in_l = _relayout_in_cols(w_in[l].astype(BF16))
        proj = _in_proj(stream, norm_mix_g[l].astype(F32).reshape(1, d), w_in_l, bsz, total)

        s5c = _s5_tables(s5_lam_re[l], s5_lam_im[l], s5_b_re[l], s5_b_im[l], s5_c_re[l], s5_c_im[l],
                         s5_d[l], s5_log_step[l])
        lb = lb_all[l][None, :]
        lbv = jnp.concatenate([jnp.log(lb), jnp.log1p(-lb), 1.0 - lb], axis=0)
        consts = list(s5c) + [
            s5_w_glu[l].astype(BF16), s5_out_g[l].astype(F32).reshape(1, -1), ltri,
            hg_conv_w[l].astype(F32), lbv, mall, lvl, rsel, bones, avg, hg_out_g[l].astype(F32).reshape(1, -1),
            intra, inter_t, to_state_t, carry_row, bmask, ret_out_g[l].astype(F32).reshape(1, -1),
        ]
        j = l // 2
        if l % 2 == 0:
            ffn_f32 = [ffn_w1[j], ffn_w3[j], ffn_w2[j]]
        else:
            dff_e = moe_w1.shape[-1]
            ffn_f32 = [moe_w1[j].reshape(N_EXPERTS * d, dff_e), moe_w3[j].reshape(N_EXPERTS * d, dff_e),
                       moe_w2[j].reshape(N_EXPERTS * dff_e, d)]
        mixed, ffn_bf16 = _mixers(proj.reshape(bsz, total, IN_COLS), cos_t, sin_t, consts, bsz, n_chunks,
                                  to_bf16=[a.astype(F32) for a in ffn_f32])
        mixed = mixed.reshape(n, D_MIX)

        g_ffn = norm_ffn_g[l].astype(F32).reshape(1, d)
        w_out_l = w_out[l].astype(BF16)
        if l % 2 == 0:
            h1, hn = _out_proj(mixed, stream, w_out_l, g_ffn, bsz, total)
            stream = (_ffn(hn, h1, *ffn_bf16),)
            y01 = None
        else:
            router_pad = jnp.zeros((d, ROUTE_LANES), F32).at[:, :N_EXPERTS].set(moe_router[j].astype(F32))
            h1, hn, route, fields, counts_row = _out_proj(mixed, stream, w_out_l, g_ffn, bsz, total, router_pad)
            tm = MOE_TILE
            sorted_tok, tile_src, pos_of_slot, tile_expert, n_valid = _moe_dispatch(fields, counts_row, tm)
            e_w1, e_w3, e_w2 = (ffn_bf16[0].reshape(N_EXPERTS, d, dff_e), ffn_bf16[1].reshape(N_EXPERTS, d, dff_e),
                                ffn_bf16[2].reshape(N_EXPERTS, dff_e, d))
            ys = _moe_grouped(tile_expert, n_valid, tile_src, sorted_tok, hn, e_w1, e_w3, e_w2, tm)
            y01 = ys.at[pos_of_slot.reshape(-1)].get(mode='promise_in_bounds')
            if l < depth - 1:
                stream = (h1 + (route[:, TOP_K:TOP_K + 1] * y01[:n].astype(F32)
                                + route[:, TOP_K + 1:TOP_K + 2] * y01[n:].astype(F32)),)

        if l == depth - 1:
            if y01 is None:
                y01 = jnp.zeros((TOP_K * n, d), BF16)
                h1 = stream[0]
                route = jnp.zeros((n, ROUTE_LANES), F32)
            out = _final(h1, y01, route, final_norm_g.astype(F32).reshape(1, d), bsz, total)

    return out.astype(x.dtype)
```

```python
import functools
import math

import jax
import jax.numpy as jnp
import numpy as np
from jax import lax
from jax.experimental import pallas as pl
from jax.experimental.pallas import tpu as pltpu

F32 = jnp.float32
BF16 = jnp.bfloat16

CHUNK = 128
N_META = 16
PAD = CHUNK - N_META
EPS = 1e-6

S5_WIDTH = 256
S5_GROUP = 16
S5_NGROUPS = 16
S5_STATE = 64
S5_NSTATE = S5_NGROUPS * S5_STATE
S5_STATE_COLS = 256

HG_HEADS = 4
HG_DK = 64
HG_WIDTH = 256
CONV_K = 4
HG_CONV_W = 3 * HG_WIDTH
HG_LEVELS = 7

RET_HEADS = 8
RET_DK = 32
RET_DV = 64
RET_KEY_WIDTH = 256
RET_WIDTH = 512
ROPE_BASE = 10000.0

D_MIX = 1024
IN_COLS = 2816
C_U, C_HQ, C_HGATE, C_RQ, C_RK, C_RV, C_RGATE = 0, 256, 1024, 1280, 1536, 1792, 2304

N_EXPERTS = 8
TOP_K = 2
ROUTE_LANES = 128
ROUTE_FIELDS = 8
MOE_F_STEPS = 2
MOE_TILE = 512
ROW_TILE = 8
MOE_DMA_QUEUES = 2

VMEM_LIMIT = 56 * 1024 * 1024


def _sigmoid(x):
    return 1.0 / (1.0 + jnp.exp(-x))


def _split_bf16(x):
    hi = x.astype(BF16)
    lo = (x - hi.astype(F32)).astype(BF16)
    return hi, lo


def _pick_tile(n, candidates):
    for t in candidates:
        if n % t == 0:
            return t
    raise ValueError(f"no tile in {candidates} divides {n}")


def _dot(a, b):
    return jnp.dot(a, b, preferred_element_type=F32)


def _dot_nt(a, b):
    return lax.dot_general(a, b, (((1,), (1,)), ((), ())), preferred_element_type=F32)


def _dot_tn(a, b):
    return lax.dot_general(a, b, (((0,), (0,)), ((), ())), preferred_element_type=F32)


TOKEN_TILE = 640


def _stream_specs(stream, tm, total):
    d = stream[0].shape[1]
    if len(stream) == 1:
        return [pl.BlockSpec((tm, d), lambda i: (i, 0))], list(stream)
    per, seq = total // tm, total - CHUNK
    x_rows = lambda i: (pl.multiple_of((i // per) * seq + jnp.maximum((i % per) * tm - CHUNK, 0), CHUNK), 0)
    return ([pl.BlockSpec((pl.Element(tm), pl.Element(d)), x_rows),
             pl.BlockSpec((1, tm, d), lambda i: (i // per, 0, 0))], list(stream))


def _stream_tile(refs, tiles_per_seq):
    if len(refs) == 1:
        return refs[0][...]
    x_ref, head_ref = refs
    return jnp.where(pl.program_id(0) % tiles_per_seq == 0, head_ref[0], x_ref[...])


def _in_proj_kernel(*refs, tm, tiles_per_seq):
    *stream, g_ref, w_ref, o_ref = refs
    x = _stream_tile(stream, tiles_per_seq)
    y = x * lax.rsqrt(jnp.mean(x * x, axis=-1, keepdims=True) + EPS) * g_ref[...]
    proj = _dot(y.astype(BF16), w_ref[...])
    row0 = (pl.program_id(0) % tiles_per_seq) * tm
    rows = row0 + lax.broadcasted_iota(jnp.int32, (tm, 1), 0)
    o_ref[...] = jnp.where(rows >= PAD, proj, 0.0)


def _in_proj(stream, g, w_bf16, bsz, total):
    n, d = bsz * total, stream[0].shape[1]
    tm = _pick_tile(total, (TOKEN_TILE, 128))
    specs, operands = _stream_specs(stream, tm, total)
    return pl.pallas_call(
        functools.partial(_in_proj_kernel, tm=tm, tiles_per_seq=total // tm),
        grid=(n // tm,),
        in_specs=specs + [
            pl.BlockSpec((1, d), lambda i: (0, 0)),
            pl.BlockSpec(w_bf16.shape, lambda i: (0, 0), pipeline_mode=pl.Buffered(1)),
        ],
        out_specs=pl.BlockSpec((tm, IN_COLS), lambda i: (i, 0)),
        out_shape=jax.ShapeDtypeStruct((n, IN_COLS), F32),
        compiler_params=pltpu.CompilerParams(
            dimension_semantics=("arbitrary",), vmem_limit_bytes=VMEM_LIMIT),
        name="in_proj",
    )(*operands, g, w_bf16)


def _rows(parts):
    return parts[0] if len(parts) == 1 else jnp.concatenate(parts, axis=0)


def _s5_chunks(u, wb_ref, wc_ref, pn_re_ref, pn_im_ref, pp_re_ref, pp_im_ref, lam_ref,
               d_ref, wglu_ref, g_ref, ltri_ref, st_ref):
    ns = S5_NSTATE
    nb = st_ref.shape[0]
    u_bf = u.astype(BF16)
    ltri = ltri_ref[...]
    y = d_ref[...] * u
    for c0 in range(0, ns, S5_STATE_COLS):
        re, im = slice(c0, c0 + S5_STATE_COLS), slice(ns + c0, ns + c0 + S5_STATE_COLS)
        bu_re, bu_im = _dot(u_bf, wb_ref[:, re]), _dot(u_bf, wb_ref[:, im])
        pn_re, pn_im = pn_re_ref[:, re], pn_im_ref[:, re]
        w_re, w_im = [], []
        for b in range(nb):
            sl = slice(b * CHUNK, (b + 1) * CHUNK)
            w_re.append(pn_re * bu_re[sl] - pn_im * bu_im[sl])
            w_im.append(pn_re * bu_im[sl] + pn_im * bu_re[sl])
        c_re = _dot(ltri, _rows(w_re).astype(BF16))
        c_im = _dot(ltri, _rows(w_im).astype(BF16))
        lam_re, lam_im = lam_ref[0:1, re], lam_ref[1:2, re]
        pp_re, pp_im = pp_re_ref[:, re], pp_im_ref[:, re]
        st_re, st_im = [], []
        for b in range(nb):
            sl = slice(b * CHUNK, (b + 1) * CHUNK)
            s_re, s_im = st_ref[b, 0:1, re], st_ref[b, 1:2, re]
            z_re = c_re[sl] + (lam_re * s_re - lam_im * s_im)
            z_im = c_im[sl] + (lam_re * s_im + lam_im * s_re)
            t_re = pp_re * z_re - pp_im * z_im
            t_im = pp_re * z_im + pp_im * z_re
            st_ref[b, 0:1, re] = t_re[CHUNK - 1:CHUNK, :]
            st_ref[b, 1:2, re] = t_im[CHUNK - 1:CHUNK, :]
            st_re.append(t_re)
            st_im.append(t_im)
        y = y + (_dot(_rows(st_re).astype(BF16), wc_ref[re, :]) + _dot(_rows(st_im).astype(BF16), wc_ref[im, :]))
    y = 0.5 * y * (1.0 + jnp.tanh(math.sqrt(2.0 / math.pi) * (y + 0.044715 * (y * y * y))))
    y = y * _sigmoid(_dot(y.astype(BF16), wglu_ref[...]))
    return y * lax.rsqrt(jnp.mean(y * y, axis=-1, keepdims=True) + EPS) * g_ref[...]


def _hgrn2_chunks(xc_ref, gate, convw_ref, lbv_ref, mall_ref, lvl_ref, rsel_ref, bones_ref, avg_ref,
                  g_ref, st_ref):
    w = HG_WIDTH
    nb = st_ref.shape[0]
    rows = nb * CHUNK
    convs = []
    for b in range(nb):
        conv = None
        for i in range(CONV_K):
            term = xc_ref[b, pl.ds(8 - (CONV_K - 1) + i, CHUNK), :] * convw_ref[i:i + 1, :]
            conv = term if conv is None else conv + term
        convs.append(conv)
    conv = _rows(convs)
    cq, cf, v = conv[:, :w], conv[:, w:2 * w], conv[:, 2 * w:]
    q = cq * _sigmoid(cq)
    log_lb, log_1m_lb, one_m_lb = lbv_ref[0:1, :], lbv_ref[1:2, :], lbv_ref[2:3, :]
    log_sig = jnp.minimum(cf, 0.0) - jnp.log(1.0 + jnp.exp(-jnp.abs(cf)))
    b_ = log_1m_lb + log_sig
    logf = jnp.maximum(log_lb, b_) + jnp.log(1.0 + jnp.exp(-jnp.abs(log_lb - b_)))
    kk = one_m_lb * _sigmoid(-cf)
    hi, lo = _split_bf16(logf)

    lvl_sums = _dot(mall_ref[0:HG_LEVELS * rows, :], hi)
    tail = mall_ref[HG_LEVELS * rows:(HG_LEVELS + 2) * rows, :]
    cum_suf = _dot(tail, hi) + _dot(tail, lo)
    g_cum, g_suffix = cum_suf[:rows], cum_suf[rows:]

    lane = lax.broadcasted_iota(jnp.int32, (1, 2 * HG_DK), 1)
    head_sel = [jnp.where(lane < HG_DK, 1.0, 0.0), jnp.where(lane >= HG_DK, 1.0, 0.0)]
    n_pairs = HG_HEADS // 2
    scores = [[None] * n_pairs for _ in range(nb)]
    for lvl in range(HG_LEVELS):
        s = 1 << lvl
        e = jnp.exp(lvl_sums[lvl * rows:(lvl + 1) * rows])
        if s >= 8:
            qk = jnp.concatenate([(q if (r // s) % 2 else kk)[r:r + s] for r in range(0, rows, s)], axis=0)
        else:
            qk = jnp.where(rsel_ref[lvl] > 0.5, q, kk)
        x = qk * e
        m = lvl_ref[lvl] > 0.5
        for b in range(nb):
            for p in range(n_pairs):
                xp = x[b * CHUNK:(b + 1) * CHUNK, p * 128:(p + 1) * 128]
                rhs = jnp.concatenate([xp * head_sel[0], xp * head_sel[1]], axis=0).astype(BF16)
                sc = _dot_nt(xp.astype(BF16), rhs)
                scores[b][p] = jnp.where(m, sc, 0.0 if scores[b][p] is None else scores[b][p])
    v_bf = v.astype(BF16)
    bones = bones_ref[...]
    qg = (q * jnp.exp(g_cum)).astype(BF16)
    kd = (kk * jnp.exp(g_suffix)).astype(BF16)
    o_rows = []
    for b in range(nb):
        sl = slice(b * CHUNK, (b + 1) * CHUNK)
        o_parts = []
        for p in range(n_pairs):
            vp = v[sl, p * 128:(p + 1) * 128]
            vv = jnp.concatenate([vp * head_sel[0], vp * head_sel[1]], axis=0).astype(BF16)
            o_parts.append(_dot(scores[b][p].astype(BF16), vv))
        st = st_ref[b]
        o_rows.append(jnp.concatenate(o_parts, axis=1) + _dot_nt(qg[sl], st.astype(BF16)))
        upd = _dot_tn(v_bf[sl], kd[sl]) * bones.astype(F32)
        st_ref[b] = st * jnp.exp(g_cum[(b + 1) * CHUNK - 1:(b + 1) * CHUNK, :]) + upd
    o = _rows(o_rows) + _dot((q * kk).astype(BF16), bones) * v
    ms = _dot((o * o).astype(BF16), avg_ref[...])
    return o * lax.rsqrt(ms + EPS) * g_ref[...] * (gate * _sigmoid(gate))


def _ret_chunk(rq, rk, v, cos, sin, intra_ref, inter_ref, tostate_ref, carry_ref, bmask_ref, st_ref):
    hw = RET_KEY_WIDTH // 2

    def rot(t):
        t1, t2 = t[:, :hw], t[:, hw:]
        return jnp.concatenate([t1 * cos - t2 * sin, t1 * sin + t2 * cos], axis=1)

    qr, kr = rot(rq), rot(rk)
    qr_bf, v_bf = qr.astype(BF16), v.astype(BF16)
    lane_k = lax.broadcasted_iota(jnp.int32, (1, RET_KEY_WIDTH), 1) % hw
    lane_v = lax.broadcasted_iota(jnp.int32, (1, 2 * RET_DV), 1)
    v_sel = [jnp.where(lane_v < RET_DV, 1.0, 0.0), jnp.where(lane_v >= RET_DV, 1.0, 0.0)]
    half = RET_DK // 2
    o_parts = []
    for p in range(RET_HEADS // 2):
        sel = [jnp.where((lane_k >= h * half) & (lane_k < (h + 1) * half), 1.0, 0.0)
               for h in (2 * p, 2 * p + 1)]
        rhs = jnp.concatenate([kr * sel[0], kr * sel[1]], axis=0).astype(BF16)
        sc = (_dot_nt(qr_bf, rhs) * intra_ref[p]).astype(BF16)
        vp = v[:, p * 128:(p + 1) * 128]
        vv = jnp.concatenate([vp * v_sel[0], vp * v_sel[1]], axis=0).astype(BF16)
        o_parts.append(_dot(sc, vv))
    o = jnp.concatenate(o_parts, axis=1)
    st = st_ref[...]
    o = o + _dot(qr_bf, st.astype(BF16)) * inter_ref[...]
    kd = (kr * tostate_ref[...]).astype(BF16)
    st_ref[...] = st * carry_ref[...] + _dot_tn(kd, v_bf) * bmask_ref[...]
    return o


def _ret_norm_gate(o, gate, avg_ref, g_ref):
    avg = avg_ref[...]
    outs = []
    for s in range(RET_WIDTH // 256):
        os_ = o[:, s * 256:(s + 1) * 256]
        c = os_ - _dot(os_.astype(BF16), avg)
        outs.append(c * lax.rsqrt(_dot((c * c).astype(BF16), avg) + EPS))
    return jnp.concatenate(outs, axis=1) * g_ref[...] * (gate * _sigmoid(gate))


def _mixer_kernel(proj_ref, cos_ref, sin_ref,
                  wb_ref, wc_ref, pn_re_ref, pn_im_ref, pp_re_ref, pp_im_ref, lam_ref, d_ref,
                  wglu_ref, s5g_ref, ltri_ref,
                  convw_ref, lbv_ref, mall_ref, lvl_ref, rsel_ref, bones_ref, avg_ref, hgg_ref,
                  intra_ref, inter_ref, tostate_ref, carry_ref, bmask_ref, retg_ref,
                  *rest, n_cast, cast_steps):
    cast_in, o_ref, cast_out = rest[:n_cast], rest[n_cast], rest[n_cast + 1:2 * n_cast + 1]
    s5_st, hg_xc, hg_st, ret_st = rest[2 * n_cast + 1:]

    @pl.when(pl.program_id(0) == 0)
    def _():
        s5_st[...] = jnp.zeros_like(s5_st)
        hg_xc[...] = jnp.zeros_like(hg_xc)
        hg_st[...] = jnp.zeros_like(hg_st)
        ret_st[...] = jnp.zeros_like(ret_st)

    if n_cast:
        @pl.when(pl.program_id(0) < cast_steps)
        def _():
            for src, dst in zip(cast_in, cast_out):
                dst[...] = src[...].astype(dst.dtype)

    nb = proj_ref.shape[0]

    def cols(c0, width):
        return _rows([proj_ref[b, :, c0:c0 + width] for b in range(nb)])

    def emit(c0, y):
        for b in range(nb):
            o_ref[b, :, c0:c0 + y.shape[1]] = y[b * CHUNK:(b + 1) * CHUNK].astype(o_ref.dtype)

    y_a = _s5_chunks(cols(C_U, S5_WIDTH), wb_ref, wc_ref, pn_re_ref, pn_im_ref, pp_re_ref, pp_im_ref,
                     lam_ref, d_ref, wglu_ref, s5g_ref, ltri_ref, s5_st)
    emit(0, y_a)

    for b in range(nb):
        hg_xc[b, 8:8 + CHUNK, :] = proj_ref[b, :, C_HQ:C_HQ + HG_CONV_W]
    y_b = _hgrn2_chunks(hg_xc, cols(C_HGATE, HG_WIDTH), convw_ref, lbv_ref,
                        mall_ref, lvl_ref, rsel_ref, bones_ref, avg_ref, hgg_ref, hg_st)
    for b in range(nb):
        hg_xc[b, 0:8, :] = hg_xc[b, CHUNK:CHUNK + 8, :]
    emit(S5_WIDTH, y_b)

    o_c = [_ret_chunk(proj_ref[b, :, C_RQ:C_RQ + RET_KEY_WIDTH], proj_ref[b, :, C_RK:C_RK + RET_KEY_WIDTH],
                      proj_ref[b, :, C_RV:C_RV + RET_WIDTH], cos_ref[...], sin_ref[...],
                      intra_ref, inter_ref, tostate_ref, carry_ref, bmask_ref, ret_st.at[b])
           for b in range(nb)]
    y_c = _ret_norm_gate(_rows(o_c), cols(C_RGATE, RET_WIDTH), avg_ref, retg_ref)
    emit(S5_WIDTH + HG_WIDTH, y_c)


def _const_spec(a):
    nd = a.ndim
    return pl.BlockSpec(a.shape, lambda c, _nd=nd: (0,) * _nd)


def _cast_steps(n_chunks, arrays):
    for steps in range(n_chunks, 0, -1):
        if all(a.shape[0] % (steps * 16) == 0 for a in arrays):
            return steps
    raise ValueError("no slab split")


def _mixers(proj, cos, sin, consts, bsz, n_chunks, to_bf16=()):
    total = proj.shape[1]
    n_cast = len(to_bf16)
    cast_steps = _cast_steps(n_chunks, to_bf16) if n_cast else 0
    slab = lambda c: (jnp.minimum(c, cast_steps - 1), 0)
    cast_specs = [pl.BlockSpec((a.shape[0] // cast_steps, a.shape[1]), slab) for a in to_bf16]
    in_specs = [
        pl.BlockSpec((bsz, CHUNK, IN_COLS), lambda c: (0, c, 0)),
        pl.BlockSpec((CHUNK, 128), lambda c: (c, 0)),
        pl.BlockSpec((CHUNK, 128), lambda c: (c, 0)),
    ] + [_const_spec(a) for a in consts] + cast_specs
    outs = pl.pallas_call(
        functools.partial(_mixer_kernel, n_cast=n_cast, cast_steps=cast_steps),
        grid=(n_chunks,),
        in_specs=in_specs,
        out_specs=[pl.BlockSpec((bsz, CHUNK, D_MIX), lambda c: (0, c, 0))] + cast_specs,
        out_shape=[jax.ShapeDtypeStruct((bsz, total, D_MIX), BF16)]
        + [jax.ShapeDtypeStruct(a.shape, BF16) for a in to_bf16],
        scratch_shapes=[
            pltpu.VMEM((bsz, 2, S5_NSTATE), F32),
            pltpu.VMEM((bsz, CHUNK + 8, HG_CONV_W), F32),
            pltpu.VMEM((bsz, HG_WIDTH, HG_WIDTH), F32),
            pltpu.VMEM((bsz, RET_KEY_WIDTH, RET_WIDTH), F32),
        ],
        compiler_params=pltpu.CompilerParams(
            dimension_semantics=("arbitrary",), vmem_limit_bytes=VMEM_LIMIT),
        name="mixers",
    )(proj, cos, sin, *consts, *to_bf16)
    return outs[0], list(outs[1:])


def _route(hn, router_ref, ltri_ref, count_ref):
    r = router_ref[...]
    r_hi, r_lo = _split_bf16(r)
    h_hi, h_lo = _split_bf16(hn)
    tm = hn.shape[0]
    parts = _dot(jnp.concatenate([h_hi, h_lo], axis=0), jnp.concatenate([r_hi, r_lo], axis=1))
    logits = (parts[:tm, :ROUTE_LANES] + parts[:tm, ROUTE_LANES:]) + (parts[tm:, :ROUTE_LANES] + parts[tm:, ROUTE_LANES:])
    lane_i = lax.broadcasted_iota(jnp.int32, logits.shape, 1)
    lane = lane_i.astype(F32)
    neg = jnp.float32(-jnp.inf)
    logits = jnp.where(lane_i < N_EXPERTS, logits, neg)
    m1 = jnp.max(logits, axis=-1, keepdims=True)
    i1 = jnp.min(jnp.where(logits == m1, lane, float(ROUTE_LANES)), axis=-1, keepdims=True)
    rest = jnp.where(lane == i1, neg, logits)
    m2 = jnp.max(rest, axis=-1, keepdims=True)
    i2 = jnp.min(jnp.where(rest == m2, lane, float(ROUTE_LANES)), axis=-1, keepdims=True)
    e2 = jnp.exp(m2 - m1)
    g1 = 1.0 / (1.0 + e2)
    g2 = e2 / (1.0 + e2)
    onehot = jnp.where((lane == i1) | (lane == i2), 1.0, 0.0)
    before = _dot(ltri_ref[...], onehot.astype(BF16)) + count_ref[...]
    r1 = jnp.sum(jnp.where(lane == i1, before, 0.0), axis=-1, keepdims=True)
    r2 = jnp.sum(jnp.where(lane == i2, before, 0.0), axis=-1, keepdims=True)
    count_ref[...] += jnp.sum(onehot, axis=0, keepdims=True)
    out = jnp.where(lane == 0, i1, 0.0)
    out = jnp.where(lane == 1, i2, out)
    out = jnp.where(lane == 2, g1, out)
    out = jnp.where(lane == 3, g2, out)
    out = jnp.where(lane == 4, r1, out)
    return jnp.where(lane == 5, r2, out)


def _out_proj_kernel(mixed_ref, *refs, routed, n_stream, tiles_per_seq):
    stream, (w_ref, g_ref, *rest) = refs[:n_stream], refs[n_stream:]
    if routed:
        router_ref, ltri_ref, h1_ref, hn_ref, route_ref, fields_ref, count_ref = rest

        @pl.when(pl.program_id(0) == 0)
        def _():
            count_ref[...] = jnp.zeros_like(count_ref)
    else:
        h1_ref, hn_ref = rest
    h1 = _stream_tile(stream, tiles_per_seq) + _dot(mixed_ref[...], w_ref[...])
    h1_ref[...] = h1
    hn = h1 * lax.rsqrt(jnp.mean(h1 * h1, axis=-1, keepdims=True) + EPS) * g_ref[...]
    if routed:
        tm = hn.shape[0]
        for s in range(ROW_TILE):
            hn_ref[pl.ds(s, tm, stride=ROW_TILE), :] = hn[:, s * 128:(s + 1) * 128]
        slab = _route(hn, router_ref, ltri_ref, count_ref)
        route_ref[...] = slab
        fields_ref[...] = slab.T[0:ROUTE_FIELDS, :]
    else:
        hn_ref[...] = hn.astype(hn_ref.dtype)


def _out_proj(mixed, stream, w_bf16, g, bsz, total, router_pad=None):
    n, d = bsz * total, stream[0].shape[1]
    tm = _pick_tile(total, (TOKEN_TILE, 128))
    routed = router_pad is not None
    row = lambda i: (i, 0)
    fixed = lambda i: (0, 0)
    stream_specs, stream_operands = _stream_specs(stream, tm, total)
    in_specs = [pl.BlockSpec((tm, D_MIX), row)] + stream_specs + [
        pl.BlockSpec(w_bf16.shape, fixed), pl.BlockSpec((1, d), fixed)]
    out_specs = [pl.BlockSpec((tm, d), row), pl.BlockSpec((tm, d), row)]
    out_shape = [jax.ShapeDtypeStruct((n, d), F32), jax.ShapeDtypeStruct((n, d), BF16)]
    args = [mixed] + stream_operands + [w_bf16, g]
    if routed:
        assert d == ROW_TILE * 128
        out_specs[1] = pl.BlockSpec((tm * ROW_TILE, 128), row)
        out_shape[1] = jax.ShapeDtypeStruct((n * ROW_TILE, 128), F32)
        ltri = jnp.asarray(np.tril(np.ones((tm, tm), np.float32), -1), BF16)
        in_specs += [pl.BlockSpec(router_pad.shape, fixed), pl.BlockSpec((tm, tm), fixed)]
        out_specs += [pl.BlockSpec((tm, ROUTE_LANES), row), pl.BlockSpec((ROUTE_FIELDS, tm), lambda i: (0, i)),
                      pl.BlockSpec((1, ROUTE_LANES), fixed)]
        out_shape += [jax.ShapeDtypeStruct((n, ROUTE_LANES), F32),
                      jax.ShapeDtypeStruct((ROUTE_FIELDS, n), F32),
                      jax.ShapeDtypeStruct((1, ROUTE_LANES), F32)]
        args += [router_pad, ltri]
    return pl.pallas_call(
        functools.partial(_out_proj_kernel, routed=routed, n_stream=len(stream), tiles_per_seq=total // tm),
        grid=(n // tm,),
        in_specs=in_specs,
        out_specs=out_specs,
        out_shape=out_shape,
        compiler_params=pltpu.CompilerParams(
            dimension_semantics=("arbitrary",), vmem_limit_bytes=VMEM_LIMIT),
        name="out_proj_routed" if routed else "out_proj",
    )(*args)


HIDDEN_CHUNK = 256


def _swiglu_hidden(x_ref, w1_ref, w3_ref, hid_ref):
    for c0 in range(0, hid_ref.shape[1], HIDDEN_CHUNK):
        cols = slice(c0, c0 + HIDDEN_CHUNK)
        a = _dot(x_ref[...], w1_ref[:, cols])
        b = _dot(x_ref[...], w3_ref[:, cols])
        hid_ref[:, cols] = (a * _sigmoid(a) * b).astype(BF16)


def _ffn_kernel(hn_ref, h1_ref, w1_ref, w3_ref, w2_ref, o_ref, hid_ref):
    _swiglu_hidden(hn_ref, w1_ref, w3_ref, hid_ref)
    o_ref[...] = h1_ref[...] + _dot(hid_ref[...], w2_ref[...])


def _ffn(hn, h1, w1, w3, w2):
    n, d = h1.shape
    dff = w1.shape[1]
    assert dff % HIDDEN_CHUNK == 0
    tm = _pick_tile(n, (TOKEN_TILE, 128))
    resident = lambda a: pl.BlockSpec(a.shape, lambda i: (0, 0), pipeline_mode=pl.Buffered(1))
    return pl.pallas_call(
        _ffn_kernel,
        grid=(n // tm,),
        in_specs=[
            pl.BlockSpec((tm, d), lambda i: (i, 0)),
            pl.BlockSpec((tm, d), lambda i: (i, 0)),
            resident(w1), resident(w3), resident(w2),
        ],
        out_specs=pl.BlockSpec((tm, d), lambda i: (i, 0)),
        out_shape=jax.ShapeDtypeStruct((n, d), F32),
        scratch_shapes=[pltpu.VMEM((tm, dff), BF16)],
        compiler_params=pltpu.CompilerParams(
            dimension_semantics=("arbitrary",), vmem_limit_bytes=VMEM_LIMIT),
        name="ffn_dense",
    )(hn, h1, w1, w3, w2)


def _moe_kernel(te_ref, nv_ref, src_ref, tok_ref, hn_hbm, w1_ref, w3_ref, w2_ref, o_ref,
                xs_ref, xb_ref, hid_ref, acc_ref, sem, *, tm):
    i, f = pl.program_id(0), pl.program_id(1)
    nf = pl.num_programs(1)
    n_valid = nv_ref[0]
    share = tm // MOE_F_STEPS

    def row_copy(tile, r):
        tok = tok_ref[src_ref[tile] + r]
        return pltpu.make_async_copy(hn_hbm.at[pl.ds(pl.multiple_of(tok * ROW_TILE, ROW_TILE), ROW_TILE)],
                                     xs_ref.at[pl.ds(pl.multiple_of(r * ROW_TILE, ROW_TILE), ROW_TILE)], sem)

    @pl.when((i == 0) & (f == 0))
    def _():
        def body(r, carry):
            row_copy(0, r).start()
            return carry
        lax.fori_loop(0, tm, body, 0)

    @pl.when((f == 0) & (i <= n_valid))
    def _():
        pltpu.make_async_copy(xs_ref, xs_ref, sem).wait()
        for s in range(ROW_TILE):
            xb_ref[:, s * 128:(s + 1) * 128] = xs_ref[pl.ds(s, tm, stride=ROW_TILE), :].astype(BF16)

    @pl.when(i < n_valid)
    def _():
        for r in range(share):
            row_copy(i + 1, f * share + r).start(priority=r % MOE_DMA_QUEUES)
        _swiglu_hidden(xb_ref, w1_ref.at[0], w3_ref.at[0], hid_ref)
        part = _dot(hid_ref[...], w2_ref[0])

        @pl.when(f == 0)
        def _():
            acc_ref[...] = part

        @pl.when(f > 0)
        def _():
            acc_ref[...] += part

    @pl.when(f == nf - 1)
    def _():
        o_ref[...] = jnp.where(i < n_valid, acc_ref[...], 0.0).astype(o_ref.dtype)


def _live_f(i, f, nv):
    return jnp.where(i < nv[0], f, 0)


def _moe_grouped(tile_expert, n_valid, tile_src, sorted_tok, hn, w1, w3, w2, tm):
    d = ROW_TILE * hn.shape[1]
    n_tiles = tile_expert.shape[0]
    dff = w1.shape[2]
    tf = dff // MOE_F_STEPS
    grid_spec = pltpu.PrefetchScalarGridSpec(
        num_scalar_prefetch=4,
        grid=(n_tiles, MOE_F_STEPS),
        in_specs=[
            pl.BlockSpec(memory_space=pl.ANY),
            pl.BlockSpec((1, d, tf), lambda i, f, te, nv, *_: (te[i], 0, _live_f(i, f, nv))),
            pl.BlockSpec((1, d, tf), lambda i, f, te, nv, *_: (te[i], 0, _live_f(i, f, nv))),
            pl.BlockSpec((1, tf, d), lambda i, f, te, nv, *_: (te[i], _live_f(i, f, nv), 0)),
        ],
        out_specs=pl.BlockSpec((tm, d), lambda i, f, *_: (i, 0)),
        scratch_shapes=[pltpu.VMEM((tm * ROW_TILE, 128), F32), pltpu.VMEM((tm, d), BF16),
                        pltpu.VMEM((tm, tf), BF16), pltpu.VMEM((tm, d), F32), pltpu.SemaphoreType.DMA(())],
    )
    return pl.pallas_call(
        functools.partial(_moe_kernel, tm=tm),
        grid_spec=grid_spec,
        out_shape=jax.ShapeDtypeStruct((n_tiles * tm, d), BF16),
        compiler_params=pltpu.CompilerParams(
            dimension_semantics=("arbitrary", "arbitrary"), vmem_limit_bytes=VMEM_LIMIT),
        name="moe_grouped",
    )(tile_expert, n_valid, tile_src, sorted_tok, hn, w1, w3, w2)


def _moe_dispatch(fields, counts_row, tm):
    n = fields.shape[1]
    n_slots = n * TOP_K
    n_tiles = (n_slots + N_EXPERTS * tm + tm - 1) // tm
    counts = counts_row[0, :N_EXPERTS].astype(jnp.int32)
    padded = ((counts + tm - 1) // tm) * tm
    pend = jnp.cumsum(padded)
    pstart = pend - padded
    gstart = jnp.cumsum(counts) - counts
    eid = fields[0:TOP_K].astype(jnp.int32)
    rank = fields[2 * TOP_K:3 * TOP_K].astype(jnp.int32)
    is_e = [eid == e for e in range(N_EXPERTS)]
    lookup = lambda table: sum(jnp.where(m, table[e], 0) for e, m in enumerate(is_e))
    pos = lookup(pstart) + rank
    tok_bits = max(1, (n - 1).bit_length())
    assert n_slots < (1 << (31 - tok_bits))
    tok = jnp.broadcast_to(jnp.arange(n, dtype=jnp.int32)[None, :], (TOP_K, n))
    packed = jnp.sort((((lookup(gstart) + rank) << tok_bits) | tok).reshape(-1))
    sorted_tok = packed & ((1 << tok_bits) - 1)
    sorted_tok = jnp.concatenate([sorted_tok, jnp.zeros((tm,), jnp.int32)])
    tile_start = jnp.arange(n_tiles, dtype=jnp.int32) * tm
    tile_expert = jnp.minimum(jnp.sum((tile_start[:, None] >= pend[None, :]).astype(jnp.int32), axis=1),
                              N_EXPERTS - 1)
    tile_src = jnp.clip(tile_start - (pstart - gstart)[tile_expert], 0, n_slots)
    n_valid = (pend[-1] // tm).astype(jnp.int32).reshape(1)
    return sorted_tok, tile_src, pos, tile_expert, n_valid


def _final_kernel(h1_ref, y0_ref, y1_ref, route_ref, g_ref, o_ref):
    g0 = route_ref[:, TOP_K:TOP_K + 1]
    g1 = route_ref[:, TOP_K + 1:TOP_K + 2]
    h = h1_ref[...] + (g0 * y0_ref[...].astype(F32) + g1 * y1_ref[...].astype(F32))
    o_ref[...] = h * lax.rsqrt(jnp.mean(h * h, axis=-1, keepdims=True) + EPS) * g_ref[...]


def _final(h1, y01, route, g, bsz, total):
    n, d = h1.shape
    seq = total - CHUNK
    tm = _pick_tile(seq, (1024, 128))
    per_seq = seq // tm
    rows = lambda width, base=0: pl.BlockSpec(
        (pl.Element(tm), pl.Element(width)),
        lambda b, k: (pl.multiple_of(base + b * total + CHUNK + k * tm, CHUNK), 0))
    out = pl.pallas_call(
        _final_kernel,
        grid=(bsz, per_seq),
        in_specs=[rows(d), rows(d), rows(d, n), rows(ROUTE_LANES), pl.BlockSpec((1, d), lambda b, k: (0, 0))],
        out_specs=pl.BlockSpec((tm, d), lambda b, k: (b * per_seq + k, 0)),
        out_shape=jax.ShapeDtypeStruct((bsz * seq, d), F32),
        compiler_params=pltpu.CompilerParams(
            dimension_semantics=("arbitrary", "arbitrary"), vmem_limit_bytes=VMEM_LIMIT),
        name="final_norm",
    )(h1, y01, y01, route, g)
    return out.reshape(bsz, seq, d)


def _relayout_in_cols(w):
    d = w.shape[0]

    def halves_first(block):
        return block.reshape(d, RET_HEADS, 2, RET_DK // 2).transpose(0, 2, 1, 3).reshape(d, RET_KEY_WIDTH)

    return jnp.concatenate([w[:, :C_RQ], halves_first(w[:, C_RQ:C_RK]), halves_first(w[:, C_RK:C_RV]),
                            w[:, C_RV:]], axis=1)


def _hg_tables(nb):
    c = CHUNK
    i = np.arange(c)[:, None]
    t = np.arange(c)[None, :]
    blocks, masks, rsel = [], [], []
    for lvl in range(HG_LEVELS):
        s = 1 << lvl
        r = (i // (2 * s)) * (2 * s) + s - 1
        right = i > r
        m = np.where(right, (t > r) & (t <= i), (t > i) & (t <= r))
        blocks.append(m)
        j = t
        same = (i // (2 * s)) == (j // (2 * s))
        mk = same & right & (j <= r)
        masks.append(np.concatenate([mk, mk], axis=1))
        if s < 8:
            rsel.append(np.broadcast_to(right, (c, HG_WIDTH)))
    blocks.append(t <= i)
    blocks.append(t > i)
    eye = np.eye(nb, dtype=np.float32)
    mall = np.concatenate([np.kron(eye, blk.astype(np.float32)) for blk in blocks], axis=0)
    lvl = np.stack(masks, axis=0).astype(np.float32)
    ch = np.arange(HG_WIDTH)
    bones = (ch[:, None] // HG_DK == ch[None, :] // HG_DK).astype(np.float32)
    rsel = np.stack([np.tile(r, (nb, 1)) for r in rsel], axis=0).astype(np.float32)
    return mall, lvl, rsel, bones


def _ret_tables():
    f32 = jnp.float32
    log_gamma = jnp.log1p(-jnp.power(2.0, -5.0 - jnp.arange(RET_HEADS, dtype=f32)))
    n = jnp.arange(CHUNK, dtype=f32)
    lg = log_gamma[:, None]
    causal = jnp.tril(jnp.ones((CHUNK, CHUNK), dtype=bool))
    intra = jnp.exp(jnp.where(causal[None], (n[:, None] - n[None, :])[None] * lg[:, :, None], -jnp.inf))
    scale = RET_DK ** -0.5
    inter = jnp.exp((n[None, :] + 1.0) * lg)
    to_state = jnp.exp((CHUNK - 1.0 - n[None, :]) * lg)
    carry = jnp.exp(CHUNK * lg)[:, 0]
    head_of_v = np.arange(RET_WIDTH) // RET_DV
    head_of_k = (np.arange(RET_KEY_WIDTH) % (RET_KEY_WIDTH // 2)) // (RET_DK // 2)
    inter_t = (inter * scale).T[:, head_of_v]
    to_state_t = to_state.T[:, head_of_k]
    carry_row = carry[head_of_v][None, :]
    bmask = jnp.asarray((head_of_k[:, None] == head_of_v[None, :]).astype(np.float32))
    intra_pairs = (intra * scale).reshape(RET_HEADS // 2, 2, CHUNK, CHUNK)
    intra_pairs = jnp.concatenate([intra_pairs[:, 0], intra_pairs[:, 1]], axis=2)
    return intra_pairs, inter_t, to_state_t, carry_row, bmask


def _s5_tables(lam_re, lam_im, b_re, b_im, c_re, c_im, d_skip, log_step):
    f32 = jnp.float32
    lam = lax.complex(lam_re.astype(f32), lam_im.astype(f32))
    step = jnp.exp(log_step.astype(f32))[:, None]
    lam_dt = lam * step
    lam_bar = jnp.exp(lam_dt)
    b_bar = ((lam_bar - 1.0) / lam)[..., None] * lax.complex(b_re.astype(f32), b_im.astype(f32))
    eye = jnp.eye(S5_NGROUPS, dtype=f32)
    wb_re = jnp.einsum('gph,gk->ghkp', jnp.real(b_bar), eye).reshape(S5_WIDTH, S5_NSTATE)
    wb_im = jnp.einsum('gph,gk->ghkp', jnp.imag(b_bar), eye).reshape(S5_WIDTH, S5_NSTATE)
    wb = jnp.concatenate([wb_re, wb_im], axis=1)
    wc_re = jnp.einsum('ghp,gk->gpkh', c_re.astype(f32), eye).reshape(S5_NSTATE, S5_WIDTH)
    wc_im = jnp.einsum('ghp,gk->gpkh', c_im.astype(f32), eye).reshape(S5_NSTATE, S5_WIDTH)
    wc = jnp.concatenate([wc_re, -wc_im], axis=0)
    t = jnp.arange(CHUNK, dtype=f32)[:, None, None]
    pp = jnp.exp(lam_dt[None] * t).reshape(CHUNK, S5_NSTATE)
    pn = jnp.exp(-lam_dt[None] * t).reshape(CHUNK, S5_NSTATE)
    lam_rows = jnp.stack([jnp.real(lam_bar).reshape(-1), jnp.imag(lam_bar).reshape(-1)], axis=0)
    return (wb.astype(BF16), wc.astype(BF16), jnp.real(pn), jnp.imag(pn), jnp.real(pp), jnp.imag(pp),
            lam_rows, d_skip.astype(f32).reshape(1, S5_WIDTH))


def kernel(x, meta_tokens, norm_mix_g, w_in, s5_lam_re, s5_lam_im, s5_b_re, s5_b_im, s5_c_re, s5_c_im, s5_d, s5_log_step, s5_w_glu, s5_out_g, hg_conv_w, hg_lb_param, hg_out_g, ret_out_g, w_out, norm_ffn_g, ffn_w1, ffn_w3, ffn_w2, moe_router, moe_w1, moe_w3, moe_w2, final_norm_g):
    bsz, seq_len, d = x.shape
    depth = w_in.shape[0]
    total = seq_len + CHUNK
    n_chunks = total // CHUNK
    n = bsz * total

    meta = jnp.broadcast_to(meta_tokens.astype(F32)[None], (bsz, N_META, d))
    tm0 = _pick_tile(total, (TOKEN_TILE, 128))
    head = jnp.concatenate([jnp.zeros((bsz, PAD, d), F32), meta, x[:, :tm0 - CHUNK].astype(F32)], axis=1)
    stream = (x.astype(F32).reshape(bsz * seq_len, d), head)

    pos = (jnp.arange(total) - PAD).astype(F32)
    half = RET_DK // 2
    inv_freq = ROPE_BASE ** (-jnp.arange(half, dtype=F32) / half)
    ang = pos[:, None] * inv_freq[None, :]
    cos_t = jnp.tile(jnp.cos(ang), (1, RET_HEADS))
    sin_t = jnp.tile(jnp.sin(ang), (1, RET_HEADS))

    lb_all = jnp.cumsum(jax.nn.softmax(hg_lb_param.astype(F32), axis=0), axis=0)
    lb_all = lb_all - lb_all[0]

    mall_np, lvl_np, rsel_np, bones_np = _hg_tables(bsz)
    mall = jnp.asarray(mall_np, BF16)
    lvl = jnp.asarray(lvl_np, F32)
    rsel = jnp.asarray(rsel_np, F32)
    bones = jnp.asarray(bones_np, BF16)
    avg = jnp.asarray(bones_np / HG_DK, BF16)
    ltri = jnp.asarray(np.kron(np.eye(bsz, dtype=np.float32),
                               np.tril(np.ones((CHUNK, CHUNK), np.float32))), BF16)
    intra, inter_t, to_state_t, carry_row, bmask = _ret_tables()

    out = None
    for l in range(depth):
        w_in_l = _relayout_in_cols(w_in[l].astype(BF16))
        proj = _in_proj(stream, norm_mix_g[l].astype(F32).reshape(1, d), w_in_l, bsz, total)

        s5c = _s5_tables(s5_lam_re[l], s5_lam_im[l], s5_b_re[l], s5_b_im[l], s5_c_re[l], s5_c_im[l],
                         s5_d[l], s5_log_step[l])
        lb = lb_all[l][None, :]
        lbv = jnp.concatenate([jnp.log(lb), jnp.log1p(-lb), 1.0 - lb], axis=0)
        consts = list(s5c) + [
            s5_w_glu[l].astype(BF16), s5_out_g[l].astype(F32).reshape(1, -1), ltri,
            hg_conv_w[l].astype(F32), lbv, mall, lvl, rsel, bones, avg, hg_out_g[l].astype(F32).reshape(1, -1),
            intra, inter_t, to_state_t, carry_row, bmask, ret_out_g[l].astype(F32).reshape(1, -1),
        ]
        j = l // 2
        if l % 2 == 0:
            ffn_f32 = [ffn_w1[j], ffn_w3[j], ffn_w2[j]]
        else:
            dff_e = moe_w1.shape[-1]
            ffn_f32 = [moe_w1[j].reshape(N_EXPERTS * d, dff_e), moe_w3[j].reshape(N_EXPERTS * d, dff_e),
                       moe_w2[j].reshape(N_EXPERTS * dff_e, d)]
        mixed, ffn_bf16 = _mixers(proj.reshape(bsz, total, IN_COLS), cos_t, sin_t, consts, bsz, n_chunks,
                                  to_bf16=[a.astype(F32) for a in ffn_f32])
        mixed = mixed.reshape(n, D_MIX)

        g_ffn = norm_ffn_g[l].astype(F32).reshape(1, d)
        w_out_l = w_out[l].astype(BF16)
        if l % 2 == 0:
            h1, hn = _out_proj(mixed, stream, w_out_l, g_ffn, bsz, total)
            stream = (_ffn(hn, h1, *ffn_bf16),)
            y01 = None
        else:
            router_pad = jnp.zeros((d, ROUTE_LANES), F32).at[:, :N_EXPERTS].set(moe_router[j].astype(F32))
            h1, hn, route, fields, counts_row = _out_proj(mixed, stream, w_out_l, g_ffn, bsz, total, router_pad)
            tm = MOE_TILE
            sorted_tok, tile_src, pos_of_slot, tile_expert, n_valid = _moe_dispatch(fields, counts_row, tm)
            e_w1, e_w3, e_w2 = (ffn_bf16[0].reshape(N_EXPERTS, d, dff_e), ffn_bf16[1].reshape(N_EXPERTS, d, dff_e),
                                ffn_bf16[2].reshape(N_EXPERTS, dff_e, d))
            ys = _moe_grouped(tile_expert, n_valid, tile_src, sorted_tok, hn, e_w1, e_w3, e_w2, tm)
            y01 = ys.at[pos_of_slot.reshape(-1)].get(mode='promise_in_bounds')
            if l < depth - 1:
                stream = (h1 + (route[:, TOP_K:TOP_K + 1] * y01[:n].astype(F32)
                                + route[:, TOP_K + 1:TOP_K + 2] * y01[n:].astype(F32)),)

        if l == depth - 1:
            if y01 is None:
                y01 = jnp.zeros((TOP_K * n, d), BF16)
                h1 = stream[0]
                route = jnp.zeros((n, ROUTE_LANES), F32)
            out = _final(h1, y01, route, final_norm_g.astype(F32).reshape(1, d), bsz, total)

    return out.astype(x.dtype)
```

```python
import functools
import math

import jax
import jax.numpy as jnp
import numpy as np
from jax import lax
from jax.experimental import pallas as pl
from jax.experimental.pallas import tpu as pltpu

F32 = jnp.float32
BF16 = jnp.bfloat16

CHUNK = 128
N_META = 16
PAD = CHUNK - N_META
EPS = 1e-6

S5_WIDTH = 256
S5_GROUP = 16
S5_NGROUPS = 16
S5_STATE = 64
S5_NSTATE = S5_NGROUPS * S5_STATE
S5_STATE_COLS = 256

HG_HEADS = 4
HG_DK = 64
HG_WIDTH = 256
CONV_K = 4
HG_CONV_W = 3 * HG_WIDTH
HG_LEVELS = 7

RET_HEADS = 8
RET_DK = 32
RET_DV = 64
RET_KEY_WIDTH = 256
RET_WIDTH = 512
ROPE_BASE = 10000.0

D_MIX = 1024
IN_COLS = 2816
C_U, C_HQ, C_HGATE, C_RQ, C_RK, C_RV, C_RGATE = 0, 256, 1024, 1280, 1536, 1792, 2304

N_EXPERTS = 8
TOP_K = 2
ROUTE_LANES = 128
ROUTE_FIELDS = 8
MOE_F_STEPS = 2
MOE_TILE = 512
ROW_TILE = 8
MOE_DMA_QUEUES = 2

VMEM_LIMIT = 56 * 1024 * 1024


def _sigmoid(x):
    return 1.0 / (1.0 + jnp.exp(-x))


def _split_bf16(x):
    hi = x.astype(BF16)
    lo = (x - hi.astype(F32)).astype(BF16)
    return hi, lo


def _pick_tile(n, candidates):
    for t in candidates:
        if n % t == 0:
            return t
    raise ValueError(f"no tile in {candidates} divides {n}")


def _dot(a, b):
    return jnp.dot(a, b, preferred_element_type=F32)


def _dot_nt(a, b):
    return lax.dot_general(a, b, (((1,), (1,)), ((), ())), preferred_element_type=F32)


def _dot_tn(a, b):
    return lax.dot_general(a, b, (((0,), (0,)), ((), ())), preferred_element_type=F32)


TOKEN_TILE = 640


def _stream_specs(stream, tm, total):
    d = stream[0].shape[1]
    if len(stream) == 1:
        return [pl.BlockSpec((tm, d), lambda i: (i, 0))], list(stream)
    per, seq = total // tm, total - CHUNK
    x_rows = lambda i: (pl.multiple_of((i // per) * seq + jnp.maximum((i % per) * tm - CHUNK, 0), CHUNK), 0)
    return ([pl.BlockSpec((pl.Element(tm), pl.Element(d)), x_rows),
             pl.BlockSpec((1, tm, d), lambda i: (i // per, 0, 0))], list(stream))


def _stream_tile(refs, tiles_per_seq):
    if len(refs) == 1:
        return refs[0][...]
    x_ref, head_ref = refs
    return jnp.where(pl.program_id(0) % tiles_per_seq == 0, head_ref[0], x_ref[...])


def _in_proj_kernel(*refs, tm, tiles_per_seq):
    *stream, g_ref, w_ref, o_ref = refs
    x = _stream_tile(stream, tiles_per_seq)
    y = x * lax.rsqrt(jnp.mean(x * x, axis=-1, keepdims=True) + EPS) * g_ref[...]
    proj = _dot(y.astype(BF16), w_ref[...])
    row0 = (pl.program_id(0) % tiles_per_seq) * tm
    rows = row0 + lax.broadcasted_iota(jnp.int32, (tm, 1), 0)
    o_ref[...] = jnp.where(rows >= PAD, proj, 0.0)


def _in_proj(stream, g, w_bf16, bsz, total):
    n, d = bsz * total, stream[0].shape[1]
    tm = _pick_tile(total, (TOKEN_TILE, 128))
    specs, operands = _stream_specs(stream, tm, total)
    return pl.pallas_call(
        functools.partial(_in_proj_kernel, tm=tm, tiles_per_seq=total // tm),
        grid=(n // tm,),
        in_specs=specs + [
            pl.BlockSpec((1, d), lambda i: (0, 0)),
            pl.BlockSpec(w_bf16.shape, lambda i: (0, 0), pipeline_mode=pl.Buffered(1)),
        ],
        out_specs=pl.BlockSpec((tm, IN_COLS), lambda i: (i, 0)),
        out_shape=jax.ShapeDtypeStruct((n, IN_COLS), F32),
        compiler_params=pltpu.CompilerParams(
            dimension_semantics=("arbitrary",), vmem_limit_bytes=VMEM_LIMIT),
        name="in_proj",
    )(*operands, g, w_bf16)


def _rows(parts):
    return parts[0] if len(parts) == 1 else jnp.concatenate(parts, axis=0)


def _s5_chunks(u, wb_ref, wc_ref, pn_re_ref, pn_im_ref, pp_re_ref, pp_im_ref, lam_ref,
               d_ref, wglu_ref, g_ref, ltri_ref, st_ref):
    ns = S5_NSTATE
    nb = st_ref.shape[0]
    u_bf = u.astype(BF16)
    ltri = ltri_ref[...]
    y = d_ref[...] * u
    for c0 in range(0, ns, S5_STATE_COLS):
        re, im = slice(c0, c0 + S5_STATE_COLS), slice(ns + c0, ns + c0 + S5_STATE_COLS)
        bu_re, bu_im = _dot(u_bf, wb_ref[:, re]), _dot(u_bf, wb_ref[:, im])
        pn_re, pn_im = pn_re_ref[:, re], pn_im_ref[:, re]
        w_re, w_im = [], []
        for b in range(nb):
            sl = slice(b * CHUNK, (b + 1) * CHUNK)
            w_re.append(pn_re * bu_re[sl] - pn_im * bu_im[sl])
            w_im.append(pn_re * bu_im[sl] + pn_im * bu_re[sl])
        c_re = _dot(ltri, _rows(w_re).astype(BF16))
        c_im = _dot(ltri, _rows(w_im).astype(BF16))
        lam_re, lam_im = lam_ref[0:1, re], lam_ref[1:2, re]
        pp_re, pp_im = pp_re_ref[:, re], pp_im_ref[:, re]
        st_re, st_im = [], []
        for b in range(nb):
            sl = slice(b * CHUNK, (b + 1) * CHUNK)
            s_re, s_im = st_ref[b, 0:1, re], st_ref[b, 1:2, re]
            z_re = c_re[sl] + (lam_re * s_re - lam_im * s_im)
            z_im = c_im[sl] + (lam_re * s_im + lam_im * s_re)
            t_re = pp_re * z_re - pp_im * z_im
            t_im = pp_re * z_im + pp_im * z_re
            st_ref[b, 0:1, re] = t_re[CHUNK - 1:CHUNK, :]
            st_ref[b, 1:2, re] = t_im[CHUNK - 1:CHUNK, :]
            st_re.append(t_re)
            st_im.append(t_im)
        y = y + (_dot(_rows(st_re).astype(BF16), wc_ref[re, :]) + _dot(_rows(st_im).astype(BF16), wc_ref[im, :]))
    y = 0.5 * y * (1.0 + jnp.tanh(math.sqrt(2.0 / math.pi) * (y + 0.044715 * (y * y * y))))
    y = y * _sigmoid(_dot(y.astype(BF16), wglu_ref[...]))
    return y * lax.rsqrt(jnp.mean(y * y, axis=-1, keepdims=True) + EPS) * g_ref[...]


def _hgrn2_chunks(xc_ref, gate, convw_ref, lbv_ref, mall_ref, lvl_ref, rsel_ref, bones_ref, avg_ref,
                  g_ref, st_ref):
    w = HG_WIDTH
    nb = st_ref.shape[0]
    rows = nb * CHUNK
    convs = []
    for b in range(nb):
        conv = None
        for i in range(CONV_K):
            term = xc_ref[b, pl.ds(8 - (CONV_K - 1) + i, CHUNK), :] * convw_ref[i:i + 1, :]
            conv = term if conv is None else conv + term
        convs.append(conv)
    conv = _rows(convs)
    cq, cf, v = conv[:, :w], conv[:, w:2 * w], conv[:, 2 * w:]
    q = cq * _sigmoid(cq)
    log_lb, log_1m_lb, one_m_lb = lbv_ref[0:1, :], lbv_ref[1:2, :], lbv_ref[2:3, :]
    log_sig = jnp.minimum(cf, 0.0) - jnp.log(1.0 + jnp.exp(-jnp.abs(cf)))
    b_ = log_1m_lb + log_sig
    logf = jnp.maximum(log_lb, b_) + jnp.log(1.0 + jnp.exp(-jnp.abs(log_lb - b_)))
    kk = one_m_lb * _sigmoid(-cf)
    hi, lo = _split_bf16(logf)

    tail = mall_ref[HG_LEVELS * rows:(HG_LEVELS + 2) * rows, :]
    cum_suf = _dot(tail, hi) + _dot(tail, lo)
    g_cum, g_suffix = cum_suf[:rows], cum_suf[rows:]

    lane = lax.broadcasted_iota(jnp.int32, (1, 2 * HG_DK), 1)
    head_sel = [jnp.where(lane < HG_DK, 1.0, 0.0), jnp.where(lane >= HG_DK, 1.0, 0.0)]
    n_pairs = HG_HEADS // 2
    scores = [[None] * n_pairs for _ in range(nb)]
    for lvl in range(HG_LEVELS):
        s = 1 << lvl
        e = jnp.exp(_dot(mall_ref[lvl * rows:(lvl + 1) * rows, :], hi))
        if s >= 8:
            qk = jnp.concatenate([(q if (r // s) % 2 else kk)[r:r + s] for r in range(0, rows, s)], axis=0)
        else:
            qk = jnp.where(rsel_ref[lvl] > 0.5, q, kk)
        x = qk * e
        m = lvl_ref[lvl] > 0.5
        for b in range(nb):
            for p in range(n_pairs):
                xp = x[b * CHUNK:(b + 1) * CHUNK, p * 128:(p + 1) * 128]
                rhs = jnp.concatenate([xp * head_sel[0], xp * head_sel[1]], axis=0).astype(BF16)
                sc = _dot_nt(xp.astype(BF16), rhs)
                scores[b][p] = jnp.where(m, sc, 0.0 if scores[b][p] is None else scores[b][p])
    v_bf = v.astype(BF16)
    bones = bones_ref[...]
    qg = (q * jnp.exp(g_cum)).astype(BF16)
    kd = (kk * jnp.exp(g_suffix)).astype(BF16)
    o_rows = []
    for b in range(nb):
        sl = slice(b * CHUNK, (b + 1) * CHUNK)
        o_parts = []
        for p in range(n_pairs):
            vp = v[sl, p * 128:(p + 1) * 128]
            vv = jnp.concatenate([vp * head_sel[0], vp * head_sel[1]], axis=0).astype(BF16)
            o_parts.append(_dot(scores[b][p].astype(BF16), vv))
        st = st_ref[b]
        o_rows.append(jnp.concatenate(o_parts, axis=1) + _dot_nt(qg[sl], st.astype(BF16)))
        upd = _dot_tn(v_bf[sl], kd[sl]) * bones.astype(F32)
        st_ref[b] = st * jnp.exp(g_cum[(b + 1) * CHUNK - 1:(b + 1) * CHUNK, :]) + upd
    o = _rows(o_rows) + _dot((q * kk).astype(BF16), bones) * v
    ms = _dot((o * o).astype(BF16), avg_ref[...])
    return o * lax.rsqrt(ms + EPS) * g_ref[...] * (gate * _sigmoid(gate))


def _ret_chunk(rq, rk, v, cos, sin, intra_ref, inter_ref, tostate_ref, carry_ref, bmask_ref, st_ref):
    hw = RET_KEY_WIDTH // 2

    def rot(t):
        t1, t2 = t[:, :hw], t[:, hw:]
        return jnp.concatenate([t1 * cos - t2 * sin, t1 * sin + t2 * cos], axis=1)

    qr, kr = rot(rq), rot(rk)
    qr_bf, v_bf = qr.astype(BF16), v.astype(BF16)
    lane_k = lax.broadcasted_iota(jnp.int32, (1, RET_KEY_WIDTH), 1) % hw
    lane_v = lax.broadcasted_iota(jnp.int32, (1, 2 * RET_DV), 1)
    v_sel = [jnp.where(lane_v < RET_DV, 1.0, 0.0), jnp.where(lane_v >= RET_DV, 1.0, 0.0)]
    half = RET_DK // 2
    o_parts = []
    for p in range(RET_HEADS // 2):
        sel = [jnp.where((lane_k >= h * half) & (lane_k < (h + 1) * half), 1.0, 0.0)
               for h in (2 * p, 2 * p + 1)]
        rhs = jnp.concatenate([kr * sel[0], kr * sel[1]], axis=0).astype(BF16)
        sc = (_dot_nt(qr_bf, rhs) * intra_ref[p]).astype(BF16)
        vp = v[:, p * 128:(p + 1) * 128]
        vv = jnp.concatenate([vp * v_sel[0], vp * v_sel[1]], axis=0).astype(BF16)
        o_parts.append(_dot(sc, vv))
    o = jnp.concatenate(o_parts, axis=1)
    st = st_ref[...]
    o = o + _dot(qr_bf, st.astype(BF16)) * inter_ref[...]
    kd = (kr * tostate_ref[...]).astype(BF16)
    st_ref[...] = st * carry_ref[...] + _dot_tn(kd, v_bf) * bmask_ref[...]
    return o


def _ret_norm_gate(o, gate, avg_ref, g_ref):
    avg = avg_ref[...]
    outs = []
    for s in range(RET_WIDTH // 256):
        os_ = o[:, s * 256:(s + 1) * 256]
        c = os_ - _dot(os_.astype(BF16), avg)
        outs.append(c * lax.rsqrt(_dot((c * c).astype(BF16), avg) + EPS))
    return jnp.concatenate(outs, axis=1) * g_ref[...] * (gate * _sigmoid(gate))


def _mixer_kernel(proj_ref, cos_ref, sin_ref,
                  wb_ref, wc_ref, pn_re_ref, pn_im_ref, pp_re_ref, pp_im_ref, lam_ref, d_ref,
                  wglu_ref, s5g_ref, ltri_ref,
                  convw_ref, lbv_ref, mall_ref, lvl_ref, rsel_ref, bones_ref, avg_ref, hgg_ref,
                  intra_ref, inter_ref, tostate_ref, carry_ref, bmask_ref, retg_ref,
                  *rest, n_cast, cast_steps):
    cast_in, o_ref, cast_out = rest[:n_cast], rest[n_cast], rest[n_cast + 1:2 * n_cast + 1]
    s5_st, hg_xc, hg_st, ret_st = rest[2 * n_cast + 1:]

    @pl.when(pl.program_id(0) == 0)
    def _():
        s5_st[...] = jnp.zeros_like(s5_st)
        hg_xc[...] = jnp.zeros_like(hg_xc)
        hg_st[...] = jnp.zeros_like(hg_st)
        ret_st[...] = jnp.zeros_like(ret_st)

    if n_cast:
        @pl.when(pl.program_id(0) < cast_steps)
        def _():
            for src, dst in zip(cast_in, cast_out):
                dst[...] = src[...].astype(dst.dtype)

    nb = proj_ref.shape[0]

    def cols(c0, width):
        return _rows([proj_ref[b, :, c0:c0 + width] for b in range(nb)])

    def emit(c0, y):
        for b in range(nb):
            o_ref[b, :, c0:c0 + y.shape[1]] = y[b * CHUNK:(b + 1) * CHUNK].astype(o_ref.dtype)

    y_a = _s5_chunks(cols(C_U, S5_WIDTH), wb_ref, wc_ref, pn_re_ref, pn_im_ref, pp_re_ref, pp_im_ref,
                     lam_ref, d_ref, wglu_ref, s5g_ref, ltri_ref, s5_st)
    emit(0, y_a)

    for b in range(nb):
        hg_xc[b, 8:8 + CHUNK, :] = proj_ref[b, :, C_HQ:C_HQ + HG_CONV_W]
    y_b = _hgrn2_chunks(hg_xc, cols(C_HGATE, HG_WIDTH), convw_ref, lbv_ref,
                        mall_ref, lvl_ref, rsel_ref, bones_ref, avg_ref, hgg_ref, hg_st)
    for b in range(nb):
        hg_xc[b, 0:8, :] = hg_xc[b, CHUNK:CHUNK + 8, :]
    emit(S5_WIDTH, y_b)

    o_c = [_ret_chunk(proj_ref[b, :, C_RQ:C_RQ + RET_KEY_WIDTH], proj_ref[b, :, C_RK:C_RK + RET_KEY_WIDTH],
                      proj_ref[b, :, C_RV:C_RV + RET_WIDTH], cos_ref[...], sin_ref[...],
                      intra_ref, inter_ref, tostate_ref, carry_ref, bmask_ref, ret_st.at[b])
           for b in range(nb)]
    y_c = _ret_norm_gate(_rows(o_c), cols(C_RGATE, RET_WIDTH), avg_ref, retg_ref)
    emit(S5_WIDTH + HG_WIDTH, y_c)


def _const_spec(a):
    nd = a.ndim
    return pl.BlockSpec(a.shape, lambda c, _nd=nd: (0,) * _nd)


def _cast_steps(n_chunks, arrays):
    for steps in range(n_chunks, 0, -1):
        if all(a.shape[0] % (steps * 16) == 0 for a in arrays):
            return steps
    raise ValueError("no slab split")


def _mixers(proj, cos, sin, consts, bsz, n_chunks, to_bf16=()):
    total = proj.shape[1]
    n_cast = len(to_bf16)
    cast_steps = _cast_steps(n_chunks, to_bf16) if n_cast else 0
    slab = lambda c: (jnp.minimum(c, cast_steps - 1), 0)
    cast_specs = [pl.BlockSpec((a.shape[0] // cast_steps, a.shape[1]), slab) for a in to_bf16]
    in_specs = [
        pl.BlockSpec((bsz, CHUNK, IN_COLS), lambda c: (0, c, 0)),
        pl.BlockSpec((CHUNK, 128), lambda c: (c, 0)),
        pl.BlockSpec((CHUNK, 128), lambda c: (c, 0)),
    ] + [_const_spec(a) for a in consts] + cast_specs
    outs = pl.pallas_call(
        functools.partial(_mixer_kernel, n_cast=n_cast, cast_steps=cast_steps),
        grid=(n_chunks,),
        in_specs=in_specs,
        out_specs=[pl.BlockSpec((bsz, CHUNK, D_MIX), lambda c: (0, c, 0))] + cast_specs,
        out_shape=[jax.ShapeDtypeStruct((bsz, total, D_MIX), BF16)]
        + [jax.ShapeDtypeStruct(a.shape, BF16) for a in to_bf16],
        scratch_shapes=[
            pltpu.VMEM((bsz, 2, S5_NSTATE), F32),
            pltpu.VMEM((bsz, CHUNK + 8, HG_CONV_W), F32),
            pltpu.VMEM((bsz, HG_WIDTH, HG_WIDTH), F32),
            pltpu.VMEM((bsz, RET_KEY_WIDTH, RET_WIDTH), F32),
        ],
        compiler_params=pltpu.CompilerParams(
            dimension_semantics=("arbitrary",), vmem_limit_bytes=VMEM_LIMIT),
        name="mixers",
    )(proj, cos, sin, *consts, *to_bf16)
    return outs[0], list(outs[1:])


def _route(hn, router_ref, ltri_ref, count_ref):
    r = router_ref[...]
    r_hi, r_lo = _split_bf16(r)
    h_hi, h_lo = _split_bf16(hn)
    tm = hn.shape[0]
    parts = _dot(jnp.concatenate([h_hi, h_lo], axis=0), jnp.concatenate([r_hi, r_lo], axis=1))
    logits = (parts[:tm, :ROUTE_LANES] + parts[:tm, ROUTE_LANES:]) + (parts[tm:, :ROUTE_LANES] + parts[tm:, ROUTE_LANES:])
    lane_i = lax.broadcasted_iota(jnp.int32, logits.shape, 1)
    lane = lane_i.astype(F32)
    neg = jnp.float32(-jnp.inf)
    logits = jnp.where(lane_i < N_EXPERTS, logits, neg)
    m1 = jnp.max(logits, axis=-1, keepdims=True)
    i1 = jnp.min(jnp.where(logits == m1, lane, float(ROUTE_LANES)), axis=-1, keepdims=True)
    rest = jnp.where(lane == i1, neg, logits)
    m2 = jnp.max(rest, axis=-1, keepdims=True)
    i2 = jnp.min(jnp.where(rest == m2, lane, float(ROUTE_LANES)), axis=-1, keepdims=True)
    e2 = jnp.exp(m2 - m1)
    g1 = 1.0 / (1.0 + e2)
    g2 = e2 / (1.0 + e2)
    onehot = jnp.where((lane == i1) | (lane == i2), 1.0, 0.0)
    before = _dot(ltri_ref[...], onehot.astype(BF16)) + count_ref[...]
    r1 = jnp.sum(jnp.where(lane == i1, before, 0.0), axis=-1, keepdims=True)
    r2 = jnp.sum(jnp.where(lane == i2, before, 0.0), axis=-1, keepdims=True)
    count_ref[...] += jnp.sum(onehot, axis=0, keepdims=True)
    out = jnp.where(lane == 0, i1, 0.0)
    out = jnp.where(lane == 1, i2, out)
    out = jnp.where(lane == 2, g1, out)
    out = jnp.where(lane == 3, g2, out)
    out = jnp.where(lane == 4, r1, out)
    return jnp.where(lane == 5, r2, out)


def _out_proj_kernel(mixed_ref, *refs, routed, n_stream, tiles_per_seq):
    stream, (w_ref, g_ref, *rest) = refs[:n_stream], refs[n_stream:]
    if routed:
        router_ref, ltri_ref, h1_ref, hn_ref, route_ref, fields_ref, count_ref = rest

        @pl.when(pl.program_id(0) == 0)
        def _():
            count_ref[...] = jnp.zeros_like(count_ref)
    else:
        h1_ref, hn_ref = rest
    h1 = _stream_tile(stream, tiles_per_seq) + _dot(mixed_ref[...], w_ref[...])
    h1_ref[...] = h1
    hn = h1 * lax.rsqrt(jnp.mean(h1 * h1, axis=-1, keepdims=True) + EPS) * g_ref[...]
    if routed:
        tm = hn.shape[0]
        for s in range(ROW_TILE):
            hn_ref[pl.ds(s, tm, stride=ROW_TILE), :] = hn[:, s * 128:(s + 1) * 128]
        slab = _route(hn, router_ref, ltri_ref, count_ref)
        route_ref[...] = slab
        fields_ref[...] = slab.T[0:ROUTE_FIELDS, :]
    else:
        hn_ref[...] = hn.astype(hn_ref.dtype)


def _out_proj(mixed, stream, w_bf16, g, bsz, total, router_pad=None):
    n, d = bsz * total, stream[0].shape[1]
    tm = _pick_tile(total, (TOKEN_TILE, 128))
    routed = router_pad is not None
    row = lambda i: (i, 0)
    fixed = lambda i: (0, 0)
    stream_specs, stream_operands = _stream_specs(stream, tm, total)
    in_specs = [pl.BlockSpec((tm, D_MIX), row)] + stream_specs + [
        pl.BlockSpec(w_bf16.shape, fixed), pl.BlockSpec((1, d), fixed)]
    out_specs = [pl.BlockSpec((tm, d), row), pl.BlockSpec((tm, d), row)]
    out_shape = [jax.ShapeDtypeStruct((n, d), F32), jax.ShapeDtypeStruct((n, d), BF16)]
    args = [mixed] + stream_operands + [w_bf16, g]
    if routed:
        assert d == ROW_TILE * 128
        out_specs[1] = pl.BlockSpec((tm * ROW_TILE, 128), row)
        out_shape[1] = jax.ShapeDtypeStruct((n * ROW_TILE, 128), F32)
        ltri = jnp.asarray(np.tril(np.ones((tm, tm), np.float32), -1), BF16)
        in_specs += [pl.BlockSpec(router_pad.shape, fixed), pl.BlockSpec((tm, tm), fixed)]
        out_specs += [pl.BlockSpec((tm, ROUTE_LANES), row), pl.BlockSpec((ROUTE_FIELDS, tm), lambda i: (0, i)),
                      pl.BlockSpec((1, ROUTE_LANES), fixed)]
        out_shape += [jax.ShapeDtypeStruct((n, ROUTE_LANES), F32),
                      jax.ShapeDtypeStruct((ROUTE_FIELDS, n), F32),
                      jax.ShapeDtypeStruct((1, ROUTE_LANES), F32)]
        args += [router_pad, ltri]
    return pl.pallas_call(
        functools.partial(_out_proj_kernel, routed=routed, n_stream=len(stream), tiles_per_seq=total // tm),
        grid=(n // tm,),
        in_specs=in_specs,
        out_specs=out_specs,
        out_shape=out_shape,
        compiler_params=pltpu.CompilerParams(
            dimension_semantics=("arbitrary",), vmem_limit_bytes=VMEM_LIMIT),
        name="out_proj_routed" if routed else "out_proj",
    )(*args)


HIDDEN_CHUNK = 256


def _swiglu_hidden(x_ref, w1_ref, w3_ref, hid_ref):
    for c0 in range(0, hid_ref.shape[1], HIDDEN_CHUNK):
        cols = slice(c0, c0 + HIDDEN_CHUNK)
        a = _dot(x_ref[...], w1_ref[:, cols])
        b = _dot(x_ref[...], w3_ref[:, cols])
        hid_ref[:, cols] = (a * _sigmoid(a) * b).astype(BF16)


def _ffn_kernel(hn_ref, h1_ref, w1_ref, w3_ref, w2_ref, o_ref, hid_ref):
    _swiglu_hidden(hn_ref, w1_ref, w3_ref, hid_ref)
    o_ref[...] = h1_ref[...] + _dot(hid_ref[...], w2_ref[...])


def _ffn(hn, h1, w1, w3, w2):
    n, d = h1.shape
    dff = w1.shape[1]
    assert dff % HIDDEN_CHUNK == 0
    tm = _pick_tile(n, (TOKEN_TILE, 128))
    resident = lambda a: pl.BlockSpec(a.shape, lambda i: (0, 0), pipeline_mode=pl.Buffered(1))
    return pl.pallas_call(
        _ffn_kernel,
        grid=(n // tm,),
        in_specs=[
            pl.BlockSpec((tm, d), lambda i: (i, 0)),
            pl.BlockSpec((tm, d), lambda i: (i, 0)),
            resident(w1), resident(w3), resident(w2),
        ],
        out_specs=pl.BlockSpec((tm, d), lambda i: (i, 0)),
        out_shape=jax.ShapeDtypeStruct((n, d), F32),
        scratch_shapes=[pltpu.VMEM((tm, dff), BF16)],
        compiler_params=pltpu.CompilerParams(
            dimension_semantics=("arbitrary",), vmem_limit_bytes=VMEM_LIMIT),
        name="ffn_dense",
    )(hn, h1, w1, w3, w2)


def _moe_kernel(te_ref, nv_ref, src_ref, tok_ref, hn_hbm, w1_ref, w3_ref, w2_ref, o_ref,
                xs_ref, xb_ref, hid_ref, acc_ref, sem, *, tm):
    i, f = pl.program_id(0), pl.program_id(1)
    nf = pl.num_programs(1)
    n_valid = nv_ref[0]
    share = tm // MOE_F_STEPS

    def row_copy(tile, r):
        tok = tok_ref[src_ref[tile] + r]
        return pltpu.make_async_copy(hn_hbm.at[pl.ds(pl.multiple_of(tok * ROW_TILE, ROW_TILE), ROW_TILE)],
                                     xs_ref.at[pl.ds(pl.multiple_of(r * ROW_TILE, ROW_TILE), ROW_TILE)], sem)

    @pl.when((i == 0) & (f == 0))
    def _():
        def body(r, carry):
            row_copy(0, r).start()
            return carry
        lax.fori_loop(0, tm, body, 0)

    @pl.when((f == 0) & (i <= n_valid))
    def _():
        pltpu.make_async_copy(xs_ref, xs_ref, sem).wait()
        for s in range(ROW_TILE):
            xb_ref[:, s * 128:(s + 1) * 128] = xs_ref[pl.ds(s, tm, stride=ROW_TILE), :].astype(BF16)

    @pl.when(i < n_valid)
    def _():
        for r in range(share):
            row_copy(i + 1, f * share + r).start(priority=r % MOE_DMA_QUEUES)
        _swiglu_hidden(xb_ref, w1_ref.at[0], w3_ref.at[0], hid_ref)
        part = _dot(hid_ref[...], w2_ref[0])

        @pl.when(f == 0)
        def _():
            acc_ref[...] = part

        @pl.when(f > 0)
        def _():
            acc_ref[...] += part

    @pl.when(f == nf - 1)
    def _():
        o_ref[...] = jnp.where(i < n_valid, acc_ref[...], 0.0).astype(o_ref.dtype)


def _live_f(i, f, nv):
    return jnp.where(i < nv[0], f, 0)


def _moe_grouped(tile_expert, n_valid, tile_src, sorted_tok, hn, w1, w3, w2, tm):
    d = ROW_TILE * hn.shape[1]
    n_tiles = tile_expert.shape[0]
    dff = w1.shape[2]
    tf = dff // MOE_F_STEPS
    grid_spec = pltpu.PrefetchScalarGridSpec(
        num_scalar_prefetch=4,
        grid=(n_tiles, MOE_F_STEPS),
        in_specs=[
            pl.BlockSpec(memory_space=pl.ANY),
            pl.BlockSpec((1, d, tf), lambda i, f, te, nv, *_: (te[i], 0, _live_f(i, f, nv))),
            pl.BlockSpec((1, d, tf), lambda i, f, te, nv, *_: (te[i], 0, _live_f(i, f, nv))),
            pl.BlockSpec((1, tf, d), lambda i, f, te, nv, *_: (te[i], _live_f(i, f, nv), 0)),
        ],
        out_specs=pl.BlockSpec((tm, d), lambda i, f, *_: (i, 0)),
        scratch_shapes=[pltpu.VMEM((tm * ROW_TILE, 128), F32), pltpu.VMEM((tm, d), BF16),
                        pltpu.VMEM((tm, tf), BF16), pltpu.VMEM((tm, d), F32), pltpu.SemaphoreType.DMA(())],
    )
    return pl.pallas_call(
        functools.partial(_moe_kernel, tm=tm),
        grid_spec=grid_spec,
        out_shape=jax.ShapeDtypeStruct((n_tiles * tm, d), BF16),
        compiler_params=pltpu.CompilerParams(
            dimension_semantics=("arbitrary", "arbitrary"), vmem_limit_bytes=VMEM_LIMIT),
        name="moe_grouped",
    )(tile_expert, n_valid, tile_src, sorted_tok, hn, w1, w3, w2)


def _moe_dispatch(fields, counts_row, tm):
    n = fields.shape[1]
    n_slots = n * TOP_K
    n_tiles = (n_slots + N_EXPERTS * tm + tm - 1) // tm
    counts = counts_row[0, :N_EXPERTS].astype(jnp.int32)
    padded = ((counts + tm - 1) // tm) * tm
    pend = jnp.cumsum(padded)
    pstart = pend - padded
    gstart = jnp.cumsum(counts) - counts
    eid = fields[0:TOP_K].astype(jnp.int32)
    rank = fields[2 * TOP_K:3 * TOP_K].astype(jnp.int32)
    is_e = [eid == e for e in range(N_EXPERTS)]
    lookup = lambda table: sum(jnp.where(m, table[e], 0) for e, m in enumerate(is_e))
    pos = lookup(pstart) + rank
    tok_bits = max(1, (n - 1).bit_length())
    assert n_slots < (1 << (31 - tok_bits))
    tok = jnp.broadcast_to(jnp.arange(n, dtype=jnp.int32)[None, :], (TOP_K, n))
    packed = jnp.sort((((lookup(gstart) + rank) << tok_bits) | tok).reshape(-1))
    sorted_tok = packed & ((1 << tok_bits) - 1)
    sorted_tok = jnp.concatenate([sorted_tok, jnp.zeros((tm,), jnp.int32)])
    tile_start = jnp.arange(n_tiles, dtype=jnp.int32) * tm
    tile_expert = jnp.minimum(jnp.sum((tile_start[:, None] >= pend[None, :]).astype(jnp.int32), axis=1),
                              N_EXPERTS - 1)
    tile_src = jnp.clip(tile_start - (pstart - gstart)[tile_expert], 0, n_slots)
    n_valid = (pend[-1] // tm).astype(jnp.int32).reshape(1)
    return sorted_tok, tile_src, pos, tile_expert, n_valid


def _final_kernel(h1_ref, y0_ref, y1_ref, route_ref, g_ref, o_ref):
    g0 = route_ref[:, TOP_K:TOP_K + 1]
    g1 = route_ref[:, TOP_K + 1:TOP_K + 2]
    h = h1_ref[...] + (g0 * y0_ref[...].astype(F32) + g1 * y1_ref[...].astype(F32))
    o_ref[...] = h * lax.rsqrt(jnp.mean(h * h, axis=-1, keepdims=True) + EPS) * g_ref[...]


def _final(h1, y01, route, g, bsz, total):
    n, d = h1.shape
    seq = total - CHUNK
    tm = _pick_tile(seq, (1024, 128))
    per_seq = seq // tm
    rows = lambda width, base=0: pl.BlockSpec(
        (pl.Element(tm), pl.Element(width)),
        lambda b, k: (pl.multiple_of(base + b * total + CHUNK + k * tm, CHUNK), 0))
    out = pl.pallas_call(
        _final_kernel,
        grid=(bsz, per_seq),
        in_specs=[rows(d), rows(d), rows(d, n), rows(ROUTE_LANES), pl.BlockSpec((1, d), lambda b, k: (0, 0))],
        out_specs=pl.BlockSpec((tm, d), lambda b, k: (b * per_seq + k, 0)),
        out_shape=jax.ShapeDtypeStruct((bsz * seq, d), F32),
        compiler_params=pltpu.CompilerParams(
            dimension_semantics=("arbitrary", "arbitrary"), vmem_limit_bytes=VMEM_LIMIT),
        name="final_norm",
    )(h1, y01, y01, route, g)
    return out.reshape(bsz, seq, d)


def _relayout_in_cols(w):
    d = w.shape[0]

    def halves_first(block):
        return block.reshape(d, RET_HEADS, 2, RET_DK // 2).transpose(0, 2, 1, 3).reshape(d, RET_KEY_WIDTH)

    return jnp.concatenate([w[:, :C_RQ], halves_first(w[:, C_RQ:C_RK]), halves_first(w[:, C_RK:C_RV]),
                            w[:, C_RV:]], axis=1)


def _hg_tables(nb):
    c = CHUNK
    i = np.arange(c)[:, None]
    t = np.arange(c)[None, :]
    blocks, masks, rsel = [], [], []
    for lvl in range(HG_LEVELS):
        s = 1 << lvl
        r = (i // (2 * s)) * (2 * s) + s - 1
        right = i > r
        m = np.where(right, (t > r) & (t <= i), (t > i) & (t <= r))
        blocks.append(m)
        j = t
        same = (i // (2 * s)) == (j // (2 * s))
        mk = same & right & (j <= r)
        masks.append(np.concatenate([mk, mk], axis=1))
        if s < 8:
            rsel.append(np.broadcast_to(right, (c, HG_WIDTH)))
    blocks.append(t <= i)
    blocks.append(t > i)
    eye = np.eye(nb, dtype=np.float32)
    mall = np.concatenate([np.kron(eye, blk.astype(np.float32)) for blk in blocks], axis=0)
    lvl = np.stack(masks, axis=0).astype(np.float32)
    ch = np.arange(HG_WIDTH)
    bones = (ch[:, None] // HG_DK == ch[None, :] // HG_DK).astype(np.float32)
    rsel = np.stack([np.tile(r, (nb, 1)) for r in rsel], axis=0).astype(np.float32)
    return mall, lvl, rsel, bones


def _ret_tables():
    f32 = jnp.float32
    log_gamma = jnp.log1p(-jnp.power(2.0, -5.0 - jnp.arange(RET_HEADS, dtype=f32)))
    n = jnp.arange(CHUNK, dtype=f32)
    lg = log_gamma[:, None]
    causal = jnp.tril(jnp.ones((CHUNK, CHUNK), dtype=bool))
    intra = jnp.exp(jnp.where(causal[None], (n[:, None] - n[None, :])[None] * lg[:, :, None], -jnp.inf))
    scale = RET_DK ** -0.5
    inter = jnp.exp((n[None, :] + 1.0) * lg)
    to_state = jnp.exp((CHUNK - 1.0 - n[None, :]) * lg)
    carry = jnp.exp(CHUNK * lg)[:, 0]
    head_of_v = np.arange(RET_WIDTH) // RET_DV
    head_of_k = (np.arange(RET_KEY_WIDTH) % (RET_KEY_WIDTH // 2)) // (RET_DK // 2)
    inter_t = (inter * scale).T[:, head_of_v]
    to_state_t = to_state.T[:, head_of_k]
    carry_row = carry[head_of_v][None, :]
    bmask = jnp.asarray((head_of_k[:, None] == head_of_v[None, :]).astype(np.float32))
    intra_pairs = (intra * scale).reshape(RET_HEADS // 2, 2, CHUNK, CHUNK)
    intra_pairs = jnp.concatenate([intra_pairs[:, 0], intra_pairs[:, 1]], axis=2)
    return intra_pairs, inter_t, to_state_t, carry_row, bmask


def _s5_tables(lam_re, lam_im, b_re, b_im, c_re, c_im, d_skip, log_step):
    f32 = jnp.float32
    lam = lax.complex(lam_re.astype(f32), lam_im.astype(f32))
    step = jnp.exp(log_step.astype(f32))[:, None]
    lam_dt = lam * step
    lam_bar = jnp.exp(lam_dt)
    b_bar = ((lam_bar - 1.0) / lam)[..., None] * lax.complex(b_re.astype(f32), b_im.astype(f32))
    eye = jnp.eye(S5_NGROUPS, dtype=f32)
    wb_re = jnp.einsum('gph,gk->ghkp', jnp.real(b_bar), eye).reshape(S5_WIDTH, S5_NSTATE)
    wb_im = jnp.einsum('gph,gk->ghkp', jnp.imag(b_bar), eye).reshape(S5_WIDTH, S5_NSTATE)
    wb = jnp.concatenate([wb_re, wb_im], axis=1)
    wc_re = jnp.einsum('ghp,gk->gpkh', c_re.astype(f32), eye).reshape(S5_NSTATE, S5_WIDTH)
    wc_im = jnp.einsum('ghp,gk->gpkh', c_im.astype(f32), eye).reshape(S5_NSTATE, S5_WIDTH)
    wc = jnp.concatenate([wc_re, -wc_im], axis=0)
    t = jnp.arange(CHUNK, dtype=f32)[:, None, None]
    pp = jnp.exp(lam_dt[None] * t).reshape(CHUNK, S5_NSTATE)
    pn = jnp.exp(-lam_dt[None] * t).reshape(CHUNK, S5_NSTATE)
    lam_rows = jnp.stack([jnp.real(lam_bar).reshape(-1), jnp.imag(lam_bar).reshape(-1)], axis=0)
    return (wb.astype(BF16), wc.astype(BF16), jnp.real(pn), jnp.imag(pn), jnp.real(pp), jnp.imag(pp),
            lam_rows, d_skip.astype(f32).reshape(1, S5_WIDTH))


def kernel(x, meta_tokens, norm_mix_g, w_in, s5_lam_re, s5_lam_im, s5_b_re, s5_b_im, s5_c_re, s5_c_im, s5_d, s5_log_step, s5_w_glu, s5_out_g, hg_conv_w, hg_lb_param, hg_out_g, ret_out_g, w_out, norm_ffn_g, ffn_w1, ffn_w3, ffn_w2, moe_router, moe_w1, moe_w3, moe_w2, final_norm_g):
    bsz, seq_len, d = x.shape
    depth = w_in.shape[0]
    total = seq_len + CHUNK
    n_chunks = total // CHUNK
    n = bsz * total

    meta = jnp.broadcast_to(meta_tokens.astype(F32)[None], (bsz, N_META, d))
    tm0 = _pick_tile(total, (TOKEN_TILE, 128))
    head = jnp.concatenate([jnp.zeros((bsz, PAD, d), F32), meta, x[:, :tm0 - CHUNK].astype(F32)], axis=1)
    stream = (x.astype(F32).reshape(bsz * seq_len, d), head)

    pos = (jnp.arange(total) - PAD).astype(F32)
    half = RET_DK // 2
    inv_freq = ROPE_BASE ** (-jnp.arange(half, dtype=F32) / half)
    ang = pos[:, None] * inv_freq[None, :]
    cos_t = jnp.tile(jnp.cos(ang), (1, RET_HEADS))
    sin_t = jnp.tile(jnp.sin(ang), (1, RET_HEADS))

    lb_all = jnp.cumsum(jax.nn.softmax(hg_lb_param.astype(F32), axis=0), axis=0)
    lb_all = lb_all - lb_all[0]

    mall_np, lvl_np, rsel_np, bones_np = _hg_tables(bsz)
    mall = jnp.asarray(mall_np, BF16)
    lvl = jnp.asarray(lvl_np, F32)
    rsel = jnp.asarray(rsel_np, F32)
    bones = jnp.asarray(bones_np, BF16)
    avg = jnp.asarray(bones_np / HG_DK, BF16)
    ltri = jnp.asarray(np.kron(np.eye(bsz, dtype=np.float32),
                               np.tril(np.ones((CHUNK, CHUNK), np.float32))), BF16)
    intra, inter_t, to_state_t, carry_row, bmask = _ret_tables()

    out = None
    for l in range(depth):
        w_in_l = _relayout_in_cols(w_in[l].astype(BF16))
        proj = _in_proj(stream, norm_mix_g[l].astype(F32).reshape(1, d), w_in_l, bsz, total)

        s5c = _s5_tables(s5_lam_re[l], s5_lam_im[l], s5_b_re[l], s5_b_im[l], s5_c_re[l], s5_c_im[l],
                         s5_d[l], s5_log_step[l])
        lb = lb_all[l][None, :]
        lbv = jnp.concatenate([jnp.log(lb), jnp.log1p(-lb), 1.0 - lb], axis=0)
        consts = list(s5c) + [
            s5_w_glu[l].astype(BF16), s5_out_g[l].astype(F32).reshape(1, -1), ltri,
            hg_conv_w[l].astype(F32), lbv, mall, lvl, rsel, bones, avg, hg_out_g[l].astype(F32).reshape(1, -1),
            intra, inter_t, to_state_t, carry_row, bmask, ret_out_g[l].astype(F32).reshape(1, -1),
        ]
        j = l // 2
        if l % 2 == 0:
            ffn_f32 = [ffn_w1[j], ffn_w3[j], ffn_w2[j]]
        else:
            dff_e = moe_w1.shape[-1]
            ffn_f32 = [moe_w1[j].reshape(N_EXPERTS * d, dff_e), moe_w3[j].reshape(N_EXPERTS * d, dff_e),
                       moe_w2[j].reshape(N_EXPERTS * dff_e, d)]
        mixed, ffn_bf16 = _mixers(proj.reshape(bsz, total, IN_COLS), cos_t, sin_t, consts, bsz, n_chunks,
                                  to_bf16=[a.astype(F32) for a in ffn_f32])
        mixed = mixed.reshape(n, D_MIX)

        g_ffn = norm_ffn_g[l].astype(F32).reshape(1, d)
        w_out_l = w_out[l].astype(BF16)
        if l % 2 == 0:
            h1, hn = _out_proj(mixed, stream, w_out_l, g_ffn, bsz, total)
            stream = (_ffn(hn, h1, *ffn_bf16),)
            y01 = None
        else:
            router_pad = jnp.zeros((d, ROUTE_LANES), F32).at[:, :N_EXPERTS].set(moe_router[j].astype(F32))
            h1, hn, route, fields, counts_row = _out_proj(mixed, stream, w_out_l, g_ffn, bsz, total, router_pad)
            tm = MOE_TILE
            sorted_tok, tile_src, pos_of_slot, tile_expert, n_valid = _moe_dispatch(fields, counts_row, tm)
            e_w1, e_w3, e_w2 = (ffn_bf16[0].reshape(N_EXPERTS, d, dff_e), ffn_bf16[1].reshape(N_EXPERTS, d, dff_e),
                                ffn_bf16[2].reshape(N_EXPERTS, dff_e, d))
            ys = _moe_grouped(tile_expert, n_valid, tile_src, sorted_tok, hn, e_w1, e_w3, e_w2, tm)
            y01 = ys.at[pos_of_slot.reshape(-1)].get(mode='promise_in_bounds')
            if l < depth - 1:
                stream = (h1 + (route[:, TOP_K:TOP_K + 1] * y01[:n].astype(F32)
                                + route[:, TOP_K + 1:TOP_K + 2] * y01[n:].astype(F32)),)

        if l == depth - 1:
            if y01 is None:
                y01 = jnp.zeros((TOP_K * n, d), BF16)
                h1 = stream[0]
                route = jnp.zeros((n, ROUTE_LANES), F32)
            out = _final(h1, y01, route, final_norm_g.astype(F32).reshape(1, d), bsz, total)

    return out.astype(x.dtype)
```
